```python
import math
import jax, jax.numpy as jnp
from jax import lax
import numpy as np

D_MODEL = 2048
BATCH = 2
SEQ = 8192
DEPTH = 2

GRID_W = 64
CTX_LEN = 256
BLOCK_Q = 128
EPS = 1e-6
ROPE_THETA = 10000.0

A_HEADS = 8
A_KV_HEADS = 2
A_HEAD_DIM = 128
A_Q_W = A_HEADS * A_HEAD_DIM
A_KV_W = A_KV_HEADS * A_HEAD_DIM
A_QKV_W = A_Q_W + 2 * A_KV_W
HY_CH = D_MODEL // 2
HY_ORDER = 2
HY_FILTER_W = 64
HY_BANDS = 16
HY_EMB = 1 + 2 * HY_BANDS
HY_FAST = 0.3
HY_SLOW = 1.5
HY_TARGET = 1e-2
AB_IN_W = A_QKV_W + (HY_ORDER + 1) * HY_CH
AB_MIX_W = A_Q_W + HY_CH
C_HEADS = 16
C_HEAD_DIM = 64
C_Q_W = C_HEADS * 2 * C_HEAD_DIM
C_V_W = C_HEADS * 2 * C_HEAD_DIM
C_IN_W = 2 * C_Q_W + C_V_W
C_MIX_W = C_V_W
D_FF = 5632
N_EVEN = (DEPTH + 1) // 2
N_ODD = DEPTH // 2
F32 = jnp.float32

kernel_name = 'hybrid_dit_gqa_hyena_diffattn_prefix'


def rmsnorm(x, g):
    xf = x.astype(F32)
    y = xf * lax.rsqrt(jnp.mean(xf * xf, axis=-1, keepdims=True) + EPS)
    return (y * g.astype(F32)).astype(x.dtype)


def dwconv3(x, w, b):
    xp = jnp.pad(x, ((0, 0), (1, 1), (0, 0)))
    return xp[:, :-2] * w[0] + xp[:, 1:-1] * w[1] + xp[:, 2:] * w[2] + b


def axial_rope(n_tok, rot_dim):
    rows = n_tok // GRID_W
    row = jnp.broadcast_to(jnp.arange(rows, dtype=jnp.int32)[:, None], (rows, GRID_W)).reshape(n_tok)
    col = jnp.broadcast_to(jnp.arange(GRID_W, dtype=jnp.int32)[None, :], (rows, GRID_W)).reshape(n_tok)
    axis_dim = rot_dim // 2
    inv_freq = ROPE_THETA ** (-jnp.arange(0, axis_dim, 2, dtype=F32) / axis_dim)
    ang = jnp.concatenate([row.astype(F32)[:, None] * inv_freq,
                           col.astype(F32)[:, None] * inv_freq], axis=-1)
    return jnp.cos(ang), jnp.sin(ang)


def apply_rope(x, cos, sin):
    bshape = (1, cos.shape[0]) + (1,) * (x.ndim - 3) + (cos.shape[1],)
    co = cos.reshape(bshape).astype(x.dtype)
    si = sin.reshape(bshape).astype(x.dtype)
    x1 = x[..., 0::2]
    x2 = x[..., 1::2]
    return jnp.stack([x1 * co - x2 * si, x1 * si + x2 * co], axis=-1).reshape(x.shape)


def sweep_query_blocks(attend, q):
    b, n = q.shape[:2]
    nb = n // BLOCK_Q
    qb = jnp.moveaxis(q.reshape((b, nb, BLOCK_Q) + q.shape[2:]), 1, 0)
    ob = lax.map(attend, qb)
    return jnp.moveaxis(ob, 0, 1).reshape(b, n, ob.shape[-1])


def gqa_attend(q, k, v):
    b, tq = q.shape[:2]
    qg = q.reshape(b, tq, A_KV_HEADS, A_HEADS // A_KV_HEADS, A_HEAD_DIM)
    s = jnp.einsum('bqkgd,bskd->bkgqs', qg, k).astype(F32) * (A_HEAD_DIM ** -0.5)
    p = jax.nn.softmax(s, axis=-1).astype(v.dtype)
    o = jnp.einsum('bkgqs,bskd->bqkgd', p, v)
    return o.reshape(b, tq, A_Q_W)


def diff_attend(q, k, v, lam, subln_g, lambda_init):
    b, tq = q.shape[:2]
    s = jnp.einsum('bqhmd,bshmd->bhmqs', q, k).astype(F32) * (C_HEAD_DIM ** -0.5)
    p = jax.nn.softmax(s, axis=-1)
    p_diff = (p[:, :, 0] - lam * p[:, :, 1]).astype(v.dtype)
    o = jnp.einsum('bhqs,bshe->bqhe', p_diff, v)
    o = rmsnorm(o, subln_g) * (1.0 - lambda_init)
    return o.reshape(b, tq, C_MIX_W)


def hyena_filters(n_tok, w1, b1, w2, b2, w3, freq):
    t = jnp.arange(n_tok, dtype=F32)
    t_norm = t / max(n_tok - 1, 1)
    w = (2.0 * math.pi / n_tok) * t
    bands = jnp.linspace(1e-4, HY_BANDS - 1, HY_BANDS, dtype=F32)
    z = w[:, None] * bands
    feats = jnp.concatenate([t_norm[:, None], jnp.cos(z), -jnp.sin(z)], axis=-1)
    h = jnp.sin(freq[0] * (feats @ w1 + b1))
    h = jnp.sin(freq[1] * (h @ w2 + b2))
    h = (h @ w3).astype(F32).reshape(n_tok, HY_ORDER, 2, HY_CH)
    deltas = jnp.abs(jnp.linspace(math.log(HY_TARGET) / HY_FAST, math.log(HY_TARGET) / HY_SLOW, HY_CH, dtype=F32))
    decay = jnp.exp(-t_norm[:, None] * deltas)
    h = h * decay[:, None, None, :]
    filt = jnp.concatenate([h[:, :, 0], jnp.zeros((1, HY_ORDER, HY_CH), F32), h[:0:-1, :, 1]], axis=0)
    filt = filt / jnp.sum(jnp.abs(filt), axis=0, keepdims=True)
    return jnp.fft.rfft(filt, axis=0)


def long_conv(u, filt_f, skip):
    n = u.shape[1]
    uf = jnp.fft.rfft(u.astype(F32), n=2 * n, axis=1)
    y = jnp.fft.irfft(uf * filt_f, n=2 * n, axis=1)[:, :n]
    return (y + u.astype(F32) * skip.astype(F32)).astype(u.dtype)


def hyena(p, conv_w, conv_b, w1, b1, w2, b2, w3, freq, skip):
    n = p.shape[1]
    p = dwconv3(p, conv_w, conv_b)
    v, x1, x2 = jnp.split(p, 3, axis=-1)
    filt_f = hyena_filters(n, w1, b1, w2, b2, w3, freq)
    z = x1 * long_conv(v, filt_f[:, 0], skip[0])
    z = x2 * long_conv(z, filt_f[:, 1], skip[1])
    return z


def mixer_ab(h_lat, h_ctx, with_ctx, w_in, w_out, qk_g, hy_params):
    b, n, _ = h_lat.shape

    def heads(t, nh):
        return t.reshape(t.shape[:2] + (nh, A_HEAD_DIM))

    kv_c = h_ctx @ w_in[:, A_Q_W:A_QKV_W]
    k_c = rmsnorm(heads(kv_c[..., :A_KV_W], A_KV_HEADS), qk_g[1])
    v_c = heads(kv_c[..., A_KV_W:], A_KV_HEADS)
    p = h_lat @ w_in
    cos, sin = axial_rope(n, A_HEAD_DIM)
    q = apply_rope(rmsnorm(heads(p[..., :A_Q_W], A_HEADS), qk_g[0]), cos, sin)
    k = apply_rope(rmsnorm(heads(p[..., A_Q_W:A_Q_W + A_KV_W], A_KV_HEADS), qk_g[1]), cos, sin)
    v = heads(p[..., A_Q_W + A_KV_W:A_QKV_W], A_KV_HEADS)
    k_all = jnp.concatenate([k, k_c], axis=1)
    v_all = jnp.concatenate([v, v_c], axis=1)
    o_att = sweep_query_blocks(lambda qb: gqa_attend(qb, k_all, v_all), q)
    o_hy = hyena(p[..., A_QKV_W:], *hy_params)
    y_lat = jnp.concatenate([o_att, o_hy], axis=-1) @ w_out
    if not with_ctx:
        return y_lat, None
    q_c = rmsnorm(heads(h_ctx @ w_in[:, :A_Q_W], A_HEADS), qk_g[0])
    o_att_c = gqa_attend(q_c, k_c, v_c)
    o_hy_c = hyena(h_ctx @ w_in[:, A_QKV_W:], *hy_params)
    y_ctx = jnp.concatenate([o_att_c, o_hy_c], axis=-1) @ w_out
    return y_lat, y_ctx


def mixer_c(h_lat, h_ctx, with_ctx, lambda_init, w_in, w_out, lam_vecs, subln_g):
    b, n, _ = h_lat.shape

    def qk_heads(t):
        return t.reshape(t.shape[:2] + (C_HEADS, 2, C_HEAD_DIM))

    def v_heads(t):
        return t.reshape(t.shape[:2] + (C_HEADS, 2 * C_HEAD_DIM))

    lv = lam_vecs.astype(F32)
    lam = jnp.exp(jnp.sum(lv[0] * lv[1])) - jnp.exp(jnp.sum(lv[2] * lv[3])) + lambda_init
    kv_c = h_ctx @ w_in[:, C_Q_W:]
    k_c = qk_heads(kv_c[..., :C_Q_W])
    v_c = v_heads(kv_c[..., C_Q_W:])
    p = h_lat @ w_in
    cos, sin = axial_rope(n, C_HEAD_DIM)
    q = apply_rope(qk_heads(p[..., :C_Q_W]), cos, sin)
    k = apply_rope(qk_heads(p[..., C_Q_W:2 * C_Q_W]), cos, sin)
    v = v_heads(p[..., 2 * C_Q_W:])
    k_all = jnp.concatenate([k, k_c], axis=1)
    v_all = jnp.concatenate([v, v_c], axis=1)
    o = sweep_query_blocks(lambda qb: diff_attend(qb, k_all, v_all, lam, subln_g, lambda_init), q)
    y_lat = o @ w_out
    if not with_ctx:
        return y_lat, None
    q_c = qk_heads(h_ctx @ w_in[:, :C_Q_W])
    y_ctx = diff_attend(q_c, k_c, v_c, lam, subln_g, lambda_init) @ w_out
    return y_lat, y_ctx


def conv_ffn(h, w_up, conv_w, conv_b, w_down):
    u = dwconv3(h @ w_up, conv_w, conv_b)
    a, g = jnp.split(u, 2, axis=-1)
    return (a * jax.nn.silu(g)) @ w_down


def setup_inputs(seed: int = 0) -> dict:
    key = jax.random.key(seed)
    ks = jax.random.split(key, 27)

    def nrm(k, shape, scale=1.0):
        return jax.random.normal(k, shape, F32) * scale

    def gain(k, shape):
        return 1.0 + 0.1 * jax.random.normal(k, shape, F32)

    return {
        'x': nrm(ks[0], (BATCH, SEQ, D_MODEL)),
        'c': nrm(ks[1], (BATCH, D_MODEL)),
        'ctx': nrm(ks[2], (BATCH, CTX_LEN, D_MODEL)),
        'c_ctx': nrm(ks[3], (D_MODEL,)),
        'ada_w': nrm(ks[4], (DEPTH, D_MODEL, 6 * D_MODEL), D_MODEL ** -0.5),
        'ada_b': nrm(ks[5], (DEPTH, 6 * D_MODEL), 0.01),
        'norm_g': gain(ks[6], (DEPTH, 4, D_MODEL)),
        'ab_w_in': nrm(ks[7], (N_EVEN, D_MODEL, AB_IN_W), D_MODEL ** -0.5),
        'ab_w_out': nrm(ks[8], (N_EVEN, AB_MIX_W, D_MODEL), AB_MIX_W ** -0.5),
        'ab_qk_g': gain(ks[9], (N_EVEN, 2, A_HEAD_DIM)),
        'hy_conv_w': nrm(ks[10], (N_EVEN, 3, 3 * HY_CH), 3 ** -0.5),
        'hy_conv_b': nrm(ks[11], (N_EVEN, 3 * HY_CH), 0.01),
        'hy_w1': nrm(ks[12], (N_EVEN, HY_EMB, HY_FILTER_W), HY_EMB ** -0.5),
        'hy_b1': nrm(ks[13], (N_EVEN, HY_FILTER_W), 0.1),
        'hy_w2': nrm(ks[14], (N_EVEN, HY_FILTER_W, HY_FILTER_W), HY_FILTER_W ** -0.5),
        'hy_b2': nrm(ks[15], (N_EVEN, HY_FILTER_W), 0.1),
        'hy_w3': nrm(ks[16], (N_EVEN, HY_FILTER_W, HY_ORDER * 2 * HY_CH), HY_FILTER_W ** -0.5),
        'hy_freq': gain(ks[17], (N_EVEN, 2, HY_FILTER_W)),
        'hy_skip': nrm(ks[18], (N_EVEN, HY_ORDER, HY_CH)),
        'dc_w_in': nrm(ks[19], (N_ODD, D_MODEL, C_IN_W), D_MODEL ** -0.5),
        'dc_w_out': nrm(ks[20], (N_ODD, C_MIX_W, D_MODEL), C_MIX_W ** -0.5),
        'dc_lambda': nrm(ks[21], (N_ODD, 4, C_HEAD_DIM), 0.1),
        'dc_subln_g': gain(ks[22], (N_ODD, 2 * C_HEAD_DIM)),
        'ffn_w_up': nrm(ks[23], (DEPTH, D_MODEL, 2 * D_FF), D_MODEL ** -0.5),
        'ffn_conv_w': nrm(ks[24], (DEPTH, 3, 2 * D_FF), 3 ** -0.5),
        'ffn_conv_b': nrm(ks[25], (DEPTH, 2 * D_FF), 0.01),
        'ffn_w_down': nrm(ks[26], (DEPTH, D_FF, D_MODEL), D_FF ** -0.5),
    }


def reference(x, c, ctx, c_ctx, ada_w, ada_b, norm_g, ab_w_in, ab_w_out, ab_qk_g,
              hy_conv_w, hy_conv_b, hy_w1, hy_b1, hy_w2, hy_b2, hy_w3, hy_freq, hy_skip,
              dc_w_in, dc_w_out, dc_lambda, dc_subln_g,
              ffn_w_up, ffn_conv_w, ffn_conv_b, ffn_w_down):
    for i in range(DEPTH):
        j = i // 2
        with_ctx = i < DEPTH - 1
        mod = jax.nn.silu(c) @ ada_w[i] + ada_b[i]
        mod_c = jax.nn.silu(c_ctx) @ ada_w[i] + ada_b[i]
        sh1, sc1, g1, sh2, sc2, g2 = jnp.split(mod[:, None, :], 6, axis=-1)
        sh1c, sc1c, g1c, sh2c, sc2c, g2c = jnp.split(mod_c, 6, axis=-1)
        h = rmsnorm(x, norm_g[i, 0]) * (1.0 + sc1) + sh1
        hc = rmsnorm(ctx, norm_g[i, 0]) * (1.0 + sc1c) + sh1c
        if i % 2 == 0:
            hy_params = (hy_conv_w[j], hy_conv_b[j], hy_w1[j], hy_b1[j], hy_w2[j], hy_b2[j],
                         hy_w3[j], hy_freq[j], hy_skip[j])
            y, yc = mixer_ab(h, hc, with_ctx, ab_w_in[j], ab_w_out[j], ab_qk_g[j], hy_params)
        else:
            lambda_init = 0.8 - 0.6 * math.exp(-0.3 * i)
            y, yc = mixer_c(h, hc, with_ctx, lambda_init, dc_w_in[j], dc_w_out[j], dc_lambda[j], dc_subln_g[j])
        x = x + g1 * rmsnorm(y, norm_g[i, 1])
        h = rmsnorm(x, norm_g[i, 2]) * (1.0 + sc2) + sh2
        x = x + g2 * rmsnorm(conv_ffn(h, ffn_w_up[i], ffn_conv_w[i], ffn_conv_b[i], ffn_w_down[i]), norm_g[i, 3])
        if with_ctx:
            ctx = ctx + g1c * rmsnorm(yc, norm_g[i, 1])
            hc = rmsnorm(ctx, norm_g[i, 2]) * (1.0 + sc2c) + sh2c
            ctx = ctx + g2c * rmsnorm(conv_ffn(hc, ffn_w_up[i], ffn_conv_w[i], ffn_conv_b[i], ffn_w_down[i]), norm_g[i, 3])
    return x
```

```python
import functools
import math

import jax
import jax.numpy as jnp
from jax import lax
from jax.experimental import pallas as pl
from jax.experimental.pallas import tpu as pltpu

F32 = jnp.float32
BF16 = jnp.bfloat16
HIGHEST = lax.Precision.HIGHEST

D_MODEL = 2048
GRID_W = 64
EPS = 1e-6
ROPE_THETA = 10000.0
A_HEADS, A_KV_HEADS, A_HEAD_DIM = 8, 2, 128
A_Q_W = A_HEADS * A_HEAD_DIM
A_KV_W = A_KV_HEADS * A_HEAD_DIM
A_QKV_W = A_Q_W + 2 * A_KV_W
HY_CH = D_MODEL // 2
HY_ORDER = 2
HY_FILTER_W = 64
HY_BANDS = 16
HY_FAST, HY_SLOW, HY_TARGET = 0.3, 1.5, 1e-2
C_HEADS, C_HEAD_DIM = 16, 64
C_Q_W = C_HEADS * 2 * C_HEAD_DIM
D_FF = 5632

LANES = 128
BF16_SUBLANES = 16
VMEM_LIMIT = 56 * 1024 * 1024
DFT_RADIX = 128


def _cparams(*sem):
    return pltpu.CompilerParams(dimension_semantics=sem, vmem_limit_bytes=VMEM_LIMIT)


def _rms(x, g):
    return x * lax.rsqrt(jnp.mean(x * x, axis=-1, keepdims=True) + EPS) * g


def _ada_kernel(c_ref, w_ref, b_ref, o_ref):
    c = c_ref[...]
    s = c * jax.nn.sigmoid(c)
    o_ref[0] = jnp.dot(s, w_ref[0], precision=HIGHEST, preferred_element_type=F32) + b_ref[0]


def ada_mod(cvec, ada_w, ada_b):
    depth, d, n6 = ada_w.shape
    rows = cvec.shape[0]
    tn = 1024
    return pl.pallas_call(
        _ada_kernel,
        grid=(depth, n6 // tn),
        in_specs=[pl.BlockSpec((rows, d), lambda l, j: (0, 0)),
                  pl.BlockSpec((1, d, tn), lambda l, j: (l, 0, j)),
                  pl.BlockSpec((1, 1, tn), lambda l, j: (l, 0, j))],
        out_specs=pl.BlockSpec((1, rows, tn), lambda l, j: (l, 0, j)),
        out_shape=jax.ShapeDtypeStruct((depth, rows, n6), F32),
        compiler_params=_cparams("arbitrary", "arbitrary"),
        name="ada_mod",
    )(cvec, ada_w, ada_b.reshape(depth, 1, n6))


def _norm_mod_kernel(x_ref, g_ref, sc_ref, sh_ref, o_ref):
    y = _rms(x_ref[0], g_ref[...])
    o_ref[0] = (y * (1.0 + sc_ref[0]) + sh_ref[0]).astype(o_ref.dtype)


def norm_mod(x, g, sc, sh):
    b, n, d = x.shape
    tm = min(n, 512)
    return pl.pallas_call(
        _norm_mod_kernel,
        grid=(b, n // tm),
        in_specs=[pl.BlockSpec((1, tm, d), lambda bi, i: (bi, i, 0)),
                  pl.BlockSpec((1, d), lambda bi, i: (0, 0)),
                  pl.BlockSpec((1, 1, d), lambda bi, i: (bi, 0, 0)),
                  pl.BlockSpec((1, 1, d), lambda bi, i: (bi, 0, 0))],
        out_specs=pl.BlockSpec((1, tm, d), lambda bi, i: (bi, i, 0)),
        out_shape=jax.ShapeDtypeStruct((b, n, d), BF16),
        compiler_params=_cparams("arbitrary", "arbitrary"),
        name="norm_mod",
    )(x, g.reshape(1, d), sc.reshape(b, 1, d), sh.reshape(b, 1, d))


def _inproj_kernel(*refs, hd, norm, rope):
    if rope:
        h_ref, w_ref, g_ref, cos_ref, sin_ref, o_ref = refs
    else:
        h_ref, w_ref, g_ref, o_ref = refs
    acc = jnp.dot(h_ref[0], w_ref[...], preferred_element_type=F32)
    tn = acc.shape[1]
    if rope:
        cosf = cos_ref[...]
        sinf = sin_ref[...]
        if hd != LANES:
            lane = lax.broadcasted_iota(jnp.int32, cosf.shape, 1)
            first_half = (lane % hd) < (hd // 2)
    for c in range(tn // LANES):
        x = acc[:, c * LANES:(c + 1) * LANES]
        if norm:
            x = x * lax.rsqrt(jnp.mean(x * x, axis=-1, keepdims=True) + EPS)
        x = x * g_ref[:, c * LANES:(c + 1) * LANES]
        if rope:
            if hd == LANES:
                partner = pltpu.roll(x, LANES // 2, 1)
            else:
                partner = jnp.where(first_half, pltpu.roll(x, LANES - hd // 2, 1),
                                    pltpu.roll(x, hd // 2, 1))
            x = x * cosf + partner * sinf
        o_ref[0, :, c * LANES:(c + 1) * LANES] = x.astype(o_ref.dtype)


def inproj(h, w, gvec, cosf=None, sinf=None, *, hd=LANES, norm=False, out_dtype=BF16):
    b, n, d = h.shape
    ncol = w.shape[1]
    rope = cosf is not None
    tm = min(n, 512)
    tn = min(ncol, 1024)
    in_specs = [pl.BlockSpec((1, tm, d), lambda bi, i, j: (bi, i, 0)),
                pl.BlockSpec((d, tn), lambda bi, i, j: (0, j)),
                pl.BlockSpec((1, tn), lambda bi, i, j: (0, j))]
    args = [h, w, gvec.reshape(1, ncol)]
    if rope:
        in_specs += [pl.BlockSpec((tm, LANES), lambda bi, i, j: (i, 0)),
                     pl.BlockSpec((tm, LANES), lambda bi, i, j: (i, 0))]
        args += [cosf, sinf]
    return pl.pallas_call(
        functools.partial(_inproj_kernel, hd=hd, norm=norm, rope=rope),
        grid=(b, n // tm, ncol // tn),
        in_specs=in_specs,
        out_specs=pl.BlockSpec((1, tm, tn), lambda bi, i, j: (bi, i, j)),
        out_shape=jax.ShapeDtypeStruct((b, n, ncol), out_dtype),
        compiler_params=_cparams("arbitrary", "arbitrary", "arbitrary"),
        name="inproj",
    )(*args)


def _dwconv_rows(u, u_prev_row, u_next_row, cw, cb):
    rows = u.shape[0]
    r = lax.broadcasted_iota(jnp.int32, u.shape, 0)
    up = jnp.where(r == 0, u_prev_row, pltpu.roll(u, 1, 0))
    dn = jnp.where(r == rows - 1, u_next_row, pltpu.roll(u, rows - 1, 0))
    return cw[0:1] * up + cw[1:2] * u + cw[2:3] * dn + cb


def _inproj_conv_kernel(hm_ref, hp_ref, hn_ref, w_ref, cw_ref, cb_ref, o_ref, *, n_tiles):
    i = pl.program_id(1)
    w = w_ref[...]
    u = jnp.dot(hm_ref[0], w, preferred_element_type=F32)
    up = jnp.dot(hp_ref[0], w, preferred_element_type=F32)
    un = jnp.dot(hn_ref[0], w, preferred_element_type=F32)
    halo = up.shape[0]
    prev_row = jnp.where(i > 0, up[halo - 1:halo], 0.0)
    next_row = jnp.where(i < n_tiles - 1, un[0:1], 0.0)
    o_ref[0, 0] = _dwconv_rows(u, prev_row, next_row, cw_ref[...], cb_ref[...])


def inproj_conv(h, w, conv_w, conv_b, groups):
    b, n, d = h.shape
    ncol = w.shape[1]
    c = ncol // groups
    tm = min(n, 512)
    tn = 512
    halo = BF16_SUBLANES
    nt = n // tm
    per = c // tn
    return pl.pallas_call(
        functools.partial(_inproj_conv_kernel, n_tiles=nt),
        grid=(b, nt, ncol // tn),
        in_specs=[pl.BlockSpec((1, tm, d), lambda bi, i, j: (bi, i, 0)),
                  pl.BlockSpec((1, halo, d),
                               lambda bi, i, j: (bi, jnp.maximum(i * (tm // halo) - 1, 0), 0)),
                  pl.BlockSpec((1, halo, d),
                               lambda bi, i, j: (bi, jnp.minimum((i + 1) * (tm // halo), n // halo - 1), 0)),
                  pl.BlockSpec((d, tn), lambda bi, i, j: (0, j)),
                  pl.BlockSpec((3, tn), lambda bi, i, j: (0, j)),
                  pl.BlockSpec((1, tn), lambda bi, i, j: (0, j))],
        out_specs=pl.BlockSpec((1, 1, tm, tn), lambda bi, i, j: (j // per, bi, i, j % per)),
        out_shape=jax.ShapeDtypeStruct((groups, b, n, c), F32),
        compiler_params=_cparams("arbitrary", "arbitrary", "arbitrary"),
        name="inproj_conv",
    )(h, h, h, w, conv_w, conv_b.reshape(1, ncol))


def _flash_kernel(*refs, mode, group, tq, n_kv, lambda_init):
    if mode == "diff":
        q_ref, k_ref, v_ref, lam_ref, sg_ref, o_ref, qs_ref, m_ref, l_ref, acc_ref = refs
    else:
        q_ref, k_ref, v_ref, o_ref, qs_ref, m_ref, l_ref, acc_ref = refs
    kv = pl.program_id(3)

    @pl.when(kv == 0)
    def _init():
        q = q_ref[0]
        if mode == "diff":
            lane = lax.broadcasted_iota(jnp.int32, q.shape, 1)
            zero = jnp.zeros_like(q)
            qs_ref[0:tq] = jnp.where(lane < C_HEAD_DIM, q, zero)
            qs_ref[tq:2 * tq] = jnp.where(lane >= C_HEAD_DIM, q, zero)
        else:
            for g in range(group):
                qs_ref[g * tq:(g + 1) * tq] = q[:, g * LANES:(g + 1) * LANES]
        m_ref[...] = jnp.full(m_ref.shape, -jnp.inf, F32)
        l_ref[...] = jnp.zeros(l_ref.shape, F32)
        acc_ref[...] = jnp.zeros(acc_ref.shape, F32)

    s = lax.dot_general(qs_ref[...], k_ref[0], (((1,), (1,)), ((), ())),
                        preferred_element_type=F32)
    m_prev = m_ref[...]
    m_new = jnp.maximum(m_prev, jnp.max(s, axis=-1, keepdims=True))
    alpha = jnp.exp(m_prev - m_new)
    p = jnp.exp(s - m_new)
    l_ref[...] = alpha * l_ref[...] + jnp.sum(p, axis=-1, keepdims=True)
    acc_ref[...] = alpha * acc_ref[...] + jnp.dot(p.astype(BF16), v_ref[0],
                                                  preferred_element_type=F32)
    m_ref[...] = m_new

    @pl.when(kv == n_kv - 1)
    def _fin():
        o = acc_ref[...] / l_ref[...]
        if mode == "diff":
            lv = lam_ref[...]
            lam = (jnp.exp(jnp.sum(lv[0:1] * lv[1:2], axis=-1, keepdims=True))
                   - jnp.exp(jnp.sum(lv[2:3] * lv[3:4], axis=-1, keepdims=True)) + lambda_init)
            dlt = o[0:tq] - lam * o[tq:2 * tq]
            o_ref[0] = (_rms(dlt, sg_ref[...]) * (1.0 - lambda_init)).astype(o_ref.dtype)
        else:
            for g in range(group):
                o_ref[0, :, g * LANES:(g + 1) * LANES] = o[g * tq:(g + 1) * tq].astype(o_ref.dtype)


def _kv_tile(s):
    for t in (768, 1024, 512, 256, 128):
        if s % t == 0:
            return t
    return s


def flash_attention(q, k, v, *, mode, lam_vecs=None, subln_g=None, lambda_init=0.0):
    b, n, qw = q.shape
    s = k.shape[1]
    if mode == "diff":
        group, n_groups, qblk = 2, qw // LANES, LANES
        tq = min(n, 512)
    else:
        group = A_HEADS // A_KV_HEADS
        n_groups, qblk = A_KV_HEADS, group * LANES
        tq = min(n, 256)
    tk = _kv_tile(s)
    n_kv = s // tk
    in_specs = [pl.BlockSpec((1, tq, qblk), lambda bi, g, i, j: (bi, i, g)),
                pl.BlockSpec((1, tk, LANES), lambda bi, g, i, j: (bi, j, g)),
                pl.BlockSpec((1, tk, LANES), lambda bi, g, i, j: (bi, j, g))]
    args = [q, k, v]
    if mode == "diff":
        in_specs += [pl.BlockSpec(lam_vecs.shape, lambda bi, g, i, j: (0, 0)),
                     pl.BlockSpec((1, LANES), lambda bi, g, i, j: (0, 0))]
        args += [lam_vecs, subln_g.reshape(1, LANES)]
    return pl.pallas_call(
        functools.partial(_flash_kernel, mode=mode, group=group, tq=tq, n_kv=n_kv,
                          lambda_init=lambda_init),
        grid=(b, n_groups, n // tq, n_kv),
        in_specs=in_specs,
        out_specs=pl.BlockSpec((1, tq, qblk), lambda bi, g, i, j: (bi, i, g)),
        out_shape=jax.ShapeDtypeStruct((b, n, qw), BF16),
        scratch_shapes=[pltpu.VMEM((group * tq, LANES), BF16),
                        pltpu.VMEM((group * tq, 1), F32),
                        pltpu.VMEM((group * tq, 1), F32),
                        pltpu.VMEM((group * tq, LANES), F32)],
        compiler_params=_cparams("arbitrary", "arbitrary", "arbitrary", "arbitrary"),
        name="flash_" + mode,
    )(*args)


def _outproj_kernel(*refs, n_lhs):
    lhs = refs[:n_lhs]
    ws = refs[n_lhs:2 * n_lhs]
    x_ref, g_ref, gate_ref, o_ref = refs[2 * n_lhs:]
    y = jnp.dot(lhs[0][0], ws[0][...], preferred_element_type=F32)
    for a, w in zip(lhs[1:], ws[1:]):
        y = y + jnp.dot(a[0], w[...], preferred_element_type=F32)
    o_ref[0] = x_ref[0] + gate_ref[0] * _rms(y, g_ref[...])


def outproj_residual(lhs_list, w_list, x, g, gate):
    b, n, d = x.shape
    tm = min(n, 512)
    n_lhs = len(lhs_list)
    in_specs = [pl.BlockSpec((1, tm, a.shape[2]), lambda bi, i: (bi, i, 0)) for a in lhs_list]
    in_specs += [pl.BlockSpec(w.shape, lambda bi, i: (0, 0)) for w in w_list]
    in_specs += [pl.BlockSpec((1, tm, d), lambda bi, i: (bi, i, 0)),
                 pl.BlockSpec((1, d), lambda bi, i: (0, 0)),
                 pl.BlockSpec((1, 1, d), lambda bi, i: (bi, 0, 0))]
    return pl.pallas_call(
        functools.partial(_outproj_kernel, n_lhs=n_lhs),
        grid=(b, n // tm),
        in_specs=in_specs,
        out_specs=pl.BlockSpec((1, tm, d), lambda bi, i: (bi, i, 0)),
        out_shape=jax.ShapeDtypeStruct((b, n, d), F32),
        compiler_params=_cparams("arbitrary", "arbitrary"),
        name="outproj",
    )(*lhs_list, *w_list, x, g.reshape(1, d), gate.reshape(b, 1, d))


def _ffn_kernel(xm_ref, xp_ref, xn_ref, gn_ref, sc_ref, sh_ref, wa_ref, wg_ref, cwa_ref, cwg_ref,
                cba_ref, cbg_ref, wd_ref, g3_ref, gate_ref, o_ref, hx_ref, *, tm, halo, n_tiles, n_f):
    i = pl.program_id(1)
    j = pl.program_id(2)

    @pl.when(j == 0)
    def _prologue():
        def nm(x):
            return (_rms(x, gn_ref[...]) * (1.0 + sc_ref[0]) + sh_ref[0]).astype(BF16)
        hx_ref[halo:halo + tm] = nm(xm_ref[0])
        zero = jnp.zeros((halo, xm_ref.shape[2]), BF16)
        hx_ref[0:halo] = jnp.where(i > 0, nm(xp_ref[0]), zero)
        hx_ref[halo + tm:] = jnp.where(i < n_tiles - 1, nm(xn_ref[0]), zero)

    hx = hx_ref[...]
    rows = tm + 2 * halo

    def conv(u, cw, cb):
        up = pltpu.roll(u, 1, 0)
        dn = pltpu.roll(u, rows - 1, 0)
        return (cw[0:1] * up + cw[1:2] * u + cw[2:3] * dn + cb)[halo:halo + tm]

    a = conv(jnp.dot(hx, wa_ref[...], preferred_element_type=F32), cwa_ref[...], cba_ref[...])
    g = conv(jnp.dot(hx, wg_ref[...], preferred_element_type=F32), cwg_ref[...], cbg_ref[...])
    act = (a * (g * jax.nn.sigmoid(g))).astype(BF16)
    contrib = jnp.dot(act, wd_ref[...], preferred_element_type=F32)

    @pl.when(j == 0)
    def _first():
        o_ref[0] = contrib

    @pl.when(j > 0)
    def _rest():
        o_ref[0] += contrib

    @pl.when(j == n_f - 1)
    def _epilogue():
        o_ref[0] = xm_ref[0] + gate_ref[0] * _rms(o_ref[0], g3_ref[...])


def conv_ffn_residual(x, gn, sc, sh, w_up, conv_w, conv_b, w_down, g3, gate):
    b, n, d = x.shape
    dff = w_down.shape[0]
    tm = min(n, 512)
    tf = 512
    halo = BF16_SUBLANES
    nt = n // tm
    n_f = dff // tf
    row = lambda bi, i, j: (bi, i, 0)
    vec = lambda bi, i, j: (0, 0)
    bvec = lambda bi, i, j: (bi, 0, 0)
    conv_b = conv_b.reshape(1, 2 * dff)
    return pl.pallas_call(
        functools.partial(_ffn_kernel, tm=tm, halo=halo, n_tiles=nt, n_f=n_f),
        grid=(b, nt, n_f),
        in_specs=[pl.BlockSpec((1, tm, d), row),
                  pl.BlockSpec((1, halo, d),
                               lambda bi, i, j: (bi, jnp.maximum(i * (tm // halo) - 1, 0), 0)),
                  pl.BlockSpec((1, halo, d),
                               lambda bi, i, j: (bi, jnp.minimum((i + 1) * (tm // halo), n // halo - 1), 0)),
                  pl.BlockSpec((1, d), vec),
                  pl.BlockSpec((1, 1, d), bvec),
                  pl.BlockSpec((1, 1, d), bvec),
                  pl.BlockSpec((d, tf), lambda bi, i, j: (0, j)),
                  pl.BlockSpec((d, tf), lambda bi, i, j: (0, n_f + j)),
                  pl.BlockSpec((3, tf), lambda bi, i, j: (0, j)),
                  pl.BlockSpec((3, tf), lambda bi, i, j: (0, n_f + j)),
                  pl.BlockSpec((1, tf), lambda bi, i, j: (0, j)),
                  pl.BlockSpec((1, tf), lambda bi, i, j: (0, n_f + j)),
                  pl.BlockSpec((tf, d), lambda bi, i, j: (j, 0)),
                  pl.BlockSpec((1, d), vec),
                  pl.BlockSpec((1, 1, d), bvec)],
        out_specs=pl.BlockSpec((1, tm, d), row),
        out_shape=jax.ShapeDtypeStruct((b, n, d), F32),
        scratch_shapes=[pltpu.VMEM((tm + 2 * halo, d), BF16)],
        compiler_params=_cparams("arbitrary", "arbitrary", "arbitrary"),
        name="conv_ffn",
    )(x, x, x, gn.reshape(1, d), sc.reshape(b, 1, d), sh.reshape(b, 1, d), w_up, w_up,
      conv_w, conv_w, conv_b, conv_b, w_down, g3.reshape(1, d), gate.reshape(b, 1, d))


def _filter_kernel(bands_ref, w1_ref, b1_ref, w2_ref, b2_ref, w3_ref, fr_ref, dl_ref,
                   o_ref, s_ref, *, n_tok, tr):
    i = pl.program_id(1)
    r = (i * tr + lax.broadcasted_iota(jnp.int32, (tr, 1), 0)).astype(F32)
    t = jnp.where(r < n_tok, r, 2.0 * n_tok - r)
    t_norm = t / float(max(n_tok - 1, 1))
    wang = (2.0 * math.pi / n_tok) * t
    z = wang * bands_ref[...]
    lane = lax.broadcasted_iota(jnp.int32, z.shape, 1)
    feats = jnp.where(lane == 0, t_norm,
                      jnp.where(lane <= HY_BANDS, jnp.cos(z),
                                jnp.where(lane <= 2 * HY_BANDS, -jnp.sin(z), 0.0)))
    fr = fr_ref[...]
    h = jnp.sin(fr[0:1] * (jnp.dot(feats, w1_ref[...], precision=HIGHEST,
                                   preferred_element_type=F32) + b1_ref[...]))
    h = jnp.sin(fr[1:2] * (jnp.dot(h, w2_ref[...], precision=HIGHEST,
                                   preferred_element_type=F32) + b2_ref[...]))
    h = jnp.dot(h, w3_ref[...], precision=HIGHEST, preferred_element_type=F32)
    decay = jnp.exp(-t_norm * dl_ref[...])
    out = jnp.where(r == n_tok, 0.0, h * decay)
    o_ref[0] = out
    part = jnp.sum(jnp.abs(out), axis=0, keepdims=True)

    @pl.when(i == 0)
    def _first():
        s_ref[0] = part

    @pl.when(i > 0)
    def _rest():
        s_ref[0] += part


def hyena_filter_time(n_tok, w1, b1, w2, b2, w3, freq):
    ch = HY_CH
    ll = 2 * n_tok
    tr = min(n_tok, 512)
    per_half = n_tok // tr
    lane = jnp.arange(LANES)
    bands = jnp.linspace(1e-4, HY_BANDS - 1, HY_BANDS, dtype=F32)
    bands_l = jnp.where((lane >= 1) & (lane <= HY_BANDS), bands[jnp.clip(lane - 1, 0, HY_BANDS - 1)],
                        jnp.where((lane > HY_BANDS) & (lane <= 2 * HY_BANDS),
                                  bands[jnp.clip(lane - 1 - HY_BANDS, 0, HY_BANDS - 1)], 0.0))
    w1p = jnp.zeros((LANES, HY_FILTER_W), F32).at[:w1.shape[0]].set(w1)
    deltas = jnp.abs(jnp.linspace(math.log(HY_TARGET) / HY_FAST, math.log(HY_TARGET) / HY_SLOW,
                                  ch, dtype=F32))
    const = lambda o, i: (0, 0)
    return pl.pallas_call(
        functools.partial(_filter_kernel, n_tok=n_tok, tr=tr),
        grid=(HY_ORDER, ll // tr),
        in_specs=[pl.BlockSpec((1, LANES), const),
                  pl.BlockSpec((LANES, HY_FILTER_W), const),
                  pl.BlockSpec((1, HY_FILTER_W), const),
                  pl.BlockSpec((HY_FILTER_W, HY_FILTER_W), const),
                  pl.BlockSpec((1, HY_FILTER_W), const),
                  pl.BlockSpec((HY_FILTER_W, ch), lambda o, i: (0, o * 2 + i // per_half)),
                  pl.BlockSpec((2, HY_FILTER_W), const),
                  pl.BlockSpec((1, ch), const)],
        out_specs=[pl.BlockSpec((1, tr, ch), lambda o, i: (o, i, 0)),
                   pl.BlockSpec((1, 1, ch), lambda o, i: (o, 0, 0))],
        out_shape=[jax.ShapeDtypeStruct((HY_ORDER, ll, ch), F32),
                   jax.ShapeDtypeStruct((HY_ORDER, 1, ch), F32)],
        compiler_params=_cparams("arbitrary", "arbitrary"),
        name="hyena_filter",
    )(bands_l.reshape(1, LANES), w1p, b1.reshape(1, -1), w2, b2.reshape(1, -1), w3, freq,
      deltas.reshape(1, ch))


def _dft_dot(f, x):
    return jnp.dot(f.astype(BF16), x.astype(BF16), preferred_element_type=F32)


def _stage_a_kernel(f_ref, x_ref, o_ref):
    parts = x_ref.shape[0]
    x = x_ref[0] if parts == 1 else jnp.concatenate([x_ref[p] for p in range(parts)], axis=0)
    y = _dft_dot(f_ref[...], x)
    half = y.shape[0] // 2
    o_ref[0] = y[:half]
    o_ref[1] = y[half:]


def dft_stage_a(f, x):
    p, rows, w = x.shape
    k1 = p * rows
    tl = min(w, 4096)
    return pl.pallas_call(
        _stage_a_kernel,
        grid=(w // tl,),
        in_specs=[pl.BlockSpec(f.shape, lambda j: (0, 0)),
                  pl.BlockSpec((p, rows, tl), lambda j: (0, 0, j))],
        out_specs=pl.BlockSpec((2, k1, tl), lambda j: (0, 0, j)),
        out_shape=jax.ShapeDtypeStruct((2, k1, w), F32),
        compiler_params=_cparams("arbitrary"),
        name="dft_stage_a",
    )(f, x)


def _stage_c_filter_kernel(a_ref, g_ref, sc_ref, o_ref, *, kc):
    for k in range(kc):
        r = jnp.concatenate([a_ref[0, k], a_ref[1, k]], axis=0)
        y = _dft_dot(g_ref[k], r) * sc_ref[...]
        half = y.shape[0] // 2
        o_ref[0, k] = y[:half]
        o_ref[1, k] = y[half:]


def dft_stage_c_filter(a, g, scale):
    _, k1, r, c = a.shape
    kc = 8
    tc = 256
    blk = pl.BlockSpec((2, kc, r, tc), lambda i, j: (0, i, 0, j))
    return pl.pallas_call(
        functools.partial(_stage_c_filter_kernel, kc=kc),
        grid=(k1 // kc, c // tc),
        in_specs=[blk,
                  pl.BlockSpec((kc, 2 * r, 2 * r), lambda i, j: (i, 0, 0)),
                  pl.BlockSpec((1, tc), lambda i, j: (0, j))],
        out_specs=blk,
        out_shape=jax.ShapeDtypeStruct(a.shape, F32),
        compiler_params=_cparams("arbitrary", "arbitrary"),
        name="dft_stage_c_filter",
    )(a, g, scale)


def _stage_c_conv_kernel(a_ref, h_ref, g_ref, gi_ref, o_ref, *, kc):
    for k in range(kc):
        r = jnp.concatenate([a_ref[0, k], a_ref[1, k]], axis=0)
        y = _dft_dot(g_ref[k], r)
        half = y.shape[0] // 2
        yre, yim = y[:half], y[half:]
        hre, him = h_ref[0, k], h_ref[1, k]
        p = jnp.concatenate([yre * hre - yim * him, yre * him + yim * hre], axis=0)
        d = _dft_dot(gi_ref[k], p)
        o_ref[0, k] = d[:half]
        o_ref[1, k] = d[half:]


def dft_stage_c_conv(a, hspec, g, gi):
    _, k1, r, c = a.shape
    kc = 8
    tc = 256
    blk = pl.BlockSpec((2, kc, r, tc), lambda i, j: (0, i, 0, j))
    mat = pl.BlockSpec((kc, 2 * r, 2 * r), lambda i, j: (i, 0, 0))
    return pl.pallas_call(
        functools.partial(_stage_c_conv_kernel, kc=kc),
        grid=(k1 // kc, c // tc),
        in_specs=[blk, blk, mat, mat],
        out_specs=blk,
        out_shape=jax.ShapeDtypeStruct(a.shape, F32),
        compiler_params=_cparams("arbitrary", "arbitrary"),
        name="dft_stage_c_conv",
    )(a, hspec, g, gi)


def _stage_a_inv_kernel(f_ref, d_ref, u_ref, gt_ref, sk_ref, o_ref):
    d = jnp.concatenate([d_ref[0], d_ref[1]], axis=0)
    y = _dft_dot(f_ref[...], d)
    half = y.shape[0] // 2
    sk = sk_ref[...]
    o_ref[0] = (gt_ref[0] * (y[:half] + u_ref[0] * sk)).astype(o_ref.dtype)
    o_ref[1] = (gt_ref[1] * (y[half:] + u_ref[1] * sk)).astype(o_ref.dtype)


def dft_stage_a_inv(f, d, u, gate, skip_flat, out_dtype):
    _, k1, w = d.shape
    half = k1 // 2
    tl = skip_flat.shape[1]
    io = pl.BlockSpec((2, half, tl), lambda j: (0, 0, j))
    return pl.pallas_call(
        _stage_a_inv_kernel,
        grid=(w // tl,),
        in_specs=[pl.BlockSpec(f.shape, lambda j: (0, 0)),
                  pl.BlockSpec((2, k1, tl), lambda j: (0, 0, j)),
                  io, io,
                  pl.BlockSpec((1, tl), lambda j: (0, 0))],
        out_specs=io,
        out_shape=jax.ShapeDtypeStruct((2, half, w), out_dtype),
        compiler_params=_cparams("arbitrary"),
        name="dft_stage_a_inv",
    )(f, d, u, gate, skip_flat)


def _dft_tables(n_tok):
    ll = 2 * n_tok
    r = DFT_RADIX
    k1n = ll // r
    half = k1n // 2
    two_pi = 2.0 * math.pi

    def cs(m, period):
        ang = (m % period).astype(F32) * (two_pi / period)
        return jnp.cos(ang), jnp.sin(ang)

    k1 = jnp.arange(k1n, dtype=jnp.int32)
    c, s = cs(k1[:, None] * k1[None, :half], k1n)
    fa_data = jnp.concatenate([jnp.concatenate([c, s], 1), jnp.concatenate([-s, c], 1)], 0)
    c, s = cs(k1[:, None] * k1[None, :], k1n)
    fa_filt = jnp.concatenate([c, -s], 0)
    c, s = cs(k1[:half, None] * k1[None, :], k1n)
    fa_inv = jnp.concatenate([jnp.concatenate([c, -s], 1), jnp.concatenate([s, c], 1)], 0)
    idx = jnp.arange(r, dtype=jnp.int32)
    m = idx[None, None, :] * (idx[None, :, None] * k1n + k1[:, None, None])
    c, s = cs(m, ll)
    g_fwd = jnp.concatenate([jnp.concatenate([c, s], 2), jnp.concatenate([-s, c], 2)], 1)
    ct, st = jnp.swapaxes(c, 1, 2), jnp.swapaxes(s, 1, 2)
    g_inv = jnp.concatenate([jnp.concatenate([ct, -st], 2), jnp.concatenate([st, ct], 2)], 1)
    return fa_data, fa_filt, fa_inv, g_fwd, g_inv


def hyena_long(pc, filt, l1, skip, out_dtype):
    _, b, n, c = pc.shape
    assert b == 2, "the two batch entries ride as real/imaginary parts"
    ll = 2 * n
    r = DFT_RADIX
    k1n = ll // r
    half = k1n // 2
    w = r * c
    fa_data, fa_filt, fa_inv, g_fwd, g_inv = _dft_tables(n)
    tl = 4096
    u = pc[0].reshape(2, half, w)
    for o in range(HY_ORDER):
        af = dft_stage_a(fa_filt, filt[o].reshape(1, k1n, w))
        hspec = dft_stage_c_filter(af.reshape(2, k1n, r, c), g_fwd, 1.0 / (l1[o] * ll))
        a = dft_stage_a(fa_data, u)
        d = dft_stage_c_conv(a.reshape(2, k1n, r, c), hspec, g_fwd, g_inv)
        last = o == HY_ORDER - 1
        u = dft_stage_a_inv(fa_inv, d.reshape(2, k1n, w), u, pc[o + 1].reshape(2, half, w),
                            jnp.tile(skip[o], tl // c).reshape(1, tl),
                            out_dtype if last else F32)
    return u.reshape(b, n, c)


def _hyena_short_kernel(ff_ref, fr_ref, fi_ref, pc_ref, filt_ref, l1_ref, skip_ref, o_ref, *, n, ll):
    u = [pc_ref[0, 0], pc_ref[0, 1]]
    for o in range(HY_ORDER):
        hs = _dft_dot(fr_ref[...], filt_ref[o]) * (1.0 / (l1_ref[o] * ll))
        y = _dft_dot(ff_ref[...], jnp.concatenate(u, axis=0))
        yre, yim, hre, him = y[:ll], y[ll:], hs[:ll], hs[ll:]
        p = jnp.concatenate([yre * hre - yim * him, yre * him + yim * hre], axis=0)
        d = _dft_dot(fi_ref[...], p)
        sk = skip_ref[o:o + 1]
        u = [pc_ref[o + 1, bi] * (d[bi * n:(bi + 1) * n] + u[bi] * sk) for bi in range(2)]
    o_ref[0] = u[0].astype(o_ref.dtype)
    o_ref[1] = u[1].astype(o_ref.dtype)


def hyena_short(pc, filt, l1, skip, out_dtype):
    _, b, n, c = pc.shape
    assert b == 2
    ll = 2 * n
    two_pi = 2.0 * math.pi
    kk = jnp.arange(ll, dtype=jnp.int32)
    ang = ((kk[:, None] * kk[None, :]) % ll).astype(F32) * (two_pi / ll)
    cf, sf = jnp.cos(ang), jnp.sin(ang)
    f_real = jnp.concatenate([cf, -sf], 0)
    cn, sn = cf[:, :n], sf[:, :n]
    f_fwd = jnp.concatenate([jnp.concatenate([cn, sn], 1), jnp.concatenate([-sn, cn], 1)], 0)
    ci, si = cf[:n, :], sf[:n, :]
    f_inv = jnp.concatenate([jnp.concatenate([ci, -si], 1), jnp.concatenate([si, ci], 1)], 0)
    tc = 256
    const = lambda j: (0, 0)
    return pl.pallas_call(
        functools.partial(_hyena_short_kernel, n=n, ll=ll),
        grid=(c // tc,),
        in_specs=[pl.BlockSpec(f_fwd.shape, const),
                  pl.BlockSpec(f_real.shape, const),
                  pl.BlockSpec(f_inv.shape, const),
                  pl.BlockSpec((3, 2, n, tc), lambda j: (0, 0, 0, j)),
                  pl.BlockSpec((HY_ORDER, ll, tc), lambda j: (0, 0, j)),
                  pl.BlockSpec((HY_ORDER, 1, tc), lambda j: (0, 0, j)),
                  pl.BlockSpec((HY_ORDER, tc), lambda j: (0, j))],
        out_specs=pl.BlockSpec((2, n, tc), lambda j: (0, 0, j)),
        out_shape=jax.ShapeDtypeStruct((2, n, c), out_dtype),
        compiler_params=_cparams("arbitrary"),
        name="hyena_short",
    )(f_fwd, f_real, f_inv, pc, filt, l1, skip)


def hyena_mixer(pc, w1, b1, w2, b2, w3, freq, skip, out_dtype):
    n = pc.shape[2]
    filt, l1 = hyena_filter_time(n, w1, b1, w2, b2, w3, freq)
    if (2 * n) % (DFT_RADIX * BF16_SUBLANES) == 0 and n >= 1024:
        return hyena_long(pc, filt, l1, skip, out_dtype)
    return hyena_short(pc, filt, l1, skip, out_dtype)


def _deinterleave_cols(w, hd):
    lead = w.shape[:-1]
    wh = w.reshape(lead + (w.shape[-1] // hd, hd // 2, 2))
    return jnp.concatenate([wh[..., 0], wh[..., 1]], axis=-1).reshape(w.shape)


def _rope_tables(n_tok, rot_dim):
    rows = n_tok // GRID_W
    row = jnp.broadcast_to(jnp.arange(rows, dtype=jnp.int32)[:, None], (rows, GRID_W)).reshape(n_tok)
    col = jnp.broadcast_to(jnp.arange(GRID_W, dtype=jnp.int32)[None, :], (rows, GRID_W)).reshape(n_tok)
    axis_dim = rot_dim // 2
    inv_freq = ROPE_THETA ** (-jnp.arange(0, axis_dim, 2, dtype=F32) / axis_dim)
    ang = jnp.concatenate([row.astype(F32)[:, None] * inv_freq,
                           col.astype(F32)[:, None] * inv_freq], axis=-1)
    c, s = jnp.cos(ang), jnp.sin(ang)
    reps = LANES // rot_dim
    return jnp.tile(jnp.concatenate([c, c], -1), (1, reps)), jnp.tile(jnp.concatenate([-s, s], -1), (1, reps))


def _mixer_ab(h, hc, with_ctx, w_in, w_out, qk_g, hy):
    b, n, _ = h.shape
    conv_w, conv_b, w1, b1, w2, b2, w3, freq, skip = hy
    scale = A_HEAD_DIM ** -0.5
    wq = _deinterleave_cols(w_in[:, :A_Q_W], A_HEAD_DIM).astype(BF16)
    wk = _deinterleave_cols(w_in[:, A_Q_W:A_Q_W + A_KV_W], A_HEAD_DIM).astype(BF16)
    wv = w_in[:, A_Q_W + A_KV_W:A_QKV_W].astype(BF16)
    wh = w_in[:, A_QKV_W:].astype(BF16)
    gq = jnp.tile(_deinterleave_cols(qk_g[0], A_HEAD_DIM) * scale, A_HEADS)
    gk = jnp.tile(_deinterleave_cols(qk_g[1], A_HEAD_DIM), A_KV_HEADS)
    ones_v = jnp.ones((A_KV_W,), F32)
    cosf, sinf = _rope_tables(n, A_HEAD_DIM)

    q = inproj(h, wq, gq, cosf, sinf, norm=True)
    k = inproj(h, wk, gk, cosf, sinf, norm=True)
    v = inproj(h, wv, ones_v)
    k_c = inproj(hc, wk, gk, norm=True)
    v_c = inproj(hc, wv, ones_v)
    k_all = jnp.concatenate([k, k_c], axis=1)
    v_all = jnp.concatenate([v, v_c], axis=1)
    o_att = flash_attention(q, k_all, v_all, mode="gqa")
    pc = inproj_conv(h, wh, conv_w, conv_b, HY_ORDER + 1)
    o_hy = hyena_mixer(pc, w1, b1, w2, b2, w3, freq, skip, BF16)
    wo = w_out.astype(BF16)
    lat = ([o_att, o_hy], [wo[:A_Q_W], wo[A_Q_W:]])
    if not with_ctx:
        return lat, None
    q_c = inproj(hc, wq, gq, norm=True)
    o_att_c = flash_attention(q_c, k_c, v_c, mode="gqa")
    pc_c = inproj_conv(hc, wh, conv_w, conv_b, HY_ORDER + 1)
    o_hy_c = hyena_mixer(pc_c, w1, b1, w2, b2, w3, freq, skip, BF16)
    return lat, ([o_att_c, o_hy_c], [wo[:A_Q_W], wo[A_Q_W:]])


def _mixer_c(h, hc, with_ctx, lambda_init, w_in, w_out, lam_vecs, subln_g):
    b, n, _ = h.shape
    scale = C_HEAD_DIM ** -0.5
    wq = _deinterleave_cols(w_in[:, :C_Q_W], C_HEAD_DIM).astype(BF16)
    wk = _deinterleave_cols(w_in[:, C_Q_W:2 * C_Q_W], C_HEAD_DIM).astype(BF16)
    wv = w_in[:, 2 * C_Q_W:].astype(BF16)
    gq = jnp.full((C_Q_W,), scale, F32)
    ones = jnp.ones((C_Q_W,), F32)
    cosf, sinf = _rope_tables(n, C_HEAD_DIM)

    q = inproj(h, wq, gq, cosf, sinf, hd=C_HEAD_DIM)
    k = inproj(h, wk, ones, cosf, sinf, hd=C_HEAD_DIM)
    v = inproj(h, wv, ones)
    k_c = inproj(hc, wk, ones)
    v_c = inproj(hc, wv, ones)
    k_all = jnp.concatenate([k, k_c], axis=1)
    v_all = jnp.concatenate([v, v_c], axis=1)
    attn = functools.partial(flash_attention, mode="diff", lam_vecs=lam_vecs, subln_g=subln_g,
                             lambda_init=lambda_init)
    wo = w_out.astype(BF16)
    lat = ([attn(q, k_all, v_all)], [wo])
    if not with_ctx:
        return lat, None
    q_c = inproj(hc, wq, gq)
    return lat, ([attn(q_c, k_c, v_c)], [wo])


def kernel(x, c, ctx, c_ctx, ada_w, ada_b, norm_g, ab_w_in, ab_w_out, ab_qk_g,
           hy_conv_w, hy_conv_b, hy_w1, hy_b1, hy_w2, hy_b2, hy_w3, hy_freq, hy_skip,
           dc_w_in, dc_w_out, dc_lambda, dc_subln_g,
           ffn_w_up, ffn_conv_w, ffn_conv_b, ffn_w_down):
    depth = ada_w.shape[0]
    b, n, d = x.shape
    pad_rows = (-(b + 1)) % 8
    cvec = jnp.concatenate([c, c_ctx[None, :], jnp.zeros((pad_rows, d), F32)], axis=0)
    mod_all = ada_mod(cvec, ada_w, ada_b)
    for i in range(depth):
        j = i // 2
        with_ctx = i < depth - 1
        mod = mod_all[i, :b].reshape(b, 6, d)
        mod_c = jnp.broadcast_to(mod_all[i, b].reshape(1, 6, d), (b, 6, d))
        sh1, sc1, g1, sh2, sc2, g2 = (mod[:, t] for t in range(6))
        sh1c, sc1c, g1c, sh2c, sc2c, g2c = (mod_c[:, t] for t in range(6))
        h = norm_mod(x, norm_g[i, 0], sc1, sh1)
        hc = norm_mod(ctx, norm_g[i, 0], sc1c, sh1c)
        if i % 2 == 0:
            hy = (hy_conv_w[j], hy_conv_b[j], hy_w1[j], hy_b1[j], hy_w2[j], hy_b2[j],
                  hy_w3[j], hy_freq[j], hy_skip[j])
            lat, cx = _mixer_ab(h, hc, with_ctx, ab_w_in[j], ab_w_out[j], ab_qk_g[j], hy)
        else:
            lambda_init = 0.8 - 0.6 * math.exp(-0.3 * i)
            lat, cx = _mixer_c(h, hc, with_ctx, lambda_init, dc_w_in[j], dc_w_out[j],
                               dc_lambda[j], dc_subln_g[j])
        w_up = ffn_w_up[i].astype(BF16)
        w_down = ffn_w_down[i].astype(BF16)
        x = outproj_residual(lat[0], lat[1], x, norm_g[i, 1], g1)
        x = conv_ffn_residual(x, norm_g[i, 2], sc2, sh2, w_up, ffn_conv_w[i], ffn_conv_b[i],
                              w_down, norm_g[i, 3], g2)
        if with_ctx:
            ctx = outproj_residual(cx[0], cx[1], ctx, norm_g[i, 1], g1c)
            ctx = conv_ffn_residual(ctx, norm_g[i, 2], sc2c, sh2c, w_up, ffn_conv_w[i],
                                    ffn_conv_b[i], w_down, norm_g[i, 3], g2c)
    return x
```

```python
import functools
import math

import jax
import jax.numpy as jnp
from jax import lax
from jax.experimental import pallas as pl
from jax.experimental.pallas import tpu as pltpu

F32 = jnp.float32
BF16 = jnp.bfloat16
HIGHEST = lax.Precision.HIGHEST

D_MODEL = 2048
GRID_W = 64
EPS = 1e-6
ROPE_THETA = 10000.0
A_HEADS, A_KV_HEADS, A_HEAD_DIM = 8, 2, 128
A_Q_W = A_HEADS * A_HEAD_DIM
A_KV_W = A_KV_HEADS * A_HEAD_DIM
A_QKV_W = A_Q_W + 2 * A_KV_W
HY_CH = D_MODEL // 2
HY_ORDER = 2
HY_FILTER_W = 64
HY_BANDS = 16
HY_FAST, HY_SLOW, HY_TARGET = 0.3, 1.5, 1e-2
C_HEADS, C_HEAD_DIM = 16, 64
C_Q_W = C_HEADS * 2 * C_HEAD_DIM
D_FF = 5632

LANES = 128
BF16_SUBLANES = 16
VMEM_LIMIT = 56 * 1024 * 1024
DFT_RADIX = 128
FLASH_ROWS = 2048
FLASH_SUB_ROWS = 128
LOG2E = math.log2(math.e)


def _cparams(*sem):
    return pltpu.CompilerParams(dimension_semantics=sem, vmem_limit_bytes=VMEM_LIMIT)


def _rms(x, g):
    return x * lax.rsqrt(jnp.mean(x * x, axis=-1, keepdims=True) + EPS) * g


def _ada_kernel(c_ref, w_ref, b_ref, o_ref):
    c = c_ref[...]
    s = c * jax.nn.sigmoid(c)
    o_ref[0] = jnp.dot(s, w_ref[0], precision=HIGHEST, preferred_element_type=F32) + b_ref[0]


def ada_mod(cvec, ada_w, ada_b):
    depth, d, n6 = ada_w.shape
    rows = cvec.shape[0]
    tn = 1024
    return pl.pallas_call(
        _ada_kernel,
        grid=(depth, n6 // tn),
        in_specs=[pl.BlockSpec((rows, d), lambda l, j: (0, 0)),
                  pl.BlockSpec((1, d, tn), lambda l, j: (l, 0, j)),
                  pl.BlockSpec((1, 1, tn), lambda l, j: (l, 0, j))],
        out_specs=pl.BlockSpec((1, rows, tn), lambda l, j: (l, 0, j)),
        out_shape=jax.ShapeDtypeStruct((depth, rows, n6), F32),
        compiler_params=_cparams("arbitrary", "arbitrary"),
        name="ada_mod",
    )(cvec, ada_w, ada_b.reshape(depth, 1, n6))


def _norm_mod_kernel(x_ref, g_ref, sc_ref, sh_ref, o_ref):
    y = _rms(x_ref[0], g_ref[...])
    o_ref[0] = (y * (1.0 + sc_ref[0]) + sh_ref[0]).astype(o_ref.dtype)


def norm_mod(x, g, sc, sh):
    b, n, d = x.shape
    tm = min(n, 512)
    return pl.pallas_call(
        _norm_mod_kernel,
        grid=(b, n // tm),
        in_specs=[pl.BlockSpec((1, tm, d), lambda bi, i: (bi, i, 0)),
                  pl.BlockSpec((1, d), lambda bi, i: (0, 0)),
                  pl.BlockSpec((1, 1, d), lambda bi, i: (bi, 0, 0)),
                  pl.BlockSpec((1, 1, d), lambda bi, i: (bi, 0, 0))],
        out_specs=pl.BlockSpec((1, tm, d), lambda bi, i: (bi, i, 0)),
        out_shape=jax.ShapeDtypeStruct((b, n, d), BF16),
        compiler_params=_cparams("arbitrary", "arbitrary"),
        name="norm_mod",
    )(x, g.reshape(1, d), sc.reshape(b, 1, d), sh.reshape(b, 1, d))


def _inproj_kernel(*refs, hd, norm, rope):
    if rope:
        h_ref, w_ref, g_ref, cos_ref, sin_ref, o_ref = refs
    else:
        h_ref, w_ref, g_ref, o_ref = refs
    acc = jnp.dot(h_ref[0], w_ref[...], preferred_element_type=F32)
    tn = acc.shape[1]
    if rope:
        cosf = cos_ref[...]
        sinf = sin_ref[...]
        if hd != LANES:
            lane = lax.broadcasted_iota(jnp.int32, cosf.shape, 1)
            first_half = (lane % hd) < (hd // 2)
    for c in range(tn // LANES):
        x = acc[:, c * LANES:(c + 1) * LANES]
        if norm:
            x = x * lax.rsqrt(jnp.mean(x * x, axis=-1, keepdims=True) + EPS)
        x = x * g_ref[:, c * LANES:(c + 1) * LANES]
        if rope:
            if hd == LANES:
                partner = pltpu.roll(x, LANES // 2, 1)
            else:
                partner = jnp.where(first_half, pltpu.roll(x, LANES - hd // 2, 1),
                                    pltpu.roll(x, hd // 2, 1))
            x = x * cosf + partner * sinf
        o_ref[0, :, c * LANES:(c + 1) * LANES] = x.astype(o_ref.dtype)


def inproj(h, w, gvec, cosf=None, sinf=None, *, hd=LANES, norm=False, out_dtype=BF16):
    b, n, d = h.shape
    ncol = w.shape[1]
    rope = cosf is not None
    tm = min(n, 512)
    tn = min(ncol, 1024)
    in_specs = [pl.BlockSpec((1, tm, d), lambda bi, i, j: (bi, i, 0)),
                pl.BlockSpec((d, tn), lambda bi, i, j: (0, j)),
                pl.BlockSpec((1, tn), lambda bi, i, j: (0, j))]
    args = [h, w, gvec.reshape(1, ncol)]
    if rope:
        in_specs += [pl.BlockSpec((tm, LANES), lambda bi, i, j: (i, 0)),
                     pl.BlockSpec((tm, LANES), lambda bi, i, j: (i, 0))]
        args += [cosf, sinf]
    return pl.pallas_call(
        functools.partial(_inproj_kernel, hd=hd, norm=norm, rope=rope),
        grid=(b, n // tm, ncol // tn),
        in_specs=in_specs,
        out_specs=pl.BlockSpec((1, tm, tn), lambda bi, i, j: (bi, i, j)),
        out_shape=jax.ShapeDtypeStruct((b, n, ncol), out_dtype),
        compiler_params=_cparams("arbitrary", "arbitrary", "arbitrary"),
        name="inproj",
    )(*args)


def _dwconv_rows(u, u_prev_row, u_next_row, cw, cb):
    rows = u.shape[0]
    r = lax.broadcasted_iota(jnp.int32, u.shape, 0)
    up = jnp.where(r == 0, u_prev_row, pltpu.roll(u, 1, 0))
    dn = jnp.where(r == rows - 1, u_next_row, pltpu.roll(u, rows - 1, 0))
    return cw[0:1] * up + cw[1:2] * u + cw[2:3] * dn + cb


def _inproj_conv_kernel(hm_ref, hp_ref, hn_ref, w_ref, cw_ref, cb_ref, o_ref, *, n_tiles):
    i = pl.program_id(1)
    w = w_ref[...]
    u = jnp.dot(hm_ref[0], w, preferred_element_type=F32)
    up = jnp.dot(hp_ref[0], w, preferred_element_type=F32)
    un = jnp.dot(hn_ref[0], w, preferred_element_type=F32)
    halo = up.shape[0]
    prev_row = jnp.where(i > 0, up[halo - 1:halo], 0.0)
    next_row = jnp.where(i < n_tiles - 1, un[0:1], 0.0)
    o_ref[0, 0] = _dwconv_rows(u, prev_row, next_row, cw_ref[...], cb_ref[...])


def inproj_conv(h, w, conv_w, conv_b, groups):
    b, n, d = h.shape
    ncol = w.shape[1]
    c = ncol // groups
    tm = min(n, 512)
    tn = 512
    halo = BF16_SUBLANES
    nt = n // tm
    per = c // tn
    return pl.pallas_call(
        functools.partial(_inproj_conv_kernel, n_tiles=nt),
        grid=(b, nt, ncol // tn),
        in_specs=[pl.BlockSpec((1, tm, d), lambda bi, i, j: (bi, i, 0)),
                  pl.BlockSpec((1, halo, d),
                               lambda bi, i, j: (bi, jnp.maximum(i * (tm // halo) - 1, 0), 0)),
                  pl.BlockSpec((1, halo, d),
                               lambda bi, i, j: (bi, jnp.minimum((i + 1) * (tm // halo), n // halo - 1), 0)),
                  pl.BlockSpec((d, tn), lambda bi, i, j: (0, j)),
                  pl.BlockSpec((3, tn), lambda bi, i, j: (0, j)),
                  pl.BlockSpec((1, tn), lambda bi, i, j: (0, j))],
        out_specs=pl.BlockSpec((1, 1, tm, tn), lambda bi, i, j: (j // per, bi, i, j % per)),
        out_shape=jax.ShapeDtypeStruct((groups, b, n, c), F32),
        compiler_params=_cparams("arbitrary", "arbitrary", "arbitrary"),
        name="inproj_conv",
    )(h, h, h, w, conv_w, conv_b.reshape(1, ncol))


def _flash_kernel(*refs, mode, group, tq, tk, n_kv, lambda_init):
    if mode == "diff":
        q_ref, k_ref, v_ref, lam_ref, sg_ref, o_ref, qs_ref, m_ref, acc_ref = refs
    else:
        q_ref, k_ref, v_ref, o_ref, qs_ref, m_ref, acc_ref = refs
    kv = pl.program_id(3)

    @pl.when(kv == 0)
    def _init():
        q = q_ref[0]
        if mode == "diff":
            lane = lax.broadcasted_iota(jnp.int32, q.shape, 1)
            zero = jnp.zeros_like(q)
            qs_ref[0:tq] = jnp.where(lane < C_HEAD_DIM, q, zero)
            qs_ref[tq:2 * tq] = jnp.where(lane >= C_HEAD_DIM, q, zero)
        else:
            for g in range(group):
                qs_ref[g * tq:(g + 1) * tq] = q[:, g * LANES:(g + 1) * LANES]
        m_ref[...] = jnp.full(m_ref.shape, -jnp.inf, F32)
        acc_ref[...] = jnp.zeros(acc_ref.shape, F32)

    k = k_ref[0]
    v = v_ref[0]
    v_aug = jnp.concatenate([v, jnp.ones_like(v)], axis=1)
    for c in range(group * tq // FLASH_SUB_ROWS):
        sl = slice(c * FLASH_SUB_ROWS, (c + 1) * FLASH_SUB_ROWS)
        s = lax.dot_general(qs_ref[sl], k, (((1,), (1,)), ((), ())),
                            preferred_element_type=F32)
        m_prev = m_ref[sl]
        m_next = jnp.maximum(m_prev, jnp.max(s, axis=-1, keepdims=True))
        alpha = jnp.exp2(m_prev - m_next)
        p = jnp.exp2(s - jnp.tile(m_next, (1, tk // LANES)))
        acc_ref[sl] = jnp.tile(alpha, (1, 2)) * acc_ref[sl] + jnp.dot(
            p.astype(BF16), v_aug, preferred_element_type=F32)
        m_ref[sl] = m_next

    @pl.when(kv == n_kv - 1)
    def _fin():
        acc = acc_ref[...]
        o = acc[:, :LANES] / acc[:, LANES:]
        if mode == "diff":
            lv = lam_ref[...]
            lam = (jnp.exp(jnp.sum(lv[0:1] * lv[1:2], axis=-1, keepdims=True))
                   - jnp.exp(jnp.sum(lv[2:3] * lv[3:4], axis=-1, keepdims=True)) + lambda_init)
            dlt = o[0:tq] - lam * o[tq:2 * tq]
            o_ref[0] = (_rms(dlt, sg_ref[...]) * (1.0 - lambda_init)).astype(o_ref.dtype)
        else:
            for g in range(group):
                o_ref[0, :, g * LANES:(g + 1) * LANES] = o[g * tq:(g + 1) * tq].astype(o_ref.dtype)


def _kv_tile(s):
    for t in (768, 1024, 512, 256, 128):
        if s % t == 0:
            return t
    return s


def flash_attention(q, k, v, *, mode, lam_vecs=None, subln_g=None, lambda_init=0.0):
    b, n, qw = q.shape
    s = k.shape[1]
    if mode == "diff":
        group, n_groups, qblk = 2, qw // LANES, LANES
    else:
        group = A_HEADS // A_KV_HEADS
        n_groups, qblk = A_KV_HEADS, group * LANES
    tq = min(n, FLASH_ROWS // group)
    tk = _kv_tile(s)
    n_kv = s // tk
    in_specs = [pl.BlockSpec((1, tq, qblk), lambda bi, g, i, j: (bi, i, g)),
                pl.BlockSpec((1, tk, LANES), lambda bi, g, i, j: (bi, j, g)),
                pl.BlockSpec((1, tk, LANES), lambda bi, g, i, j: (bi, j, g))]
    args = [q, k, v]
    if mode == "diff":
        in_specs += [pl.BlockSpec(lam_vecs.shape, lambda bi, g, i, j: (0, 0)),
                     pl.BlockSpec((1, LANES), lambda bi, g, i, j: (0, 0))]
        args += [lam_vecs, subln_g.reshape(1, LANES)]
    return pl.pallas_call(
        functools.partial(_flash_kernel, mode=mode, group=group, tq=tq, tk=tk, n_kv=n_kv,
                          lambda_init=lambda_init),
        grid=(b, n_groups, n // tq, n_kv),
        in_specs=in_specs,
        out_specs=pl.BlockSpec((1, tq, qblk), lambda bi, g, i, j: (bi, i, g)),
        out_shape=jax.ShapeDtypeStruct((b, n, qw), BF16),
        scratch_shapes=[pltpu.VMEM((group * tq, LANES), BF16),
                        pltpu.VMEM((group * tq, LANES), F32),
                        pltpu.VMEM((group * tq, 2 * LANES), F32)],
        compiler_params=_cparams("arbitrary", "arbitrary", "arbitrary", "arbitrary"),
        name="flash_" + mode,
    )(*args)


def _outproj_kernel(*refs, n_lhs):
    lhs = refs[:n_lhs]
    ws = refs[n_lhs:2 * n_lhs]
    x_ref, g_ref, gate_ref, o_ref = refs[2 * n_lhs:]
    y = jnp.dot(lhs[0][0], ws[0][...], preferred_element_type=F32)
    for a, w in zip(lhs[1:], ws[1:]):
        y = y + jnp.dot(a[0], w[...], preferred_element_type=F32)
    o_ref[0] = x_ref[0] + gate_ref[0] * _rms(y, g_ref[...])


def outproj_residual(lhs_list, w_list, x, g, gate):
    b, n, d = x.shape
    tm = min(n, 512)
    n_lhs = len(lhs_list)
    in_specs = [pl.BlockSpec((1, tm, a.shape[2]), lambda bi, i: (bi, i, 0)) for a in lhs_list]
    in_specs += [pl.BlockSpec(w.shape, lambda bi, i: (0, 0)) for w in w_list]
    in_specs += [pl.BlockSpec((1, tm, d), lambda bi, i: (bi, i, 0)),
                 pl.BlockSpec((1, d), lambda bi, i: (0, 0)),
                 pl.BlockSpec((1, 1, d), lambda bi, i: (bi, 0, 0))]
    return pl.pallas_call(
        functools.partial(_outproj_kernel, n_lhs=n_lhs),
        grid=(b, n // tm),
        in_specs=in_specs,
        out_specs=pl.BlockSpec((1, tm, d), lambda bi, i: (bi, i, 0)),
        out_shape=jax.ShapeDtypeStruct((b, n, d), F32),
        compiler_params=_cparams("arbitrary", "arbitrary"),
        name="outproj",
    )(*lhs_list, *w_list, x, g.reshape(1, d), gate.reshape(b, 1, d))


def _ffn_kernel(xm_ref, xp_ref, xn_ref, gn_ref, sc_ref, sh_ref, wa_ref, wg_ref, cwa_ref, cwg_ref,
                cba_ref, cbg_ref, wd_ref, g3_ref, gate_ref, o_ref, hx_ref, *, tm, halo, n_tiles, n_f):
    i = pl.program_id(1)
    j = pl.program_id(2)

    @pl.when(j == 0)
    def _prologue():
        def nm(x):
            return (_rms(x, gn_ref[...]) * (1.0 + sc_ref[0]) + sh_ref[0]).astype(BF16)
        hx_ref[halo:halo + tm] = nm(xm_ref[0])
        zero = jnp.zeros((halo, xm_ref.shape[2]), BF16)
        hx_ref[0:halo] = jnp.where(i > 0, nm(xp_ref[0]), zero)
        hx_ref[halo + tm:] = jnp.where(i < n_tiles - 1, nm(xn_ref[0]), zero)

    hx = hx_ref[...]
    rows = tm + 2 * halo

    def conv(u, cw, cb):
        up = pltpu.roll(u, 1, 0)
        dn = pltpu.roll(u, rows - 1, 0)
        return (cw[0:1] * up + cw[1:2] * u + cw[2:3] * dn + cb)[halo:halo + tm]

    a = conv(jnp.dot(hx, wa_ref[...], preferred_element_type=F32), cwa_ref[...], cba_ref[...])
    g = conv(jnp.dot(hx, wg_ref[...], preferred_element_type=F32), cwg_ref[...], cbg_ref[...])
    act = (a * (g * jax.nn.sigmoid(g))).astype(BF16)
    contrib = jnp.dot(act, wd_ref[...], preferred_element_type=F32)

    @pl.when(j == 0)
    def _first():
        o_ref[0] = contrib

    @pl.when(j > 0)
    def _rest():
        o_ref[0] += contrib

    @pl.when(j == n_f - 1)
    def _epilogue():
        o_ref[0] = xm_ref[0] + gate_ref[0] * _rms(o_ref[0], g3_ref[...])


def conv_ffn_residual(x, gn, sc, sh, w_up, conv_w, conv_b, w_down, g3, gate):
    b, n, d = x.shape
    dff = w_down.shape[0]
    tm = min(n, 512)
    tf = 512
    halo = BF16_SUBLANES
    nt = n // tm
    n_f = dff // tf
    row = lambda bi, i, j: (bi, i, 0)
    vec = lambda bi, i, j: (0, 0)
    bvec = lambda bi, i, j: (bi, 0, 0)
    conv_b = conv_b.reshape(1, 2 * dff)
    return pl.pallas_call(
        functools.partial(_ffn_kernel, tm=tm, halo=halo, n_tiles=nt, n_f=n_f),
        grid=(b, nt, n_f),
        in_specs=[pl.BlockSpec((1, tm, d), row),
                  pl.BlockSpec((1, halo, d),
                               lambda bi, i, j: (bi, jnp.maximum(i * (tm // halo) - 1, 0), 0)),
                  pl.BlockSpec((1, halo, d),
                               lambda bi, i, j: (bi, jnp.minimum((i + 1) * (tm // halo), n // halo - 1), 0)),
                  pl.BlockSpec((1, d), vec),
                  pl.BlockSpec((1, 1, d), bvec),
                  pl.BlockSpec((1, 1, d), bvec),
                  pl.BlockSpec((d, tf), lambda bi, i, j: (0, j)),
                  pl.BlockSpec((d, tf), lambda bi, i, j: (0, n_f + j)),
                  pl.BlockSpec((3, tf), lambda bi, i, j: (0, j)),
                  pl.BlockSpec((3, tf), lambda bi, i, j: (0, n_f + j)),
                  pl.BlockSpec((1, tf), lambda bi, i, j: (0, j)),
                  pl.BlockSpec((1, tf), lambda bi, i, j: (0, n_f + j)),
                  pl.BlockSpec((tf, d), lambda bi, i, j: (j, 0)),
                  pl.BlockSpec((1, d), vec),
                  pl.BlockSpec((1, 1, d), bvec)],
        out_specs=pl.BlockSpec((1, tm, d), row),
        out_shape=jax.ShapeDtypeStruct((b, n, d), F32),
        scratch_shapes=[pltpu.VMEM((tm + 2 * halo, d), BF16)],
        compiler_params=_cparams("arbitrary", "arbitrary", "arbitrary"),
        name="conv_ffn",
    )(x, x, x, gn.reshape(1, d), sc.reshape(b, 1, d), sh.reshape(b, 1, d), w_up, w_up,
      conv_w, conv_w, conv_b, conv_b, w_down, g3.reshape(1, d), gate.reshape(b, 1, d))


def _filter_kernel(bands_ref, w1_ref, b1_ref, w2_ref, b2_ref, w3_ref, fr_ref, dl_ref,
                   o_ref, s_ref, *, n_tok, tr):
    i = pl.program_id(1)
    r = (i * tr + lax.broadcasted_iota(jnp.int32, (tr, 1), 0)).astype(F32)
    t = jnp.where(r < n_tok, r, 2.0 * n_tok - r)
    t_norm = t / float(max(n_tok - 1, 1))
    wang = (2.0 * math.pi / n_tok) * t
    z = wang * bands_ref[...]
    lane = lax.broadcasted_iota(jnp.int32, z.shape, 1)
    feats = jnp.where(lane == 0, t_norm,
                      jnp.where(lane <= HY_BANDS, jnp.cos(z),
                                jnp.where(lane <= 2 * HY_BANDS, -jnp.sin(z), 0.0)))
    fr = fr_ref[...]
    h = jnp.sin(fr[0:1] * (jnp.dot(feats, w1_ref[...], precision=HIGHEST,
                                   preferred_element_type=F32) + b1_ref[...]))
    h = jnp.sin(fr[1:2] * (jnp.dot(h, w2_ref[...], precision=HIGHEST,
                                   preferred_element_type=F32) + b2_ref[...]))
    h = jnp.dot(h, w3_ref[...], precision=HIGHEST, preferred_element_type=F32)
    decay = jnp.exp(-t_norm * dl_ref[...])
    out = jnp.where(r == n_tok, 0.0, h * decay)
    o_ref[0] = out
    part = jnp.sum(jnp.abs(out), axis=0, keepdims=True)

    @pl.when(i == 0)
    def _first():
        s_ref[0] = part

    @pl.when(i > 0)
    def _rest():
        s_ref[0] += part


def hyena_filter_time(n_tok, w1, b1, w2, b2, w3, freq):
    ch = HY_CH
    ll = 2 * n_tok
    tr = min(n_tok, 512)
    per_half = n_tok // tr
    lane = jnp.arange(LANES)
    bands = jnp.linspace(1e-4, HY_BANDS - 1, HY_BANDS, dtype=F32)
    bands_l = jnp.where((lane >= 1) & (lane <= HY_BANDS), bands[jnp.clip(lane - 1, 0, HY_BANDS - 1)],
                        jnp.where((lane > HY_BANDS) & (lane <= 2 * HY_BANDS),
                                  bands[jnp.clip(lane - 1 - HY_BANDS, 0, HY_BANDS - 1)], 0.0))
    w1p = jnp.zeros((LANES, HY_FILTER_W), F32).at[:w1.shape[0]].set(w1)
    deltas = jnp.abs(jnp.linspace(math.log(HY_TARGET) / HY_FAST, math.log(HY_TARGET) / HY_SLOW,
                                  ch, dtype=F32))
    const = lambda o, i: (0, 0)
    return pl.pallas_call(
        functools.partial(_filter_kernel, n_tok=n_tok, tr=tr),
        grid=(HY_ORDER, ll // tr),
        in_specs=[pl.BlockSpec((1, LANES), const),
                  pl.BlockSpec((LANES, HY_FILTER_W), const),
                  pl.BlockSpec((1, HY_FILTER_W), const),
                  pl.BlockSpec((HY_FILTER_W, HY_FILTER_W), const),
                  pl.BlockSpec((1, HY_FILTER_W), const),
                  pl.BlockSpec((HY_FILTER_W, ch), lambda o, i: (0, o * 2 + i // per_half)),
                  pl.BlockSpec((2, HY_FILTER_W), const),
                  pl.BlockSpec((1, ch), const)],
        out_specs=[pl.BlockSpec((1, tr, ch), lambda o, i: (o, i, 0)),
                   pl.BlockSpec((1, 1, ch), lambda o, i: (o, 0, 0))],
        out_shape=[jax.ShapeDtypeStruct((HY_ORDER, ll, ch), F32),
                   jax.ShapeDtypeStruct((HY_ORDER, 1, ch), F32)],
        compiler_params=_cparams("arbitrary", "arbitrary"),
        name="hyena_filter",
    )(bands_l.reshape(1, LANES), w1p, b1.reshape(1, -1), w2, b2.reshape(1, -1), w3, freq,
      deltas.reshape(1, ch))


def _dft_dot(f, x):
    return jnp.dot(f.astype(BF16), x.astype(BF16), preferred_element_type=F32)


def _stage_a_kernel(f_ref, x_ref, o_ref):
    parts = x_ref.shape[0]
    x = x_ref[0] if parts == 1 else jnp.concatenate([x_ref[p] for p in range(parts)], axis=0)
    y = _dft_dot(f_ref[...], x)
    half = y.shape[0] // 2
    o_ref[0] = y[:half]
    o_ref[1] = y[half:]


def dft_stage_a(f, x):
    p, rows, w = x.shape
    k1 = p * rows
    tl = min(w, 4096)
    return pl.pallas_call(
        _stage_a_kernel,
        grid=(w // tl,),
        in_specs=[pl.BlockSpec(f.shape, lambda j: (0, 0)),
                  pl.BlockSpec((p, rows, tl), lambda j: (0, 0, j))],
        out_specs=pl.BlockSpec((2, k1, tl), lambda j: (0, 0, j)),
        out_shape=jax.ShapeDtypeStruct((2, k1, w), F32),
        compiler_params=_cparams("arbitrary"),
        name="dft_stage_a",
    )(f, x)


def _stage_c_filter_kernel(a_ref, g_ref, sc_ref, o_ref, *, kc):
    for k in range(kc):
        r = jnp.concatenate([a_ref[0, k], a_ref[1, k]], axis=0)
        y = _dft_dot(g_ref[k], r) * sc_ref[...]
        half = y.shape[0] // 2
        o_ref[0, k] = y[:half]
        o_ref[1, k] = y[half:]


def dft_stage_c_filter(a, g, scale):
    _, k1, r, c = a.shape
    kc = 8
    tc = 256
    blk = pl.BlockSpec((2, kc, r, tc), lambda i, j: (0, i, 0, j))
    return pl.pallas_call(
        functools.partial(_stage_c_filter_kernel, kc=kc),
        grid=(k1 // kc, c // tc),
        in_specs=[blk,
                  pl.BlockSpec((kc, 2 * r, 2 * r), lambda i, j: (i, 0, 0)),
                  pl.BlockSpec((1, tc), lambda i, j: (0, j))],
        out_specs=blk,
        out_shape=jax.ShapeDtypeStruct(a.shape, F32),
        compiler_params=_cparams("arbitrary", "arbitrary"),
        name="dft_stage_c_filter",
    )(a, g, scale)


def _stage_c_conv_kernel(a_ref, h_ref, g_ref, gi_ref, o_ref, *, kc):
    for k in range(kc):
        r = jnp.concatenate([a_ref[0, k], a_ref[1, k]], axis=0)
        y = _dft_dot(g_ref[k], r)
        half = y.shape[0] // 2
        yre, yim = y[:half], y[half:]
        hre, him = h_ref[0, k], h_ref[1, k]
        p = jnp.concatenate([yre * hre - yim * him, yre * him + yim * hre], axis=0)
        d = _dft_dot(gi_ref[k], p)
        o_ref[0, k] = d[:half]
        o_ref[1, k] = d[half:]


def dft_stage_c_conv(a, hspec, g, gi):
    _, k1, r, c = a.shape
    kc = 8
    tc = 256
    blk = pl.BlockSpec((2, kc, r, tc), lambda i, j: (0, i, 0, j))
    mat = pl.BlockSpec((kc, 2 * r, 2 * r), lambda i, j: (i, 0, 0))
    return pl.pallas_call(
        functools.partial(_stage_c_conv_kernel, kc=kc),
        grid=(k1 // kc, c // tc),
        in_specs=[blk, blk, mat, mat],
        out_specs=blk,
        out_shape=jax.ShapeDtypeStruct(a.shape, F32),
        compiler_params=_cparams("arbitrary", "arbitrary"),
        name="dft_stage_c_conv",
    )(a, hspec, g, gi)


def _stage_a_inv_kernel(f_ref, d_ref, u_ref, gt_ref, sk_ref, o_ref):
    d = jnp.concatenate([d_ref[0], d_ref[1]], axis=0)
    y = _dft_dot(f_ref[...], d)
    half = y.shape[0] // 2
    sk = sk_ref[...]
    o_ref[0] = (gt_ref[0] * (y[:half] + u_ref[0] * sk)).astype(o_ref.dtype)
    o_ref[1] = (gt_ref[1] * (y[half:] + u_ref[1] * sk)).astype(o_ref.dtype)


def dft_stage_a_inv(f, d, u, gate, skip_flat, out_dtype):
    _, k1, w = d.shape
    half = k1 // 2
    tl = skip_flat.shape[1]
    io = pl.BlockSpec((2, half, tl), lambda j: (0, 0, j))
    return pl.pallas_call(
        _stage_a_inv_kernel,
        grid=(w // tl,),
        in_specs=[pl.BlockSpec(f.shape, lambda j: (0, 0)),
                  pl.BlockSpec((2, k1, tl), lambda j: (0, 0, j)),
                  io, io,
                  pl.BlockSpec((1, tl), lambda j: (0, 0))],
        out_specs=io,
        out_shape=jax.ShapeDtypeStruct((2, half, w), out_dtype),
        compiler_params=_cparams("arbitrary"),
        name="dft_stage_a_inv",
    )(f, d, u, gate, skip_flat)


def _dft_tables(n_tok):
    ll = 2 * n_tok
    r = DFT_RADIX
    k1n = ll // r
    half = k1n // 2
    two_pi = 2.0 * math.pi

    def cs(m, period):
        ang = (m % period).astype(F32) * (two_pi / period)
        return jnp.cos(ang), jnp.sin(ang)

    k1 = jnp.arange(k1n, dtype=jnp.int32)
    c, s = cs(k1[:, None] * k1[None, :half], k1n)
    fa_data = jnp.concatenate([jnp.concatenate([c, s], 1), jnp.concatenate([-s, c], 1)], 0)
    c, s = cs(k1[:, None] * k1[None, :], k1n)
    fa_filt = jnp.concatenate([c, -s], 0)
    c, s = cs(k1[:half, None] * k1[None, :], k1n)
    fa_inv = jnp.concatenate([jnp.concatenate([c, -s], 1), jnp.concatenate([s, c], 1)], 0)
    idx = jnp.arange(r, dtype=jnp.int32)
    m = idx[None, None, :] * (idx[None, :, None] * k1n + k1[:, None, None])
    c, s = cs(m, ll)
    g_fwd = jnp.concatenate([jnp.concatenate([c, s], 2), jnp.concatenate([-s, c], 2)], 1)
    ct, st = jnp.swapaxes(c, 1, 2), jnp.swapaxes(s, 1, 2)
    g_inv = jnp.concatenate([jnp.concatenate([ct, -st], 2), jnp.concatenate([st, ct], 2)], 1)
    return fa_data, fa_filt, fa_inv, g_fwd, g_inv


def hyena_long(pc, filt, l1, skip, out_dtype):
    _, b, n, c = pc.shape
    assert b == 2, "the two batch entries ride as real/imaginary parts"
    ll = 2 * n
    r = DFT_RADIX
    k1n = ll // r
    half = k1n // 2
    w = r * c
    fa_data, fa_filt, fa_inv, g_fwd, g_inv = _dft_tables(n)
    tl = 4096
    u = pc[0].reshape(2, half, w)
    for o in range(HY_ORDER):
        af = dft_stage_a(fa_filt, filt[o].reshape(1, k1n, w))
        hspec = dft_stage_c_filter(af.reshape(2, k1n, r, c), g_fwd, 1.0 / (l1[o] * ll))
        a = dft_stage_a(fa_data, u)
        d = dft_stage_c_conv(a.reshape(2, k1n, r, c), hspec, g_fwd, g_inv)
        last = o == HY_ORDER - 1
        u = dft_stage_a_inv(fa_inv, d.reshape(2, k1n, w), u, pc[o + 1].reshape(2, half, w),
                            jnp.tile(skip[o], tl // c).reshape(1, tl),
                            out_dtype if last else F32)
    return u.reshape(b, n, c)


def _hyena_short_kernel(ff_ref, fr_ref, fi_ref, pc_ref, filt_ref, l1_ref, skip_ref, o_ref, *, n, ll):
    u = [pc_ref[0, 0], pc_ref[0, 1]]
    for o in range(HY_ORDER):
        hs = _dft_dot(fr_ref[...], filt_ref[o]) * (1.0 / (l1_ref[o] * ll))
        y = _dft_dot(ff_ref[...], jnp.concatenate(u, axis=0))
        yre, yim, hre, him = y[:ll], y[ll:], hs[:ll], hs[ll:]
        p = jnp.concatenate([yre * hre - yim * him, yre * him + yim * hre], axis=0)
        d = _dft_dot(fi_ref[...], p)
        sk = skip_ref[o:o + 1]
        u = [pc_ref[o + 1, bi] * (d[bi * n:(bi + 1) * n] + u[bi] * sk) for bi in range(2)]
    o_ref[0] = u[0].astype(o_ref.dtype)
    o_ref[1] = u[1].astype(o_ref.dtype)


def hyena_short(pc, filt, l1, skip, out_dtype):
    _, b, n, c = pc.shape
    assert b == 2
    ll = 2 * n
    two_pi = 2.0 * math.pi
    kk = jnp.arange(ll, dtype=jnp.int32)
    ang = ((kk[:, None] * kk[None, :]) % ll).astype(F32) * (two_pi / ll)
    cf, sf = jnp.cos(ang), jnp.sin(ang)
    f_real = jnp.concatenate([cf, -sf], 0)
    cn, sn = cf[:, :n], sf[:, :n]
    f_fwd = jnp.concatenate([jnp.concatenate([cn, sn], 1), jnp.concatenate([-sn, cn], 1)], 0)
    ci, si = cf[:n, :], sf[:n, :]
    f_inv = jnp.concatenate([jnp.concatenate([ci, -si], 1), jnp.concatenate([si, ci], 1)], 0)
    tc = 256
    const = lambda j: (0, 0)
    return pl.pallas_call(
        functools.partial(_hyena_short_kernel, n=n, ll=ll),
        grid=(c // tc,),
        in_specs=[pl.BlockSpec(f_fwd.shape, const),
                  pl.BlockSpec(f_real.shape, const),
                  pl.BlockSpec(f_inv.shape, const),
                  pl.BlockSpec((3, 2, n, tc), lambda j: (0, 0, 0, j)),
                  pl.BlockSpec((HY_ORDER, ll, tc), lambda j: (0, 0, j)),
                  pl.BlockSpec((HY_ORDER, 1, tc), lambda j: (0, 0, j)),
                  pl.BlockSpec((HY_ORDER, tc), lambda j: (0, j))],
        out_specs=pl.BlockSpec((2, n, tc), lambda j: (0, 0, j)),
        out_shape=jax.ShapeDtypeStruct((2, n, c), out_dtype),
        compiler_params=_cparams("arbitrary"),
        name="hyena_short",
    )(f_fwd, f_real, f_inv, pc, filt, l1, skip)


def hyena_mixer(pc, w1, b1, w2, b2, w3, freq, skip, out_dtype):
    n = pc.shape[2]
    filt, l1 = hyena_filter_time(n, w1, b1, w2, b2, w3, freq)
    if (2 * n) % (DFT_RADIX * BF16_SUBLANES) == 0 and n >= 1024:
        return hyena_long(pc, filt, l1, skip, out_dtype)
    return hyena_short(pc, filt, l1, skip, out_dtype)


def _deinterleave_cols(w, hd):
    lead = w.shape[:-1]
    wh = w.reshape(lead + (w.shape[-1] // hd, hd // 2, 2))
    return jnp.concatenate([wh[..., 0], wh[..., 1]], axis=-1).reshape(w.shape)


def _rope_tables(n_tok, rot_dim):
    rows = n_tok // GRID_W
    row = jnp.broadcast_to(jnp.arange(rows, dtype=jnp.int32)[:, None], (rows, GRID_W)).reshape(n_tok)
    col = jnp.broadcast_to(jnp.arange(GRID_W, dtype=jnp.int32)[None, :], (rows, GRID_W)).reshape(n_tok)
    axis_dim = rot_dim // 2
    inv_freq = ROPE_THETA ** (-jnp.arange(0, axis_dim, 2, dtype=F32) / axis_dim)
    ang = jnp.concatenate([row.astype(F32)[:, None] * inv_freq,
                           col.astype(F32)[:, None] * inv_freq], axis=-1)
    c, s = jnp.cos(ang), jnp.sin(ang)
    reps = LANES // rot_dim
    return jnp.tile(jnp.concatenate([c, c], -1), (1, reps)), jnp.tile(jnp.concatenate([-s, s], -1), (1, reps))


def _mixer_ab(h, hc, with_ctx, w_in, w_out, qk_g, hy):
    b, n, _ = h.shape
    conv_w, conv_b, w1, b1, w2, b2, w3, freq, skip = hy
    scale = A_HEAD_DIM ** -0.5 * LOG2E
    wq = _deinterleave_cols(w_in[:, :A_Q_W], A_HEAD_DIM).astype(BF16)
    wk = _deinterleave_cols(w_in[:, A_Q_W:A_Q_W + A_KV_W], A_HEAD_DIM).astype(BF16)
    wv = w_in[:, A_Q_W + A_KV_W:A_QKV_W].astype(BF16)
    wh = w_in[:, A_QKV_W:].astype(BF16)
    gq = jnp.tile(_deinterleave_cols(qk_g[0], A_HEAD_DIM) * scale, A_HEADS)
    gk = jnp.tile(_deinterleave_cols(qk_g[1], A_HEAD_DIM), A_KV_HEADS)
    ones_v = jnp.ones((A_KV_W,), F32)
    cosf, sinf = _rope_tables(n, A_HEAD_DIM)

    q = inproj(h, wq, gq, cosf, sinf, norm=True)
    k = inproj(h, wk, gk, cosf, sinf, norm=True)
    v = inproj(h, wv, ones_v)
    k_c = inproj(hc, wk, gk, norm=True)
    v_c = inproj(hc, wv, ones_v)
    k_all = jnp.concatenate([k, k_c], axis=1)
    v_all = jnp.concatenate([v, v_c], axis=1)
    o_att = flash_attention(q, k_all, v_all, mode="gqa")
    pc = inproj_conv(h, wh, conv_w, conv_b, HY_ORDER + 1)
    o_hy = hyena_mixer(pc, w1, b1, w2, b2, w3, freq, skip, BF16)
    wo = w_out.astype(BF16)
    lat = ([o_att, o_hy], [wo[:A_Q_W], wo[A_Q_W:]])
    if not with_ctx:
        return lat, None
    q_c = inproj(hc, wq, gq, norm=True)
    o_att_c = flash_attention(q_c, k_c, v_c, mode="gqa")
    pc_c = inproj_conv(hc, wh, conv_w, conv_b, HY_ORDER + 1)
    o_hy_c = hyena_mixer(pc_c, w1, b1, w2, b2, w3, freq, skip, BF16)
    return lat, ([o_att_c, o_hy_c], [wo[:A_Q_W], wo[A_Q_W:]])


def _mixer_c(h, hc, with_ctx, lambda_init, w_in, w_out, lam_vecs, subln_g):
    b, n, _ = h.shape
    scale = C_HEAD_DIM ** -0.5 * LOG2E
    wq = _deinterleave_cols(w_in[:, :C_Q_W], C_HEAD_DIM).astype(BF16)
    wk = _deinterleave_cols(w_in[:, C_Q_W:2 * C_Q_W], C_HEAD_DIM).astype(BF16)
    wv = w_in[:, 2 * C_Q_W:].astype(BF16)
    gq = jnp.full((C_Q_W,), scale, F32)
    ones = jnp.ones((C_Q_W,), F32)
    cosf, sinf = _rope_tables(n, C_HEAD_DIM)

    q = inproj(h, wq, gq, cosf, sinf, hd=C_HEAD_DIM)
    k = inproj(h, wk, ones, cosf, sinf, hd=C_HEAD_DIM)
    v = inproj(h, wv, ones)
    k_c = inproj(hc, wk, ones)
    v_c = inproj(hc, wv, ones)
    k_all = jnp.concatenate([k, k_c], axis=1)
    v_all = jnp.concatenate([v, v_c], axis=1)
    attn = functools.partial(flash_attention, mode="diff", lam_vecs=lam_vecs, subln_g=subln_g,
                             lambda_init=lambda_init)
    wo = w_out.astype(BF16)
    lat = ([attn(q, k_all, v_all)], [wo])
    if not with_ctx:
        return lat, None
    q_c = inproj(hc, wq, gq)
    return lat, ([attn(q_c, k_c, v_c)], [wo])


def kernel(x, c, ctx, c_ctx, ada_w, ada_b, norm_g, ab_w_in, ab_w_out, ab_qk_g,
           hy_conv_w, hy_conv_b, hy_w1, hy_b1, hy_w2, hy_b2, hy_w3, hy_freq, hy_skip,
           dc_w_in, dc_w_out, dc_lambda, dc_subln_g,
           ffn_w_up, ffn_conv_w, ffn_conv_b, ffn_w_down):
    depth = ada_w.shape[0]
    b, n, d = x.shape
    pad_rows = (-(b + 1)) % 8
    cvec = jnp.concatenate([c, c_ctx[None, :], jnp.zeros((pad_rows, d), F32)], axis=0)
    mod_all = ada_mod(cvec, ada_w, ada_b)
    for i in range(depth):
        j = i // 2
        with_ctx = i < depth - 1
        mod = mod_all[i, :b].reshape(b, 6, d)
        mod_c = jnp.broadcast_to(mod_all[i, b].reshape(1, 6, d), (b, 6, d))
        sh1, sc1, g1, sh2, sc2, g2 = (mod[:, t] for t in range(6))
        sh1c, sc1c, g1c, sh2c, sc2c, g2c = (mod_c[:, t] for t in range(6))
        h = norm_mod(x, norm_g[i, 0], sc1, sh1)
        hc = norm_mod(ctx, norm_g[i, 0], sc1c, sh1c)
        if i % 2 == 0:
            hy = (hy_conv_w[j], hy_conv_b[j], hy_w1[j], hy_b1[j], hy_w2[j], hy_b2[j],
                  hy_w3[j], hy_freq[j], hy_skip[j])
            lat, cx = _mixer_ab(h, hc, with_ctx, ab_w_in[j], ab_w_out[j], ab_qk_g[j], hy)
        else:
            lambda_init = 0.8 - 0.6 * math.exp(-0.3 * i)
            lat, cx = _mixer_c(h, hc, with_ctx, lambda_init, dc_w_in[j], dc_w_out[j],
                               dc_lambda[j], dc_subln_g[j])
        w_up = ffn_w_up[i].astype(BF16)
        w_down = ffn_w_down[i].astype(BF16)
        x = outproj_residual(lat[0], lat[1], x, norm_g[i, 1], g1)
        x = conv_ffn_residual(x, norm_g[i, 2], sc2, sh2, w_up, ffn_conv_w[i], ffn_conv_b[i],
                              w_down, norm_g[i, 3], g2)
        if with_ctx:
            ctx = outproj_residual(cx[0], cx[1], ctx, norm_g[i, 1], g1c)
            ctx = conv_ffn_residual(ctx, norm_g[i, 2], sc2c, sh2c, w_up, ffn_conv_w[i],
                                    ffn_conv_b[i], w_down, norm_g[i, 3], g2c)
    return x
```

```python
import functools
import math

import jax
import jax.numpy as jnp
from jax import lax
from jax.experimental import pallas as pl
from jax.experimental.pallas import tpu as pltpu

F32 = jnp.float32
BF16 = jnp.bfloat16
HIGHEST = lax.Precision.HIGHEST

D_MODEL = 2048
GRID_W = 64
EPS = 1e-6
ROPE_THETA = 10000.0
A_HEADS, A_KV_HEADS, A_HEAD_DIM = 8, 2, 128
A_Q_W = A_HEADS * A_HEAD_DIM
A_KV_W = A_KV_HEADS * A_HEAD_DIM
A_QKV_W = A_Q_W + 2 * A_KV_W
HY_CH = D_MODEL // 2
HY_ORDER = 2
HY_FILTER_W = 64
HY_BANDS = 16
HY_FAST, HY_SLOW, HY_TARGET = 0.3, 1.5, 1e-2
C_HEADS, C_HEAD_DIM = 16, 64
C_Q_W = C_HEADS * 2 * C_HEAD_DIM
D_FF = 5632

LANES = 128
BF16_SUBLANES = 16
VMEM_LIMIT = 56 * 1024 * 1024
DFT_RADIX = 128
FLASH_ROWS = 2048
FLASH_SUB_ROWS = 128
LOG2E = math.log2(math.e)


def _cparams(*sem):
    return pltpu.CompilerParams(dimension_semantics=sem, vmem_limit_bytes=VMEM_LIMIT)


def _rms(x, g):
    return x * lax.rsqrt(jnp.mean(x * x, axis=-1, keepdims=True) + EPS) * g


def _ada_kernel(c_ref, w_ref, b_ref, o_ref):
    c = c_ref[...]
    s = c * jax.nn.sigmoid(c)
    o_ref[0] = jnp.dot(s, w_ref[0], precision=HIGHEST, preferred_element_type=F32) + b_ref[0]


def ada_mod(cvec, ada_w, ada_b):
    depth, d, n6 = ada_w.shape
    rows = cvec.shape[0]
    tn = 1024
    return pl.pallas_call(
        _ada_kernel,
        grid=(depth, n6 // tn),
        in_specs=[pl.BlockSpec((rows, d), lambda l, j: (0, 0)),
                  pl.BlockSpec((1, d, tn), lambda l, j: (l, 0, j)),
                  pl.BlockSpec((1, 1, tn), lambda l, j: (l, 0, j))],
        out_specs=pl.BlockSpec((1, rows, tn), lambda l, j: (l, 0, j)),
        out_shape=jax.ShapeDtypeStruct((depth, rows, n6), F32),
        compiler_params=_cparams("arbitrary", "arbitrary"),
        name="ada_mod",
    )(cvec, ada_w, ada_b.reshape(depth, 1, n6))


def _norm_mod_kernel(x_ref, g_ref, sc_ref, sh_ref, o_ref):
    y = _rms(x_ref[0], g_ref[...])
    o_ref[0] = (y * (1.0 + sc_ref[0]) + sh_ref[0]).astype(o_ref.dtype)


def norm_mod(x, g, sc, sh):
    b, n, d = x.shape
    tm = min(n, 512)
    return pl.pallas_call(
        _norm_mod_kernel,
        grid=(b, n // tm),
        in_specs=[pl.BlockSpec((1, tm, d), lambda bi, i: (bi, i, 0)),
                  pl.BlockSpec((1, d), lambda bi, i: (0, 0)),
                  pl.BlockSpec((1, 1, d), lambda bi, i: (bi, 0, 0)),
                  pl.BlockSpec((1, 1, d), lambda bi, i: (bi, 0, 0))],
        out_specs=pl.BlockSpec((1, tm, d), lambda bi, i: (bi, i, 0)),
        out_shape=jax.ShapeDtypeStruct((b, n, d), BF16),
        compiler_params=_cparams("arbitrary", "arbitrary"),
        name="norm_mod",
    )(x, g.reshape(1, d), sc.reshape(b, 1, d), sh.reshape(b, 1, d))


def _inproj_kernel(*refs, hd, norm, rope):
    if rope:
        h_ref, w_ref, g_ref, cos_ref, sin_ref, o_ref = refs
    else:
        h_ref, w_ref, g_ref, o_ref = refs
    acc = jnp.dot(h_ref[0], w_ref[...], preferred_element_type=F32)
    tn = acc.shape[1]
    if rope:
        cosf = cos_ref[...]
        sinf = sin_ref[...]
        if hd != LANES:
            lane = lax.broadcasted_iota(jnp.int32, cosf.shape, 1)
            first_half = (lane % hd) < (hd // 2)
    for c in range(tn // LANES):
        x = acc[:, c * LANES:(c + 1) * LANES]
        if norm:
            x = x * lax.rsqrt(jnp.mean(x * x, axis=-1, keepdims=True) + EPS)
        x = x * g_ref[:, c * LANES:(c + 1) * LANES]
        if rope:
            if hd == LANES:
                partner = pltpu.roll(x, LANES // 2, 1)
            else:
                partner = jnp.where(first_half, pltpu.roll(x, LANES - hd // 2, 1),
                                    pltpu.roll(x, hd // 2, 1))
            x = x * cosf + partner * sinf
        o_ref[0, :, c * LANES:(c + 1) * LANES] = x.astype(o_ref.dtype)


def inproj(h, w, gvec, cosf=None, sinf=None, *, hd=LANES, norm=False, out_dtype=BF16):
    b, n, d = h.shape
    ncol = w.shape[1]
    rope = cosf is not None
    tm = min(n, 512)
    tn = min(ncol, 1024)
    in_specs = [pl.BlockSpec((1, tm, d), lambda bi, i, j: (bi, i, 0)),
                pl.BlockSpec((d, tn), lambda bi, i, j: (0, j)),
                pl.BlockSpec((1, tn), lambda bi, i, j: (0, j))]
    args = [h, w, gvec.reshape(1, ncol)]
    if rope:
        in_specs += [pl.BlockSpec((tm, LANES), lambda bi, i, j: (i, 0)),
                     pl.BlockSpec((tm, LANES), lambda bi, i, j: (i, 0))]
        args += [cosf, sinf]
    return pl.pallas_call(
        functools.partial(_inproj_kernel, hd=hd, norm=norm, rope=rope),
        grid=(b, n // tm, ncol // tn),
        in_specs=in_specs,
        out_specs=pl.BlockSpec((1, tm, tn), lambda bi, i, j: (bi, i, j)),
        out_shape=jax.ShapeDtypeStruct((b, n, ncol), out_dtype),
        compiler_params=_cparams("arbitrary", "arbitrary", "arbitrary"),
        name="inproj",
    )(*args)


def _dwconv_rows(u, u_prev_row, u_next_row, cw, cb):
    rows = u.shape[0]
    r = lax.broadcasted_iota(jnp.int32, u.shape, 0)
    up = jnp.where(r == 0, u_prev_row, pltpu.roll(u, 1, 0))
    dn = jnp.where(r == rows - 1, u_next_row, pltpu.roll(u, rows - 1, 0))
    return cw[0:1] * up + cw[1:2] * u + cw[2:3] * dn + cb


def _inproj_conv_kernel(hm_ref, hp_ref, hn_ref, w_ref, cw_ref, cb_ref, o_ref, *, n_tiles):
    i = pl.program_id(1)
    w = w_ref[...]
    u = jnp.dot(hm_ref[0], w, preferred_element_type=F32)
    up = jnp.dot(hp_ref[0], w, preferred_element_type=F32)
    un = jnp.dot(hn_ref[0], w, preferred_element_type=F32)
    halo = up.shape[0]
    prev_row = jnp.where(i > 0, up[halo - 1:halo], 0.0)
    next_row = jnp.where(i < n_tiles - 1, un[0:1], 0.0)
    o_ref[0, 0] = _dwconv_rows(u, prev_row, next_row, cw_ref[...], cb_ref[...])


def inproj_conv(h, w, conv_w, conv_b, groups):
    b, n, d = h.shape
    ncol = w.shape[1]
    c = ncol // groups
    tm = min(n, 512)
    tn = 512
    halo = BF16_SUBLANES
    nt = n // tm
    per = c // tn
    return pl.pallas_call(
        functools.partial(_inproj_conv_kernel, n_tiles=nt),
        grid=(b, nt, ncol // tn),
        in_specs=[pl.BlockSpec((1, tm, d), lambda bi, i, j: (bi, i, 0)),
                  pl.BlockSpec((1, halo, d),
                               lambda bi, i, j: (bi, jnp.maximum(i * (tm // halo) - 1, 0), 0)),
                  pl.BlockSpec((1, halo, d),
                               lambda bi, i, j: (bi, jnp.minimum((i + 1) * (tm // halo), n // halo - 1), 0)),
                  pl.BlockSpec((d, tn), lambda bi, i, j: (0, j)),
                  pl.BlockSpec((3, tn), lambda bi, i, j: (0, j)),
                  pl.BlockSpec((1, tn), lambda bi, i, j: (0, j))],
        out_specs=pl.BlockSpec((1, 1, tm, tn), lambda bi, i, j: (j // per, bi, i, j % per)),
        out_shape=jax.ShapeDtypeStruct((groups, b, n, c), F32),
        compiler_params=_cparams("arbitrary", "arbitrary", "arbitrary"),
        name="inproj_conv",
    )(h, h, h, w, conv_w, conv_b.reshape(1, ncol))


def _flash_kernel(*refs, mode, group, tq, tk, n_kv, lambda_init):
    if mode == "diff":
        q_ref, k_ref, v_ref, lam_ref, sg_ref, o_ref, qs_ref, m_ref, acc_ref = refs
    else:
        q_ref, k_ref, v_ref, o_ref, qs_ref, m_ref, acc_ref = refs
    kv = pl.program_id(3)

    @pl.when(kv == 0)
    def _init():
        q = q_ref[0]
        if mode == "diff":
            lane = lax.broadcasted_iota(jnp.int32, q.shape, 1)
            zero = jnp.zeros_like(q)
            qs_ref[0:tq] = jnp.where(lane < C_HEAD_DIM, q, zero)
            qs_ref[tq:2 * tq] = jnp.where(lane >= C_HEAD_DIM, q, zero)
        else:
            for g in range(group):
                qs_ref[g * tq:(g + 1) * tq] = q[:, g * LANES:(g + 1) * LANES]
        m_ref[...] = jnp.full(m_ref.shape, -jnp.inf, F32)
        acc_ref[...] = jnp.zeros(acc_ref.shape, F32)

    k = k_ref[0]
    v = v_ref[0]
    v_aug = jnp.concatenate([v, jnp.ones_like(v)], axis=1)
    for c in range(group * tq // FLASH_SUB_ROWS):
        sl = slice(c * FLASH_SUB_ROWS, (c + 1) * FLASH_SUB_ROWS)
        s = lax.dot_general(qs_ref[sl], k, (((1,), (1,)), ((), ())),
                            preferred_element_type=F32)
        m_prev = m_ref[sl]
        m_next = jnp.maximum(m_prev, jnp.max(s, axis=-1, keepdims=True))
        alpha = jnp.exp2(m_prev - m_next)
        p = jnp.exp2(s - jnp.tile(m_next, (1, tk // LANES)))
        acc_ref[sl] = jnp.tile(alpha, (1, 2)) * acc_ref[sl] + jnp.dot(
            p.astype(BF16), v_aug, preferred_element_type=F32)
        m_ref[sl] = m_next

    @pl.when(kv == n_kv - 1)
    def _fin():
        acc = acc_ref[...]
        o = acc[:, :LANES] / acc[:, LANES:]
        if mode == "diff":
            lv = lam_ref[...]
            lam = (jnp.exp(jnp.sum(lv[0:1] * lv[1:2], axis=-1, keepdims=True))
                   - jnp.exp(jnp.sum(lv[2:3] * lv[3:4], axis=-1, keepdims=True)) + lambda_init)
            dlt = o[0:tq] - lam * o[tq:2 * tq]
            o_ref[0] = (_rms(dlt, sg_ref[...]) * (1.0 - lambda_init)).astype(o_ref.dtype)
        else:
            for g in range(group):
                o_ref[0, :, g * LANES:(g + 1) * LANES] = o[g * tq:(g + 1) * tq].astype(o_ref.dtype)


def _kv_tile(s):
    for t in (2816, 2048, 1024, 768, 512, 256, 128):
        if s % t == 0:
            return t
    return s


def flash_attention(q, k, v, *, mode, lam_vecs=None, subln_g=None, lambda_init=0.0):
    b, n, qw = q.shape
    s = k.shape[1]
    if mode == "diff":
        group, n_groups, qblk = 2, qw // LANES, LANES
    else:
        group = A_HEADS // A_KV_HEADS
        n_groups, qblk = A_KV_HEADS, group * LANES
    tq = min(n, FLASH_ROWS // group)
    tk = _kv_tile(s)
    n_kv = s // tk
    in_specs = [pl.BlockSpec((1, tq, qblk), lambda bi, g, i, j: (bi, i, g)),
                pl.BlockSpec((1, tk, LANES), lambda bi, g, i, j: (bi, j, g)),
                pl.BlockSpec((1, tk, LANES), lambda bi, g, i, j: (bi, j, g))]
    args = [q, k, v]
    if mode == "diff":
        in_specs += [pl.BlockSpec(lam_vecs.shape, lambda bi, g, i, j: (0, 0)),
                     pl.BlockSpec((1, LANES), lambda bi, g, i, j: (0, 0))]
        args += [lam_vecs, subln_g.reshape(1, LANES)]
    return pl.pallas_call(
        functools.partial(_flash_kernel, mode=mode, group=group, tq=tq, tk=tk, n_kv=n_kv,
                          lambda_init=lambda_init),
        grid=(b, n_groups, n // tq, n_kv),
        in_specs=in_specs,
        out_specs=pl.BlockSpec((1, tq, qblk), lambda bi, g, i, j: (bi, i, g)),
        out_shape=jax.ShapeDtypeStruct((b, n, qw), BF16),
        scratch_shapes=[pltpu.VMEM((group * tq, LANES), BF16),
                        pltpu.VMEM((group * tq, LANES), F32),
                        pltpu.VMEM((group * tq, 2 * LANES), F32)],
        compiler_params=_cparams("arbitrary", "arbitrary", "arbitrary", "arbitrary"),
        name="flash_" + mode,
    )(*args)


def _outproj_kernel(*refs, n_lhs):
    lhs = refs[:n_lhs]
    ws = refs[n_lhs:2 * n_lhs]
    x_ref, g_ref, gate_ref, o_ref = refs[2 * n_lhs:]
    y = jnp.dot(lhs[0][0], ws[0][...], preferred_element_type=F32)
    for a, w in zip(lhs[1:], ws[1:]):
        y = y + jnp.dot(a[0], w[...], preferred_element_type=F32)
    o_ref[0] = x_ref[0] + gate_ref[0] * _rms(y, g_ref[...])


def outproj_residual(lhs_list, w_list, x, g, gate):
    b, n, d = x.shape
    tm = min(n, 512)
    n_lhs = len(lhs_list)
    in_specs = [pl.BlockSpec((1, tm, a.shape[2]), lambda bi, i: (bi, i, 0)) for a in lhs_list]
    in_specs += [pl.BlockSpec(w.shape, lambda bi, i: (0, 0)) for w in w_list]
    in_specs += [pl.BlockSpec((1, tm, d), lambda bi, i: (bi, i, 0)),
                 pl.BlockSpec((1, d), lambda bi, i: (0, 0)),
                 pl.BlockSpec((1, 1, d), lambda bi, i: (bi, 0, 0))]
    return pl.pallas_call(
        functools.partial(_outproj_kernel, n_lhs=n_lhs),
        grid=(b, n // tm),
        in_specs=in_specs,
        out_specs=pl.BlockSpec((1, tm, d), lambda bi, i: (bi, i, 0)),
        out_shape=jax.ShapeDtypeStruct((b, n, d), F32),
        compiler_params=_cparams("arbitrary", "arbitrary"),
        name="outproj",
    )(*lhs_list, *w_list, x, g.reshape(1, d), gate.reshape(b, 1, d))


def _ffn_kernel(xm_ref, xp_ref, xn_ref, gn_ref, sc_ref, sh_ref, wa_ref, wg_ref, cwa_ref, cwg_ref,
                cba_ref, cbg_ref, wd_ref, g3_ref, gate_ref, o_ref, hx_ref, act_ref,
                *, tm, halo, n_tiles, n_f):
    i = pl.program_id(1)
    j = pl.program_id(2)

    @pl.when(j == 0)
    def _prologue():
        def nm(x):
            return (_rms(x, gn_ref[...]) * (1.0 + sc_ref[0]) + sh_ref[0]).astype(BF16)
        hx_ref[halo:halo + tm] = nm(xm_ref[0])
        zero = jnp.zeros((halo, xm_ref.shape[2]), BF16)
        hx_ref[0:halo] = jnp.where(i > 0, nm(xp_ref[0]), zero)
        hx_ref[halo + tm:] = jnp.where(i < n_tiles - 1, nm(xn_ref[0]), zero)
        act_ref[...] = jnp.zeros(act_ref.shape, BF16)
        o_ref[0] = jnp.zeros(o_ref.shape[1:], F32)

    o_ref[0] += jnp.dot(act_ref[(j + 1) % 2], wd_ref[...], preferred_element_type=F32)

    hx = hx_ref[...]
    rows = tm + 2 * halo

    def conv(u, cw, cb):
        up = pltpu.roll(u, 1, 0)
        dn = pltpu.roll(u, rows - 1, 0)
        return (cw[0:1] * up + cw[1:2] * u + cw[2:3] * dn + cb)[halo:halo + tm]

    a = conv(jnp.dot(hx, wa_ref[...], preferred_element_type=F32), cwa_ref[...], cba_ref[...])
    g = conv(jnp.dot(hx, wg_ref[...], preferred_element_type=F32), cwg_ref[...], cbg_ref[...])
    act_ref[j % 2] = (a * (g * jax.nn.sigmoid(g))).astype(BF16)

    @pl.when(j == n_f)
    def _epilogue():
        o_ref[0] = xm_ref[0] + gate_ref[0] * _rms(o_ref[0], g3_ref[...])


def conv_ffn_residual(x, gn, sc, sh, w_up, conv_w, conv_b, w_down, g3, gate):
    b, n, d = x.shape
    dff = w_down.shape[0]
    tm = min(n, 512)
    tf = 512
    halo = BF16_SUBLANES
    nt = n // tm
    n_f = dff // tf
    row = lambda bi, i, j: (bi, i, 0)
    vec = lambda bi, i, j: (0, 0)
    bvec = lambda bi, i, j: (bi, 0, 0)
    up_a = lambda bi, i, j: (0, jnp.minimum(j, n_f - 1))
    up_g = lambda bi, i, j: (0, n_f + jnp.minimum(j, n_f - 1))
    conv_b = conv_b.reshape(1, 2 * dff)
    return pl.pallas_call(
        functools.partial(_ffn_kernel, tm=tm, halo=halo, n_tiles=nt, n_f=n_f),
        grid=(b, nt, n_f + 1),
        in_specs=[pl.BlockSpec((1, tm, d), row),
                  pl.BlockSpec((1, halo, d),
                               lambda bi, i, j: (bi, jnp.maximum(i * (tm // halo) - 1, 0), 0)),
                  pl.BlockSpec((1, halo, d),
                               lambda bi, i, j: (bi, jnp.minimum((i + 1) * (tm // halo), n // halo - 1), 0)),
                  pl.BlockSpec((1, d), vec),
                  pl.BlockSpec((1, 1, d), bvec),
                  pl.BlockSpec((1, 1, d), bvec),
                  pl.BlockSpec((d, tf), up_a),
                  pl.BlockSpec((d, tf), up_g),
                  pl.BlockSpec((3, tf), up_a),
                  pl.BlockSpec((3, tf), up_g),
                  pl.BlockSpec((1, tf), up_a),
                  pl.BlockSpec((1, tf), up_g),
                  pl.BlockSpec((tf, d), lambda bi, i, j: (jnp.maximum(j - 1, 0), 0)),
                  pl.BlockSpec((1, d), vec),
                  pl.BlockSpec((1, 1, d), bvec)],
        out_specs=pl.BlockSpec((1, tm, d), row),
        out_shape=jax.ShapeDtypeStruct((b, n, d), F32),
        scratch_shapes=[pltpu.VMEM((tm + 2 * halo, d), BF16),
                        pltpu.VMEM((2, tm, tf), BF16)],
        compiler_params=_cparams("arbitrary", "arbitrary", "arbitrary"),
        name="conv_ffn",
    )(x, x, x, gn.reshape(1, d), sc.reshape(b, 1, d), sh.reshape(b, 1, d), w_up, w_up,
      conv_w, conv_w, conv_b, conv_b, w_down, g3.reshape(1, d), gate.reshape(b, 1, d))


def _filter_kernel(bands_ref, w1_ref, b1_ref, w2_ref, b2_ref, w3_ref, fr_ref, dl_ref,
                   o_ref, s_ref, *, n_tok, tr):
    i = pl.program_id(1)
    r = (i * tr + lax.broadcasted_iota(jnp.int32, (tr, 1), 0)).astype(F32)
    t = jnp.where(r < n_tok, r, 2.0 * n_tok - r)
    t_norm = t / float(max(n_tok - 1, 1))
    wang = (2.0 * math.pi / n_tok) * t
    z = wang * bands_ref[...]
    lane = lax.broadcasted_iota(jnp.int32, z.shape, 1)
    feats = jnp.where(lane == 0, t_norm,
                      jnp.where(lane <= HY_BANDS, jnp.cos(z),
                                jnp.where(lane <= 2 * HY_BANDS, -jnp.sin(z), 0.0)))
    fr = fr_ref[...]
    h = jnp.sin(fr[0:1] * (jnp.dot(feats, w1_ref[...], precision=HIGHEST,
                                   preferred_element_type=F32) + b1_ref[...]))
    h = jnp.sin(fr[1:2] * (jnp.dot(h, w2_ref[...], precision=HIGHEST,
                                   preferred_element_type=F32) + b2_ref[...]))
    h = jnp.dot(h.astype(BF16), w3_ref[...].astype(BF16), preferred_element_type=F32)
    decay = jnp.exp(-t_norm * dl_ref[...])
    out = jnp.where(r == n_tok, 0.0, h * decay)
    o_ref[0] = out
    part = jnp.sum(jnp.abs(out), axis=0, keepdims=True)

    @pl.when(i == 0)
    def _first():
        s_ref[0] = part

    @pl.when(i > 0)
    def _rest():
        s_ref[0] += part


def hyena_filter_time(n_tok, w1, b1, w2, b2, w3, freq):
    ch = HY_CH
    ll = 2 * n_tok
    tr = min(n_tok, 512)
    per_half = n_tok // tr
    lane = jnp.arange(LANES)
    bands = jnp.linspace(1e-4, HY_BANDS - 1, HY_BANDS, dtype=F32)
    bands_l = jnp.where((lane >= 1) & (lane <= HY_BANDS), bands[jnp.clip(lane - 1, 0, HY_BANDS - 1)],
                        jnp.where((lane > HY_BANDS) & (lane <= 2 * HY_BANDS),
                                  bands[jnp.clip(lane - 1 - HY_BANDS, 0, HY_BANDS - 1)], 0.0))
    w1p = jnp.zeros((LANES, HY_FILTER_W), F32).at[:w1.shape[0]].set(w1)
    deltas = jnp.abs(jnp.linspace(math.log(HY_TARGET) / HY_FAST, math.log(HY_TARGET) / HY_SLOW,
                                  ch, dtype=F32))
    const = lambda o, i: (0, 0)
    return pl.pallas_call(
        functools.partial(_filter_kernel, n_tok=n_tok, tr=tr),
        grid=(HY_ORDER, ll // tr),
        in_specs=[pl.BlockSpec((1, LANES), const),
                  pl.BlockSpec((LANES, HY_FILTER_W), const),
                  pl.BlockSpec((1, HY_FILTER_W), const),
                  pl.BlockSpec((HY_FILTER_W, HY_FILTER_W), const),
                  pl.BlockSpec((1, HY_FILTER_W), const),
                  pl.BlockSpec((HY_FILTER_W, ch), lambda o, i: (0, o * 2 + i // per_half)),
                  pl.BlockSpec((2, HY_FILTER_W), const),
                  pl.BlockSpec((1, ch), const)],
        out_specs=[pl.BlockSpec((1, tr, ch), lambda o, i: (o, i, 0)),
                   pl.BlockSpec((1, 1, ch), lambda o, i: (o, 0, 0))],
        out_shape=[jax.ShapeDtypeStruct((HY_ORDER, ll, ch), F32),
                   jax.ShapeDtypeStruct((HY_ORDER, 1, ch), F32)],
        compiler_params=_cparams("arbitrary", "arbitrary"),
        name="hyena_filter",
    )(bands_l.reshape(1, LANES), w1p, b1.reshape(1, -1), w2, b2.reshape(1, -1), w3, freq,
      deltas.reshape(1, ch))


def _dft_dot(f, x):
    return jnp.dot(f.astype(BF16), x.astype(BF16), preferred_element_type=F32)


def _stage_a_kernel(f_ref, x_ref, o_ref):
    parts = x_ref.shape[0]
    x = x_ref[0] if parts == 1 else jnp.concatenate([x_ref[p] for p in range(parts)], axis=0)
    y = _dft_dot(f_ref[...], x)
    half = y.shape[0] // 2
    o_ref[0] = y[:half]
    o_ref[1] = y[half:]


def dft_stage_a(f, x):
    p, rows, w = x.shape
    k1 = p * rows
    tl = min(w, 4096)
    return pl.pallas_call(
        _stage_a_kernel,
        grid=(w // tl,),
        in_specs=[pl.BlockSpec(f.shape, lambda j: (0, 0)),
                  pl.BlockSpec((p, rows, tl), lambda j: (0, 0, j))],
        out_specs=pl.BlockSpec((2, k1, tl), lambda j: (0, 0, j)),
        out_shape=jax.ShapeDtypeStruct((2, k1, w), F32),
        compiler_params=_cparams("arbitrary"),
        name="dft_stage_a",
    )(f, x)


def _stage_c_filter_kernel(a_ref, g_ref, sc_ref, o_ref, *, kc):
    for k in range(kc):
        r = jnp.concatenate([a_ref[0, k], a_ref[1, k]], axis=0)
        y = _dft_dot(g_ref[k], r) * sc_ref[...]
        half = y.shape[0] // 2
        o_ref[0, k] = y[:half]
        o_ref[1, k] = y[half:]


def dft_stage_c_filter(a, g, scale):
    _, k1, r, c = a.shape
    kc = 8
    tc = 256
    blk = pl.BlockSpec((2, kc, r, tc), lambda i, j: (0, i, 0, j))
    return pl.pallas_call(
        functools.partial(_stage_c_filter_kernel, kc=kc),
        grid=(k1 // kc, c // tc),
        in_specs=[blk,
                  pl.BlockSpec((kc, 2 * r, 2 * r), lambda i, j: (i, 0, 0)),
                  pl.BlockSpec((1, tc), lambda i, j: (0, j))],
        out_specs=blk,
        out_shape=jax.ShapeDtypeStruct(a.shape, F32),
        compiler_params=_cparams("arbitrary", "arbitrary"),
        name="dft_stage_c_filter",
    )(a, g, scale)


def _stage_c_conv_kernel(a_ref, h_ref, g_ref, gi_ref, o_ref, *, kc):
    for k in range(kc):
        r = jnp.concatenate([a_ref[0, k], a_ref[1, k]], axis=0)
        y = _dft_dot(g_ref[k], r)
        half = y.shape[0] // 2
        yre, yim = y[:half], y[half:]
        hre, him = h_ref[0, k], h_ref[1, k]
        p = jnp.concatenate([yre * hre - yim * him, yre * him + yim * hre], axis=0)
        d = _dft_dot(gi_ref[k], p)
        o_ref[0, k] = d[:half]
        o_ref[1, k] = d[half:]


def dft_stage_c_conv(a, hspec, g, gi):
    _, k1, r, c = a.shape
    kc = 8
    tc = 256
    blk = pl.BlockSpec((2, kc, r, tc), lambda i, j: (0, i, 0, j))
    mat = pl.BlockSpec((kc, 2 * r, 2 * r), lambda i, j: (i, 0, 0))
    return pl.pallas_call(
        functools.partial(_stage_c_conv_kernel, kc=kc),
        grid=(k1 // kc, c // tc),
        in_specs=[blk, blk, mat, mat],
        out_specs=blk,
        out_shape=jax.ShapeDtypeStruct(a.shape, F32),
        compiler_params=_cparams("arbitrary", "arbitrary"),
        name="dft_stage_c_conv",
    )(a, hspec, g, gi)


def _stage_a_inv_kernel(f_ref, d_ref, u_ref, gt_ref, sk_ref, o_ref):
    d = jnp.concatenate([d_ref[0], d_ref[1]], axis=0)
    y = _dft_dot(f_ref[...], d)
    half = y.shape[0] // 2
    sk = sk_ref[...]
    o_ref[0] = (gt_ref[0] * (y[:half] + u_ref[0] * sk)).astype(o_ref.dtype)
    o_ref[1] = (gt_ref[1] * (y[half:] + u_ref[1] * sk)).astype(o_ref.dtype)


def dft_stage_a_inv(f, d, u, gate, skip_flat, out_dtype):
    _, k1, w = d.shape
    half = k1 // 2
    tl = skip_flat.shape[1]
    io = pl.BlockSpec((2, half, tl), lambda j: (0, 0, j))
    return pl.pallas_call(
        _stage_a_inv_kernel,
        grid=(w // tl,),
        in_specs=[pl.BlockSpec(f.shape, lambda j: (0, 0)),
                  pl.BlockSpec((2, k1, tl), lambda j: (0, 0, j)),
                  io, io,
                  pl.BlockSpec((1, tl), lambda j: (0, 0))],
        out_specs=io,
        out_shape=jax.ShapeDtypeStruct((2, half, w), out_dtype),
        compiler_params=_cparams("arbitrary"),
        name="dft_stage_a_inv",
    )(f, d, u, gate, skip_flat)


def _dft_tables(n_tok):
    ll = 2 * n_tok
    r = DFT_RADIX
    k1n = ll // r
    half = k1n // 2
    two_pi = 2.0 * math.pi

    def cs(m, period):
        ang = (m % period).astype(F32) * (two_pi / period)
        return jnp.cos(ang), jnp.sin(ang)

    k1 = jnp.arange(k1n, dtype=jnp.int32)
    c, s = cs(k1[:, None] * k1[None, :half], k1n)
    fa_data = jnp.concatenate([jnp.concatenate([c, s], 1), jnp.concatenate([-s, c], 1)], 0)
    c, s = cs(k1[:, None] * k1[None, :], k1n)
    fa_filt = jnp.concatenate([c, -s], 0)
    c, s = cs(k1[:half, None] * k1[None, :], k1n)
    fa_inv = jnp.concatenate([jnp.concatenate([c, -s], 1), jnp.concatenate([s, c], 1)], 0)
    idx = jnp.arange(r, dtype=jnp.int32)
    m = idx[None, None, :] * (idx[None, :, None] * k1n + k1[:, None, None])
    c, s = cs(m, ll)
    g_fwd = jnp.concatenate([jnp.concatenate([c, s], 2), jnp.concatenate([-s, c], 2)], 1)
    ct, st = jnp.swapaxes(c, 1, 2), jnp.swapaxes(s, 1, 2)
    g_inv = jnp.concatenate([jnp.concatenate([ct, -st], 2), jnp.concatenate([st, ct], 2)], 1)
    return fa_data, fa_filt, fa_inv, g_fwd, g_inv


def hyena_long(pc, filt, l1, skip, out_dtype):
    _, b, n, c = pc.shape
    assert b == 2, "the two batch entries ride as real/imaginary parts"
    ll = 2 * n
    r = DFT_RADIX
    k1n = ll // r
    half = k1n // 2
    w = r * c
    fa_data, fa_filt, fa_inv, g_fwd, g_inv = _dft_tables(n)
    tl = 4096
    u = pc[0].reshape(2, half, w)
    for o in range(HY_ORDER):
        af = dft_stage_a(fa_filt, filt[o].reshape(1, k1n, w))
        hspec = dft_stage_c_filter(af.reshape(2, k1n, r, c), g_fwd, 1.0 / (l1[o] * ll))
        a = dft_stage_a(fa_data, u)
        d = dft_stage_c_conv(a.reshape(2, k1n, r, c), hspec, g_fwd, g_inv)
        last = o == HY_ORDER - 1
        u = dft_stage_a_inv(fa_inv, d.reshape(2, k1n, w), u, pc[o + 1].reshape(2, half, w),
                            jnp.tile(skip[o], tl // c).reshape(1, tl),
                            out_dtype if last else F32)
    return u.reshape(b, n, c)


def _hyena_short_kernel(ff_ref, fr_ref, fi_ref, pc_ref, filt_ref, l1_ref, skip_ref, o_ref, *, n, ll):
    u = [pc_ref[0, 0], pc_ref[0, 1]]
    for o in range(HY_ORDER):
        hs = _dft_dot(fr_ref[...], filt_ref[o]) * (1.0 / (l1_ref[o] * ll))
        y = _dft_dot(ff_ref[...], jnp.concatenate(u, axis=0))
        yre, yim, hre, him = y[:ll], y[ll:], hs[:ll], hs[ll:]
        p = jnp.concatenate([yre * hre - yim * him, yre * him + yim * hre], axis=0)
        d = _dft_dot(fi_ref[...], p)
        sk = skip_ref[o:o + 1]
        u = [pc_ref[o + 1, bi] * (d[bi * n:(bi + 1) * n] + u[bi] * sk) for bi in range(2)]
    o_ref[0] = u[0].astype(o_ref.dtype)
    o_ref[1] = u[1].astype(o_ref.dtype)


def hyena_short(pc, filt, l1, skip, out_dtype):
    _, b, n, c = pc.shape
    assert b == 2
    ll = 2 * n
    two_pi = 2.0 * math.pi
    kk = jnp.arange(ll, dtype=jnp.int32)
    ang = ((kk[:, None] * kk[None, :]) % ll).astype(F32) * (two_pi / ll)
    cf, sf = jnp.cos(ang), jnp.sin(ang)
    f_real = jnp.concatenate([cf, -sf], 0)
    cn, sn = cf[:, :n], sf[:, :n]
    f_fwd = jnp.concatenate([jnp.concatenate([cn, sn], 1), jnp.concatenate([-sn, cn], 1)], 0)
    ci, si = cf[:n, :], sf[:n, :]
    f_inv = jnp.concatenate([jnp.concatenate([ci, -si], 1), jnp.concatenate([si, ci], 1)], 0)
    tc = 256
    const = lambda j: (0, 0)
    return pl.pallas_call(
        functools.partial(_hyena_short_kernel, n=n, ll=ll),
        grid=(c // tc,),
        in_specs=[pl.BlockSpec(f_fwd.shape, const),
                  pl.BlockSpec(f_real.shape, const),
                  pl.BlockSpec(f_inv.shape, const),
                  pl.BlockSpec((3, 2, n, tc), lambda j: (0, 0, 0, j)),
                  pl.BlockSpec((HY_ORDER, ll, tc), lambda j: (0, 0, j)),
                  pl.BlockSpec((HY_ORDER, 1, tc), lambda j: (0, 0, j)),
                  pl.BlockSpec((HY_ORDER, tc), lambda j: (0, j))],
        out_specs=pl.BlockSpec((2, n, tc), lambda j: (0, 0, j)),
        out_shape=jax.ShapeDtypeStruct((2, n, c), out_dtype),
        compiler_params=_cparams("arbitrary"),
        name="hyena_short",
    )(f_fwd, f_real, f_inv, pc, filt, l1, skip)


def hyena_mixer(pc, w1, b1, w2, b2, w3, freq, skip, out_dtype):
    n = pc.shape[2]
    filt, l1 = hyena_filter_time(n, w1, b1, w2, b2, w3, freq)
    if (2 * n) % (DFT_RADIX * BF16_SUBLANES) == 0 and n >= 1024:
        return hyena_long(pc, filt, l1, skip, out_dtype)
    return hyena_short(pc, filt, l1, skip, out_dtype)


def _deinterleave_cols(w, hd):
    lead = w.shape[:-1]
    wh = w.reshape(lead + (w.shape[-1] // hd, hd // 2, 2))
    return jnp.concatenate([wh[..., 0], wh[..., 1]], axis=-1).reshape(w.shape)


def _rope_tables(n_tok, rot_dim):
    rows = n_tok // GRID_W
    row = jnp.broadcast_to(jnp.arange(rows, dtype=jnp.int32)[:, None], (rows, GRID_W)).reshape(n_tok)
    col = jnp.broadcast_to(jnp.arange(GRID_W, dtype=jnp.int32)[None, :], (rows, GRID_W)).reshape(n_tok)
    axis_dim = rot_dim // 2
    inv_freq = ROPE_THETA ** (-jnp.arange(0, axis_dim, 2, dtype=F32) / axis_dim)
    ang = jnp.concatenate([row.astype(F32)[:, None] * inv_freq,
                           col.astype(F32)[:, None] * inv_freq], axis=-1)
    c, s = jnp.cos(ang), jnp.sin(ang)
    reps = LANES // rot_dim
    return jnp.tile(jnp.concatenate([c, c], -1), (1, reps)), jnp.tile(jnp.concatenate([-s, s], -1), (1, reps))


def _mixer_ab(h, hc, with_ctx, w_in, w_out, qk_g, hy):
    b, n, _ = h.shape
    conv_w, conv_b, w1, b1, w2, b2, w3, freq, skip = hy
    scale = A_HEAD_DIM ** -0.5 * LOG2E
    wq = _deinterleave_cols(w_in[:, :A_Q_W], A_HEAD_DIM).astype(BF16)
    wk = _deinterleave_cols(w_in[:, A_Q_W:A_Q_W + A_KV_W], A_HEAD_DIM).astype(BF16)
    wv = w_in[:, A_Q_W + A_KV_W:A_QKV_W].astype(BF16)
    wh = w_in[:, A_QKV_W:].astype(BF16)
    gq = jnp.tile(_deinterleave_cols(qk_g[0], A_HEAD_DIM) * scale, A_HEADS)
    gk = jnp.tile(_deinterleave_cols(qk_g[1], A_HEAD_DIM), A_KV_HEADS)
    ones_v = jnp.ones((A_KV_W,), F32)
    cosf, sinf = _rope_tables(n, A_HEAD_DIM)

    q = inproj(h, wq, gq, cosf, sinf, norm=True)
    k = inproj(h, wk, gk, cosf, sinf, norm=True)
    v = inproj(h, wv, ones_v)
    k_c = inproj(hc, wk, gk, norm=True)
    v_c = inproj(hc, wv, ones_v)
    k_all = jnp.concatenate([k, k_c], axis=1)
    v_all = jnp.concatenate([v, v_c], axis=1)
    o_att = flash_attention(q, k_all, v_all, mode="gqa")
    pc = inproj_conv(h, wh, conv_w, conv_b, HY_ORDER + 1)
    o_hy = hyena_mixer(pc, w1, b1, w2, b2, w3, freq, skip, BF16)
    wo = w_out.astype(BF16)
    lat = ([o_att, o_hy], [wo[:A_Q_W], wo[A_Q_W:]])
    if not with_ctx:
        return lat, None
    q_c = inproj(hc, wq, gq, norm=True)
    o_att_c = flash_attention(q_c, k_c, v_c, mode="gqa")
    pc_c = inproj_conv(hc, wh, conv_w, conv_b, HY_ORDER + 1)
    o_hy_c = hyena_mixer(pc_c, w1, b1, w2, b2, w3, freq, skip, BF16)
    return lat, ([o_att_c, o_hy_c], [wo[:A_Q_W], wo[A_Q_W:]])


def _mixer_c(h, hc, with_ctx, lambda_init, w_in, w_out, lam_vecs, subln_g):
    b, n, _ = h.shape
    scale = C_HEAD_DIM ** -0.5 * LOG2E
    wq = _deinterleave_cols(w_in[:, :C_Q_W], C_HEAD_DIM).astype(BF16)
    wk = _deinterleave_cols(w_in[:, C_Q_W:2 * C_Q_W], C_HEAD_DIM).astype(BF16)
    wv = w_in[:, 2 * C_Q_W:].astype(BF16)
    gq = jnp.full((C_Q_W,), scale, F32)
    ones = jnp.ones((C_Q_W,), F32)
    cosf, sinf = _rope_tables(n, C_HEAD_DIM)

    q = inproj(h, wq, gq, cosf, sinf, hd=C_HEAD_DIM)
    k = inproj(h, wk, ones, cosf, sinf, hd=C_HEAD_DIM)
    v = inproj(h, wv, ones)
    k_c = inproj(hc, wk, ones)
    v_c = inproj(hc, wv, ones)
    k_all = jnp.concatenate([k, k_c], axis=1)
    v_all = jnp.concatenate([v, v_c], axis=1)
    attn = functools.partial(flash_attention, mode="diff", lam_vecs=lam_vecs, subln_g=subln_g,
                             lambda_init=lambda_init)
    wo = w_out.astype(BF16)
    lat = ([attn(q, k_all, v_all)], [wo])
    if not with_ctx:
        return lat, None
    q_c = inproj(hc, wq, gq)
    return lat, ([attn(q_c, k_c, v_c)], [wo])


def kernel(x, c, ctx, c_ctx, ada_w, ada_b, norm_g, ab_w_in, ab_w_out, ab_qk_g,
           hy_conv_w, hy_conv_b, hy_w1, hy_b1, hy_w2, hy_b2, hy_w3, hy_freq, hy_skip,
           dc_w_in, dc_w_out, dc_lambda, dc_subln_g,
           ffn_w_up, ffn_conv_w, ffn_conv_b, ffn_w_down):
    depth = ada_w.shape[0]
    b, n, d = x.shape
    pad_rows = (-(b + 1)) % 8
    cvec = jnp.concatenate([c, c_ctx[None, :], jnp.zeros((pad_rows, d), F32)], axis=0)
    mod_all = ada_mod(cvec, ada_w, ada_b)
    for i in range(depth):
        j = i // 2
        with_ctx = i < depth - 1
        mod = mod_all[i, :b].reshape(b, 6, d)
        mod_c = jnp.broadcast_to(mod_all[i, b].reshape(1, 6, d), (b, 6, d))
        sh1, sc1, g1, sh2, sc2, g2 = (mod[:, t] for t in range(6))
        sh1c, sc1c, g1c, sh2c, sc2c, g2c = (mod_c[:, t] for t in range(6))
        h = norm_mod(x, norm_g[i, 0], sc1, sh1)
        hc = norm_mod(ctx, norm_g[i, 0], sc1c, sh1c)
        if i % 2 == 0:
            hy = (hy_conv_w[j], hy_conv_b[j], hy_w1[j], hy_b1[j], hy_w2[j], hy_b2[j],
                  hy_w3[j], hy_freq[j], hy_skip[j])
            lat, cx = _mixer_ab(h, hc, with_ctx, ab_w_in[j], ab_w_out[j], ab_qk_g[j], hy)
        else:
            lambda_init = 0.8 - 0.6 * math.exp(-0.3 * i)
            lat, cx = _mixer_c(h, hc, with_ctx, lambda_init, dc_w_in[j], dc_w_out[j],
                               dc_lambda[j], dc_subln_g[j])
        w_up = ffn_w_up[i].astype(BF16)
        w_down = ffn_w_down[i].astype(BF16)
        x = outproj_residual(lat[0], lat[1], x, norm_g[i, 1], g1)
        x = conv_ffn_residual(x, norm_g[i, 2], sc2, sh2, w_up, ffn_conv_w[i], ffn_conv_b[i],
                              w_down, norm_g[i, 3], g2)
        if with_ctx:
            ctx = outproj_residual(cx[0], cx[1], ctx, norm_g[i, 1], g1c)
            ctx = conv_ffn_residual(ctx, norm_g[i, 2], sc2c, sh2c, w_up, ffn_conv_w[i],
                                    ffn_conv_b[i], w_down, norm_g[i, 3], g2c)
    return x
```

```python
import functools
import math

import jax
import jax.numpy as jnp
from jax import lax
from jax.experimental import pallas as pl
from jax.experimental.pallas import tpu as pltpu

F32 = jnp.float32
BF16 = jnp.bfloat16
HIGHEST = lax.Precision.HIGHEST

D_MODEL = 2048
GRID_W = 64
EPS = 1e-6
ROPE_THETA = 10000.0
A_HEADS, A_KV_HEADS, A_HEAD_DIM = 8, 2, 128
A_Q_W = A_HEADS * A_HEAD_DIM
A_KV_W = A_KV_HEADS * A_HEAD_DIM
A_QKV_W = A_Q_W + 2 * A_KV_W
HY_CH = D_MODEL // 2
HY_ORDER = 2
HY_FILTER_W = 64
HY_BANDS = 16
HY_FAST, HY_SLOW, HY_TARGET = 0.3, 1.5, 1e-2
C_HEADS, C_HEAD_DIM = 16, 64
C_Q_W = C_HEADS * 2 * C_HEAD_DIM
D_FF = 5632

LANES = 128
BF16_SUBLANES = 16
VMEM_LIMIT = 56 * 1024 * 1024
DFT_RADIX = 128
FLASH_ROWS = 2048
FLASH_SUB_ROWS = 128
LOG2E = math.log2(math.e)


def _cparams(*sem):
    return pltpu.CompilerParams(dimension_semantics=sem, vmem_limit_bytes=VMEM_LIMIT)


def _rms(x, g):
    return x * lax.rsqrt(jnp.mean(x * x, axis=-1, keepdims=True) + EPS) * g


def _ada_kernel(c_ref, w_ref, b_ref, o_ref):
    c = c_ref[...]
    s = c * jax.nn.sigmoid(c)
    o_ref[0] = jnp.dot(s, w_ref[0], precision=HIGHEST, preferred_element_type=F32) + b_ref[0]


def ada_mod(cvec, ada_w, ada_b):
    depth, d, n6 = ada_w.shape
    rows = cvec.shape[0]
    tn = 1024
    return pl.pallas_call(
        _ada_kernel,
        grid=(depth, n6 // tn),
        in_specs=[pl.BlockSpec((rows, d), lambda l, j: (0, 0)),
                  pl.BlockSpec((1, d, tn), lambda l, j: (l, 0, j)),
                  pl.BlockSpec((1, 1, tn), lambda l, j: (l, 0, j))],
        out_specs=pl.BlockSpec((1, rows, tn), lambda l, j: (l, 0, j)),
        out_shape=jax.ShapeDtypeStruct((depth, rows, n6), F32),
        compiler_params=_cparams("arbitrary", "arbitrary"),
        name="ada_mod",
    )(cvec, ada_w, ada_b.reshape(depth, 1, n6))


def _norm_mod_kernel(x_ref, g_ref, sc_ref, sh_ref, o_ref):
    y = _rms(x_ref[0], g_ref[...])
    o_ref[0] = (y * (1.0 + sc_ref[0]) + sh_ref[0]).astype(o_ref.dtype)


def norm_mod(x, g, sc, sh):
    b, n, d = x.shape
    tm = min(n, 512)
    return pl.pallas_call(
        _norm_mod_kernel,
        grid=(b, n // tm),
        in_specs=[pl.BlockSpec((1, tm, d), lambda bi, i: (bi, i, 0)),
                  pl.BlockSpec((1, d), lambda bi, i: (0, 0)),
                  pl.BlockSpec((1, 1, d), lambda bi, i: (bi, 0, 0)),
                  pl.BlockSpec((1, 1, d), lambda bi, i: (bi, 0, 0))],
        out_specs=pl.BlockSpec((1, tm, d), lambda bi, i: (bi, i, 0)),
        out_shape=jax.ShapeDtypeStruct((b, n, d), BF16),
        compiler_params=_cparams("arbitrary", "arbitrary"),
        name="norm_mod",
    )(x, g.reshape(1, d), sc.reshape(b, 1, d), sh.reshape(b, 1, d))


def _inproj_kernel(*refs, hd, norm, rope, appending):
    o_ref = refs[-1]
    if appending:
        refs = refs[:-2]
    else:
        refs = refs[:-1]
    if rope:
        h_ref, w_ref, g_ref, cos_ref, sin_ref = refs
    else:
        h_ref, w_ref, g_ref = refs
    acc = jnp.dot(h_ref[0], w_ref[...], preferred_element_type=F32)
    tn = acc.shape[1]
    if rope:
        cosf = cos_ref[...]
        sinf = sin_ref[...]
        if hd != LANES:
            lane = lax.broadcasted_iota(jnp.int32, cosf.shape, 1)
            first_half = (lane % hd) < (hd // 2)
    for c in range(tn // LANES):
        x = acc[:, c * LANES:(c + 1) * LANES]
        if norm:
            x = x * lax.rsqrt(jnp.mean(x * x, axis=-1, keepdims=True) + EPS)
        x = x * g_ref[:, c * LANES:(c + 1) * LANES]
        if rope:
            if hd == LANES:
                partner = pltpu.roll(x, LANES // 2, 1)
            else:
                partner = jnp.where(first_half, pltpu.roll(x, LANES - hd // 2, 1),
                                    pltpu.roll(x, hd // 2, 1))
            x = x * cosf + partner * sinf
        o_ref[0, :, c * LANES:(c + 1) * LANES] = x.astype(o_ref.dtype)


def inproj(h, w, gvec, cosf=None, sinf=None, *, hd=LANES, norm=False, out_dtype=BF16,
           extra_rows=0, append_to=None):
    b, n, d = h.shape
    ncol = w.shape[1]
    rope = cosf is not None
    tm = min(n, 512)
    tn = min(ncol, 1024)
    in_specs = [pl.BlockSpec((1, tm, d), lambda bi, i, j: (bi, i, 0)),
                pl.BlockSpec((d, tn), lambda bi, i, j: (0, j)),
                pl.BlockSpec((1, tn), lambda bi, i, j: (0, j))]
    args = [h, w, gvec.reshape(1, ncol)]
    if rope:
        in_specs += [pl.BlockSpec((tm, LANES), lambda bi, i, j: (i, 0)),
                     pl.BlockSpec((tm, LANES), lambda bi, i, j: (i, 0))]
        args += [cosf, sinf]
    appending = append_to is not None
    if appending:
        total = append_to.shape[1]
        first = (total - n) // tm
        assert first * tm + n == total and append_to.dtype == out_dtype
        in_specs.append(pl.BlockSpec(memory_space=pl.ANY))
        args.append(append_to)
        aliases = {len(args) - 1: 0}
    else:
        total, first, aliases = n + extra_rows, 0, {}
    return pl.pallas_call(
        functools.partial(_inproj_kernel, hd=hd, norm=norm, rope=rope, appending=appending),
        grid=(b, n // tm, ncol // tn),
        in_specs=in_specs,
        out_specs=pl.BlockSpec((1, tm, tn), lambda bi, i, j: (bi, first + i, j)),
        out_shape=jax.ShapeDtypeStruct((b, total, ncol), out_dtype),
        input_output_aliases=aliases,
        compiler_params=_cparams("arbitrary", "arbitrary", "arbitrary"),
        name="inproj",
    )(*args)


def _dwconv_rows(u, u_prev_row, u_next_row, cw, cb):
    rows = u.shape[0]
    r = lax.broadcasted_iota(jnp.int32, u.shape, 0)
    up = jnp.where(r == 0, u_prev_row, pltpu.roll(u, 1, 0))
    dn = jnp.where(r == rows - 1, u_next_row, pltpu.roll(u, rows - 1, 0))
    return cw[0:1] * up + cw[1:2] * u + cw[2:3] * dn + cb


def _inproj_conv_kernel(hm_ref, hp_ref, hn_ref, w_ref, cw_ref, cb_ref, o_ref, *, n_tiles):
    i = pl.program_id(1)
    w = w_ref[...]
    u = jnp.dot(hm_ref[0], w, preferred_element_type=F32)
    up = jnp.dot(hp_ref[0], w, preferred_element_type=F32)
    un = jnp.dot(hn_ref[0], w, preferred_element_type=F32)
    halo = up.shape[0]
    prev_row = jnp.where(i > 0, up[halo - 1:halo], 0.0)
    next_row = jnp.where(i < n_tiles - 1, un[0:1], 0.0)
    o_ref[0, 0] = _dwconv_rows(u, prev_row, next_row, cw_ref[...], cb_ref[...])


def inproj_conv(h, w, conv_w, conv_b, groups):
    b, n, d = h.shape
    ncol = w.shape[1]
    c = ncol // groups
    tm = min(n, 512)
    tn = 512
    halo = BF16_SUBLANES
    nt = n // tm
    per = c // tn
    return pl.pallas_call(
        functools.partial(_inproj_conv_kernel, n_tiles=nt),
        grid=(b, nt, ncol // tn),
        in_specs=[pl.BlockSpec((1, tm, d), lambda bi, i, j: (bi, i, 0)),
                  pl.BlockSpec((1, halo, d),
                               lambda bi, i, j: (bi, jnp.maximum(i * (tm // halo) - 1, 0), 0)),
                  pl.BlockSpec((1, halo, d),
                               lambda bi, i, j: (bi, jnp.minimum((i + 1) * (tm // halo), n // halo - 1), 0)),
                  pl.BlockSpec((d, tn), lambda bi, i, j: (0, j)),
                  pl.BlockSpec((3, tn), lambda bi, i, j: (0, j)),
                  pl.BlockSpec((1, tn), lambda bi, i, j: (0, j))],
        out_specs=pl.BlockSpec((1, 1, tm, tn), lambda bi, i, j: (j // per, bi, i, j % per)),
        out_shape=jax.ShapeDtypeStruct((groups, b, n, c), F32),
        compiler_params=_cparams("arbitrary", "arbitrary", "arbitrary"),
        name="inproj_conv",
    )(h, h, h, w, conv_w, conv_b.reshape(1, ncol))


def _flash_kernel(*refs, mode, group, tq, tk, n_kv, lambda_init):
    if mode == "diff":
        q_ref, k_ref, v_ref, lam_ref, sg_ref, o_ref, qs_ref, m_ref, acc_ref = refs
    else:
        q_ref, k_ref, v_ref, o_ref, qs_ref, m_ref, acc_ref = refs
    kv = pl.program_id(3)

    @pl.when(kv == 0)
    def _init():
        q = q_ref[0]
        if mode == "diff":
            lane = lax.broadcasted_iota(jnp.int32, q.shape, 1)
            zero = jnp.zeros_like(q)
            qs_ref[0:tq] = jnp.where(lane < C_HEAD_DIM, q, zero)
            qs_ref[tq:2 * tq] = jnp.where(lane >= C_HEAD_DIM, q, zero)
        else:
            for g in range(group):
                qs_ref[g * tq:(g + 1) * tq] = q[:, g * LANES:(g + 1) * LANES]
        m_ref[...] = jnp.full(m_ref.shape, -jnp.inf, F32)
        acc_ref[...] = jnp.zeros(acc_ref.shape, F32)

    k = k_ref[0]
    v = v_ref[0]
    v_aug = jnp.concatenate([v, jnp.ones_like(v)], axis=1)
    for c in range(group * tq // FLASH_SUB_ROWS):
        sl = slice(c * FLASH_SUB_ROWS, (c + 1) * FLASH_SUB_ROWS)
        s = lax.dot_general(qs_ref[sl], k, (((1,), (1,)), ((), ())),
                            preferred_element_type=F32)
        m_prev = m_ref[sl]
        m_next = jnp.maximum(m_prev, jnp.max(s, axis=-1, keepdims=True))
        alpha = jnp.exp2(m_prev - m_next)
        p = jnp.exp2(s - jnp.tile(m_next, (1, tk // LANES)))
        acc_ref[sl] = jnp.tile(alpha, (1, 2)) * acc_ref[sl] + jnp.dot(
            p.astype(BF16), v_aug, preferred_element_type=F32)
        m_ref[sl] = m_next

    @pl.when(kv == n_kv - 1)
    def _fin():
        acc = acc_ref[...]
        o = acc[:, :LANES] / acc[:, LANES:]
        if mode == "diff":
            lv = lam_ref[...]
            lam = (jnp.exp(jnp.sum(lv[0:1] * lv[1:2], axis=-1, keepdims=True))
                   - jnp.exp(jnp.sum(lv[2:3] * lv[3:4], axis=-1, keepdims=True)) + lambda_init)
            dlt = o[0:tq] - lam * o[tq:2 * tq]
            o_ref[0] = (_rms(dlt, sg_ref[...]) * (1.0 - lambda_init)).astype(o_ref.dtype)
        else:
            for g in range(group):
                o_ref[0, :, g * LANES:(g + 1) * LANES] = o[g * tq:(g + 1) * tq].astype(o_ref.dtype)


def _kv_tile(s):
    for t in (2816, 2048, 1024, 768, 512, 256, 128):
        if s % t == 0:
            return t
    return s


def flash_attention(q, k, v, *, mode, lam_vecs=None, subln_g=None, lambda_init=0.0, kv_start=0):
    b, n, qw = q.shape
    s = k.shape[1] - kv_start
    if mode == "diff":
        group, n_groups, qblk = 2, qw // LANES, LANES
    else:
        group = A_HEADS // A_KV_HEADS
        n_groups, qblk = A_KV_HEADS, group * LANES
    tq = min(n, FLASH_ROWS // group)
    tk = _kv_tile(s)
    n_kv = s // tk
    j0 = kv_start // tk
    assert j0 * tk == kv_start
    in_specs = [pl.BlockSpec((1, tq, qblk), lambda bi, g, i, j: (bi, i, g)),
                pl.BlockSpec((1, tk, LANES), lambda bi, g, i, j: (bi, j0 + j, g)),
                pl.BlockSpec((1, tk, LANES), lambda bi, g, i, j: (bi, j0 + j, g))]
    args = [q, k, v]
    if mode == "diff":
        in_specs += [pl.BlockSpec(lam_vecs.shape, lambda bi, g, i, j: (0, 0)),
                     pl.BlockSpec((1, LANES), lambda bi, g, i, j: (0, 0))]
        args += [lam_vecs, subln_g.reshape(1, LANES)]
    return pl.pallas_call(
        functools.partial(_flash_kernel, mode=mode, group=group, tq=tq, tk=tk, n_kv=n_kv,
                          lambda_init=lambda_init),
        grid=(b, n_groups, n // tq, n_kv),
        in_specs=in_specs,
        out_specs=pl.BlockSpec((1, tq, qblk), lambda bi, g, i, j: (bi, i, g)),
        out_shape=jax.ShapeDtypeStruct((b, n, qw), BF16),
        scratch_shapes=[pltpu.VMEM((group * tq, LANES), BF16),
                        pltpu.VMEM((group * tq, LANES), F32),
                        pltpu.VMEM((group * tq, 2 * LANES), F32)],
        compiler_params=_cparams("arbitrary", "arbitrary", "arbitrary", "arbitrary"),
        name="flash_" + mode,
    )(*args)


def _outproj_kernel(*refs, n_lhs):
    lhs = refs[:n_lhs]
    ws = refs[n_lhs:2 * n_lhs]
    x_ref, g_ref, gate_ref, o_ref = refs[2 * n_lhs:]
    y = jnp.dot(lhs[0][0], ws[0][...], preferred_element_type=F32)
    for a, w in zip(lhs[1:], ws[1:]):
        y = y + jnp.dot(a[0], w[...], preferred_element_type=F32)
    o_ref[0] = x_ref[0] + gate_ref[0] * _rms(y, g_ref[...])


def outproj_residual(lhs_list, w_list, x, g, gate):
    b, n, d = x.shape
    tm = min(n, 512)
    n_lhs = len(lhs_list)
    in_specs = [pl.BlockSpec((1, tm, a.shape[2]), lambda bi, i: (bi, i, 0)) for a in lhs_list]
    in_specs += [pl.BlockSpec(w.shape, lambda bi, i: (0, 0)) for w in w_list]
    in_specs += [pl.BlockSpec((1, tm, d), lambda bi, i: (bi, i, 0)),
                 pl.BlockSpec((1, d), lambda bi, i: (0, 0)),
                 pl.BlockSpec((1, 1, d), lambda bi, i: (bi, 0, 0))]
    return pl.pallas_call(
        functools.partial(_outproj_kernel, n_lhs=n_lhs),
        grid=(b, n // tm),
        in_specs=in_specs,
        out_specs=pl.BlockSpec((1, tm, d), lambda bi, i: (bi, i, 0)),
        out_shape=jax.ShapeDtypeStruct((b, n, d), F32),
        compiler_params=_cparams("arbitrary", "arbitrary"),
        name="outproj",
    )(*lhs_list, *w_list, x, g.reshape(1, d), gate.reshape(b, 1, d))


def _ffn_kernel(xm_ref, xp_ref, xn_ref, gn_ref, sc_ref, sh_ref, wa_ref, wg_ref, cwa_ref, cwg_ref,
                cba_ref, cbg_ref, wd_ref, g3_ref, gate_ref, o_ref, hx_ref,
                *, tm, halo, n_tiles, n_f):
    i = pl.program_id(1)
    j = pl.program_id(2)

    @pl.when(j == 0)
    def _prologue():
        def nm(x):
            return (_rms(x, gn_ref[...]) * (1.0 + sc_ref[0]) + sh_ref[0]).astype(BF16)
        hx_ref[halo:halo + tm] = nm(xm_ref[0])
        zero = jnp.zeros((halo, xm_ref.shape[2]), BF16)
        hx_ref[0:halo] = jnp.where(i > 0, nm(xp_ref[0]), zero)
        hx_ref[halo + tm:] = jnp.where(i < n_tiles - 1, nm(xn_ref[0]), zero)
        o_ref[0] = jnp.zeros(o_ref.shape[1:], F32)

    hx = hx_ref[...]
    rows = tm + 2 * halo

    def conv(u, cw, cb):
        up = pltpu.roll(u, 1, 0)
        dn = pltpu.roll(u, rows - 1, 0)
        return (cw[0:1] * up + cw[1:2] * u + cw[2:3] * dn + cb)[halo:halo + tm]

    a = conv(jnp.dot(hx, wa_ref[...], preferred_element_type=F32), cwa_ref[...], cba_ref[...])
    g = conv(jnp.dot(hx, wg_ref[...], preferred_element_type=F32), cwg_ref[...], cbg_ref[...])
    act = (a * (g * jax.nn.sigmoid(g))).astype(BF16)
    o_ref[0] += jnp.dot(act, wd_ref[...], preferred_element_type=F32)

    @pl.when(j == n_f - 1)
    def _epilogue():
        o_ref[0] = xm_ref[0] + gate_ref[0] * _rms(o_ref[0], g3_ref[...])


def conv_ffn_residual(x, gn, sc, sh, w_up, conv_w, conv_b, w_down, g3, gate):
    b, n, d = x.shape
    dff = w_down.shape[0]
    tm = min(n, 512)
    tf = 512
    halo = BF16_SUBLANES
    nt = n // tm
    n_f = dff // tf
    row = lambda bi, i, j: (bi, i, 0)
    vec = lambda bi, i, j: (0, 0)
    bvec = lambda bi, i, j: (bi, 0, 0)
    up_a = lambda bi, i, j: (0, j)
    up_g = lambda bi, i, j: (0, n_f + j)
    conv_b = conv_b.reshape(1, 2 * dff)
    return pl.pallas_call(
        functools.partial(_ffn_kernel, tm=tm, halo=halo, n_tiles=nt, n_f=n_f),
        grid=(b, nt, n_f),
        in_specs=[pl.BlockSpec((1, tm, d), row),
                  pl.BlockSpec((1, halo, d),
                               lambda bi, i, j: (bi, jnp.maximum(i * (tm // halo) - 1, 0), 0)),
                  pl.BlockSpec((1, halo, d),
                               lambda bi, i, j: (bi, jnp.minimum((i + 1) * (tm // halo), n // halo - 1), 0)),
                  pl.BlockSpec((1, d), vec),
                  pl.BlockSpec((1, 1, d), bvec),
                  pl.BlockSpec((1, 1, d), bvec),
                  pl.BlockSpec((d, tf), up_a),
                  pl.BlockSpec((d, tf), up_g),
                  pl.BlockSpec((3, tf), up_a),
                  pl.BlockSpec((3, tf), up_g),
                  pl.BlockSpec((1, tf), up_a),
                  pl.BlockSpec((1, tf), up_g),
                  pl.BlockSpec((tf, d), lambda bi, i, j: (j, 0)),
                  pl.BlockSpec((1, d), vec),
                  pl.BlockSpec((1, 1, d), bvec)],
        out_specs=pl.BlockSpec((1, tm, d), row),
        out_shape=jax.ShapeDtypeStruct((b, n, d), F32),
        scratch_shapes=[pltpu.VMEM((tm + 2 * halo, d), BF16)],
        compiler_params=_cparams("arbitrary", "arbitrary", "arbitrary"),
        name="conv_ffn",
    )(x, x, x, gn.reshape(1, d), sc.reshape(b, 1, d), sh.reshape(b, 1, d), w_up, w_up,
      conv_w, conv_w, conv_b, conv_b, w_down, g3.reshape(1, d), gate.reshape(b, 1, d))


def _filter_kernel(bands_ref, w1_ref, b1_ref, w2_ref, b2_ref, w3_ref, fr_ref, dl_ref,
                   o_ref, s_ref, *, n_tok, tr):
    i = pl.program_id(0)

    @pl.when(i == 0)
    def _init():
        s_ref[...] = jnp.zeros(s_ref.shape, F32)

    r = (i * tr + lax.broadcasted_iota(jnp.int32, (tr, 1), 0)).astype(F32)
    t = jnp.where(r < n_tok, r, 2.0 * n_tok - r)
    t_norm = t / float(max(n_tok - 1, 1))
    wang = (2.0 * math.pi / n_tok) * t
    z = wang * bands_ref[...]
    lane = lax.broadcasted_iota(jnp.int32, z.shape, 1)
    feats = jnp.where(lane == 0, t_norm,
                      jnp.where(lane <= HY_BANDS, jnp.cos(z),
                                jnp.where(lane <= 2 * HY_BANDS, -jnp.sin(z), 0.0)))
    fr = fr_ref[...]
    h = jnp.sin(fr[0:1] * (jnp.dot(feats, w1_ref[...], precision=HIGHEST,
                                   preferred_element_type=F32) + b1_ref[...]))
    h = jnp.sin(fr[1:2] * (jnp.dot(h, w2_ref[...], precision=HIGHEST,
                                   preferred_element_type=F32) + b2_ref[...]))
    hb = h.astype(BF16)
    decay = jnp.where(r == n_tok, 0.0, jnp.exp(-t_norm * dl_ref[...]))
    for o in range(HY_ORDER):
        out = jnp.dot(hb, w3_ref[o].astype(BF16), preferred_element_type=F32) * decay
        o_ref[o] = out
        s_ref[o] += jnp.sum(jnp.abs(out), axis=0, keepdims=True)


def hyena_filter_time(n_tok, w1, b1, w2, b2, w3, freq):
    ch = HY_CH
    ll = 2 * n_tok
    tr = min(n_tok, 512)
    per_half = n_tok // tr
    lane = jnp.arange(LANES)
    bands = jnp.linspace(1e-4, HY_BANDS - 1, HY_BANDS, dtype=F32)
    bands_l = jnp.where((lane >= 1) & (lane <= HY_BANDS), bands[jnp.clip(lane - 1, 0, HY_BANDS - 1)],
                        jnp.where((lane > HY_BANDS) & (lane <= 2 * HY_BANDS),
                                  bands[jnp.clip(lane - 1 - HY_BANDS, 0, HY_BANDS - 1)], 0.0))
    w1p = jnp.zeros((LANES, HY_FILTER_W), F32).at[:w1.shape[0]].set(w1)
    deltas = jnp.abs(jnp.linspace(math.log(HY_TARGET) / HY_FAST, math.log(HY_TARGET) / HY_SLOW,
                                  ch, dtype=F32))
    w3o = jnp.swapaxes(w3.reshape(HY_FILTER_W, HY_ORDER, 2 * ch), 0, 1)
    const = lambda i: (0, 0)
    return pl.pallas_call(
        functools.partial(_filter_kernel, n_tok=n_tok, tr=tr),
        grid=(ll // tr,),
        in_specs=[pl.BlockSpec((1, LANES), const),
                  pl.BlockSpec((LANES, HY_FILTER_W), const),
                  pl.BlockSpec((1, HY_FILTER_W), const),
                  pl.BlockSpec((HY_FILTER_W, HY_FILTER_W), const),
                  pl.BlockSpec((1, HY_FILTER_W), const),
                  pl.BlockSpec((HY_ORDER, HY_FILTER_W, ch), lambda i: (0, 0, i // per_half)),
                  pl.BlockSpec((2, HY_FILTER_W), const),
                  pl.BlockSpec((1, ch), const)],
        out_specs=[pl.BlockSpec((HY_ORDER, tr, ch), lambda i: (0, i, 0)),
                   pl.BlockSpec((HY_ORDER, 1, ch), lambda i: (0, 0, 0))],
        out_shape=[jax.ShapeDtypeStruct((HY_ORDER, ll, ch), F32),
                   jax.ShapeDtypeStruct((HY_ORDER, 1, ch), F32)],
        compiler_params=_cparams("arbitrary"),
        name="hyena_filter",
    )(bands_l.reshape(1, LANES), w1p, b1.reshape(1, -1), w2, b2.reshape(1, -1), w3o, freq,
      deltas.reshape(1, ch))


def _dft_dot(f, x):
    return jnp.dot(f.astype(BF16), x.astype(BF16), preferred_element_type=F32)


def _stage_a_kernel(f_ref, x_ref, o_ref):
    parts = x_ref.shape[0]
    x = x_ref[0] if parts == 1 else jnp.concatenate([x_ref[p] for p in range(parts)], axis=0)
    y = _dft_dot(f_ref[...], x).astype(o_ref.dtype)
    half = y.shape[0] // 2
    o_ref[0] = y[:half]
    o_ref[1] = y[half:]


def dft_stage_a(f, x):
    p, rows, w = x.shape
    k1 = p * rows
    tl = min(w, 4096)
    return pl.pallas_call(
        _stage_a_kernel,
        grid=(w // tl,),
        in_specs=[pl.BlockSpec(f.shape, lambda j: (0, 0)),
                  pl.BlockSpec((p, rows, tl), lambda j: (0, 0, j))],
        out_specs=pl.BlockSpec((2, k1, tl), lambda j: (0, 0, j)),
        out_shape=jax.ShapeDtypeStruct((2, k1, w), BF16),
        compiler_params=_cparams("arbitrary"),
        name="dft_stage_a",
    )(f, x)


def _stage_c_filter_kernel(a_ref, g_ref, sc_ref, o_ref, *, kc):
    for k in range(kc):
        r = jnp.concatenate([a_ref[0, k], a_ref[1, k]], axis=0)
        y = (_dft_dot(g_ref[k], r) * sc_ref[...]).astype(o_ref.dtype)
        half = y.shape[0] // 2
        o_ref[0, k] = y[:half]
        o_ref[1, k] = y[half:]


def dft_stage_c_filter(a, g, scale):
    _, k1, r, c = a.shape
    kc = 8
    tc = 256
    blk = pl.BlockSpec((2, kc, r, tc), lambda i, j: (0, i, 0, j))
    return pl.pallas_call(
        functools.partial(_stage_c_filter_kernel, kc=kc),
        grid=(k1 // kc, c // tc),
        in_specs=[blk,
                  pl.BlockSpec((kc, 2 * r, 2 * r), lambda i, j: (i, 0, 0)),
                  pl.BlockSpec((1, tc), lambda i, j: (0, j))],
        out_specs=blk,
        out_shape=jax.ShapeDtypeStruct(a.shape, BF16),
        compiler_params=_cparams("arbitrary", "arbitrary"),
        name="dft_stage_c_filter",
    )(a, g, scale)


def _stage_c_conv_kernel(a_ref, h_ref, g_ref, gi_ref, o_ref, *, kc):
    for k in range(kc):
        r = jnp.concatenate([a_ref[0, k], a_ref[1, k]], axis=0)
        y = _dft_dot(g_ref[k], r)
        half = y.shape[0] // 2
        yre, yim = y[:half], y[half:]
        hre, him = h_ref[0, k].astype(F32), h_ref[1, k].astype(F32)
        p = jnp.concatenate([yre * hre - yim * him, yre * him + yim * hre], axis=0)
        d = _dft_dot(gi_ref[k], p).astype(o_ref.dtype)
        o_ref[0, k] = d[:half]
        o_ref[1, k] = d[half:]


def dft_stage_c_conv(a, hspec, g, gi):
    _, k1, r, c = a.shape
    kc = 8
    tc = 256
    blk = pl.BlockSpec((2, kc, r, tc), lambda i, j: (0, i, 0, j))
    mat = pl.BlockSpec((kc, 2 * r, 2 * r), lambda i, j: (i, 0, 0))
    return pl.pallas_call(
        functools.partial(_stage_c_conv_kernel, kc=kc),
        grid=(k1 // kc, c // tc),
        in_specs=[blk, blk, mat, mat],
        out_specs=blk,
        out_shape=jax.ShapeDtypeStruct(a.shape, BF16),
        compiler_params=_cparams("arbitrary", "arbitrary"),
        name="dft_stage_c_conv",
    )(a, hspec, g, gi)


def _stage_a_inv_kernel(f_ref, d_ref, u_ref, gt_ref, sk_ref, o_ref):
    d = jnp.concatenate([d_ref[0], d_ref[1]], axis=0)
    y = _dft_dot(f_ref[...], d)
    half = y.shape[0] // 2
    sk = sk_ref[...]
    o_ref[0] = (gt_ref[0] * (y[:half] + u_ref[0] * sk)).astype(o_ref.dtype)
    o_ref[1] = (gt_ref[1] * (y[half:] + u_ref[1] * sk)).astype(o_ref.dtype)


def dft_stage_a_inv(f, d, u, gate, skip_flat, out_dtype):
    _, k1, w = d.shape
    half = k1 // 2
    tl = skip_flat.shape[1]
    io = pl.BlockSpec((2, half, tl), lambda j: (0, 0, j))
    return pl.pallas_call(
        _stage_a_inv_kernel,
        grid=(w // tl,),
        in_specs=[pl.BlockSpec(f.shape, lambda j: (0, 0)),
                  pl.BlockSpec((2, k1, tl), lambda j: (0, 0, j)),
                  io, io,
                  pl.BlockSpec((1, tl), lambda j: (0, 0))],
        out_specs=io,
        out_shape=jax.ShapeDtypeStruct((2, half, w), out_dtype),
        compiler_params=_cparams("arbitrary"),
        name="dft_stage_a_inv",
    )(f, d, u, gate, skip_flat)


def _dft_tables(n_tok):
    ll = 2 * n_tok
    r = DFT_RADIX
    k1n = ll // r
    half = k1n // 2
    two_pi = 2.0 * math.pi

    def cs(m, period):
        ang = (m % period).astype(F32) * (two_pi / period)
        return jnp.cos(ang), jnp.sin(ang)

    k1 = jnp.arange(k1n, dtype=jnp.int32)
    c, s = cs(k1[:, None] * k1[None, :half], k1n)
    fa_data = jnp.concatenate([jnp.concatenate([c, s], 1), jnp.concatenate([-s, c], 1)], 0)
    c, s = cs(k1[:, None] * k1[None, :], k1n)
    fa_filt = jnp.concatenate([c, -s], 0)
    c, s = cs(k1[:half, None] * k1[None, :], k1n)
    fa_inv = jnp.concatenate([jnp.concatenate([c, -s], 1), jnp.concatenate([s, c], 1)], 0)
    idx = jnp.arange(r, dtype=jnp.int32)
    ca, sa = cs(idx[:, None] * idx[None, :], r)
    cb, sb = cs(k1[:, None] * idx[None, :], ll)
    c = ca[None] * cb[:, None, :] - sa[None] * sb[:, None, :]
    s = sa[None] * cb[:, None, :] + ca[None] * sb[:, None, :]
    g_fwd = jnp.concatenate([jnp.concatenate([c, s], 2), jnp.concatenate([-s, c], 2)], 1)
    ct, st = jnp.swapaxes(c, 1, 2), jnp.swapaxes(s, 1, 2)
    g_inv = jnp.concatenate([jnp.concatenate([ct, -st], 2), jnp.concatenate([st, ct], 2)], 1)
    return tuple(t.astype(BF16) for t in (fa_data, fa_filt, fa_inv, g_fwd, g_inv))


def hyena_long(pc, filt, l1, skip, out_dtype):
    _, b, n, c = pc.shape
    assert b == 2, "the two batch entries ride as real/imaginary parts"
    ll = 2 * n
    r = DFT_RADIX
    k1n = ll // r
    half = k1n // 2
    w = r * c
    fa_data, fa_filt, fa_inv, g_fwd, g_inv = _dft_tables(n)
    tl = 4096
    u = pc[0].reshape(2, half, w)
    for o in range(HY_ORDER):
        af = dft_stage_a(fa_filt, filt[o].reshape(1, k1n, w))
        hspec = dft_stage_c_filter(af.reshape(2, k1n, r, c), g_fwd, 1.0 / (l1[o] * ll))
        a = dft_stage_a(fa_data, u)
        d = dft_stage_c_conv(a.reshape(2, k1n, r, c), hspec, g_fwd, g_inv)
        last = o == HY_ORDER - 1
        u = dft_stage_a_inv(fa_inv, d.reshape(2, k1n, w), u, pc[o + 1].reshape(2, half, w),
                            jnp.tile(skip[o], tl // c).reshape(1, tl),
                            out_dtype if last else F32)
    return u.reshape(b, n, c)


def _hyena_short_kernel(ff_ref, fr_ref, fi_ref, pc_ref, filt_ref, l1_ref, skip_ref, o_ref, *, n, ll):
    u = [pc_ref[0, 0], pc_ref[0, 1]]
    for o in range(HY_ORDER):
        hs = _dft_dot(fr_ref[...], filt_ref[o]) * (1.0 / (l1_ref[o] * ll))
        y = _dft_dot(ff_ref[...], jnp.concatenate(u, axis=0))
        yre, yim, hre, him = y[:ll], y[ll:], hs[:ll], hs[ll:]
        p = jnp.concatenate([yre * hre - yim * him, yre * him + yim * hre], axis=0)
        d = _dft_dot(fi_ref[...], p)
        sk = skip_ref[o:o + 1]
        u = [pc_ref[o + 1, bi] * (d[bi * n:(bi + 1) * n] + u[bi] * sk) for bi in range(2)]
    o_ref[0] = u[0].astype(o_ref.dtype)
    o_ref[1] = u[1].astype(o_ref.dtype)


def hyena_short(pc, filt, l1, skip, out_dtype):
    _, b, n, c = pc.shape
    assert b == 2
    ll = 2 * n
    two_pi = 2.0 * math.pi
    kk = jnp.arange(ll, dtype=jnp.int32)
    ang = ((kk[:, None] * kk[None, :]) % ll).astype(F32) * (two_pi / ll)
    cf, sf = jnp.cos(ang), jnp.sin(ang)
    f_real = jnp.concatenate([cf, -sf], 0)
    cn, sn = cf[:, :n], sf[:, :n]
    f_fwd = jnp.concatenate([jnp.concatenate([cn, sn], 1), jnp.concatenate([-sn, cn], 1)], 0)
    ci, si = cf[:n, :], sf[:n, :]
    f_inv = jnp.concatenate([jnp.concatenate([ci, -si], 1), jnp.concatenate([si, ci], 1)], 0)
    tc = 256
    const = lambda j: (0, 0)
    return pl.pallas_call(
        functools.partial(_hyena_short_kernel, n=n, ll=ll),
        grid=(c // tc,),
        in_specs=[pl.BlockSpec(f_fwd.shape, const),
                  pl.BlockSpec(f_real.shape, const),
                  pl.BlockSpec(f_inv.shape, const),
                  pl.BlockSpec((3, 2, n, tc), lambda j: (0, 0, 0, j)),
                  pl.BlockSpec((HY_ORDER, ll, tc), lambda j: (0, 0, j)),
                  pl.BlockSpec((HY_ORDER, 1, tc), lambda j: (0, 0, j)),
                  pl.BlockSpec((HY_ORDER, tc), lambda j: (0, j))],
        out_specs=pl.BlockSpec((2, n, tc), lambda j: (0, 0, j)),
        out_shape=jax.ShapeDtypeStruct((2, n, c), out_dtype),
        compiler_params=_cparams("arbitrary"),
        name="hyena_short",
    )(f_fwd, f_real, f_inv, pc, filt, l1, skip)


def hyena_mixer(pc, w1, b1, w2, b2, w3, freq, skip, out_dtype):
    n = pc.shape[2]
    filt, l1 = hyena_filter_time(n, w1, b1, w2, b2, w3, freq)
    if (2 * n) % (DFT_RADIX * BF16_SUBLANES) == 0 and n >= 1024:
        return hyena_long(pc, filt, l1, skip, out_dtype)
    return hyena_short(pc, filt, l1, skip, out_dtype)


def _deinterleave_cols(w, hd):
    lead = w.shape[:-1]
    wh = w.reshape(lead + (w.shape[-1] // hd, hd // 2, 2))
    return jnp.concatenate([wh[..., 0], wh[..., 1]], axis=-1).reshape(w.shape)


def _rope_tables(n_tok, rot_dim):
    rows = n_tok // GRID_W
    row = jnp.broadcast_to(jnp.arange(rows, dtype=jnp.int32)[:, None], (rows, GRID_W)).reshape(n_tok)
    col = jnp.broadcast_to(jnp.arange(GRID_W, dtype=jnp.int32)[None, :], (rows, GRID_W)).reshape(n_tok)
    axis_dim = rot_dim // 2
    inv_freq = ROPE_THETA ** (-jnp.arange(0, axis_dim, 2, dtype=F32) / axis_dim)
    ang = jnp.concatenate([row.astype(F32)[:, None] * inv_freq,
                           col.astype(F32)[:, None] * inv_freq], axis=-1)
    c, s = jnp.cos(ang), jnp.sin(ang)
    reps = LANES // rot_dim
    return jnp.tile(jnp.concatenate([c, c], -1), (1, reps)), jnp.tile(jnp.concatenate([-s, s], -1), (1, reps))


def _mixer_ab(h, hc, with_ctx, w_in, w_out, qk_g, hy):
    b, n, _ = h.shape
    conv_w, conv_b, w1, b1, w2, b2, w3, freq, skip = hy
    scale = A_HEAD_DIM ** -0.5 * LOG2E
    wq = _deinterleave_cols(w_in[:, :A_Q_W], A_HEAD_DIM).astype(BF16)
    wk = _deinterleave_cols(w_in[:, A_Q_W:A_Q_W + A_KV_W], A_HEAD_DIM).astype(BF16)
    wv = w_in[:, A_Q_W + A_KV_W:A_QKV_W].astype(BF16)
    wh = w_in[:, A_QKV_W:].astype(BF16)
    gq = jnp.tile(_deinterleave_cols(qk_g[0], A_HEAD_DIM) * scale, A_HEADS)
    gk = jnp.tile(_deinterleave_cols(qk_g[1], A_HEAD_DIM), A_KV_HEADS)
    ones_v = jnp.ones((A_KV_W,), F32)
    cosf, sinf = _rope_tables(n, A_HEAD_DIM)

    n_ctx = hc.shape[1]
    q = inproj(h, wq, gq, cosf, sinf, norm=True)
    k_all = inproj(hc, wk, gk, norm=True,
                   append_to=inproj(h, wk, gk, cosf, sinf, norm=True, extra_rows=n_ctx))
    v_all = inproj(hc, wv, ones_v, append_to=inproj(h, wv, ones_v, extra_rows=n_ctx))
    o_att = flash_attention(q, k_all, v_all, mode="gqa")
    pc = inproj_conv(h, wh, conv_w, conv_b, HY_ORDER + 1)
    o_hy = hyena_mixer(pc, w1, b1, w2, b2, w3, freq, skip, BF16)
    wo = w_out.astype(BF16)
    lat = ([o_att, o_hy], [wo[:A_Q_W], wo[A_Q_W:]])
    if not with_ctx:
        return lat, None
    q_c = inproj(hc, wq, gq, norm=True)
    o_att_c = flash_attention(q_c, k_all, v_all, mode="gqa", kv_start=n)
    pc_c = inproj_conv(hc, wh, conv_w, conv_b, HY_ORDER + 1)
    o_hy_c = hyena_mixer(pc_c, w1, b1, w2, b2, w3, freq, skip, BF16)
    return lat, ([o_att_c, o_hy_c], [wo[:A_Q_W], wo[A_Q_W:]])


def _mixer_c(h, hc, with_ctx, lambda_init, w_in, w_out, lam_vecs, subln_g):
    b, n, _ = h.shape
    scale = C_HEAD_DIM ** -0.5 * LOG2E
    wq = _deinterleave_cols(w_in[:, :C_Q_W], C_HEAD_DIM).astype(BF16)
    wk = _deinterleave_cols(w_in[:, C_Q_W:2 * C_Q_W], C_HEAD_DIM).astype(BF16)
    wv = w_in[:, 2 * C_Q_W:].astype(BF16)
    gq = jnp.full((C_Q_W,), scale, F32)
    ones = jnp.ones((C_Q_W,), F32)
    cosf, sinf = _rope_tables(n, C_HEAD_DIM)

    n_ctx = hc.shape[1]
    q = inproj(h, wq, gq, cosf, sinf, hd=C_HEAD_DIM)
    k_all = inproj(hc, wk, ones,
                   append_to=inproj(h, wk, ones, cosf, sinf, hd=C_HEAD_DIM, extra_rows=n_ctx))
    v_all = inproj(hc, wv, ones, append_to=inproj(h, wv, ones, extra_rows=n_ctx))
    attn = functools.partial(flash_attention, mode="diff", lam_vecs=lam_vecs, subln_g=subln_g,
                             lambda_init=lambda_init)
    wo = w_out.astype(BF16)
    lat = ([attn(q, k_all, v_all)], [wo])
    if not with_ctx:
        return lat, None
    q_c = inproj(hc, wq, gq)
    return lat, ([attn(q_c, k_all, v_all, kv_start=n)], [wo])


def kernel(x, c, ctx, c_ctx, ada_w, ada_b, norm_g, ab_w_in, ab_w_out, ab_qk_g,
           hy_conv_w, hy_conv_b, hy_w1, hy_b1, hy_w2, hy_b2, hy_w3, hy_freq, hy_skip,
           dc_w_in, dc_w_out, dc_lambda, dc_subln_g,
           ffn_w_up, ffn_conv_w, ffn_conv_b, ffn_w_down):
    depth = ada_w.shape[0]
    b, n, d = x.shape
    pad_rows = (-(b + 1)) % 8
    cvec = jnp.concatenate([c, c_ctx[None, :], jnp.zeros((pad_rows, d), F32)], axis=0)
    mod_all = ada_mod(cvec, ada_w, ada_b)
    for i in range(depth):
        j = i // 2
        with_ctx = i < depth - 1
        mod = mod_all[i, :b].reshape(b, 6, d)
        mod_c = jnp.broadcast_to(mod_all[i, b].reshape(1, 6, d), (b, 6, d))
        sh1, sc1, g1, sh2, sc2, g2 = (mod[:, t] for t in range(6))
        sh1c, sc1c, g1c, sh2c, sc2c, g2c = (mod_c[:, t] for t in range(6))
        h = norm_mod(x, norm_g[i, 0], sc1, sh1)
        hc = norm_mod(ctx, norm_g[i, 0], sc1c, sh1c)
        if i % 2 == 0:
            hy = (hy_conv_w[j], hy_conv_b[j], hy_w1[j], hy_b1[j], hy_w2[j], hy_b2[j],
                  hy_w3[j], hy_freq[j], hy_skip[j])
            lat, cx = _mixer_ab(h, hc, with_ctx, ab_w_in[j], ab_w_out[j], ab_qk_g[j], hy)
        else:
            lambda_init = 0.8 - 0.6 * math.exp(-0.3 * i)
            lat, cx = _mixer_c(h, hc, with_ctx, lambda_init, dc_w_in[j], dc_w_out[j],
                               dc_lambda[j], dc_subln_g[j])
        w_up = ffn_w_up[i].astype(BF16)
        w_down = ffn_w_down[i].astype(BF16)
        x = outproj_residual(lat[0], lat[1], x, norm_g[i, 1], g1)
        x = conv_ffn_residual(x, norm_g[i, 2], sc2, sh2, w_up, ffn_conv_w[i], ffn_conv_b[i],
                              w_down, norm_g[i, 3], g2)
        if with_ctx:
            ctx = outproj_residual(cx[0], cx[1], ctx, norm_g[i, 1], g1c)
            ctx = conv_ffn_residual(ctx, norm_g[i, 2], sc2c, sh2c, w_up, ffn_conv_w[i],
                                    ffn_conv_b[i], w_down, norm_g[i, 3], g2c)
    return x
```

```python
import functools
import math

import jax
import jax.numpy as jnp
from jax import lax
from jax.experimental import pallas as pl
from jax.experimental.pallas import tpu as pltpu

F32 = jnp.float32
BF16 = jnp.bfloat16
HIGHEST = lax.Precision.HIGHEST

D_MODEL = 2048
GRID_W = 64
EPS = 1e-6
ROPE_THETA = 10000.0
A_HEADS, A_KV_HEADS, A_HEAD_DIM = 8, 2, 128
A_Q_W = A_HEADS * A_HEAD_DIM
A_KV_W = A_KV_HEADS * A_HEAD_DIM
A_QKV_W = A_Q_W + 2 * A_KV_W
HY_CH = D_MODEL // 2
HY_ORDER = 2
HY_FILTER_W = 64
HY_BANDS = 16
HY_FAST, HY_SLOW, HY_TARGET = 0.3, 1.5, 1e-2
C_HEADS, C_HEAD_DIM = 16, 64
C_Q_W = C_HEADS * 2 * C_HEAD_DIM
D_FF = 5632

LANES = 128
BF16_SUBLANES = 16
VMEM_LIMIT = 56 * 1024 * 1024
DFT_RADIX = 128
FLASH_ROWS = 2048
FLASH_SUB_ROWS = 128
LOG2E = math.log2(math.e)


def _cparams(*sem):
    return pltpu.CompilerParams(dimension_semantics=sem, vmem_limit_bytes=VMEM_LIMIT)


def _rms(x, g):
    return x * lax.rsqrt(jnp.mean(x * x, axis=-1, keepdims=True) + EPS) * g


def _ada_kernel(c_ref, w_ref, b_ref, o_ref):
    c = c_ref[...]
    s = c * jax.nn.sigmoid(c)
    rows = s.shape[0]
    s_hi = s.astype(BF16).astype(F32)
    s2 = jnp.concatenate([s_hi, s - s_hi], axis=0).astype(BF16)
    w = w_ref[0]
    w_hi = w.astype(BF16)
    w_lo = (w - w_hi.astype(F32)).astype(BF16)
    both = jnp.dot(s2, w_hi, preferred_element_type=F32)
    hi_lo = jnp.dot(s2, w_lo, preferred_element_type=F32)[:rows]
    o_ref[0] = both[:rows] + both[rows:] + hi_lo + b_ref[0]


def ada_mod(cvec, ada_w, ada_b):
    depth, d, n6 = ada_w.shape
    rows = cvec.shape[0]
    tn = 1024
    return pl.pallas_call(
        _ada_kernel,
        grid=(depth, n6 // tn),
        in_specs=[pl.BlockSpec((rows, d), lambda l, j: (0, 0)),
                  pl.BlockSpec((1, d, tn), lambda l, j: (l, 0, j)),
                  pl.BlockSpec((1, 1, tn), lambda l, j: (l, 0, j))],
        out_specs=pl.BlockSpec((1, rows, tn), lambda l, j: (l, 0, j)),
        out_shape=jax.ShapeDtypeStruct((depth, rows, n6), F32),
        compiler_params=_cparams("arbitrary", "arbitrary"),
        name="ada_mod",
    )(cvec, ada_w, ada_b.reshape(depth, 1, n6))


def _norm_mod_kernel(x_ref, g_ref, sc_ref, sh_ref, o_ref):
    y = _rms(x_ref[0], g_ref[...])
    o_ref[0] = (y * (1.0 + sc_ref[0]) + sh_ref[0]).astype(o_ref.dtype)


def norm_mod(x, g, sc, sh):
    b, n, d = x.shape
    tm = min(n, 512)
    return pl.pallas_call(
        _norm_mod_kernel,
        grid=(b, n // tm),
        in_specs=[pl.BlockSpec((1, tm, d), lambda bi, i: (bi, i, 0)),
                  pl.BlockSpec((1, d), lambda bi, i: (0, 0)),
                  pl.BlockSpec((1, 1, d), lambda bi, i: (bi, 0, 0)),
                  pl.BlockSpec((1, 1, d), lambda bi, i: (bi, 0, 0))],
        out_specs=pl.BlockSpec((1, tm, d), lambda bi, i: (bi, i, 0)),
        out_shape=jax.ShapeDtypeStruct((b, n, d), BF16),
        compiler_params=_cparams("arbitrary", "arbitrary"),
        name="norm_mod",
    )(x, g.reshape(1, d), sc.reshape(b, 1, d), sh.reshape(b, 1, d))


def _inproj_kernel(*refs, hd, norm, rope, appending):
    o_ref = refs[-1]
    if appending:
        refs = refs[:-2]
    else:
        refs = refs[:-1]
    if rope:
        h_ref, w_ref, g_ref, cos_ref, sin_ref = refs
    else:
        h_ref, w_ref, g_ref = refs
    acc = jnp.dot(h_ref[0], w_ref[...], preferred_element_type=F32)
    tn = acc.shape[1]
    if rope:
        cosf = cos_ref[...]
        sinf = sin_ref[...]
        if hd != LANES:
            lane = lax.broadcasted_iota(jnp.int32, cosf.shape, 1)
            first_half = (lane % hd) < (hd // 2)
    for c in range(tn // LANES):
        x = acc[:, c * LANES:(c + 1) * LANES]
        if norm:
            x = x * lax.rsqrt(jnp.mean(x * x, axis=-1, keepdims=True) + EPS)
        x = x * g_ref[:, c * LANES:(c + 1) * LANES]
        if rope:
            if hd == LANES:
                partner = pltpu.roll(x, LANES // 2, 1)
            else:
                partner = jnp.where(first_half, pltpu.roll(x, LANES - hd // 2, 1),
                                    pltpu.roll(x, hd // 2, 1))
            x = x * cosf + partner * sinf
        o_ref[0, :, c * LANES:(c + 1) * LANES] = x.astype(o_ref.dtype)


def inproj(h, w, gvec, cosf=None, sinf=None, *, hd=LANES, norm=False, out_dtype=BF16,
           extra_rows=0, append_to=None):
    b, n, d = h.shape
    ncol = w.shape[1]
    rope = cosf is not None
    tm = min(n, 512)
    tn = min(ncol, 1024)
    in_specs = [pl.BlockSpec((1, tm, d), lambda bi, i, j: (bi, i, 0)),
                pl.BlockSpec((d, tn), lambda bi, i, j: (0, j)),
                pl.BlockSpec((1, tn), lambda bi, i, j: (0, j))]
    args = [h, w, gvec.reshape(1, ncol)]
    if rope:
        in_specs += [pl.BlockSpec((tm, LANES), lambda bi, i, j: (i, 0)),
                     pl.BlockSpec((tm, LANES), lambda bi, i, j: (i, 0))]
        args += [cosf, sinf]
    appending = append_to is not None
    if appending:
        total = append_to.shape[1]
        first = (total - n) // tm
        assert first * tm + n == total and append_to.dtype == out_dtype
        in_specs.append(pl.BlockSpec(memory_space=pl.ANY))
        args.append(append_to)
        aliases = {len(args) - 1: 0}
    else:
        total, first, aliases = n + extra_rows, 0, {}
    return pl.pallas_call(
        functools.partial(_inproj_kernel, hd=hd, norm=norm, rope=rope, appending=appending),
        grid=(b, n // tm, ncol // tn),
        in_specs=in_specs,
        out_specs=pl.BlockSpec((1, tm, tn), lambda bi, i, j: (bi, first + i, j)),
        out_shape=jax.ShapeDtypeStruct((b, total, ncol), out_dtype),
        input_output_aliases=aliases,
        compiler_params=_cparams("arbitrary", "arbitrary", "arbitrary"),
        name="inproj",
    )(*args)


def _dwconv_rows(u, u_prev_row, u_next_row, cw, cb):
    rows = u.shape[0]
    r = lax.broadcasted_iota(jnp.int32, u.shape, 0)
    up = jnp.where(r == 0, u_prev_row, pltpu.roll(u, 1, 0))
    dn = jnp.where(r == rows - 1, u_next_row, pltpu.roll(u, rows - 1, 0))
    return cw[0:1] * up + cw[1:2] * u + cw[2:3] * dn + cb


def _inproj_conv_kernel(hm_ref, hp_ref, hn_ref, w_ref, cw_ref, cb_ref, o_ref, *, n_tiles):
    i = pl.program_id(1)
    w = w_ref[...]
    u = jnp.dot(hm_ref[0], w, preferred_element_type=F32)
    up = jnp.dot(hp_ref[0], w, preferred_element_type=F32)
    un = jnp.dot(hn_ref[0], w, preferred_element_type=F32)
    halo = up.shape[0]
    prev_row = jnp.where(i > 0, up[halo - 1:halo], 0.0)
    next_row = jnp.where(i < n_tiles - 1, un[0:1], 0.0)
    o_ref[0, 0] = _dwconv_rows(u, prev_row, next_row, cw_ref[...], cb_ref[...])


def inproj_conv(h, w, conv_w, conv_b, groups):
    b, n, d = h.shape
    ncol = w.shape[1]
    c = ncol // groups
    tm = min(n, 512)
    tn = 512
    halo = BF16_SUBLANES
    nt = n // tm
    per = c // tn
    return pl.pallas_call(
        functools.partial(_inproj_conv_kernel, n_tiles=nt),
        grid=(b, nt, ncol // tn),
        in_specs=[pl.BlockSpec((1, tm, d), lambda bi, i, j: (bi, i, 0)),
                  pl.BlockSpec((1, halo, d),
                               lambda bi, i, j: (bi, jnp.maximum(i * (tm // halo) - 1, 0), 0)),
                  pl.BlockSpec((1, halo, d),
                               lambda bi, i, j: (bi, jnp.minimum((i + 1) * (tm // halo), n // halo - 1), 0)),
                  pl.BlockSpec((d, tn), lambda bi, i, j: (0, j)),
                  pl.BlockSpec((3, tn), lambda bi, i, j: (0, j)),
                  pl.BlockSpec((1, tn), lambda bi, i, j: (0, j))],
        out_specs=pl.BlockSpec((1, 1, tm, tn), lambda bi, i, j: (j // per, bi, i, j % per)),
        out_shape=jax.ShapeDtypeStruct((groups, b, n, c), F32),
        compiler_params=_cparams("arbitrary", "arbitrary", "arbitrary"),
        name="inproj_conv",
    )(h, h, h, w, conv_w, conv_b.reshape(1, ncol))


def _flash_kernel(*refs, mode, group, tq, tk, n_kv, lambda_init):
    if mode == "diff":
        q_ref, k_ref, v_ref, lam_ref, sg_ref, o_ref, qs_ref, m_ref, acc_ref = refs
    else:
        q_ref, k_ref, v_ref, o_ref, qs_ref, m_ref, acc_ref = refs
    kv = pl.program_id(3)

    @pl.when(kv == 0)
    def _init():
        q = q_ref[0]
        if mode == "diff":
            lane = lax.broadcasted_iota(jnp.int32, q.shape, 1)
            zero = jnp.zeros_like(q)
            qs_ref[0:tq] = jnp.where(lane < C_HEAD_DIM, q, zero)
            qs_ref[tq:2 * tq] = jnp.where(lane >= C_HEAD_DIM, q, zero)
        else:
            for g in range(group):
                qs_ref[g * tq:(g + 1) * tq] = q[:, g * LANES:(g + 1) * LANES]
        m_ref[...] = jnp.full(m_ref.shape, -jnp.inf, F32)
        acc_ref[...] = jnp.zeros(acc_ref.shape, F32)

    k = k_ref[0]
    v = v_ref[0]
    v_aug = jnp.concatenate([v, jnp.ones_like(v)], axis=1)
    for c in range(group * tq // FLASH_SUB_ROWS):
        sl = slice(c * FLASH_SUB_ROWS, (c + 1) * FLASH_SUB_ROWS)
        s = lax.dot_general(qs_ref[sl], k, (((1,), (1,)), ((), ())),
                            preferred_element_type=F32)
        m_prev = m_ref[sl]
        m_next = jnp.maximum(m_prev, jnp.max(s, axis=-1, keepdims=True))
        alpha = jnp.exp2(m_prev - m_next)
        p = jnp.exp2(s - jnp.tile(m_next, (1, tk // LANES)))
        acc_ref[sl] = jnp.tile(alpha, (1, 2)) * acc_ref[sl] + jnp.dot(
            p.astype(BF16), v_aug, preferred_element_type=F32)
        m_ref[sl] = m_next

    @pl.when(kv == n_kv - 1)
    def _fin():
        acc = acc_ref[...]
        o = acc[:, :LANES] / acc[:, LANES:]
        if mode == "diff":
            lv = lam_ref[...]
            lam = (jnp.exp(jnp.sum(lv[0:1] * lv[1:2], axis=-1, keepdims=True))
                   - jnp.exp(jnp.sum(lv[2:3] * lv[3:4], axis=-1, keepdims=True)) + lambda_init)
            dlt = o[0:tq] - lam * o[tq:2 * tq]
            o_ref[0] = (_rms(dlt, sg_ref[...]) * (1.0 - lambda_init)).astype(o_ref.dtype)
        else:
            for g in range(group):
                o_ref[0, :, g * LANES:(g + 1) * LANES] = o[g * tq:(g + 1) * tq].astype(o_ref.dtype)


def _kv_tile(s):
    for t in (2816, 2048, 1024, 768, 512, 256, 128):
        if s % t == 0:
            return t
    return s


def flash_attention(q, k, v, *, mode, lam_vecs=None, subln_g=None, lambda_init=0.0, kv_start=0):
    b, n, qw = q.shape
    s = k.shape[1] - kv_start
    if mode == "diff":
        group, n_groups, qblk = 2, qw // LANES, LANES
    else:
        group = A_HEADS // A_KV_HEADS
        n_groups, qblk = A_KV_HEADS, group * LANES
    tq = min(n, FLASH_ROWS // group)
    tk = _kv_tile(s)
    n_kv = s // tk
    j0 = kv_start // tk
    assert j0 * tk == kv_start
    in_specs = [pl.BlockSpec((1, tq, qblk), lambda bi, g, i, j: (bi, i, g)),
                pl.BlockSpec((1, tk, LANES), lambda bi, g, i, j: (bi, j0 + j, g)),
                pl.BlockSpec((1, tk, LANES), lambda bi, g, i, j: (bi, j0 + j, g))]
    args = [q, k, v]
    if mode == "diff":
        in_specs += [pl.BlockSpec(lam_vecs.shape, lambda bi, g, i, j: (0, 0)),
                     pl.BlockSpec((1, LANES), lambda bi, g, i, j: (0, 0))]
        args += [lam_vecs, subln_g.reshape(1, LANES)]
    return pl.pallas_call(
        functools.partial(_flash_kernel, mode=mode, group=group, tq=tq, tk=tk, n_kv=n_kv,
                          lambda_init=lambda_init),
        grid=(b, n_groups, n // tq, n_kv),
        in_specs=in_specs,
        out_specs=pl.BlockSpec((1, tq, qblk), lambda bi, g, i, j: (bi, i, g)),
        out_shape=jax.ShapeDtypeStruct((b, n, qw), BF16),
        scratch_shapes=[pltpu.VMEM((group * tq, LANES), BF16),
                        pltpu.VMEM((group * tq, LANES), F32),
                        pltpu.VMEM((group * tq, 2 * LANES), F32)],
        compiler_params=_cparams("arbitrary", "arbitrary", "arbitrary", "arbitrary"),
        name="flash_" + mode,
    )(*args)


def _outproj_kernel(*refs, n_lhs):
    lhs = refs[:n_lhs]
    ws = refs[n_lhs:2 * n_lhs]
    x_ref, g_ref, gate_ref, o_ref = refs[2 * n_lhs:]
    y = jnp.dot(lhs[0][0], ws[0][...], preferred_element_type=F32)
    for a, w in zip(lhs[1:], ws[1:]):
        y = y + jnp.dot(a[0], w[...], preferred_element_type=F32)
    o_ref[0] = x_ref[0] + gate_ref[0] * _rms(y, g_ref[...])


def outproj_residual(lhs_list, w_list, x, g, gate):
    b, n, d = x.shape
    tm = min(n, 512)
    n_lhs = len(lhs_list)
    in_specs = [pl.BlockSpec((1, tm, a.shape[2]), lambda bi, i: (bi, i, 0)) for a in lhs_list]
    in_specs += [pl.BlockSpec(w.shape, lambda bi, i: (0, 0)) for w in w_list]
    in_specs += [pl.BlockSpec((1, tm, d), lambda bi, i: (bi, i, 0)),
                 pl.BlockSpec((1, d), lambda bi, i: (0, 0)),
                 pl.BlockSpec((1, 1, d), lambda bi, i: (bi, 0, 0))]
    return pl.pallas_call(
        functools.partial(_outproj_kernel, n_lhs=n_lhs),
        grid=(b, n // tm),
        in_specs=in_specs,
        out_specs=pl.BlockSpec((1, tm, d), lambda bi, i: (bi, i, 0)),
        out_shape=jax.ShapeDtypeStruct((b, n, d), F32),
        compiler_params=_cparams("arbitrary", "arbitrary"),
        name="outproj",
    )(*lhs_list, *w_list, x, g.reshape(1, d), gate.reshape(b, 1, d))


def _ffn_kernel(xm_ref, xp_ref, xn_ref, gn_ref, sc_ref, sh_ref, wa_ref, wg_ref, cwa_ref, cwg_ref,
                cba_ref, cbg_ref, wd_ref, g3_ref, gate_ref, o_ref, hx_ref,
                *, tm, halo, n_tiles, n_f):
    i = pl.program_id(1)
    j = pl.program_id(2)

    @pl.when(j == 0)
    def _prologue():
        def nm(x):
            return (_rms(x, gn_ref[...]) * (1.0 + sc_ref[0]) + sh_ref[0]).astype(BF16)
        hx_ref[halo:halo + tm] = nm(xm_ref[0])
        zero = jnp.zeros((halo, xm_ref.shape[2]), BF16)
        hx_ref[0:halo] = jnp.where(i > 0, nm(xp_ref[0]), zero)
        hx_ref[halo + tm:] = jnp.where(i < n_tiles - 1, nm(xn_ref[0]), zero)
        o_ref[0] = jnp.zeros(o_ref.shape[1:], F32)

    hx = hx_ref[...]
    rows = tm + 2 * halo

    def conv(u, cw, cb):
        up = pltpu.roll(u, 1, 0)
        dn = pltpu.roll(u, rows - 1, 0)
        return (cw[0:1] * up + cw[1:2] * u + cw[2:3] * dn + cb)[halo:halo + tm]

    a = conv(jnp.dot(hx, wa_ref[...], preferred_element_type=F32), cwa_ref[...], cba_ref[...])
    g = conv(jnp.dot(hx, wg_ref[...], preferred_element_type=F32), cwg_ref[...], cbg_ref[...])
    act = (a * (g * jax.nn.sigmoid(g))).astype(BF16)
    o_ref[0] += jnp.dot(act, wd_ref[...], preferred_element_type=F32)

    @pl.when(j == n_f - 1)
    def _epilogue():
        o_ref[0] = xm_ref[0] + gate_ref[0] * _rms(o_ref[0], g3_ref[...])


def conv_ffn_residual(x, gn, sc, sh, w_up, conv_w, conv_b, w_down, g3, gate, layer):
    b, n, d = x.shape
    dff = w_down.shape[1]
    tm = min(n, 512)
    tf = 512
    halo = BF16_SUBLANES
    nt = n // tm
    n_f = dff // tf
    row = lambda bi, i, j: (bi, i, 0)
    vec = lambda bi, i, j: (0, 0)
    bvec = lambda bi, i, j: (bi, 0, 0)
    up_a = lambda bi, i, j: (0, j)
    up_g = lambda bi, i, j: (0, n_f + j)
    wup_a = lambda bi, i, j: (layer, 0, j)
    wup_g = lambda bi, i, j: (layer, 0, n_f + j)
    conv_b = conv_b.reshape(1, 2 * dff)
    return pl.pallas_call(
        functools.partial(_ffn_kernel, tm=tm, halo=halo, n_tiles=nt, n_f=n_f),
        grid=(b, nt, n_f),
        in_specs=[pl.BlockSpec((1, tm, d), row),
                  pl.BlockSpec((1, halo, d),
                               lambda bi, i, j: (bi, jnp.maximum(i * (tm // halo) - 1, 0), 0)),
                  pl.BlockSpec((1, halo, d),
                               lambda bi, i, j: (bi, jnp.minimum((i + 1) * (tm // halo), n // halo - 1), 0)),
                  pl.BlockSpec((1, d), vec),
                  pl.BlockSpec((1, 1, d), bvec),
                  pl.BlockSpec((1, 1, d), bvec),
                  pl.BlockSpec((None, d, tf), wup_a),
                  pl.BlockSpec((None, d, tf), wup_g),
                  pl.BlockSpec((3, tf), up_a),
                  pl.BlockSpec((3, tf), up_g),
                  pl.BlockSpec((1, tf), up_a),
                  pl.BlockSpec((1, tf), up_g),
                  pl.BlockSpec((None, tf, d), lambda bi, i, j: (layer, j, 0)),
                  pl.BlockSpec((1, d), vec),
                  pl.BlockSpec((1, 1, d), bvec)],
        out_specs=pl.BlockSpec((1, tm, d), row),
        out_shape=jax.ShapeDtypeStruct((b, n, d), F32),
        scratch_shapes=[pltpu.VMEM((tm + 2 * halo, d), BF16)],
        compiler_params=_cparams("arbitrary", "arbitrary", "arbitrary"),
        name="conv_ffn",
    )(x, x, x, gn.reshape(1, d), sc.reshape(b, 1, d), sh.reshape(b, 1, d), w_up, w_up,
      conv_w, conv_w, conv_b, conv_b, w_down, g3.reshape(1, d), gate.reshape(b, 1, d))


def _filter_kernel(bands_ref, w1_ref, b1_ref, w2_ref, b2_ref, w3_ref, fr_ref, dl_ref,
                   o_ref, s_ref, *, n_tok, tr):
    i = pl.program_id(0)

    @pl.when(i == 0)
    def _init():
        s_ref[...] = jnp.zeros(s_ref.shape, F32)

    r = (i * tr + lax.broadcasted_iota(jnp.int32, (tr, 1), 0)).astype(F32)
    t = jnp.where(r < n_tok, r, 2.0 * n_tok - r)
    t_norm = t / float(max(n_tok - 1, 1))
    wang = (2.0 * math.pi / n_tok) * t
    z = wang * bands_ref[...]
    lane = lax.broadcasted_iota(jnp.int32, z.shape, 1)
    feats = jnp.where(lane == 0, t_norm,
                      jnp.where(lane <= HY_BANDS, jnp.cos(z),
                                jnp.where(lane <= 2 * HY_BANDS, -jnp.sin(z), 0.0)))
    fr = fr_ref[...]
    h = jnp.sin(fr[0:1] * (jnp.dot(feats, w1_ref[...], precision=HIGHEST,
                                   preferred_element_type=F32) + b1_ref[...]))
    h = jnp.sin(fr[1:2] * (jnp.dot(h, w2_ref[...], precision=HIGHEST,
                                   preferred_element_type=F32) + b2_ref[...]))
    hb = h.astype(BF16)
    decay = jnp.where(r == n_tok, 0.0, jnp.exp(-t_norm * dl_ref[...]))
    for o in range(HY_ORDER):
        out = jnp.dot(hb, w3_ref[o].astype(BF16), preferred_element_type=F32) * decay
        o_ref[o] = out
        s_ref[o] += jnp.sum(jnp.abs(out), axis=0, keepdims=True)


def hyena_filter_time(n_tok, w1, b1, w2, b2, w3, freq):
    ch = HY_CH
    ll = 2 * n_tok
    tr = min(n_tok, 512)
    per_half = n_tok // tr
    lane = jnp.arange(LANES)
    bands = jnp.linspace(1e-4, HY_BANDS - 1, HY_BANDS, dtype=F32)
    bands_l = jnp.where((lane >= 1) & (lane <= HY_BANDS), bands[jnp.clip(lane - 1, 0, HY_BANDS - 1)],
                        jnp.where((lane > HY_BANDS) & (lane <= 2 * HY_BANDS),
                                  bands[jnp.clip(lane - 1 - HY_BANDS, 0, HY_BANDS - 1)], 0.0))
    w1p = jnp.zeros((LANES, HY_FILTER_W), F32).at[:w1.shape[0]].set(w1)
    deltas = jnp.abs(jnp.linspace(math.log(HY_TARGET) / HY_FAST, math.log(HY_TARGET) / HY_SLOW,
                                  ch, dtype=F32))
    w3o = jnp.swapaxes(w3.reshape(HY_FILTER_W, HY_ORDER, 2 * ch), 0, 1)
    const = lambda i: (0, 0)
    return pl.pallas_call(
        functools.partial(_filter_kernel, n_tok=n_tok, tr=tr),
        grid=(ll // tr,),
        in_specs=[pl.BlockSpec((1, LANES), const),
                  pl.BlockSpec((LANES, HY_FILTER_W), const),
                  pl.BlockSpec((1, HY_FILTER_W), const),
                  pl.BlockSpec((HY_FILTER_W, HY_FILTER_W), const),
                  pl.BlockSpec((1, HY_FILTER_W), const),
                  pl.BlockSpec((HY_ORDER, HY_FILTER_W, ch), lambda i: (0, 0, i // per_half)),
                  pl.BlockSpec((2, HY_FILTER_W), const),
                  pl.BlockSpec((1, ch), const)],
        out_specs=[pl.BlockSpec((HY_ORDER, tr, ch), lambda i: (0, i, 0)),
                   pl.BlockSpec((HY_ORDER, 1, ch), lambda i: (0, 0, 0))],
        out_shape=[jax.ShapeDtypeStruct((HY_ORDER, ll, ch), F32),
                   jax.ShapeDtypeStruct((HY_ORDER, 1, ch), F32)],
        compiler_params=_cparams("arbitrary"),
        name="hyena_filter",
    )(bands_l.reshape(1, LANES), w1p, b1.reshape(1, -1), w2, b2.reshape(1, -1), w3o, freq,
      deltas.reshape(1, ch))


def _dft_dot(f, x):
    return jnp.dot(f.astype(BF16), x.astype(BF16), preferred_element_type=F32)


def _stage_a_kernel(f_ref, x_ref, o_ref):
    parts = x_ref.shape[0]
    x = x_ref[0] if parts == 1 else jnp.concatenate([x_ref[p] for p in range(parts)], axis=0)
    y = _dft_dot(f_ref[...], x).astype(o_ref.dtype)
    half = y.shape[0] // 2
    o_ref[0] = y[:half]
    o_ref[1] = y[half:]


def dft_stage_a(f, x, g):
    _, p, rows, w = x.shape
    k1 = p * rows
    tl = min(w, 4096)
    return pl.pallas_call(
        _stage_a_kernel,
        grid=(w // tl,),
        in_specs=[pl.BlockSpec(f.shape, lambda j: (0, 0)),
                  pl.BlockSpec((None, p, rows, tl), lambda j: (g, 0, 0, j))],
        out_specs=pl.BlockSpec((2, k1, tl), lambda j: (0, 0, j)),
        out_shape=jax.ShapeDtypeStruct((2, k1, w), BF16),
        compiler_params=_cparams("arbitrary"),
        name="dft_stage_a",
    )(f, x)


def _stage_c_filter_kernel(a_ref, g_ref, sc_ref, o_ref, *, kc):
    for k in range(kc):
        r = jnp.concatenate([a_ref[0, k], a_ref[1, k]], axis=0)
        y = (_dft_dot(g_ref[k], r) * sc_ref[...]).astype(o_ref.dtype)
        half = y.shape[0] // 2
        o_ref[0, k] = y[:half]
        o_ref[1, k] = y[half:]


def dft_stage_c_filter(a, g, scale):
    _, k1, r, c = a.shape
    kc = 8
    tc = 256
    blk = pl.BlockSpec((2, kc, r, tc), lambda i, j: (0, i, 0, j))
    return pl.pallas_call(
        functools.partial(_stage_c_filter_kernel, kc=kc),
        grid=(k1 // kc, c // tc),
        in_specs=[blk,
                  pl.BlockSpec((kc, 2 * r, 2 * r), lambda i, j: (i, 0, 0)),
                  pl.BlockSpec((1, tc), lambda i, j: (0, j))],
        out_specs=blk,
        out_shape=jax.ShapeDtypeStruct(a.shape, BF16),
        compiler_params=_cparams("arbitrary", "arbitrary"),
        name="dft_stage_c_filter",
    )(a, g, scale)


def _stage_c_conv_kernel(a_ref, h_ref, g_ref, gi_ref, o_ref, *, kc):
    for k in range(kc):
        r = jnp.concatenate([a_ref[0, k], a_ref[1, k]], axis=0)
        y = _dft_dot(g_ref[k], r)
        half = y.shape[0] // 2
        yre, yim = y[:half], y[half:]
        hre, him = h_ref[0, k].astype(F32), h_ref[1, k].astype(F32)
        p = jnp.concatenate([yre * hre - yim * him, yre * him + yim * hre], axis=0)
        d = _dft_dot(gi_ref[k], p).astype(o_ref.dtype)
        o_ref[0, k] = d[:half]
        o_ref[1, k] = d[half:]


def dft_stage_c_conv(a, hspec, g, gi):
    _, k1, r, c = a.shape
    kc = 8
    tc = 256
    blk = pl.BlockSpec((2, kc, r, tc), lambda i, j: (0, i, 0, j))
    mat = pl.BlockSpec((kc, 2 * r, 2 * r), lambda i, j: (i, 0, 0))
    return pl.pallas_call(
        functools.partial(_stage_c_conv_kernel, kc=kc),
        grid=(k1 // kc, c // tc),
        in_specs=[blk, blk, mat, mat],
        out_specs=blk,
        out_shape=jax.ShapeDtypeStruct(a.shape, BF16),
        compiler_params=_cparams("arbitrary", "arbitrary"),
        name="dft_stage_c_conv",
    )(a, hspec, g, gi)


def _stage_a_inv_kernel(f_ref, d_ref, u_ref, gt_ref, sk_ref, o_ref):
    d = jnp.concatenate([d_ref[0], d_ref[1]], axis=0)
    y = _dft_dot(f_ref[...], d)
    half = y.shape[0] // 2
    sk = sk_ref[...]
    o_ref[0] = (gt_ref[0] * (y[:half] + u_ref[0] * sk)).astype(o_ref.dtype)
    o_ref[1] = (gt_ref[1] * (y[half:] + u_ref[1] * sk)).astype(o_ref.dtype)


def dft_stage_a_inv(f, d, u, gu, gate, gg, skip_flat, out_dtype):
    _, k1, w = d.shape
    half = k1 // 2
    tl = skip_flat.shape[1]
    io = pl.BlockSpec((2, half, tl), lambda j: (0, 0, j))
    return pl.pallas_call(
        _stage_a_inv_kernel,
        grid=(w // tl,),
        in_specs=[pl.BlockSpec(f.shape, lambda j: (0, 0)),
                  pl.BlockSpec((2, k1, tl), lambda j: (0, 0, j)),
                  pl.BlockSpec((None, 2, half, tl), lambda j: (gu, 0, 0, j)),
                  pl.BlockSpec((None, 2, half, tl), lambda j: (gg, 0, 0, j)),
                  pl.BlockSpec((1, tl), lambda j: (0, 0))],
        out_specs=io,
        out_shape=jax.ShapeDtypeStruct((2, half, w), out_dtype),
        compiler_params=_cparams("arbitrary"),
        name="dft_stage_a_inv",
    )(f, d, u, gate, skip_flat)


def _dft_tables(n_tok):
    ll = 2 * n_tok
    r = DFT_RADIX
    k1n = ll // r
    half = k1n // 2
    two_pi = 2.0 * math.pi

    def cs(m, period):
        ang = (m % period).astype(F32) * (two_pi / period)
        return jnp.cos(ang), jnp.sin(ang)

    k1 = jnp.arange(k1n, dtype=jnp.int32)
    c, s = cs(k1[:, None] * k1[None, :half], k1n)
    fa_data = jnp.concatenate([jnp.concatenate([c, s], 1), jnp.concatenate([-s, c], 1)], 0)
    c, s = cs(k1[:, None] * k1[None, :], k1n)
    fa_filt = jnp.concatenate([c, -s], 0)
    c, s = cs(k1[:half, None] * k1[None, :], k1n)
    fa_inv = jnp.concatenate([jnp.concatenate([c, -s], 1), jnp.concatenate([s, c], 1)], 0)
    idx = jnp.arange(r, dtype=jnp.int32)
    ca, sa = cs(idx[:, None] * idx[None, :], r)
    cb, sb = cs(k1[:, None] * idx[None, :], ll)
    c = ca[None] * cb[:, None, :] - sa[None] * sb[:, None, :]
    s = sa[None] * cb[:, None, :] + ca[None] * sb[:, None, :]
    g_fwd = jnp.concatenate([jnp.concatenate([c, s], 2), jnp.concatenate([-s, c], 2)], 1)
    ct, st = jnp.swapaxes(c, 1, 2), jnp.swapaxes(s, 1, 2)
    g_inv = jnp.concatenate([jnp.concatenate([ct, -st], 2), jnp.concatenate([st, ct], 2)], 1)
    return tuple(t.astype(BF16) for t in (fa_data, fa_filt, fa_inv, g_fwd, g_inv))


def hyena_long(pc, filt, l1, skip, out_dtype):
    _, b, n, c = pc.shape
    assert b == 2, "the two batch entries ride as real/imaginary parts"
    ll = 2 * n
    r = DFT_RADIX
    k1n = ll // r
    half = k1n // 2
    w = r * c
    fa_data, fa_filt, fa_inv, g_fwd, g_inv = _dft_tables(n)
    tl = 4096
    pcf = pc.reshape(HY_ORDER + 1, 2, half, w)
    filtf = filt.reshape(HY_ORDER, 1, k1n, w)
    u, gu = pcf, 0
    for o in range(HY_ORDER):
        af = dft_stage_a(fa_filt, filtf, o)
        hspec = dft_stage_c_filter(af.reshape(2, k1n, r, c), g_fwd, 1.0 / (l1[o] * ll))
        a = dft_stage_a(fa_data, u, gu)
        d = dft_stage_c_conv(a.reshape(2, k1n, r, c), hspec, g_fwd, g_inv)
        last = o == HY_ORDER - 1
        z = dft_stage_a_inv(fa_inv, d.reshape(2, k1n, w), u, gu, pcf, o + 1,
                            jnp.tile(skip[o], tl // c).reshape(1, tl),
                            out_dtype if last else F32)
        u, gu = z[None], 0
    return z.reshape(b, n, c)


def _hyena_short_kernel(ff_ref, fr_ref, fi_ref, pc_ref, filt_ref, l1_ref, skip_ref, o_ref, *, n, ll):
    u = [pc_ref[0, 0], pc_ref[0, 1]]
    for o in range(HY_ORDER):
        hs = _dft_dot(fr_ref[...], filt_ref[o]) * (1.0 / (l1_ref[o] * ll))
        y = _dft_dot(ff_ref[...], jnp.concatenate(u, axis=0))
        yre, yim, hre, him = y[:ll], y[ll:], hs[:ll], hs[ll:]
        p = jnp.concatenate([yre * hre - yim * him, yre * him + yim * hre], axis=0)
        d = _dft_dot(fi_ref[...], p)
        sk = skip_ref[o:o + 1]
        u = [pc_ref[o + 1, bi] * (d[bi * n:(bi + 1) * n] + u[bi] * sk) for bi in range(2)]
    o_ref[0] = u[0].astype(o_ref.dtype)
    o_ref[1] = u[1].astype(o_ref.dtype)


def hyena_short(pc, filt, l1, skip, out_dtype):
    _, b, n, c = pc.shape
    assert b == 2
    ll = 2 * n
    two_pi = 2.0 * math.pi
    kk = jnp.arange(ll, dtype=jnp.int32)
    ang = ((kk[:, None] * kk[None, :]) % ll).astype(F32) * (two_pi / ll)
    cf, sf = jnp.cos(ang), jnp.sin(ang)
    f_real = jnp.concatenate([cf, -sf], 0)
    cn, sn = cf[:, :n], sf[:, :n]
    f_fwd = jnp.concatenate([jnp.concatenate([cn, sn], 1), jnp.concatenate([-sn, cn], 1)], 0)
    ci, si = cf[:n, :], sf[:n, :]
    f_inv = jnp.concatenate([jnp.concatenate([ci, -si], 1), jnp.concatenate([si, ci], 1)], 0)
    tc = 256
    const = lambda j: (0, 0)
    return pl.pallas_call(
        functools.partial(_hyena_short_kernel, n=n, ll=ll),
        grid=(c // tc,),
        in_specs=[pl.BlockSpec(f_fwd.shape, const),
                  pl.BlockSpec(f_real.shape, const),
                  pl.BlockSpec(f_inv.shape, const),
                  pl.BlockSpec((3, 2, n, tc), lambda j: (0, 0, 0, j)),
                  pl.BlockSpec((HY_ORDER, ll, tc), lambda j: (0, 0, j)),
                  pl.BlockSpec((HY_ORDER, 1, tc), lambda j: (0, 0, j)),
                  pl.BlockSpec((HY_ORDER, tc), lambda j: (0, j))],
        out_specs=pl.BlockSpec((2, n, tc), lambda j: (0, 0, j)),
        out_shape=jax.ShapeDtypeStruct((2, n, c), out_dtype),
        compiler_params=_cparams("arbitrary"),
        name="hyena_short",
    )(f_fwd, f_real, f_inv, pc, filt, l1, skip)


def hyena_mixer(pc, w1, b1, w2, b2, w3, freq, skip, out_dtype):
    n = pc.shape[2]
    filt, l1 = hyena_filter_time(n, w1, b1, w2, b2, w3, freq)
    if (2 * n) % (DFT_RADIX * BF16_SUBLANES) == 0 and n >= 1024:
        return hyena_long(pc, filt, l1, skip, out_dtype)
    return hyena_short(pc, filt, l1, skip, out_dtype)


def _deinterleave_cols(w, hd):
    lead = w.shape[:-1]
    wh = w.reshape(lead + (w.shape[-1] // hd, hd // 2, 2))
    return jnp.concatenate([wh[..., 0], wh[..., 1]], axis=-1).reshape(w.shape)


def _rope_tables(n_tok, rot_dim):
    rows = n_tok // GRID_W
    row = jnp.broadcast_to(jnp.arange(rows, dtype=jnp.int32)[:, None], (rows, GRID_W)).reshape(n_tok)
    col = jnp.broadcast_to(jnp.arange(GRID_W, dtype=jnp.int32)[None, :], (rows, GRID_W)).reshape(n_tok)
    axis_dim = rot_dim // 2
    inv_freq = ROPE_THETA ** (-jnp.arange(0, axis_dim, 2, dtype=F32) / axis_dim)
    ang = jnp.concatenate([row.astype(F32)[:, None] * inv_freq,
                           col.astype(F32)[:, None] * inv_freq], axis=-1)
    c, s = jnp.cos(ang), jnp.sin(ang)
    reps = LANES // rot_dim
    return jnp.tile(jnp.concatenate([c, c], -1), (1, reps)), jnp.tile(jnp.concatenate([-s, s], -1), (1, reps))


def _mixer_ab(h, hc, with_ctx, w_in, w_out, qk_g, hy):
    b, n, _ = h.shape
    conv_w, conv_b, w1, b1, w2, b2, w3, freq, skip = hy
    scale = A_HEAD_DIM ** -0.5 * LOG2E
    wq = _deinterleave_cols(w_in[:, :A_Q_W], A_HEAD_DIM).astype(BF16)
    wk = _deinterleave_cols(w_in[:, A_Q_W:A_Q_W + A_KV_W], A_HEAD_DIM).astype(BF16)
    wv = w_in[:, A_Q_W + A_KV_W:A_QKV_W].astype(BF16)
    wh = w_in[:, A_QKV_W:].astype(BF16)
    gq = jnp.tile(_deinterleave_cols(qk_g[0], A_HEAD_DIM) * scale, A_HEADS)
    gk = jnp.tile(_deinterleave_cols(qk_g[1], A_HEAD_DIM), A_KV_HEADS)
    ones_v = jnp.ones((A_KV_W,), F32)
    cosf, sinf = _rope_tables(n, A_HEAD_DIM)

    n_ctx = hc.shape[1]
    q = inproj(h, wq, gq, cosf, sinf, norm=True)
    k_all = inproj(hc, wk, gk, norm=True,
                   append_to=inproj(h, wk, gk, cosf, sinf, norm=True, extra_rows=n_ctx))
    v_all = inproj(hc, wv, ones_v, append_to=inproj(h, wv, ones_v, extra_rows=n_ctx))
    o_att = flash_attention(q, k_all, v_all, mode="gqa")
    pc = inproj_conv(h, wh, conv_w, conv_b, HY_ORDER + 1)
    o_hy = hyena_mixer(pc, w1, b1, w2, b2, w3, freq, skip, BF16)
    wo = w_out.astype(BF16)
    lat = ([o_att, o_hy], [wo[:A_Q_W], wo[A_Q_W:]])
    if not with_ctx:
        return lat, None
    q_c = inproj(hc, wq, gq, norm=True)
    o_att_c = flash_attention(q_c, k_all, v_all, mode="gqa", kv_start=n)
    pc_c = inproj_conv(hc, wh, conv_w, conv_b, HY_ORDER + 1)
    o_hy_c = hyena_mixer(pc_c, w1, b1, w2, b2, w3, freq, skip, BF16)
    return lat, ([o_att_c, o_hy_c], [wo[:A_Q_W], wo[A_Q_W:]])


def _mixer_c(h, hc, with_ctx, lambda_init, w_in, w_out, lam_vecs, subln_g):
    b, n, _ = h.shape
    scale = C_HEAD_DIM ** -0.5 * LOG2E
    wq = _deinterleave_cols(w_in[:, :C_Q_W], C_HEAD_DIM).astype(BF16)
    wk = _deinterleave_cols(w_in[:, C_Q_W:2 * C_Q_W], C_HEAD_DIM).astype(BF16)
    wv = w_in[:, 2 * C_Q_W:].astype(BF16)
    gq = jnp.full((C_Q_W,), scale, F32)
    ones = jnp.ones((C_Q_W,), F32)
    cosf, sinf = _rope_tables(n, C_HEAD_DIM)

    n_ctx = hc.shape[1]
    q = inproj(h, wq, gq, cosf, sinf, hd=C_HEAD_DIM)
    k_all = inproj(hc, wk, ones,
                   append_to=inproj(h, wk, ones, cosf, sinf, hd=C_HEAD_DIM, extra_rows=n_ctx))
    v_all = inproj(hc, wv, ones, append_to=inproj(h, wv, ones, extra_rows=n_ctx))
    attn = functools.partial(flash_attention, mode="diff", lam_vecs=lam_vecs, subln_g=subln_g,
                             lambda_init=lambda_init)
    wo = w_out.astype(BF16)
    lat = ([attn(q, k_all, v_all)], [wo])
    if not with_ctx:
        return lat, None
    q_c = inproj(hc, wq, gq)
    return lat, ([attn(q_c, k_all, v_all, kv_start=n)], [wo])


def kernel(x, c, ctx, c_ctx, ada_w, ada_b, norm_g, ab_w_in, ab_w_out, ab_qk_g,
           hy_conv_w, hy_conv_b, hy_w1, hy_b1, hy_w2, hy_b2, hy_w3, hy_freq, hy_skip,
           dc_w_in, dc_w_out, dc_lambda, dc_subln_g,
           ffn_w_up, ffn_conv_w, ffn_conv_b, ffn_w_down):
    depth = ada_w.shape[0]
    b, n, d = x.shape
    pad_rows = (-(b + 1)) % 8
    cvec = jnp.concatenate([c, c_ctx[None, :], jnp.zeros((pad_rows, d), F32)], axis=0)
    mod_all = ada_mod(cvec, ada_w, ada_b)
    w_up = ffn_w_up.astype(BF16)
    w_down = ffn_w_down.astype(BF16)
    for i in range(depth):
        j = i // 2
        with_ctx = i < depth - 1
        mod = mod_all[i, :b].reshape(b, 6, d)
        mod_c = jnp.broadcast_to(mod_all[i, b].reshape(1, 6, d), (b, 6, d))
        sh1, sc1, g1, sh2, sc2, g2 = (mod[:, t] for t in range(6))
        sh1c, sc1c, g1c, sh2c, sc2c, g2c = (mod_c[:, t] for t in range(6))
        h = norm_mod(x, norm_g[i, 0], sc1, sh1)
        hc = norm_mod(ctx, norm_g[i, 0], sc1c, sh1c)
        if i % 2 == 0:
            hy = (hy_conv_w[j], hy_conv_b[j], hy_w1[j], hy_b1[j], hy_w2[j], hy_b2[j],
                  hy_w3[j], hy_freq[j], hy_skip[j])
            lat, cx = _mixer_ab(h, hc, with_ctx, ab_w_in[j], ab_w_out[j], ab_qk_g[j], hy)
        else:
            lambda_init = 0.8 - 0.6 * math.exp(-0.3 * i)
            lat, cx = _mixer_c(h, hc, with_ctx, lambda_init, dc_w_in[j], dc_w_out[j],
                               dc_lambda[j], dc_subln_g[j])
        x = outproj_residual(lat[0], lat[1], x, norm_g[i, 1], g1)
        x = conv_ffn_residual(x, norm_g[i, 2], sc2, sh2, w_up, ffn_conv_w[i], ffn_conv_b[i],
                              w_down, norm_g[i, 3], g2, i)
        if with_ctx:
            ctx = outproj_residual(cx[0], cx[1], ctx, norm_g[i, 1], g1c)
            ctx = conv_ffn_residual(ctx, norm_g[i, 2], sc2c, sh2c, w_up, ffn_conv_w[i],
                                    ffn_conv_b[i], w_down, norm_g[i, 3], g2c, i)
    return x
```

```python
import functools
import math

import jax
import jax.numpy as jnp
from jax import lax
from jax.experimental import pallas as pl
from jax.experimental.pallas import tpu as pltpu

F32 = jnp.float32
BF16 = jnp.bfloat16
HIGHEST = lax.Precision.HIGHEST

D_MODEL = 2048
GRID_W = 64
EPS = 1e-6
ROPE_THETA = 10000.0
A_HEADS, A_KV_HEADS, A_HEAD_DIM = 8, 2, 128
A_Q_W = A_HEADS * A_HEAD_DIM
A_KV_W = A_KV_HEADS * A_HEAD_DIM
A_QKV_W = A_Q_W + 2 * A_KV_W
HY_CH = D_MODEL // 2
HY_ORDER = 2
HY_FILTER_W = 64
HY_BANDS = 16
HY_FAST, HY_SLOW, HY_TARGET = 0.3, 1.5, 1e-2
C_HEADS, C_HEAD_DIM = 16, 64
C_Q_W = C_HEADS * 2 * C_HEAD_DIM
D_FF = 5632

LANES = 128
BF16_SUBLANES = 16
VMEM_LIMIT = 56 * 1024 * 1024
DFT_RADIX = 128
FLASH_ROWS = 2048
FLASH_SUB_ROWS = 128
LOG2E = math.log2(math.e)


def _cparams(*sem):
    return pltpu.CompilerParams(dimension_semantics=sem, vmem_limit_bytes=VMEM_LIMIT)


def _rms(x, g):
    return x * lax.rsqrt(jnp.mean(x * x, axis=-1, keepdims=True) + EPS) * g


def _ada_kernel(c_ref, w_ref, b_ref, o_ref):
    c = c_ref[...]
    s = c * jax.nn.sigmoid(c)
    rows = s.shape[0]
    s_hi = s.astype(BF16).astype(F32)
    s2 = jnp.concatenate([s_hi, s - s_hi], axis=0).astype(BF16)
    w = w_ref[0]
    w_hi = w.astype(BF16)
    w_lo = (w - w_hi.astype(F32)).astype(BF16)
    both = jnp.dot(s2, w_hi, preferred_element_type=F32)
    hi_lo = jnp.dot(s2, w_lo, preferred_element_type=F32)[:rows]
    o_ref[0] = both[:rows] + both[rows:] + hi_lo + b_ref[0]


def ada_mod(cvec, ada_w, ada_b):
    depth, d, n6 = ada_w.shape
    rows = cvec.shape[0]
    tn = 1024
    return pl.pallas_call(
        _ada_kernel,
        grid=(depth, n6 // tn),
        in_specs=[pl.BlockSpec((rows, d), lambda l, j: (0, 0)),
                  pl.BlockSpec((1, d, tn), lambda l, j: (l, 0, j)),
                  pl.BlockSpec((1, 1, tn), lambda l, j: (l, 0, j))],
        out_specs=pl.BlockSpec((1, rows, tn), lambda l, j: (l, 0, j)),
        out_shape=jax.ShapeDtypeStruct((depth, rows, n6), F32),
        compiler_params=_cparams("arbitrary", "arbitrary"),
        name="ada_mod",
    )(cvec, ada_w, ada_b.reshape(depth, 1, n6))


def _norm_mod_kernel(x_ref, g_ref, sc_ref, sh_ref, o_ref):
    y = _rms(x_ref[0], g_ref[...])
    o_ref[0] = (y * (1.0 + sc_ref[0]) + sh_ref[0]).astype(o_ref.dtype)


def norm_mod(x, g, sc, sh):
    b, n, d = x.shape
    tm = min(n, 512)
    return pl.pallas_call(
        _norm_mod_kernel,
        grid=(b, n // tm),
        in_specs=[pl.BlockSpec((1, tm, d), lambda bi, i: (bi, i, 0)),
                  pl.BlockSpec((1, d), lambda bi, i: (0, 0)),
                  pl.BlockSpec((1, 1, d), lambda bi, i: (bi, 0, 0)),
                  pl.BlockSpec((1, 1, d), lambda bi, i: (bi, 0, 0))],
        out_specs=pl.BlockSpec((1, tm, d), lambda bi, i: (bi, i, 0)),
        out_shape=jax.ShapeDtypeStruct((b, n, d), BF16),
        compiler_params=_cparams("arbitrary", "arbitrary"),
        name="norm_mod",
    )(x, g.reshape(1, d), sc.reshape(b, 1, d), sh.reshape(b, 1, d))


def _inproj_kernel(*refs, hd, norm, rope, appending, n_tiles, has_tail):
    o_ref = refs[-1]
    if appending:
        refs = refs[:-2]
    else:
        refs = refs[:-1]
    if rope:
        h_ref, w_ref, g_ref, cos_ref, sin_ref = refs
    else:
        h_ref, w_ref, g_ref = refs
    i = pl.program_id(1)

    def _tail():
        o_ref[0] = jnp.zeros(o_ref.shape[1:], o_ref.dtype)

    def _body():
        acc = jnp.dot(h_ref[0], w_ref[...], preferred_element_type=F32)
        tn = acc.shape[1]
        if rope:
            cosf = cos_ref[...]
            sinf = sin_ref[...]
            if hd != LANES:
                lane = lax.broadcasted_iota(jnp.int32, cosf.shape, 1)
                first_half = (lane % hd) < (hd // 2)
        for c in range(tn // LANES):
            x = acc[:, c * LANES:(c + 1) * LANES]
            if norm:
                x = x * lax.rsqrt(jnp.mean(x * x, axis=-1, keepdims=True) + EPS)
            x = x * g_ref[:, c * LANES:(c + 1) * LANES]
            if rope:
                if hd == LANES:
                    partner = pltpu.roll(x, LANES // 2, 1)
                else:
                    partner = jnp.where(first_half, pltpu.roll(x, LANES - hd // 2, 1),
                                        pltpu.roll(x, hd // 2, 1))
                x = x * cosf + partner * sinf
            o_ref[0, :, c * LANES:(c + 1) * LANES] = x.astype(o_ref.dtype)

    if has_tail:
        pl.when(i >= n_tiles)(_tail)
        pl.when(i < n_tiles)(_body)
    else:
        _body()


def inproj(h, w, gvec, cosf=None, sinf=None, *, hd=LANES, norm=False, out_dtype=BF16,
           tail_rows=0, append_to=None, append_at=0):
    b, n, d = h.shape
    ncol = w.shape[1]
    rope = cosf is not None
    tm = min(n, 512)
    tn = min(ncol, 2048)
    n_tiles = n // tm
    last = n_tiles - 1
    in_specs = [pl.BlockSpec((1, tm, d), lambda bi, i, j: (bi, jnp.minimum(i, last), 0)),
                pl.BlockSpec((d, tn), lambda bi, i, j: (0, j)),
                pl.BlockSpec((1, tn), lambda bi, i, j: (0, j))]
    args = [h, w, gvec.reshape(1, ncol)]
    if rope:
        in_specs += [pl.BlockSpec((tm, LANES), lambda bi, i, j: (jnp.minimum(i, last), 0)),
                     pl.BlockSpec((tm, LANES), lambda bi, i, j: (jnp.minimum(i, last), 0))]
        args += [cosf, sinf]
    appending = append_to is not None
    if appending:
        total = append_to.shape[1]
        first = append_at // tm
        assert first * tm == append_at and append_at + n <= total and append_to.dtype == out_dtype
        tail_tiles = 0
        in_specs.append(pl.BlockSpec(memory_space=pl.ANY))
        args.append(append_to)
        aliases = {len(args) - 1: 0}
    else:
        tail_tiles = -(-tail_rows // tm)
        total, first, aliases = (n_tiles + tail_tiles) * tm, 0, {}
    return pl.pallas_call(
        functools.partial(_inproj_kernel, hd=hd, norm=norm, rope=rope, appending=appending,
                          n_tiles=n_tiles, has_tail=tail_tiles > 0),
        grid=(b, n_tiles + tail_tiles, ncol // tn),
        in_specs=in_specs,
        out_specs=pl.BlockSpec((1, tm, tn), lambda bi, i, j: (bi, first + i, j)),
        out_shape=jax.ShapeDtypeStruct((b, total, ncol), out_dtype),
        input_output_aliases=aliases,
        compiler_params=_cparams("arbitrary", "arbitrary", "arbitrary"),
        name="inproj",
    )(*args)


def _inproj_conv_kernel(hm_ref, hp_ref, hn_ref, w_ref, cw_ref, cb_ref, o_ref, *, n_tiles):
    i = pl.program_id(2)
    hp, hm, hn = hp_ref[0], hm_ref[0], hn_ref[0]
    halo, tm = hp.shape[0], hm.shape[0]
    zero = jnp.zeros_like(hp)
    lhs = jnp.concatenate([jnp.where(i > 0, hp, zero), hm,
                           jnp.where(i < n_tiles - 1, hn, zero)], axis=0)
    u = jnp.dot(lhs, w_ref[...], preferred_element_type=F32)
    rows = tm + 2 * halo
    cw = cw_ref[...]
    conv = cw[0:1] * pltpu.roll(u, 1, 0) + cw[1:2] * u + cw[2:3] * pltpu.roll(u, rows - 1, 0)
    o_ref[0, 0] = conv[halo:halo + tm] + cb_ref[...]


def inproj_conv(h, w, conv_w, conv_b, groups):
    b, n, d = h.shape
    ncol = w.shape[1]
    c = ncol // groups
    tm = min(n, 512)
    tn = min(c, 1024)
    halo = BF16_SUBLANES
    nt = n // tm
    per = c // tn
    return pl.pallas_call(
        functools.partial(_inproj_conv_kernel, n_tiles=nt),
        grid=(ncol // tn, b, nt),
        in_specs=[pl.BlockSpec((1, tm, d), lambda j, bi, i: (bi, i, 0)),
                  pl.BlockSpec((1, halo, d),
                               lambda j, bi, i: (bi, jnp.maximum(i * (tm // halo) - 1, 0), 0)),
                  pl.BlockSpec((1, halo, d),
                               lambda j, bi, i: (bi, jnp.minimum((i + 1) * (tm // halo), n // halo - 1), 0)),
                  pl.BlockSpec((d, tn), lambda j, bi, i: (0, j)),
                  pl.BlockSpec((3, tn), lambda j, bi, i: (0, j)),
                  pl.BlockSpec((1, tn), lambda j, bi, i: (0, j))],
        out_specs=pl.BlockSpec((1, 1, tm, tn), lambda j, bi, i: (j // per, bi, i, j % per)),
        out_shape=jax.ShapeDtypeStruct((groups, b, n, c), F32),
        compiler_params=_cparams("arbitrary", "arbitrary", "arbitrary"),
        name="inproj_conv",
    )(h, h, h, w, conv_w, conv_b.reshape(1, ncol))


def _flash_kernel(*refs, mode, group, tq, tk, n_kv, lambda_init):
    if mode == "diff":
        q_ref, k_ref, v_ref, lam_ref, sg_ref, o_ref, qs_ref, m_ref, acc_ref = refs
    else:
        q_ref, k_ref, v_ref, o_ref, qs_ref, m_ref, acc_ref = refs
    kv = pl.program_id(3)

    @pl.when(kv == 0)
    def _init():
        q = q_ref[0]
        if mode == "diff":
            lane = lax.broadcasted_iota(jnp.int32, q.shape, 1)
            zero = jnp.zeros_like(q)
            qs_ref[0:tq] = jnp.where(lane < C_HEAD_DIM, q, zero)
            qs_ref[tq:2 * tq] = jnp.where(lane >= C_HEAD_DIM, q, zero)
        else:
            for g in range(group):
                qs_ref[g * tq:(g + 1) * tq] = q[:, g * LANES:(g + 1) * LANES]
        m_ref[...] = jnp.full(m_ref.shape, -jnp.inf, F32)
        acc_ref[...] = jnp.zeros(acc_ref.shape, F32)

    k = k_ref[0]
    v = v_ref[0]
    v_aug = jnp.concatenate([v, jnp.ones_like(v)], axis=1)
    for c in range(group * tq // FLASH_SUB_ROWS):
        sl = slice(c * FLASH_SUB_ROWS, (c + 1) * FLASH_SUB_ROWS)
        s = lax.dot_general(qs_ref[sl], k, (((1,), (1,)), ((), ())),
                            preferred_element_type=F32)
        m_prev = m_ref[sl]
        m_next = jnp.maximum(m_prev, jnp.max(s, axis=-1, keepdims=True))
        alpha = jnp.exp2(m_prev - m_next)
        p = jnp.exp2(s - jnp.tile(m_next, (1, tk // LANES)))
        acc_ref[sl] = jnp.tile(alpha, (1, 2)) * acc_ref[sl] + jnp.dot(
            p.astype(BF16), v_aug, preferred_element_type=F32)
        m_ref[sl] = m_next

    @pl.when(kv == n_kv - 1)
    def _fin():
        acc = acc_ref[...]
        o = acc[:, :LANES] / acc[:, LANES:]
        if mode == "diff":
            lv = lam_ref[...]
            lam = (jnp.exp(jnp.sum(lv[0:1] * lv[1:2], axis=-1, keepdims=True))
                   - jnp.exp(jnp.sum(lv[2:3] * lv[3:4], axis=-1, keepdims=True)) + lambda_init)
            dlt = o[0:tq] - lam * o[tq:2 * tq]
            o_ref[0] = (_rms(dlt, sg_ref[...]) * (1.0 - lambda_init)).astype(o_ref.dtype)
        else:
            for g in range(group):
                o_ref[0, :, g * LANES:(g + 1) * LANES] = o[g * tq:(g + 1) * tq].astype(o_ref.dtype)


def _kv_tile(s):
    for t in (2816, 2048, 1024, 768, 512, 256, 128):
        if s % t == 0:
            return t
    return s


def flash_attention(q, k, v, *, mode, kv_len, lam_vecs=None, subln_g=None, lambda_init=0.0,
                    kv_start=0):
    b, n, qw = q.shape
    s = kv_len
    if mode == "diff":
        group, n_groups, qblk = 2, qw // LANES, LANES
    else:
        group = A_HEADS // A_KV_HEADS
        n_groups, qblk = A_KV_HEADS, group * LANES
    tq = min(n, FLASH_ROWS // group)
    tk = _kv_tile(s)
    n_kv = s // tk
    j0 = kv_start // tk
    assert j0 * tk == kv_start
    in_specs = [pl.BlockSpec((1, tq, qblk), lambda bi, g, i, j: (bi, i, g)),
                pl.BlockSpec((1, tk, LANES), lambda bi, g, i, j: (bi, j0 + j, g)),
                pl.BlockSpec((1, tk, LANES), lambda bi, g, i, j: (bi, j0 + j, g))]
    args = [q, k, v]
    if mode == "diff":
        in_specs += [pl.BlockSpec(lam_vecs.shape, lambda bi, g, i, j: (0, 0)),
                     pl.BlockSpec((1, LANES), lambda bi, g, i, j: (0, 0))]
        args += [lam_vecs, subln_g.reshape(1, LANES)]
    return pl.pallas_call(
        functools.partial(_flash_kernel, mode=mode, group=group, tq=tq, tk=tk, n_kv=n_kv,
                          lambda_init=lambda_init),
        grid=(b, n_groups, n // tq, n_kv),
        in_specs=in_specs,
        out_specs=pl.BlockSpec((1, tq, qblk), lambda bi, g, i, j: (bi, i, g)),
        out_shape=jax.ShapeDtypeStruct((b, n, qw), BF16),
        scratch_shapes=[pltpu.VMEM((group * tq, LANES), BF16),
                        pltpu.VMEM((group * tq, LANES), F32),
                        pltpu.VMEM((group * tq, 2 * LANES), F32)],
        compiler_params=_cparams("arbitrary", "arbitrary", "arbitrary", "arbitrary"),
        name="flash_" + mode,
    )(*args)


def _outproj_kernel(*refs, n_lhs):
    lhs = refs[:n_lhs]
    ws = refs[n_lhs:2 * n_lhs]
    x_ref, g_ref, gate_ref, o_ref = refs[2 * n_lhs:]
    y = jnp.dot(lhs[0][0], ws[0][...], preferred_element_type=F32)
    for a, w in zip(lhs[1:], ws[1:]):
        y = y + jnp.dot(a[0], w[...], preferred_element_type=F32)
    o_ref[0] = x_ref[0] + gate_ref[0] * _rms(y, g_ref[...])


def outproj_residual(lhs_list, w_list, x, g, gate):
    b, n, d = x.shape
    tm = min(n, 512)
    n_lhs = len(lhs_list)
    in_specs = [pl.BlockSpec((1, tm, a.shape[2]), lambda bi, i: (bi, i, 0)) for a in lhs_list]
    in_specs += [pl.BlockSpec(w.shape, lambda bi, i: (0, 0)) for w in w_list]
    in_specs += [pl.BlockSpec((1, tm, d), lambda bi, i: (bi, i, 0)),
                 pl.BlockSpec((1, d), lambda bi, i: (0, 0)),
                 pl.BlockSpec((1, 1, d), lambda bi, i: (bi, 0, 0))]
    return pl.pallas_call(
        functools.partial(_outproj_kernel, n_lhs=n_lhs),
        grid=(b, n // tm),
        in_specs=in_specs,
        out_specs=pl.BlockSpec((1, tm, d), lambda bi, i: (bi, i, 0)),
        out_shape=jax.ShapeDtypeStruct((b, n, d), F32),
        compiler_params=_cparams("arbitrary", "arbitrary"),
        name="outproj",
    )(*lhs_list, *w_list, x, g.reshape(1, d), gate.reshape(b, 1, d))


def _ffn_kernel(xm_ref, xp_ref, xn_ref, gn_ref, sc_ref, sh_ref, wa_ref, wg_ref, cwa_ref, cwg_ref,
                cba_ref, cbg_ref, wd_ref, g3_ref, gate_ref, o_ref, hx_ref,
                *, tm, halo, n_tiles, n_f):
    i = pl.program_id(1)
    j = pl.program_id(2)

    @pl.when(j == 0)
    def _prologue():
        def nm(x):
            return (_rms(x, gn_ref[...]) * (1.0 + sc_ref[0]) + sh_ref[0]).astype(BF16)
        hx_ref[halo:halo + tm] = nm(xm_ref[0])
        zero = jnp.zeros((halo, xm_ref.shape[2]), BF16)
        hx_ref[0:halo] = jnp.where(i > 0, nm(xp_ref[0]), zero)
        hx_ref[halo + tm:] = jnp.where(i < n_tiles - 1, nm(xn_ref[0]), zero)
        o_ref[0] = jnp.zeros(o_ref.shape[1:], F32)

    hx = hx_ref[...]
    rows = tm + 2 * halo

    def conv(u, cw, cb):
        up = pltpu.roll(u, 1, 0)
        dn = pltpu.roll(u, rows - 1, 0)
        return (cw[0:1] * up + cw[1:2] * u + cw[2:3] * dn + cb)[halo:halo + tm]

    a = conv(jnp.dot(hx, wa_ref[...], preferred_element_type=F32), cwa_ref[...], cba_ref[...])
    g = conv(jnp.dot(hx, wg_ref[...], preferred_element_type=F32), cwg_ref[...], cbg_ref[...])
    act = (a * (g * jax.nn.sigmoid(g))).astype(BF16)
    o_ref[0] += jnp.dot(act, wd_ref[...], preferred_element_type=F32)

    @pl.when(j == n_f - 1)
    def _epilogue():
        o_ref[0] = xm_ref[0] + gate_ref[0] * _rms(o_ref[0], g3_ref[...])


def conv_ffn_residual(x, gn, sc, sh, w_up, conv_w, conv_b, w_down, g3, gate, layer):
    b, n, d = x.shape
    dff = w_down.shape[1]
    tm = min(n, 512)
    tf = 512
    halo = BF16_SUBLANES
    nt = n // tm
    n_f = dff // tf
    row = lambda bi, i, j: (bi, i, 0)
    vec = lambda bi, i, j: (0, 0)
    bvec = lambda bi, i, j: (bi, 0, 0)
    up_a = lambda bi, i, j: (0, j)
    up_g = lambda bi, i, j: (0, n_f + j)
    wup_a = lambda bi, i, j: (layer, 0, j)
    wup_g = lambda bi, i, j: (layer, 0, n_f + j)
    conv_b = conv_b.reshape(1, 2 * dff)
    return pl.pallas_call(
        functools.partial(_ffn_kernel, tm=tm, halo=halo, n_tiles=nt, n_f=n_f),
        grid=(b, nt, n_f),
        in_specs=[pl.BlockSpec((1, tm, d), row),
                  pl.BlockSpec((1, halo, d),
                               lambda bi, i, j: (bi, jnp.maximum(i * (tm // halo) - 1, 0), 0)),
                  pl.BlockSpec((1, halo, d),
                               lambda bi, i, j: (bi, jnp.minimum((i + 1) * (tm // halo), n // halo - 1), 0)),
                  pl.BlockSpec((1, d), vec),
                  pl.BlockSpec((1, 1, d), bvec),
                  pl.BlockSpec((1, 1, d), bvec),
                  pl.BlockSpec((None, d, tf), wup_a),
                  pl.BlockSpec((None, d, tf), wup_g),
                  pl.BlockSpec((3, tf), up_a),
                  pl.BlockSpec((3, tf), up_g),
                  pl.BlockSpec((1, tf), up_a),
                  pl.BlockSpec((1, tf), up_g),
                  pl.BlockSpec((None, tf, d), lambda bi, i, j: (layer, j, 0)),
                  pl.BlockSpec((1, d), vec),
                  pl.BlockSpec((1, 1, d), bvec)],
        out_specs=pl.BlockSpec((1, tm, d), row),
        out_shape=jax.ShapeDtypeStruct((b, n, d), F32),
        scratch_shapes=[pltpu.VMEM((tm + 2 * halo, d), BF16)],
        compiler_params=_cparams("arbitrary", "arbitrary", "arbitrary"),
        name="conv_ffn",
    )(x, x, x, gn.reshape(1, d), sc.reshape(b, 1, d), sh.reshape(b, 1, d), w_up, w_up,
      conv_w, conv_w, conv_b, conv_b, w_down, g3.reshape(1, d), gate.reshape(b, 1, d))


def _filter_kernel(bands_ref, w1_ref, b1_ref, w2_ref, b2_ref, w3_ref, fr_ref, dl_ref,
                   o_ref, s_ref, *, n_tok, tr):
    i = pl.program_id(0)

    @pl.when(i == 0)
    def _init():
        s_ref[...] = jnp.zeros(s_ref.shape, F32)

    r = (i * tr + lax.broadcasted_iota(jnp.int32, (tr, 1), 0)).astype(F32)
    t = jnp.where(r < n_tok, r, 2.0 * n_tok - r)
    t_norm = t / float(max(n_tok - 1, 1))
    wang = (2.0 * math.pi / n_tok) * t
    z = wang * bands_ref[...]
    lane = lax.broadcasted_iota(jnp.int32, z.shape, 1)
    feats = jnp.where(lane == 0, t_norm,
                      jnp.where(lane <= HY_BANDS, jnp.cos(z),
                                jnp.where(lane <= 2 * HY_BANDS, -jnp.sin(z), 0.0)))
    fr = fr_ref[...]
    h = jnp.sin(fr[0:1] * (jnp.dot(feats, w1_ref[...], precision=HIGHEST,
                                   preferred_element_type=F32) + b1_ref[...]))
    h = jnp.sin(fr[1:2] * (jnp.dot(h, w2_ref[...], precision=HIGHEST,
                                   preferred_element_type=F32) + b2_ref[...]))
    hb = h.astype(BF16)
    decay = jnp.where(r == n_tok, 0.0, jnp.exp(-t_norm * dl_ref[...]))
    for o in range(HY_ORDER):
        out = jnp.dot(hb, w3_ref[o].astype(BF16), preferred_element_type=F32) * decay
        o_ref[o] = out
        s_ref[o] += jnp.sum(jnp.abs(out), axis=0, keepdims=True)


def hyena_filter_time(n_tok, w1, b1, w2, b2, w3, freq):
    ch = HY_CH
    ll = 2 * n_tok
    tr = min(n_tok, 512)
    per_half = n_tok // tr
    lane = jnp.arange(LANES)
    bands = jnp.linspace(1e-4, HY_BANDS - 1, HY_BANDS, dtype=F32)
    bands_l = jnp.where((lane >= 1) & (lane <= HY_BANDS), bands[jnp.clip(lane - 1, 0, HY_BANDS - 1)],
                        jnp.where((lane > HY_BANDS) & (lane <= 2 * HY_BANDS),
                                  bands[jnp.clip(lane - 1 - HY_BANDS, 0, HY_BANDS - 1)], 0.0))
    w1p = jnp.zeros((LANES, HY_FILTER_W), F32).at[:w1.shape[0]].set(w1)
    deltas = jnp.abs(jnp.linspace(math.log(HY_TARGET) / HY_FAST, math.log(HY_TARGET) / HY_SLOW,
                                  ch, dtype=F32))
    w3o = jnp.swapaxes(w3.reshape(HY_FILTER_W, HY_ORDER, 2 * ch), 0, 1)
    const = lambda i: (0, 0)
    return pl.pallas_call(
        functools.partial(_filter_kernel, n_tok=n_tok, tr=tr),
        grid=(ll // tr,),
        in_specs=[pl.BlockSpec((1, LANES), const),
                  pl.BlockSpec((LANES, HY_FILTER_W), const),
                  pl.BlockSpec((1, HY_FILTER_W), const),
                  pl.BlockSpec((HY_FILTER_W, HY_FILTER_W), const),
                  pl.BlockSpec((1, HY_FILTER_W), const),
                  pl.BlockSpec((HY_ORDER, HY_FILTER_W, ch), lambda i: (0, 0, i // per_half)),
                  pl.BlockSpec((2, HY_FILTER_W), const),
                  pl.BlockSpec((1, ch), const)],
        out_specs=[pl.BlockSpec((HY_ORDER, tr, ch), lambda i: (0, i, 0)),
                   pl.BlockSpec((HY_ORDER, 1, ch), lambda i: (0, 0, 0))],
        out_shape=[jax.ShapeDtypeStruct((HY_ORDER, ll, ch), F32),
                   jax.ShapeDtypeStruct((HY_ORDER, 1, ch), F32)],
        compiler_params=_cparams("arbitrary"),
        name="hyena_filter",
    )(bands_l.reshape(1, LANES), w1p, b1.reshape(1, -1), w2, b2.reshape(1, -1), w3o, freq,
      deltas.reshape(1, ch))


def _dft_dot(f, x):
    return jnp.dot(f.astype(BF16), x.astype(BF16), preferred_element_type=F32)


def _stage_a_kernel(f_ref, x_ref, o_ref):
    parts = x_ref.shape[0]
    x = x_ref[0] if parts == 1 else jnp.concatenate([x_ref[p] for p in range(parts)], axis=0)
    y = _dft_dot(f_ref[...], x).astype(o_ref.dtype)
    half = y.shape[0] // 2
    o_ref[0] = y[:half]
    o_ref[1] = y[half:]


def dft_stage_a(f, x, g):
    _, p, rows, w = x.shape
    k1 = p * rows
    tl = min(w, 4096)
    return pl.pallas_call(
        _stage_a_kernel,
        grid=(w // tl,),
        in_specs=[pl.BlockSpec(f.shape, lambda j: (0, 0)),
                  pl.BlockSpec((None, p, rows, tl), lambda j: (g, 0, 0, j))],
        out_specs=pl.BlockSpec((2, k1, tl), lambda j: (0, 0, j)),
        out_shape=jax.ShapeDtypeStruct((2, k1, w), BF16),
        compiler_params=_cparams("arbitrary"),
        name="dft_stage_a",
    )(f, x)


def _stage_c_filter_kernel(a_ref, g_ref, sc_ref, o_ref, *, kc):
    for k in range(kc):
        r = jnp.concatenate([a_ref[0, k], a_ref[1, k]], axis=0)
        y = (_dft_dot(g_ref[k], r) * sc_ref[...]).astype(o_ref.dtype)
        half = y.shape[0] // 2
        o_ref[0, k] = y[:half]
        o_ref[1, k] = y[half:]


def dft_stage_c_filter(a, g, scale):
    _, k1, r, c = a.shape
    kc = 8
    tc = 256
    blk = pl.BlockSpec((2, kc, r, tc), lambda i, j: (0, i, 0, j))
    return pl.pallas_call(
        functools.partial(_stage_c_filter_kernel, kc=kc),
        grid=(k1 // kc, c // tc),
        in_specs=[blk,
                  pl.BlockSpec((kc, 2 * r, 2 * r), lambda i, j: (i, 0, 0)),
                  pl.BlockSpec((1, tc), lambda i, j: (0, j))],
        out_specs=blk,
        out_shape=jax.ShapeDtypeStruct(a.shape, BF16),
        compiler_params=_cparams("arbitrary", "arbitrary"),
        name="dft_stage_c_filter",
    )(a, g, scale)


def _stage_c_conv_kernel(a_ref, h_ref, g_ref, gi_ref, o_ref, *, kc):
    for k in range(kc):
        r = jnp.concatenate([a_ref[0, k], a_ref[1, k]], axis=0)
        y = _dft_dot(g_ref[k], r)
        half = y.shape[0] // 2
        yre, yim = y[:half], y[half:]
        hre, him = h_ref[0, k].astype(F32), h_ref[1, k].astype(F32)
        p = jnp.concatenate([yre * hre - yim * him, yre * him + yim * hre], axis=0)
        d = _dft_dot(gi_ref[k], p).astype(o_ref.dtype)
        o_ref[0, k] = d[:half]
        o_ref[1, k] = d[half:]


def dft_stage_c_conv(a, hspec, g, gi):
    _, k1, r, c = a.shape
    kc = 8
    tc = 256
    blk = pl.BlockSpec((2, kc, r, tc), lambda i, j: (0, i, 0, j))
    mat = pl.BlockSpec((kc, 2 * r, 2 * r), lambda i, j: (i, 0, 0))
    return pl.pallas_call(
        functools.partial(_stage_c_conv_kernel, kc=kc),
        grid=(k1 // kc, c // tc),
        in_specs=[blk, blk, mat, mat],
        out_specs=blk,
        out_shape=jax.ShapeDtypeStruct(a.shape, BF16),
        compiler_params=_cparams("arbitrary", "arbitrary"),
        name="dft_stage_c_conv",
    )(a, hspec, g, gi)


def _stage_a_inv_kernel(f_ref, d_ref, u_ref, gt_ref, sk_ref, o_ref):
    d = jnp.concatenate([d_ref[0], d_ref[1]], axis=0)
    y = _dft_dot(f_ref[...], d)
    half = y.shape[0] // 2
    sk = sk_ref[...]
    o_ref[0] = (gt_ref[0] * (y[:half] + u_ref[0] * sk)).astype(o_ref.dtype)
    o_ref[1] = (gt_ref[1] * (y[half:] + u_ref[1] * sk)).astype(o_ref.dtype)


def dft_stage_a_inv(f, d, u, gu, gate, gg, skip_flat, out_dtype):
    _, k1, w = d.shape
    half = k1 // 2
    tl = skip_flat.shape[1]
    io = pl.BlockSpec((2, half, tl), lambda j: (0, 0, j))
    return pl.pallas_call(
        _stage_a_inv_kernel,
        grid=(w // tl,),
        in_specs=[pl.BlockSpec(f.shape, lambda j: (0, 0)),
                  pl.BlockSpec((2, k1, tl), lambda j: (0, 0, j)),
                  pl.BlockSpec((None, 2, half, tl), lambda j: (gu, 0, 0, j)),
                  pl.BlockSpec((None, 2, half, tl), lambda j: (gg, 0, 0, j)),
                  pl.BlockSpec((1, tl), lambda j: (0, 0))],
        out_specs=io,
        out_shape=jax.ShapeDtypeStruct((2, half, w), out_dtype),
        compiler_params=_cparams("arbitrary"),
        name="dft_stage_a_inv",
    )(f, d, u, gate, skip_flat)


def _dft_tables(n_tok):
    ll = 2 * n_tok
    r = DFT_RADIX
    k1n = ll // r
    half = k1n // 2
    two_pi = 2.0 * math.pi

    def cs(m, period):
        ang = (m % period).astype(F32) * (two_pi / period)
        return jnp.cos(ang), jnp.sin(ang)

    k1 = jnp.arange(k1n, dtype=jnp.int32)
    c, s = cs(k1[:, None] * k1[None, :half], k1n)
    fa_data = jnp.concatenate([jnp.concatenate([c, s], 1), jnp.concatenate([-s, c], 1)], 0)
    c, s = cs(k1[:, None] * k1[None, :], k1n)
    fa_filt = jnp.concatenate([c, -s], 0)
    c, s = cs(k1[:half, None] * k1[None, :], k1n)
    fa_inv = jnp.concatenate([jnp.concatenate([c, -s], 1), jnp.concatenate([s, c], 1)], 0)
    idx = jnp.arange(r, dtype=jnp.int32)
    ca, sa = cs(idx[:, None] * idx[None, :], r)
    cb, sb = cs(k1[:, None] * idx[None, :], ll)
    c = ca[None] * cb[:, None, :] - sa[None] * sb[:, None, :]
    s = sa[None] * cb[:, None, :] + ca[None] * sb[:, None, :]
    g_fwd = jnp.concatenate([jnp.concatenate([c, s], 2), jnp.concatenate([-s, c], 2)], 1)
    ct, st = jnp.swapaxes(c, 1, 2), jnp.swapaxes(s, 1, 2)
    g_inv = jnp.concatenate([jnp.concatenate([ct, -st], 2), jnp.concatenate([st, ct], 2)], 1)
    return tuple(t.astype(BF16) for t in (fa_data, fa_filt, fa_inv, g_fwd, g_inv))


def hyena_long(pc, filt, l1, skip, out_dtype):
    _, b, n, c = pc.shape
    assert b == 2, "the two batch entries ride as real/imaginary parts"
    ll = 2 * n
    r = DFT_RADIX
    k1n = ll // r
    half = k1n // 2
    w = r * c
    fa_data, fa_filt, fa_inv, g_fwd, g_inv = _dft_tables(n)
    tl = 4096
    pcf = pc.reshape(HY_ORDER + 1, 2, half, w)
    filtf = filt.reshape(HY_ORDER, 1, k1n, w)
    u, gu = pcf, 0
    for o in range(HY_ORDER):
        af = dft_stage_a(fa_filt, filtf, o)
        hspec = dft_stage_c_filter(af.reshape(2, k1n, r, c), g_fwd, 1.0 / (l1[o] * ll))
        a = dft_stage_a(fa_data, u, gu)
        d = dft_stage_c_conv(a.reshape(2, k1n, r, c), hspec, g_fwd, g_inv)
        last = o == HY_ORDER - 1
        z = dft_stage_a_inv(fa_inv, d.reshape(2, k1n, w), u, gu, pcf, o + 1,
                            jnp.tile(skip[o], tl // c).reshape(1, tl),
                            out_dtype if last else F32)
        u, gu = z[None], 0
    return z.reshape(b, n, c)


def _hyena_short_kernel(ff_ref, fr_ref, fi_ref, pc_ref, filt_ref, l1_ref, skip_ref, o_ref, *, n, ll):
    u = [pc_ref[0, 0], pc_ref[0, 1]]
    for o in range(HY_ORDER):
        hs = _dft_dot(fr_ref[...], filt_ref[o]) * (1.0 / (l1_ref[o] * ll))
        y = _dft_dot(ff_ref[...], jnp.concatenate(u, axis=0))
        yre, yim, hre, him = y[:ll], y[ll:], hs[:ll], hs[ll:]
        p = jnp.concatenate([yre * hre - yim * him, yre * him + yim * hre], axis=0)
        d = _dft_dot(fi_ref[...], p)
        sk = skip_ref[o:o + 1]
        u = [pc_ref[o + 1, bi] * (d[bi * n:(bi + 1) * n] + u[bi] * sk) for bi in range(2)]
    o_ref[0] = u[0].astype(o_ref.dtype)
    o_ref[1] = u[1].astype(o_ref.dtype)


def hyena_short(pc, filt, l1, skip, out_dtype):
    _, b, n, c = pc.shape
    assert b == 2
    ll = 2 * n
    two_pi = 2.0 * math.pi
    kk = jnp.arange(ll, dtype=jnp.int32)
    ang = ((kk[:, None] * kk[None, :]) % ll).astype(F32) * (two_pi / ll)
    cf, sf = jnp.cos(ang), jnp.sin(ang)
    f_real = jnp.concatenate([cf, -sf], 0)
    cn, sn = cf[:, :n], sf[:, :n]
    f_fwd = jnp.concatenate([jnp.concatenate([cn, sn], 1), jnp.concatenate([-sn, cn], 1)], 0)
    ci, si = cf[:n, :], sf[:n, :]
    f_inv = jnp.concatenate([jnp.concatenate([ci, -si], 1), jnp.concatenate([si, ci], 1)], 0)
    tc = 256
    const = lambda j: (0, 0)
    return pl.pallas_call(
        functools.partial(_hyena_short_kernel, n=n, ll=ll),
        grid=(c // tc,),
        in_specs=[pl.BlockSpec(f_fwd.shape, const),
                  pl.BlockSpec(f_real.shape, const),
                  pl.BlockSpec(f_inv.shape, const),
                  pl.BlockSpec((3, 2, n, tc), lambda j: (0, 0, 0, j)),
                  pl.BlockSpec((HY_ORDER, ll, tc), lambda j: (0, 0, j)),
                  pl.BlockSpec((HY_ORDER, 1, tc), lambda j: (0, 0, j)),
                  pl.BlockSpec((HY_ORDER, tc), lambda j: (0, j))],
        out_specs=pl.BlockSpec((2, n, tc), lambda j: (0, 0, j)),
        out_shape=jax.ShapeDtypeStruct((2, n, c), out_dtype),
        compiler_params=_cparams("arbitrary"),
        name="hyena_short",
    )(f_fwd, f_real, f_inv, pc, filt, l1, skip)


def hyena_mixer(pc, w1, b1, w2, b2, w3, freq, skip, out_dtype):
    n = pc.shape[2]
    filt, l1 = hyena_filter_time(n, w1, b1, w2, b2, w3, freq)
    if (2 * n) % (DFT_RADIX * BF16_SUBLANES) == 0 and n >= 1024:
        return hyena_long(pc, filt, l1, skip, out_dtype)
    return hyena_short(pc, filt, l1, skip, out_dtype)


def _deinterleave_cols(w, hd):
    lead = w.shape[:-1]
    wh = w.reshape(lead + (w.shape[-1] // hd, hd // 2, 2))
    return jnp.concatenate([wh[..., 0], wh[..., 1]], axis=-1).reshape(w.shape)


def _rope_tables(n_tok, rot_dim):
    rows = n_tok // GRID_W
    row = jnp.broadcast_to(jnp.arange(rows, dtype=jnp.int32)[:, None], (rows, GRID_W)).reshape(n_tok)
    col = jnp.broadcast_to(jnp.arange(GRID_W, dtype=jnp.int32)[None, :], (rows, GRID_W)).reshape(n_tok)
    axis_dim = rot_dim // 2
    inv_freq = ROPE_THETA ** (-jnp.arange(0, axis_dim, 2, dtype=F32) / axis_dim)
    ang = jnp.concatenate([row.astype(F32)[:, None] * inv_freq,
                           col.astype(F32)[:, None] * inv_freq], axis=-1)
    c, s = jnp.cos(ang), jnp.sin(ang)
    reps = LANES // rot_dim
    return jnp.tile(jnp.concatenate([c, c], -1), (1, reps)), jnp.tile(jnp.concatenate([-s, s], -1), (1, reps))


def _mixer_ab(h, hc, with_ctx, w_in, w_out, qk_g, hy):
    b, n, _ = h.shape
    conv_w, conv_b, w1, b1, w2, b2, w3, freq, skip = hy
    scale = A_HEAD_DIM ** -0.5 * LOG2E
    wq = _deinterleave_cols(w_in[:, :A_Q_W], A_HEAD_DIM).astype(BF16)
    wk = _deinterleave_cols(w_in[:, A_Q_W:A_Q_W + A_KV_W], A_HEAD_DIM).astype(BF16)
    wv = w_in[:, A_Q_W + A_KV_W:A_QKV_W].astype(BF16)
    wh = w_in[:, A_QKV_W:].astype(BF16)
    gq = jnp.tile(_deinterleave_cols(qk_g[0], A_HEAD_DIM) * scale, A_HEADS)
    gk = jnp.tile(_deinterleave_cols(qk_g[1], A_HEAD_DIM), A_KV_HEADS)
    ones_v = jnp.ones((A_KV_W,), F32)
    cosf, sinf = _rope_tables(n, A_HEAD_DIM)

    n_ctx = hc.shape[1]
    q = inproj(h, wq, gq, cosf, sinf, norm=True)
    k_all = inproj(hc, wk, gk, norm=True, append_at=n,
                   append_to=inproj(h, wk, gk, cosf, sinf, norm=True, tail_rows=n_ctx))
    v_all = inproj(hc, wv, ones_v, append_at=n, append_to=inproj(h, wv, ones_v, tail_rows=n_ctx))
    o_att = flash_attention(q, k_all, v_all, mode="gqa", kv_len=n + n_ctx)
    pc = inproj_conv(h, wh, conv_w, conv_b, HY_ORDER + 1)
    o_hy = hyena_mixer(pc, w1, b1, w2, b2, w3, freq, skip, BF16)
    wo = w_out.astype(BF16)
    lat = ([o_att, o_hy], [wo[:A_Q_W], wo[A_Q_W:]])
    if not with_ctx:
        return lat, None
    q_c = inproj(hc, wq, gq, norm=True)
    o_att_c = flash_attention(q_c, k_all, v_all, mode="gqa", kv_start=n, kv_len=n_ctx)
    pc_c = inproj_conv(hc, wh, conv_w, conv_b, HY_ORDER + 1)
    o_hy_c = hyena_mixer(pc_c, w1, b1, w2, b2, w3, freq, skip, BF16)
    return lat, ([o_att_c, o_hy_c], [wo[:A_Q_W], wo[A_Q_W:]])


def _mixer_c(h, hc, with_ctx, lambda_init, w_in, w_out, lam_vecs, subln_g):
    b, n, _ = h.shape
    scale = C_HEAD_DIM ** -0.5 * LOG2E
    wq = _deinterleave_cols(w_in[:, :C_Q_W], C_HEAD_DIM).astype(BF16)
    wk = _deinterleave_cols(w_in[:, C_Q_W:2 * C_Q_W], C_HEAD_DIM).astype(BF16)
    wv = w_in[:, 2 * C_Q_W:].astype(BF16)
    gq = jnp.full((C_Q_W,), scale, F32)
    ones = jnp.ones((C_Q_W,), F32)
    cosf, sinf = _rope_tables(n, C_HEAD_DIM)

    n_ctx = hc.shape[1]
    q = inproj(h, wq, gq, cosf, sinf, hd=C_HEAD_DIM)
    k_all = inproj(hc, wk, ones, append_at=n,
                   append_to=inproj(h, wk, ones, cosf, sinf, hd=C_HEAD_DIM, tail_rows=n_ctx))
    v_all = inproj(hc, wv, ones, append_at=n, append_to=inproj(h, wv, ones, tail_rows=n_ctx))
    attn = functools.partial(flash_attention, mode="diff", lam_vecs=lam_vecs, subln_g=subln_g,
                             lambda_init=lambda_init)
    wo = w_out.astype(BF16)
    lat = ([attn(q, k_all, v_all, kv_len=n + n_ctx)], [wo])
    if not with_ctx:
        return lat, None
    q_c = inproj(hc, wq, gq)
    return lat, ([attn(q_c, k_all, v_all, kv_start=n, kv_len=n_ctx)], [wo])


def kernel(x, c, ctx, c_ctx, ada_w, ada_b, norm_g, ab_w_in, ab_w_out, ab_qk_g,
           hy_conv_w, hy_conv_b, hy_w1, hy_b1, hy_w2, hy_b2, hy_w3, hy_freq, hy_skip,
           dc_w_in, dc_w_out, dc_lambda, dc_subln_g,
           ffn_w_up, ffn_conv_w, ffn_conv_b, ffn_w_down):
    depth = ada_w.shape[0]
    b, n, d = x.shape
    pad_rows = (-(b + 1)) % 8
    cvec = jnp.concatenate([c, c_ctx[None, :], jnp.zeros((pad_rows, d), F32)], axis=0)
    mod_all = ada_mod(cvec, ada_w, ada_b)
    w_up = ffn_w_up.astype(BF16)
    w_down = ffn_w_down.astype(BF16)
    for i in range(depth):
        j = i // 2
        with_ctx = i < depth - 1
        mod = mod_all[i, :b].reshape(b, 6, d)
        mod_c = jnp.broadcast_to(mod_all[i, b].reshape(1, 6, d), (b, 6, d))
        sh1, sc1, g1, sh2, sc2, g2 = (mod[:, t] for t in range(6))
        sh1c, sc1c, g1c, sh2c, sc2c, g2c = (mod_c[:, t] for t in range(6))
        h = norm_mod(x, norm_g[i, 0], sc1, sh1)
        hc = norm_mod(ctx, norm_g[i, 0], sc1c, sh1c)
        if i % 2 == 0:
            hy = (hy_conv_w[j], hy_conv_b[j], hy_w1[j], hy_b1[j], hy_w2[j], hy_b2[j],
                  hy_w3[j], hy_freq[j], hy_skip[j])
            lat, cx = _mixer_ab(h, hc, with_ctx, ab_w_in[j], ab_w_out[j], ab_qk_g[j], hy)
        else:
            lambda_init = 0.8 - 0.6 * math.exp(-0.3 * i)
            lat, cx = _mixer_c(h, hc, with_ctx, lambda_init, dc_w_in[j], dc_w_out[j],
                               dc_lambda[j], dc_subln_g[j])
        x = outproj_residual(lat[0], lat[1], x, norm_g[i, 1], g1)
        x = conv_ffn_residual(x, norm_g[i, 2], sc2, sh2, w_up, ffn_conv_w[i], ffn_conv_b[i],
                              w_down, norm_g[i, 3], g2, i)
        if with_ctx:
            ctx = outproj_residual(cx[0], cx[1], ctx, norm_g[i, 1], g1c)
            ctx = conv_ffn_residual(ctx, norm_g[i, 2], sc2c, sh2c, w_up, ffn_conv_w[i],
                                    ffn_conv_b[i], w_down, norm_g[i, 3], g2c, i)
    return x
```

```python
import functools
import math

import jax
import jax.numpy as jnp
from jax import lax
from jax.experimental import pallas as pl
from jax.experimental.pallas import tpu as pltpu

F32 = jnp.float32
BF16 = jnp.bfloat16
HIGHEST = lax.Precision.HIGHEST

D_MODEL = 2048
GRID_W = 64
EPS = 1e-6
ROPE_THETA = 10000.0
A_HEADS, A_KV_HEADS, A_HEAD_DIM = 8, 2, 128
A_Q_W = A_HEADS * A_HEAD_DIM
A_KV_W = A_KV_HEADS * A_HEAD_DIM
A_QKV_W = A_Q_W + 2 * A_KV_W
HY_CH = D_MODEL // 2
HY_ORDER = 2
HY_FILTER_W = 64
HY_BANDS = 16
HY_FAST, HY_SLOW, HY_TARGET = 0.3, 1.5, 1e-2
C_HEADS, C_HEAD_DIM = 16, 64
C_Q_W = C_HEADS * 2 * C_HEAD_DIM
D_FF = 5632

LANES = 128
BF16_SUBLANES = 16
VMEM_LIMIT = 56 * 1024 * 1024
DFT_RADIX = 128
FLASH_ROWS = 2048
FLASH_SUB_ROWS = 128
LOG2E = math.log2(math.e)


def _cparams(*sem):
    return pltpu.CompilerParams(dimension_semantics=sem, vmem_limit_bytes=VMEM_LIMIT)


def _rms(x, g):
    return x * lax.rsqrt(jnp.mean(x * x, axis=-1, keepdims=True) + EPS) * g


def _ada_kernel(c_ref, w_ref, b_ref, o_ref):
    c = c_ref[...]
    s = c * jax.nn.sigmoid(c)
    rows = s.shape[0]
    s_hi = s.astype(BF16).astype(F32)
    s2 = jnp.concatenate([s_hi, s - s_hi], axis=0).astype(BF16)
    w = w_ref[0]
    w_hi = w.astype(BF16)
    w_lo = (w - w_hi.astype(F32)).astype(BF16)
    both = jnp.dot(s2, w_hi, preferred_element_type=F32)
    hi_lo = jnp.dot(s2, w_lo, preferred_element_type=F32)[:rows]
    o_ref[0] = both[:rows] + both[rows:] + hi_lo + b_ref[0]


def ada_mod(cvec, ada_w, ada_b):
    depth, d, n6 = ada_w.shape
    rows = cvec.shape[0]
    tn = 1024
    return pl.pallas_call(
        _ada_kernel,
        grid=(depth, n6 // tn),
        in_specs=[pl.BlockSpec((rows, d), lambda l, j: (0, 0)),
                  pl.BlockSpec((1, d, tn), lambda l, j: (l, 0, j)),
                  pl.BlockSpec((1, 1, tn), lambda l, j: (l, 0, j))],
        out_specs=pl.BlockSpec((1, rows, tn), lambda l, j: (l, 0, j)),
        out_shape=jax.ShapeDtypeStruct((depth, rows, n6), F32),
        compiler_params=_cparams("arbitrary", "arbitrary"),
        name="ada_mod",
    )(cvec, ada_w, ada_b.reshape(depth, 1, n6))


def _norm_mod_kernel(x_ref, g_ref, sc_ref, sh_ref, o_ref):
    y = _rms(x_ref[0], g_ref[...])
    o_ref[0] = (y * (1.0 + sc_ref[0]) + sh_ref[0]).astype(o_ref.dtype)


def norm_mod(x, g, sc, sh):
    b, n, d = x.shape
    tm = min(n, 512)
    return pl.pallas_call(
        _norm_mod_kernel,
        grid=(b, n // tm),
        in_specs=[pl.BlockSpec((1, tm, d), lambda bi, i: (bi, i, 0)),
                  pl.BlockSpec((1, d), lambda bi, i: (0, 0)),
                  pl.BlockSpec((1, 1, d), lambda bi, i: (bi, 0, 0)),
                  pl.BlockSpec((1, 1, d), lambda bi, i: (bi, 0, 0))],
        out_specs=pl.BlockSpec((1, tm, d), lambda bi, i: (bi, i, 0)),
        out_shape=jax.ShapeDtypeStruct((b, n, d), BF16),
        compiler_params=_cparams("arbitrary", "arbitrary"),
        name="norm_mod",
    )(x, g.reshape(1, d), sc.reshape(b, 1, d), sh.reshape(b, 1, d))


def _inproj_kernel(*refs, hd, norm, rope, appending, n_tiles, has_tail):
    o_ref = refs[-1]
    if appending:
        refs = refs[:-2]
    else:
        refs = refs[:-1]
    if rope:
        h_ref, w_ref, g_ref, cos_ref, sin_ref = refs
    else:
        h_ref, w_ref, g_ref = refs
    i = pl.program_id(1)

    def _tail():
        o_ref[0] = jnp.zeros(o_ref.shape[1:], o_ref.dtype)

    def _body():
        acc = jnp.dot(h_ref[0], w_ref[...], preferred_element_type=F32)
        tn = acc.shape[1]
        if rope:
            cosf = cos_ref[...]
            sinf = sin_ref[...]
            if hd != LANES:
                lane = lax.broadcasted_iota(jnp.int32, cosf.shape, 1)
                first_half = (lane % hd) < (hd // 2)
        for c in range(tn // LANES):
            x = acc[:, c * LANES:(c + 1) * LANES]
            if norm:
                x = x * lax.rsqrt(jnp.mean(x * x, axis=-1, keepdims=True) + EPS)
            x = x * g_ref[:, c * LANES:(c + 1) * LANES]
            if rope:
                if hd == LANES:
                    partner = pltpu.roll(x, LANES // 2, 1)
                else:
                    partner = jnp.where(first_half, pltpu.roll(x, LANES - hd // 2, 1),
                                        pltpu.roll(x, hd // 2, 1))
                x = x * cosf + partner * sinf
            o_ref[0, :, c * LANES:(c + 1) * LANES] = x.astype(o_ref.dtype)

    if has_tail:
        pl.when(i >= n_tiles)(_tail)
        pl.when(i < n_tiles)(_body)
    else:
        _body()


def inproj(h, w, gvec, cosf=None, sinf=None, *, hd=LANES, norm=False, out_dtype=BF16,
           tail_rows=0, append_to=None, append_at=0):
    b, n, d = h.shape
    ncol = w.shape[1]
    rope = cosf is not None
    tm = min(n, 512)
    tn = min(ncol, 2048)
    n_tiles = n // tm
    last = n_tiles - 1
    in_specs = [pl.BlockSpec((1, tm, d), lambda bi, i, j: (bi, jnp.minimum(i, last), 0)),
                pl.BlockSpec((d, tn), lambda bi, i, j: (0, j)),
                pl.BlockSpec((1, tn), lambda bi, i, j: (0, j))]
    args = [h, w, gvec.reshape(1, ncol)]
    if rope:
        in_specs += [pl.BlockSpec((tm, LANES), lambda bi, i, j: (jnp.minimum(i, last), 0)),
                     pl.BlockSpec((tm, LANES), lambda bi, i, j: (jnp.minimum(i, last), 0))]
        args += [cosf, sinf]
    appending = append_to is not None
    if appending:
        total = append_to.shape[1]
        first = append_at // tm
        assert first * tm == append_at and append_at + n <= total and append_to.dtype == out_dtype
        tail_tiles = 0
        in_specs.append(pl.BlockSpec(memory_space=pl.ANY))
        args.append(append_to)
        aliases = {len(args) - 1: 0}
    else:
        tail_tiles = -(-tail_rows // tm)
        total, first, aliases = (n_tiles + tail_tiles) * tm, 0, {}
    return pl.pallas_call(
        functools.partial(_inproj_kernel, hd=hd, norm=norm, rope=rope, appending=appending,
                          n_tiles=n_tiles, has_tail=tail_tiles > 0),
        grid=(b, n_tiles + tail_tiles, ncol // tn),
        in_specs=in_specs,
        out_specs=pl.BlockSpec((1, tm, tn), lambda bi, i, j: (bi, first + i, j)),
        out_shape=jax.ShapeDtypeStruct((b, total, ncol), out_dtype),
        input_output_aliases=aliases,
        compiler_params=_cparams("arbitrary", "arbitrary", "arbitrary"),
        name="inproj",
    )(*args)


def _inproj_conv_kernel(hm_ref, hp_ref, hn_ref, w_ref, cw_ref, cb_ref, o_ref, *, n_tiles):
    i = pl.program_id(2)
    hp, hm, hn = hp_ref[0], hm_ref[0], hn_ref[0]
    halo, tm = hp.shape[0], hm.shape[0]
    zero = jnp.zeros_like(hp)
    lhs = jnp.concatenate([jnp.where(i > 0, hp, zero), hm,
                           jnp.where(i < n_tiles - 1, hn, zero)], axis=0)
    u = jnp.dot(lhs, w_ref[...], preferred_element_type=F32)
    rows = tm + 2 * halo
    cw = cw_ref[...]
    conv = cw[0:1] * pltpu.roll(u, 1, 0) + cw[1:2] * u + cw[2:3] * pltpu.roll(u, rows - 1, 0)
    o_ref[0, 0] = conv[halo:halo + tm] + cb_ref[...]


def inproj_conv(h, w, conv_w, conv_b, groups):
    b, n, d = h.shape
    ncol = w.shape[1]
    c = ncol // groups
    tm = min(n, 512)
    tn = min(c, 1024)
    halo = BF16_SUBLANES
    nt = n // tm
    per = c // tn
    return pl.pallas_call(
        functools.partial(_inproj_conv_kernel, n_tiles=nt),
        grid=(ncol // tn, b, nt),
        in_specs=[pl.BlockSpec((1, tm, d), lambda j, bi, i: (bi, i, 0)),
                  pl.BlockSpec((1, halo, d),
                               lambda j, bi, i: (bi, jnp.maximum(i * (tm // halo) - 1, 0), 0)),
                  pl.BlockSpec((1, halo, d),
                               lambda j, bi, i: (bi, jnp.minimum((i + 1) * (tm // halo), n // halo - 1), 0)),
                  pl.BlockSpec((d, tn), lambda j, bi, i: (0, j)),
                  pl.BlockSpec((3, tn), lambda j, bi, i: (0, j)),
                  pl.BlockSpec((1, tn), lambda j, bi, i: (0, j))],
        out_specs=pl.BlockSpec((1, 1, tm, tn), lambda j, bi, i: (j // per, bi, i, j % per)),
        out_shape=jax.ShapeDtypeStruct((groups, b, n, c), F32),
        compiler_params=_cparams("arbitrary", "arbitrary", "arbitrary"),
        name="inproj_conv",
    )(h, h, h, w, conv_w, conv_b.reshape(1, ncol))


def _flash_kernel(*refs, mode, group, tq, tk, n_kv, lambda_init):
    if mode == "diff":
        q_ref, k_ref, v_ref, lam_ref, sg_ref, o_ref, qs_ref, m_ref, acc_ref = refs
    else:
        q_ref, k_ref, v_ref, o_ref, qs_ref, m_ref, acc_ref = refs
    kv = pl.program_id(3)

    @pl.when(kv == 0)
    def _init():
        q = q_ref[0]
        if mode == "diff":
            lane = lax.broadcasted_iota(jnp.int32, q.shape, 1)
            zero = jnp.zeros_like(q)
            qs_ref[0:tq] = jnp.where(lane < C_HEAD_DIM, q, zero)
            qs_ref[tq:2 * tq] = jnp.where(lane >= C_HEAD_DIM, q, zero)
        else:
            for g in range(group):
                qs_ref[g * tq:(g + 1) * tq] = q[:, g * LANES:(g + 1) * LANES]
        m_ref[...] = jnp.full(m_ref.shape, -jnp.inf, F32)
        acc_ref[...] = jnp.zeros(acc_ref.shape, F32)

    k = k_ref[0]
    v = v_ref[0]
    v_aug = jnp.concatenate([v, jnp.ones_like(v)], axis=1)
    for c in range(group * tq // FLASH_SUB_ROWS):
        sl = slice(c * FLASH_SUB_ROWS, (c + 1) * FLASH_SUB_ROWS)
        s = lax.dot_general(qs_ref[sl], k, (((1,), (1,)), ((), ())),
                            preferred_element_type=F32)
        m_prev = m_ref[sl]
        m_next = jnp.maximum(m_prev, jnp.max(s, axis=-1, keepdims=True))
        alpha = jnp.exp2(m_prev - m_next)
        p = jnp.exp2(s - jnp.tile(m_next, (1, tk // LANES)))
        acc_ref[sl] = jnp.tile(alpha, (1, 2)) * acc_ref[sl] + jnp.dot(
            p.astype(BF16), v_aug, preferred_element_type=F32)
        m_ref[sl] = m_next

    @pl.when(kv == n_kv - 1)
    def _fin():
        acc = acc_ref[...]
        o = acc[:, :LANES] / acc[:, LANES:]
        if mode == "diff":
            lv = lam_ref[...]
            lam = (jnp.exp(jnp.sum(lv[0:1] * lv[1:2], axis=-1, keepdims=True))
                   - jnp.exp(jnp.sum(lv[2:3] * lv[3:4], axis=-1, keepdims=True)) + lambda_init)
            dlt = o[0:tq] - lam * o[tq:2 * tq]
            o_ref[0] = (_rms(dlt, sg_ref[...]) * (1.0 - lambda_init)).astype(o_ref.dtype)
        else:
            for g in range(group):
                o_ref[0, :, g * LANES:(g + 1) * LANES] = o[g * tq:(g + 1) * tq].astype(o_ref.dtype)


def _kv_tile(s):
    for t in (2816, 2048, 1024, 768, 512, 256, 128):
        if s % t == 0:
            return t
    return s


def flash_attention(q, k, v, *, mode, kv_len, lam_vecs=None, subln_g=None, lambda_init=0.0,
                    kv_start=0):
    b, n, qw = q.shape
    s = kv_len
    if mode == "diff":
        group, n_groups, qblk = 2, qw // LANES, LANES
    else:
        group = A_HEADS // A_KV_HEADS
        n_groups, qblk = A_KV_HEADS, group * LANES
    tq = min(n, FLASH_ROWS // group)
    tk = _kv_tile(s)
    n_kv = s // tk
    j0 = kv_start // tk
    assert j0 * tk == kv_start
    in_specs = [pl.BlockSpec((1, tq, qblk), lambda bi, g, i, j: (bi, i, g)),
                pl.BlockSpec((1, tk, LANES), lambda bi, g, i, j: (bi, j0 + j, g)),
                pl.BlockSpec((1, tk, LANES), lambda bi, g, i, j: (bi, j0 + j, g))]
    args = [q, k, v]
    if mode == "diff":
        in_specs += [pl.BlockSpec(lam_vecs.shape, lambda bi, g, i, j: (0, 0)),
                     pl.BlockSpec((1, LANES), lambda bi, g, i, j: (0, 0))]
        args += [lam_vecs, subln_g.reshape(1, LANES)]
    return pl.pallas_call(
        functools.partial(_flash_kernel, mode=mode, group=group, tq=tq, tk=tk, n_kv=n_kv,
                          lambda_init=lambda_init),
        grid=(b, n_groups, n // tq, n_kv),
        in_specs=in_specs,
        out_specs=pl.BlockSpec((1, tq, qblk), lambda bi, g, i, j: (bi, i, g)),
        out_shape=jax.ShapeDtypeStruct((b, n, qw), BF16),
        scratch_shapes=[pltpu.VMEM((group * tq, LANES), BF16),
                        pltpu.VMEM((group * tq, LANES), F32),
                        pltpu.VMEM((group * tq, 2 * LANES), F32)],
        compiler_params=_cparams("arbitrary", "arbitrary", "arbitrary", "arbitrary"),
        name="flash_" + mode,
    )(*args)


def _outproj_kernel(*refs, n_lhs):
    lhs = refs[:n_lhs]
    ws = refs[n_lhs:2 * n_lhs]
    x_ref, g_ref, gate_ref, o_ref = refs[2 * n_lhs:]
    y = jnp.dot(lhs[0][0].astype(BF16), ws[0][...], preferred_element_type=F32)
    for a, w in zip(lhs[1:], ws[1:]):
        y = y + jnp.dot(a[0].astype(BF16), w[...], preferred_element_type=F32)
    o_ref[0] = x_ref[0] + gate_ref[0] * _rms(y, g_ref[...])


def outproj_residual(lhs_list, w_list, x, g, gate):
    b, n, d = x.shape
    tm = min(n, 512)
    n_lhs = len(lhs_list)
    in_specs = [pl.BlockSpec((1, tm, a.shape[2]), lambda bi, i: (bi, i, 0)) for a in lhs_list]
    in_specs += [pl.BlockSpec(w.shape, lambda bi, i: (0, 0)) for w in w_list]
    in_specs += [pl.BlockSpec((1, tm, d), lambda bi, i: (bi, i, 0)),
                 pl.BlockSpec((1, d), lambda bi, i: (0, 0)),
                 pl.BlockSpec((1, 1, d), lambda bi, i: (bi, 0, 0))]
    return pl.pallas_call(
        functools.partial(_outproj_kernel, n_lhs=n_lhs),
        grid=(b, n // tm),
        in_specs=in_specs,
        out_specs=pl.BlockSpec((1, tm, d), lambda bi, i: (bi, i, 0)),
        out_shape=jax.ShapeDtypeStruct((b, n, d), F32),
        compiler_params=_cparams("arbitrary", "arbitrary"),
        name="outproj",
    )(*lhs_list, *w_list, x, g.reshape(1, d), gate.reshape(b, 1, d))


def _ffn_kernel(xm_ref, xp_ref, xn_ref, gn_ref, sc_ref, sh_ref, wa_ref, wg_ref, cwa_ref, cwg_ref,
                cba_ref, cbg_ref, wd_ref, g3_ref, gate_ref, o_ref, hx_ref,
                *, tm, halo, n_tiles, n_f):
    i = pl.program_id(1)
    j = pl.program_id(2)

    @pl.when(j == 0)
    def _prologue():
        def nm(x):
            return (_rms(x, gn_ref[...]) * (1.0 + sc_ref[0]) + sh_ref[0]).astype(BF16)
        hx_ref[halo:halo + tm] = nm(xm_ref[0])
        zero = jnp.zeros((halo, xm_ref.shape[2]), BF16)
        hx_ref[0:halo] = jnp.where(i > 0, nm(xp_ref[0]), zero)
        hx_ref[halo + tm:] = jnp.where(i < n_tiles - 1, nm(xn_ref[0]), zero)
        o_ref[0] = jnp.zeros(o_ref.shape[1:], F32)

    hx = hx_ref[...]
    rows = tm + 2 * halo

    def conv(u, cw, cb):
        up = pltpu.roll(u, 1, 0)
        dn = pltpu.roll(u, rows - 1, 0)
        return (cw[0:1] * up + cw[1:2] * u + cw[2:3] * dn + cb)[halo:halo + tm]

    a = conv(jnp.dot(hx, wa_ref[...], preferred_element_type=F32), cwa_ref[...], cba_ref[...])
    g = conv(jnp.dot(hx, wg_ref[...], preferred_element_type=F32), cwg_ref[...], cbg_ref[...])
    act = (a * (g * jax.nn.sigmoid(g))).astype(BF16)
    o_ref[0] += jnp.dot(act, wd_ref[...], preferred_element_type=F32)

    @pl.when(j == n_f - 1)
    def _epilogue():
        o_ref[0] = xm_ref[0] + gate_ref[0] * _rms(o_ref[0], g3_ref[...])


def conv_ffn_residual(x, gn, sc, sh, w_up, conv_w, conv_b, w_down, g3, gate, layer):
    b, n, d = x.shape
    dff = w_down.shape[1]
    tm = min(n, 512)
    tf = 512
    halo = BF16_SUBLANES
    nt = n // tm
    n_f = dff // tf
    row = lambda bi, i, j: (bi, i, 0)
    vec = lambda bi, i, j: (0, 0)
    bvec = lambda bi, i, j: (bi, 0, 0)
    up_a = lambda bi, i, j: (0, j)
    up_g = lambda bi, i, j: (0, n_f + j)
    wup_a = lambda bi, i, j: (layer, 0, j)
    wup_g = lambda bi, i, j: (layer, 0, n_f + j)
    conv_b = conv_b.reshape(1, 2 * dff)
    return pl.pallas_call(
        functools.partial(_ffn_kernel, tm=tm, halo=halo, n_tiles=nt, n_f=n_f),
        grid=(b, nt, n_f),
        in_specs=[pl.BlockSpec((1, tm, d), row),
                  pl.BlockSpec((1, halo, d),
                               lambda bi, i, j: (bi, jnp.maximum(i * (tm // halo) - 1, 0), 0)),
                  pl.BlockSpec((1, halo, d),
                               lambda bi, i, j: (bi, jnp.minimum((i + 1) * (tm // halo), n // halo - 1), 0)),
                  pl.BlockSpec((1, d), vec),
                  pl.BlockSpec((1, 1, d), bvec),
                  pl.BlockSpec((1, 1, d), bvec),
                  pl.BlockSpec((None, d, tf), wup_a),
                  pl.BlockSpec((None, d, tf), wup_g),
                  pl.BlockSpec((3, tf), up_a),
                  pl.BlockSpec((3, tf), up_g),
                  pl.BlockSpec((1, tf), up_a),
                  pl.BlockSpec((1, tf), up_g),
                  pl.BlockSpec((None, tf, d), lambda bi, i, j: (layer, j, 0)),
                  pl.BlockSpec((1, d), vec),
                  pl.BlockSpec((1, 1, d), bvec)],
        out_specs=pl.BlockSpec((1, tm, d), row),
        out_shape=jax.ShapeDtypeStruct((b, n, d), F32),
        scratch_shapes=[pltpu.VMEM((tm + 2 * halo, d), BF16)],
        compiler_params=_cparams("arbitrary", "arbitrary", "arbitrary"),
        name="conv_ffn",
    )(x, x, x, gn.reshape(1, d), sc.reshape(b, 1, d), sh.reshape(b, 1, d), w_up, w_up,
      conv_w, conv_w, conv_b, conv_b, w_down, g3.reshape(1, d), gate.reshape(b, 1, d))


def _filter_kernel(bands_ref, w1_ref, b1_ref, w2_ref, b2_ref, w3_ref, fr_ref, dl_ref,
                   o_ref, s_ref, *, n_tok, tr):
    i = pl.program_id(0)

    @pl.when(i == 0)
    def _init():
        s_ref[...] = jnp.zeros(s_ref.shape, F32)

    r = (i * tr + lax.broadcasted_iota(jnp.int32, (tr, 1), 0)).astype(F32)
    t = jnp.where(r < n_tok, r, 2.0 * n_tok - r)
    t_norm = t / float(max(n_tok - 1, 1))
    wang = (2.0 * math.pi / n_tok) * t
    z = wang * bands_ref[...]
    lane = lax.broadcasted_iota(jnp.int32, z.shape, 1)
    feats = jnp.where(lane == 0, t_norm,
                      jnp.where(lane <= HY_BANDS, jnp.cos(z),
                                jnp.where(lane <= 2 * HY_BANDS, -jnp.sin(z), 0.0)))
    fr = fr_ref[...]
    h = jnp.sin(fr[0:1] * (jnp.dot(feats, w1_ref[...], precision=HIGHEST,
                                   preferred_element_type=F32) + b1_ref[...]))
    h = jnp.sin(fr[1:2] * (jnp.dot(h, w2_ref[...], precision=HIGHEST,
                                   preferred_element_type=F32) + b2_ref[...]))
    hb = h.astype(BF16)
    decay = jnp.where(r == n_tok, 0.0, jnp.exp(-t_norm * dl_ref[...]))
    for o in range(HY_ORDER):
        out = jnp.dot(hb, w3_ref[o].astype(BF16), preferred_element_type=F32) * decay
        o_ref[o] = out
        s_ref[o] += jnp.sum(jnp.abs(out), axis=0, keepdims=True)


def hyena_filter_time(n_tok, w1, b1, w2, b2, w3, freq):
    ch = HY_CH
    ll = 2 * n_tok
    tr = min(n_tok, 512)
    per_half = n_tok // tr
    lane = jnp.arange(LANES)
    bands = jnp.linspace(1e-4, HY_BANDS - 1, HY_BANDS, dtype=F32)
    bands_l = jnp.where((lane >= 1) & (lane <= HY_BANDS), bands[jnp.clip(lane - 1, 0, HY_BANDS - 1)],
                        jnp.where((lane > HY_BANDS) & (lane <= 2 * HY_BANDS),
                                  bands[jnp.clip(lane - 1 - HY_BANDS, 0, HY_BANDS - 1)], 0.0))
    w1p = jnp.zeros((LANES, HY_FILTER_W), F32).at[:w1.shape[0]].set(w1)
    deltas = jnp.abs(jnp.linspace(math.log(HY_TARGET) / HY_FAST, math.log(HY_TARGET) / HY_SLOW,
                                  ch, dtype=F32))
    w3o = jnp.swapaxes(w3.reshape(HY_FILTER_W, HY_ORDER, 2 * ch), 0, 1)
    const = lambda i: (0, 0)
    return pl.pallas_call(
        functools.partial(_filter_kernel, n_tok=n_tok, tr=tr),
        grid=(ll // tr,),
        in_specs=[pl.BlockSpec((1, LANES), const),
                  pl.BlockSpec((LANES, HY_FILTER_W), const),
                  pl.BlockSpec((1, HY_FILTER_W), const),
                  pl.BlockSpec((HY_FILTER_W, HY_FILTER_W), const),
                  pl.BlockSpec((1, HY_FILTER_W), const),
                  pl.BlockSpec((HY_ORDER, HY_FILTER_W, ch), lambda i: (0, 0, i // per_half)),
                  pl.BlockSpec((2, HY_FILTER_W), const),
                  pl.BlockSpec((1, ch), const)],
        out_specs=[pl.BlockSpec((HY_ORDER, tr, ch), lambda i: (0, i, 0)),
                   pl.BlockSpec((HY_ORDER, 1, ch), lambda i: (0, 0, 0))],
        out_shape=[jax.ShapeDtypeStruct((HY_ORDER, ll, ch), F32),
                   jax.ShapeDtypeStruct((HY_ORDER, 1, ch), F32)],
        compiler_params=_cparams("arbitrary"),
        name="hyena_filter",
    )(bands_l.reshape(1, LANES), w1p, b1.reshape(1, -1), w2, b2.reshape(1, -1), w3o, freq,
      deltas.reshape(1, ch))


def _dft_dot(f, x):
    return jnp.dot(f.astype(BF16), x.astype(BF16), preferred_element_type=F32)


def _stage_a_kernel(f_ref, x_ref, o_ref):
    parts = x_ref.shape[0]
    f = f_ref[...]
    for r in range(x_ref.shape[2]):
        z = jnp.concatenate([x_ref[p, :, r, :] for p in range(parts)], axis=0)
        y = _dft_dot(f, z)
        half = y.shape[0] // 2
        o_ref[0, :, r, :] = y[:half]
        o_ref[1, :, r, :] = y[half:]


def dft_stage_a(f, x, g):
    _, p, rows, r, c = x.shape
    k1 = p * rows
    sub = 8
    return pl.pallas_call(
        _stage_a_kernel,
        grid=(r // sub,),
        in_specs=[pl.BlockSpec(f.shape, lambda j: (0, 0)),
                  pl.BlockSpec((None, p, rows, sub, c), lambda j: (g, 0, 0, j, 0))],
        out_specs=pl.BlockSpec((2, k1, sub, c), lambda j: (0, 0, j, 0)),
        out_shape=jax.ShapeDtypeStruct((2, k1, r, c), F32),
        compiler_params=_cparams("arbitrary"),
        name="dft_stage_a",
    )(f, x)


def _stage_c_filter_kernel(a_ref, g_ref, sc_ref, o_ref, *, kc):
    for k in range(kc):
        r = jnp.concatenate([a_ref[0, k], a_ref[1, k]], axis=0)
        y = (_dft_dot(g_ref[k], r) * sc_ref[...]).astype(o_ref.dtype)
        half = y.shape[0] // 2
        o_ref[0, k] = y[:half]
        o_ref[1, k] = y[half:]


def dft_stage_c_filter(a, g, scale):
    _, k1, r, c = a.shape
    kc = 8
    tc = 256
    blk = pl.BlockSpec((2, kc, r, tc), lambda i, j: (0, i, 0, j))
    return pl.pallas_call(
        functools.partial(_stage_c_filter_kernel, kc=kc),
        grid=(k1 // kc, c // tc),
        in_specs=[blk,
                  pl.BlockSpec((kc, 2 * r, 2 * r), lambda i, j: (i, 0, 0)),
                  pl.BlockSpec((1, tc), lambda i, j: (0, j))],
        out_specs=blk,
        out_shape=jax.ShapeDtypeStruct(a.shape, BF16),
        compiler_params=_cparams("arbitrary", "arbitrary"),
        name="dft_stage_c_filter",
    )(a, g, scale)


def _stage_c_conv_kernel(a_ref, h_ref, g_ref, gi_ref, o_ref, *, kc):
    for k in range(kc):
        r = jnp.concatenate([a_ref[0, k], a_ref[1, k]], axis=0)
        y = _dft_dot(g_ref[k], r)
        half = y.shape[0] // 2
        yre, yim = y[:half], y[half:]
        hre, him = h_ref[0, k].astype(F32), h_ref[1, k].astype(F32)
        p = jnp.concatenate([yre * hre - yim * him, yre * him + yim * hre], axis=0)
        d = _dft_dot(gi_ref[k], p).astype(o_ref.dtype)
        o_ref[0, k] = d[:half]
        o_ref[1, k] = d[half:]


def dft_stage_c_conv(a, hspec, g, gi):
    _, k1, r, c = a.shape
    kc = 8
    tc = 256
    blk = pl.BlockSpec((2, kc, r, tc), lambda i, j: (0, i, 0, j))
    mat = pl.BlockSpec((kc, 2 * r, 2 * r), lambda i, j: (i, 0, 0))
    return pl.pallas_call(
        functools.partial(_stage_c_conv_kernel, kc=kc),
        grid=(k1 // kc, c // tc),
        in_specs=[blk, blk, mat, mat],
        out_specs=blk,
        out_shape=jax.ShapeDtypeStruct(a.shape, F32),
        compiler_params=_cparams("arbitrary", "arbitrary"),
        name="dft_stage_c_conv",
    )(a, hspec, g, gi)


def _stage_a_inv_kernel(f_ref, d_ref, u_ref, gt_ref, sk_ref, o_ref):
    f = f_ref[...]
    sk = sk_ref[...]
    for r in range(d_ref.shape[2]):
        d = jnp.concatenate([d_ref[0, :, r, :], d_ref[1, :, r, :]], axis=0)
        y = _dft_dot(f, d)
        half = y.shape[0] // 2
        for bi in range(2):
            yb = y[bi * half:(bi + 1) * half]
            o_ref[bi, :, r, :] = gt_ref[bi, :, r, :] * (yb + u_ref[bi, :, r, :] * sk)


def dft_stage_a_inv(f, d, u, gu, gate, gg, skip):
    _, k1, r, c = d.shape
    half = k1 // 2
    sub = 8
    io = pl.BlockSpec((2, half, sub, c), lambda j: (0, 0, j, 0))
    return pl.pallas_call(
        _stage_a_inv_kernel,
        grid=(r // sub,),
        in_specs=[pl.BlockSpec(f.shape, lambda j: (0, 0)),
                  pl.BlockSpec((2, k1, sub, c), lambda j: (0, 0, j, 0)),
                  pl.BlockSpec((None, 2, half, sub, c), lambda j: (gu, 0, 0, j, 0)),
                  pl.BlockSpec((None, 2, half, sub, c), lambda j: (gg, 0, 0, j, 0)),
                  pl.BlockSpec((1, c), lambda j: (0, 0))],
        out_specs=io,
        out_shape=jax.ShapeDtypeStruct((2, half, r, c), F32),
        compiler_params=_cparams("arbitrary"),
        name="dft_stage_a_inv",
    )(f, d, u, gate, skip)


def _dft_tables(n_tok):
    ll = 2 * n_tok
    r = DFT_RADIX
    k1n = ll // r
    half = k1n // 2
    two_pi = 2.0 * math.pi

    def cs(m, period):
        ang = (m % period).astype(F32) * (two_pi / period)
        return jnp.cos(ang), jnp.sin(ang)

    k1 = jnp.arange(k1n, dtype=jnp.int32)
    c, s = cs(k1[:, None] * k1[None, :half], k1n)
    fa_data = jnp.concatenate([jnp.concatenate([c, s], 1), jnp.concatenate([-s, c], 1)], 0)
    c, s = cs(k1[:, None] * k1[None, :], k1n)
    fa_filt = jnp.concatenate([c, -s], 0)
    c, s = cs(k1[:half, None] * k1[None, :], k1n)
    fa_inv = jnp.concatenate([jnp.concatenate([c, -s], 1), jnp.concatenate([s, c], 1)], 0)
    idx = jnp.arange(r, dtype=jnp.int32)
    ca, sa = cs(idx[:, None] * idx[None, :], r)
    cb, sb = cs(k1[:, None] * idx[None, :], ll)
    c = ca[None] * cb[:, None, :] - sa[None] * sb[:, None, :]
    s = sa[None] * cb[:, None, :] + ca[None] * sb[:, None, :]
    g_fwd = jnp.concatenate([jnp.concatenate([c, s], 2), jnp.concatenate([-s, c], 2)], 1)
    ct, st = jnp.swapaxes(c, 1, 2), jnp.swapaxes(s, 1, 2)
    g_inv = jnp.concatenate([jnp.concatenate([ct, -st], 2), jnp.concatenate([st, ct], 2)], 1)
    return tuple(t.astype(BF16) for t in (fa_data, fa_filt, fa_inv, g_fwd, g_inv))


def hyena_long(pc, filt, l1, skip):
    _, b, n, c = pc.shape
    assert b == 2, "the two batch entries ride as real/imaginary parts"
    ll = 2 * n
    r = DFT_RADIX
    k1n = ll // r
    half = k1n // 2
    fa_data, fa_filt, fa_inv, g_fwd, g_inv = _dft_tables(n)
    pcs = pc.reshape(HY_ORDER + 1, 2, half, r, c)
    filts = filt.reshape(HY_ORDER, 1, k1n, r, c)
    u, gu = pcs, 0
    for o in range(HY_ORDER):
        af = dft_stage_a(fa_filt, filts, o)
        hspec = dft_stage_c_filter(af, g_fwd, 1.0 / (l1[o] * ll))
        a = dft_stage_a(fa_data, u, gu)
        d = dft_stage_c_conv(a, hspec, g_fwd, g_inv)
        z = dft_stage_a_inv(fa_inv, d, u, gu, pcs, o + 1, skip[o].reshape(1, c))
        u, gu = z[None], 0
    return z.reshape(b, n, c)


def _hyena_short_kernel(ff_ref, fr_ref, fi_ref, pc_ref, filt_ref, l1_ref, skip_ref, o_ref, *, n, ll):
    u = [pc_ref[0, 0], pc_ref[0, 1]]
    for o in range(HY_ORDER):
        hs = _dft_dot(fr_ref[...], filt_ref[o]) * (1.0 / (l1_ref[o] * ll))
        y = _dft_dot(ff_ref[...], jnp.concatenate(u, axis=0))
        yre, yim, hre, him = y[:ll], y[ll:], hs[:ll], hs[ll:]
        p = jnp.concatenate([yre * hre - yim * him, yre * him + yim * hre], axis=0)
        d = _dft_dot(fi_ref[...], p)
        sk = skip_ref[o:o + 1]
        u = [pc_ref[o + 1, bi] * (d[bi * n:(bi + 1) * n] + u[bi] * sk) for bi in range(2)]
    o_ref[0] = u[0].astype(o_ref.dtype)
    o_ref[1] = u[1].astype(o_ref.dtype)


def hyena_short(pc, filt, l1, skip, out_dtype):
    _, b, n, c = pc.shape
    assert b == 2
    ll = 2 * n
    two_pi = 2.0 * math.pi
    kk = jnp.arange(ll, dtype=jnp.int32)
    ang = ((kk[:, None] * kk[None, :]) % ll).astype(F32) * (two_pi / ll)
    cf, sf = jnp.cos(ang), jnp.sin(ang)
    f_real = jnp.concatenate([cf, -sf], 0)
    cn, sn = cf[:, :n], sf[:, :n]
    f_fwd = jnp.concatenate([jnp.concatenate([cn, sn], 1), jnp.concatenate([-sn, cn], 1)], 0)
    ci, si = cf[:n, :], sf[:n, :]
    f_inv = jnp.concatenate([jnp.concatenate([ci, -si], 1), jnp.concatenate([si, ci], 1)], 0)
    tc = 256
    const = lambda j: (0, 0)
    return pl.pallas_call(
        functools.partial(_hyena_short_kernel, n=n, ll=ll),
        grid=(c // tc,),
        in_specs=[pl.BlockSpec(f_fwd.shape, const),
                  pl.BlockSpec(f_real.shape, const),
                  pl.BlockSpec(f_inv.shape, const),
                  pl.BlockSpec((3, 2, n, tc), lambda j: (0, 0, 0, j)),
                  pl.BlockSpec((HY_ORDER, ll, tc), lambda j: (0, 0, j)),
                  pl.BlockSpec((HY_ORDER, 1, tc), lambda j: (0, 0, j)),
                  pl.BlockSpec((HY_ORDER, tc), lambda j: (0, j))],
        out_specs=pl.BlockSpec((2, n, tc), lambda j: (0, 0, j)),
        out_shape=jax.ShapeDtypeStruct((2, n, c), out_dtype),
        compiler_params=_cparams("arbitrary"),
        name="hyena_short",
    )(f_fwd, f_real, f_inv, pc, filt, l1, skip)


def hyena_mixer(pc, w1, b1, w2, b2, w3, freq, skip, out_dtype):
    n = pc.shape[2]
    filt, l1 = hyena_filter_time(n, w1, b1, w2, b2, w3, freq)
    if (2 * n) % (DFT_RADIX * BF16_SUBLANES) == 0 and n >= 1024:
        return hyena_long(pc, filt, l1, skip)
    return hyena_short(pc, filt, l1, skip, out_dtype)


def _deinterleave_cols(w, hd):
    lead = w.shape[:-1]
    wh = w.reshape(lead + (w.shape[-1] // hd, hd // 2, 2))
    return jnp.concatenate([wh[..., 0], wh[..., 1]], axis=-1).reshape(w.shape)


def _rope_tables(n_tok, rot_dim):
    rows = n_tok // GRID_W
    row = jnp.broadcast_to(jnp.arange(rows, dtype=jnp.int32)[:, None], (rows, GRID_W)).reshape(n_tok)
    col = jnp.broadcast_to(jnp.arange(GRID_W, dtype=jnp.int32)[None, :], (rows, GRID_W)).reshape(n_tok)
    axis_dim = rot_dim // 2
    inv_freq = ROPE_THETA ** (-jnp.arange(0, axis_dim, 2, dtype=F32) / axis_dim)
    ang = jnp.concatenate([row.astype(F32)[:, None] * inv_freq,
                           col.astype(F32)[:, None] * inv_freq], axis=-1)
    c, s = jnp.cos(ang), jnp.sin(ang)
    reps = LANES // rot_dim
    return jnp.tile(jnp.concatenate([c, c], -1), (1, reps)), jnp.tile(jnp.concatenate([-s, s], -1), (1, reps))


def _mixer_ab(h, hc, with_ctx, w_in, w_out, qk_g, hy):
    b, n, _ = h.shape
    conv_w, conv_b, w1, b1, w2, b2, w3, freq, skip = hy
    scale = A_HEAD_DIM ** -0.5 * LOG2E
    wq = _deinterleave_cols(w_in[:, :A_Q_W], A_HEAD_DIM).astype(BF16)
    wk = _deinterleave_cols(w_in[:, A_Q_W:A_Q_W + A_KV_W], A_HEAD_DIM).astype(BF16)
    wv = w_in[:, A_Q_W + A_KV_W:A_QKV_W].astype(BF16)
    wh = w_in[:, A_QKV_W:].astype(BF16)
    gq = jnp.tile(_deinterleave_cols(qk_g[0], A_HEAD_DIM) * scale, A_HEADS)
    gk = jnp.tile(_deinterleave_cols(qk_g[1], A_HEAD_DIM), A_KV_HEADS)
    ones_v = jnp.ones((A_KV_W,), F32)
    cosf, sinf = _rope_tables(n, A_HEAD_DIM)

    n_ctx = hc.shape[1]
    q = inproj(h, wq, gq, cosf, sinf, norm=True)
    k_all = inproj(hc, wk, gk, norm=True, append_at=n,
                   append_to=inproj(h, wk, gk, cosf, sinf, norm=True, tail_rows=n_ctx))
    v_all = inproj(hc, wv, ones_v, append_at=n, append_to=inproj(h, wv, ones_v, tail_rows=n_ctx))
    o_att = flash_attention(q, k_all, v_all, mode="gqa", kv_len=n + n_ctx)
    pc = inproj_conv(h, wh, conv_w, conv_b, HY_ORDER + 1)
    o_hy = hyena_mixer(pc, w1, b1, w2, b2, w3, freq, skip, BF16)
    wo = w_out.astype(BF16)
    lat = ([o_att, o_hy], [wo[:A_Q_W], wo[A_Q_W:]])
    if not with_ctx:
        return lat, None
    q_c = inproj(hc, wq, gq, norm=True)
    o_att_c = flash_attention(q_c, k_all, v_all, mode="gqa", kv_start=n, kv_len=n_ctx)
    pc_c = inproj_conv(hc, wh, conv_w, conv_b, HY_ORDER + 1)
    o_hy_c = hyena_mixer(pc_c, w1, b1, w2, b2, w3, freq, skip, BF16)
    return lat, ([o_att_c, o_hy_c], [wo[:A_Q_W], wo[A_Q_W:]])


def _mixer_c(h, hc, with_ctx, lambda_init, w_in, w_out, lam_vecs, subln_g):
    b, n, _ = h.shape
    scale = C_HEAD_DIM ** -0.5 * LOG2E
    wq = _deinterleave_cols(w_in[:, :C_Q_W], C_HEAD_DIM).astype(BF16)
    wk = _deinterleave_cols(w_in[:, C_Q_W:2 * C_Q_W], C_HEAD_DIM).astype(BF16)
    wv = w_in[:, 2 * C_Q_W:].astype(BF16)
    gq = jnp.full((C_Q_W,), scale, F32)
    ones = jnp.ones((C_Q_W,), F32)
    cosf, sinf = _rope_tables(n, C_HEAD_DIM)

    n_ctx = hc.shape[1]
    q = inproj(h, wq, gq, cosf, sinf, hd=C_HEAD_DIM)
    k_all = inproj(hc, wk, ones, append_at=n,
                   append_to=inproj(h, wk, ones, cosf, sinf, hd=C_HEAD_DIM, tail_rows=n_ctx))
    v_all = inproj(hc, wv, ones, append_at=n, append_to=inproj(h, wv, ones, tail_rows=n_ctx))
    attn = functools.partial(flash_attention, mode="diff", lam_vecs=lam_vecs, subln_g=subln_g,
                             lambda_init=lambda_init)
    wo = w_out.astype(BF16)
    lat = ([attn(q, k_all, v_all, kv_len=n + n_ctx)], [wo])
    if not with_ctx:
        return lat, None
    q_c = inproj(hc, wq, gq)
    return lat, ([attn(q_c, k_all, v_all, kv_start=n, kv_len=n_ctx)], [wo])


def kernel(x, c, ctx, c_ctx, ada_w, ada_b, norm_g, ab_w_in, ab_w_out, ab_qk_g,
           hy_conv_w, hy_conv_b, hy_w1, hy_b1, hy_w2, hy_b2, hy_w3, hy_freq, hy_skip,
           dc_w_in, dc_w_out, dc_lambda, dc_subln_g,
           ffn_w_up, ffn_conv_w, ffn_conv_b, ffn_w_down):
    depth = ada_w.shape[0]
    b, n, d = x.shape
    pad_rows = (-(b + 1)) % 8
    cvec = jnp.concatenate([c, c_ctx[None, :], jnp.zeros((pad_rows, d), F32)], axis=0)
    mod_all = ada_mod(cvec, ada_w, ada_b)
    w_up = ffn_w_up.astype(BF16)
    w_down = ffn_w_down.astype(BF16)
    for i in range(depth):
        j = i // 2
        with_ctx = i < depth - 1
        mod = mod_all[i, :b].reshape(b, 6, d)
        mod_c = jnp.broadcast_to(mod_all[i, b].reshape(1, 6, d), (b, 6, d))
        sh1, sc1, g1, sh2, sc2, g2 = (mod[:, t] for t in range(6))
        sh1c, sc1c, g1c, sh2c, sc2c, g2c = (mod_c[:, t] for t in range(6))
        h = norm_mod(x, norm_g[i, 0], sc1, sh1)
        hc = norm_mod(ctx, norm_g[i, 0], sc1c, sh1c)
        if i % 2 == 0:
            hy = (hy_conv_w[j], hy_conv_b[j], hy_w1[j], hy_b1[j], hy_w2[j], hy_b2[j],
                  hy_w3[j], hy_freq[j], hy_skip[j])
            lat, cx = _mixer_ab(h, hc, with_ctx, ab_w_in[j], ab_w_out[j], ab_qk_g[j], hy)
        else:
            lambda_init = 0.8 - 0.6 * math.exp(-0.3 * i)
            lat, cx = _mixer_c(h, hc, with_ctx, lambda_init, dc_w_in[j], dc_w_out[j],
                               dc_lambda[j], dc_subln_g[j])
        x = outproj_residual(lat[0], lat[1], x, norm_g[i, 1], g1)
        x = conv_ffn_residual(x, norm_g[i, 2], sc2, sh2, w_up, ffn_conv_w[i], ffn_conv_b[i],
                              w_down, norm_g[i, 3], g2, i)
        if with_ctx:
            ctx = outproj_residual(cx[0], cx[1], ctx, norm_g[i, 1], g1c)
            ctx = conv_ffn_residual(ctx, norm_g[i, 2], sc2c, sh2c, w_up, ffn_conv_w[i],
                                    ffn_conv_b[i], w_down, norm_g[i, 3], g2c, i)
    return x
```

```python
import functools
import math

import jax
import jax.numpy as jnp
from jax import lax
from jax.experimental import pallas as pl
from jax.experimental.pallas import tpu as pltpu

F32 = jnp.float32
BF16 = jnp.bfloat16
HIGHEST = lax.Precision.HIGHEST

D_MODEL = 2048
GRID_W = 64
EPS = 1e-6
ROPE_THETA = 10000.0
A_HEADS, A_KV_HEADS, A_HEAD_DIM = 8, 2, 128
A_Q_W = A_HEADS * A_HEAD_DIM
A_KV_W = A_KV_HEADS * A_HEAD_DIM
A_QKV_W = A_Q_W + 2 * A_KV_W
HY_CH = D_MODEL // 2
HY_ORDER = 2
HY_FILTER_W = 64
HY_BANDS = 16
HY_FAST, HY_SLOW, HY_TARGET = 0.3, 1.5, 1e-2
C_HEADS, C_HEAD_DIM = 16, 64
C_Q_W = C_HEADS * 2 * C_HEAD_DIM
D_FF = 5632

LANES = 128
BF16_SUBLANES = 16
VMEM_LIMIT = 56 * 1024 * 1024
DFT_RADIX = 128
FLASH_ROWS = 2048
FLASH_SUB_ROWS = 128
LOG2E = math.log2(math.e)


def _cparams(*sem):
    return pltpu.CompilerParams(dimension_semantics=sem, vmem_limit_bytes=VMEM_LIMIT)


def _rms(x, g):
    return x * lax.rsqrt(jnp.mean(x * x, axis=-1, keepdims=True) + EPS) * g


def _ada_kernel(c_ref, w_ref, b_ref, o_ref):
    c = c_ref[...]
    s = c * jax.nn.sigmoid(c)
    rows = s.shape[0]
    s_hi = s.astype(BF16).astype(F32)
    s2 = jnp.concatenate([s_hi, s - s_hi], axis=0).astype(BF16)
    w = w_ref[0]
    w_hi = w.astype(BF16)
    w_lo = (w - w_hi.astype(F32)).astype(BF16)
    both = jnp.dot(s2, w_hi, preferred_element_type=F32)
    hi_lo = jnp.dot(s2, w_lo, preferred_element_type=F32)[:rows]
    o_ref[0] = both[:rows] + both[rows:] + hi_lo + b_ref[0]


def ada_mod(cvec, ada_w, ada_b):
    depth, d, n6 = ada_w.shape
    rows = cvec.shape[0]
    tn = 1024
    return pl.pallas_call(
        _ada_kernel,
        grid=(depth, n6 // tn),
        in_specs=[pl.BlockSpec((rows, d), lambda l, j: (0, 0)),
                  pl.BlockSpec((1, d, tn), lambda l, j: (l, 0, j)),
                  pl.BlockSpec((1, 1, tn), lambda l, j: (l, 0, j))],
        out_specs=pl.BlockSpec((1, rows, tn), lambda l, j: (l, 0, j)),
        out_shape=jax.ShapeDtypeStruct((depth, rows, n6), F32),
        compiler_params=_cparams("arbitrary", "arbitrary"),
        name="ada_mod",
    )(cvec, ada_w, ada_b.reshape(depth, 1, n6))


def _norm_mod_kernel(x_ref, g_ref, sc_ref, sh_ref, o_ref):
    y = _rms(x_ref[0], g_ref[...])
    o_ref[0] = (y * (1.0 + sc_ref[0]) + sh_ref[0]).astype(o_ref.dtype)


def norm_mod(x, g, sc, sh):
    b, n, d = x.shape
    tm = min(n, 512)
    return pl.pallas_call(
        _norm_mod_kernel,
        grid=(b, n // tm),
        in_specs=[pl.BlockSpec((1, tm, d), lambda bi, i: (bi, i, 0)),
                  pl.BlockSpec((1, d), lambda bi, i: (0, 0)),
                  pl.BlockSpec((1, 1, d), lambda bi, i: (bi, 0, 0)),
                  pl.BlockSpec((1, 1, d), lambda bi, i: (bi, 0, 0))],
        out_specs=pl.BlockSpec((1, tm, d), lambda bi, i: (bi, i, 0)),
        out_shape=jax.ShapeDtypeStruct((b, n, d), BF16),
        compiler_params=_cparams("arbitrary", "arbitrary"),
        name="norm_mod",
    )(x, g.reshape(1, d), sc.reshape(b, 1, d), sh.reshape(b, 1, d))


def _inproj_kernel(*refs, hd, norm, rope, appending, n_tiles, has_tail):
    o_ref = refs[-1]
    if appending:
        refs = refs[:-2]
    else:
        refs = refs[:-1]
    if rope:
        h_ref, w_ref, g_ref, cos_ref, sin_ref = refs
    else:
        h_ref, w_ref, g_ref = refs
    i = pl.program_id(1)

    def _tail():
        o_ref[0] = jnp.zeros(o_ref.shape[1:], o_ref.dtype)

    def _body():
        acc = jnp.dot(h_ref[0], w_ref[...], preferred_element_type=F32)
        tn = acc.shape[1]
        if rope:
            cosf = cos_ref[...]
            sinf = sin_ref[...]
            if hd != LANES:
                lane = lax.broadcasted_iota(jnp.int32, cosf.shape, 1)
                first_half = (lane % hd) < (hd // 2)
        for c in range(tn // LANES):
            x = acc[:, c * LANES:(c + 1) * LANES]
            if norm:
                x = x * lax.rsqrt(jnp.mean(x * x, axis=-1, keepdims=True) + EPS)
            x = x * g_ref[:, c * LANES:(c + 1) * LANES]
            if rope:
                if hd == LANES:
                    partner = pltpu.roll(x, LANES // 2, 1)
                else:
                    partner = jnp.where(first_half, pltpu.roll(x, LANES - hd // 2, 1),
                                        pltpu.roll(x, hd // 2, 1))
                x = x * cosf + partner * sinf
            o_ref[0, :, c * LANES:(c + 1) * LANES] = x.astype(o_ref.dtype)

    if has_tail:
        pl.when(i >= n_tiles)(_tail)
        pl.when(i < n_tiles)(_body)
    else:
        _body()


def inproj(h, w, gvec, cosf=None, sinf=None, *, hd=LANES, norm=False, out_dtype=BF16,
           tail_rows=0, append_to=None, append_at=0):
    b, n, d = h.shape
    ncol = w.shape[1]
    rope = cosf is not None
    tm = min(n, 512)
    tn = min(ncol, 2048)
    n_tiles = n // tm
    last = n_tiles - 1
    in_specs = [pl.BlockSpec((1, tm, d), lambda bi, i, j: (bi, jnp.minimum(i, last), 0)),
                pl.BlockSpec((d, tn), lambda bi, i, j: (0, j)),
                pl.BlockSpec((1, tn), lambda bi, i, j: (0, j))]
    args = [h, w, gvec.reshape(1, ncol)]
    if rope:
        in_specs += [pl.BlockSpec((tm, LANES), lambda bi, i, j: (jnp.minimum(i, last), 0)),
                     pl.BlockSpec((tm, LANES), lambda bi, i, j: (jnp.minimum(i, last), 0))]
        args += [cosf, sinf]
    appending = append_to is not None
    if appending:
        total = append_to.shape[1]
        first = append_at // tm
        assert first * tm == append_at and append_at + n <= total and append_to.dtype == out_dtype
        tail_tiles = 0
        in_specs.append(pl.BlockSpec(memory_space=pl.ANY))
        args.append(append_to)
        aliases = {len(args) - 1: 0}
    else:
        tail_tiles = -(-tail_rows // tm)
        total, first, aliases = (n_tiles + tail_tiles) * tm, 0, {}
    return pl.pallas_call(
        functools.partial(_inproj_kernel, hd=hd, norm=norm, rope=rope, appending=appending,
                          n_tiles=n_tiles, has_tail=tail_tiles > 0),
        grid=(b, n_tiles + tail_tiles, ncol // tn),
        in_specs=in_specs,
        out_specs=pl.BlockSpec((1, tm, tn), lambda bi, i, j: (bi, first + i, j)),
        out_shape=jax.ShapeDtypeStruct((b, total, ncol), out_dtype),
        input_output_aliases=aliases,
        compiler_params=_cparams("arbitrary", "arbitrary", "arbitrary"),
        name="inproj",
    )(*args)


def _inproj_conv_kernel(hm_ref, hp_ref, hn_ref, w_ref, cw_ref, cb_ref, o_ref, *, n_tiles):
    i = pl.program_id(2)
    hp, hm, hn = hp_ref[0], hm_ref[0], hn_ref[0]
    halo, tm = hp.shape[0], hm.shape[0]
    zero = jnp.zeros_like(hp)
    lhs = jnp.concatenate([jnp.where(i > 0, hp, zero), hm,
                           jnp.where(i < n_tiles - 1, hn, zero)], axis=0)
    u = jnp.dot(lhs, w_ref[...], preferred_element_type=F32)
    rows = tm + 2 * halo
    cw = cw_ref[...]
    conv = cw[0:1] * pltpu.roll(u, 1, 0) + cw[1:2] * u + cw[2:3] * pltpu.roll(u, rows - 1, 0)
    o_ref[0, 0] = conv[halo:halo + tm] + cb_ref[...]


def inproj_conv(h, w, conv_w, conv_b, groups):
    b, n, d = h.shape
    ncol = w.shape[1]
    c = ncol // groups
    tm = min(n, 512)
    tn = min(c, 1024)
    halo = BF16_SUBLANES
    nt = n // tm
    per = c // tn
    return pl.pallas_call(
        functools.partial(_inproj_conv_kernel, n_tiles=nt),
        grid=(ncol // tn, b, nt),
        in_specs=[pl.BlockSpec((1, tm, d), lambda j, bi, i: (bi, i, 0)),
                  pl.BlockSpec((1, halo, d),
                               lambda j, bi, i: (bi, jnp.maximum(i * (tm // halo) - 1, 0), 0)),
                  pl.BlockSpec((1, halo, d),
                               lambda j, bi, i: (bi, jnp.minimum((i + 1) * (tm // halo), n // halo - 1), 0)),
                  pl.BlockSpec((d, tn), lambda j, bi, i: (0, j)),
                  pl.BlockSpec((3, tn), lambda j, bi, i: (0, j)),
                  pl.BlockSpec((1, tn), lambda j, bi, i: (0, j))],
        out_specs=pl.BlockSpec((1, 1, tm, tn), lambda j, bi, i: (j // per, bi, i, j % per)),
        out_shape=jax.ShapeDtypeStruct((groups, b, n, c), F32),
        compiler_params=_cparams("arbitrary", "arbitrary", "arbitrary"),
        name="inproj_conv",
    )(h, h, h, w, conv_w, conv_b.reshape(1, ncol))


def _flash_kernel(*refs, mode, group, tq, tk, n_kv, lambda_init):
    if mode == "diff":
        q_ref, k_ref, v_ref, lam_ref, sg_ref, o_ref, qs_ref, m_ref, acc_ref = refs
    else:
        q_ref, k_ref, v_ref, o_ref, qs_ref, m_ref, acc_ref = refs
    kv = pl.program_id(3)

    @pl.when(kv == 0)
    def _init():
        q = q_ref[0]
        if mode == "diff":
            lane = lax.broadcasted_iota(jnp.int32, q.shape, 1)
            zero = jnp.zeros_like(q)
            qs_ref[0:tq] = jnp.where(lane < C_HEAD_DIM, q, zero)
            qs_ref[tq:2 * tq] = jnp.where(lane >= C_HEAD_DIM, q, zero)
        else:
            for g in range(group):
                qs_ref[g * tq:(g + 1) * tq] = q[:, g * LANES:(g + 1) * LANES]
        m_ref[...] = jnp.full(m_ref.shape, -jnp.inf, F32)
        acc_ref[...] = jnp.zeros(acc_ref.shape, F32)

    k = k_ref[0]
    v = v_ref[0]
    v_aug = jnp.concatenate([v, jnp.ones_like(v)], axis=1)
    for c in range(group * tq // FLASH_SUB_ROWS):
        sl = slice(c * FLASH_SUB_ROWS, (c + 1) * FLASH_SUB_ROWS)
        s = lax.dot_general(qs_ref[sl], k, (((1,), (1,)), ((), ())),
                            preferred_element_type=F32)
        m_prev = m_ref[sl]
        m_next = jnp.maximum(m_prev, jnp.max(s, axis=-1, keepdims=True))
        alpha = jnp.exp2(m_prev - m_next)
        p = jnp.exp2(s - jnp.tile(m_next, (1, tk // LANES)))
        acc_ref[sl] = jnp.tile(alpha, (1, 2)) * acc_ref[sl] + jnp.dot(
            p.astype(BF16), v_aug, preferred_element_type=F32)
        m_ref[sl] = m_next

    @pl.when(kv == n_kv - 1)
    def _fin():
        acc = acc_ref[...]
        o = acc[:, :LANES] / acc[:, LANES:]
        if mode == "diff":
            lv = lam_ref[...]
            lam = (jnp.exp(jnp.sum(lv[0:1] * lv[1:2], axis=-1, keepdims=True))
                   - jnp.exp(jnp.sum(lv[2:3] * lv[3:4], axis=-1, keepdims=True)) + lambda_init)
            dlt = o[0:tq] - lam * o[tq:2 * tq]
            o_ref[0] = (_rms(dlt, sg_ref[...]) * (1.0 - lambda_init)).astype(o_ref.dtype)
        else:
            for g in range(group):
                o_ref[0, :, g * LANES:(g + 1) * LANES] = o[g * tq:(g + 1) * tq].astype(o_ref.dtype)


def _kv_tile(s):
    for t in (2816, 2048, 1024, 768, 512, 256, 128):
        if s % t == 0:
            return t
    return s


def flash_attention(q, k, v, *, mode, kv_len, lam_vecs=None, subln_g=None, lambda_init=0.0,
                    kv_start=0):
    b, n, qw = q.shape
    s = kv_len
    if mode == "diff":
        group, n_groups, qblk = 2, qw // LANES, LANES
    else:
        group = A_HEADS // A_KV_HEADS
        n_groups, qblk = A_KV_HEADS, group * LANES
    tq = min(n, FLASH_ROWS // group)
    tk = _kv_tile(s)
    n_kv = s // tk
    j0 = kv_start // tk
    assert j0 * tk == kv_start
    in_specs = [pl.BlockSpec((1, tq, qblk), lambda bi, g, i, j: (bi, i, g)),
                pl.BlockSpec((1, tk, LANES), lambda bi, g, i, j: (bi, j0 + j, g)),
                pl.BlockSpec((1, tk, LANES), lambda bi, g, i, j: (bi, j0 + j, g))]
    args = [q, k, v]
    if mode == "diff":
        in_specs += [pl.BlockSpec(lam_vecs.shape, lambda bi, g, i, j: (0, 0)),
                     pl.BlockSpec((1, LANES), lambda bi, g, i, j: (0, 0))]
        args += [lam_vecs, subln_g.reshape(1, LANES)]
    return pl.pallas_call(
        functools.partial(_flash_kernel, mode=mode, group=group, tq=tq, tk=tk, n_kv=n_kv,
                          lambda_init=lambda_init),
        grid=(b, n_groups, n // tq, n_kv),
        in_specs=in_specs,
        out_specs=pl.BlockSpec((1, tq, qblk), lambda bi, g, i, j: (bi, i, g)),
        out_shape=jax.ShapeDtypeStruct((b, n, qw), BF16),
        scratch_shapes=[pltpu.VMEM((group * tq, LANES), BF16),
                        pltpu.VMEM((group * tq, LANES), F32),
                        pltpu.VMEM((group * tq, 2 * LANES), F32)],
        compiler_params=_cparams("arbitrary", "arbitrary", "arbitrary", "arbitrary"),
        name="flash_" + mode,
    )(*args)


def _outproj_kernel(*refs, n_lhs):
    lhs = refs[:n_lhs]
    ws = refs[n_lhs:2 * n_lhs]
    x_ref, g_ref, gate_ref, o_ref = refs[2 * n_lhs:]
    y = jnp.dot(lhs[0][0].astype(BF16), ws[0][...], preferred_element_type=F32)
    for a, w in zip(lhs[1:], ws[1:]):
        y = y + jnp.dot(a[0].astype(BF16), w[...], preferred_element_type=F32)
    o_ref[0] = x_ref[0] + gate_ref[0] * _rms(y, g_ref[...])


def outproj_residual(lhs_list, w_list, x, g, gate):
    b, n, d = x.shape
    tm = min(n, 512)
    n_lhs = len(lhs_list)
    in_specs = [pl.BlockSpec((1, tm, a.shape[2]), lambda bi, i: (bi, i, 0)) for a in lhs_list]
    in_specs += [pl.BlockSpec(w.shape, lambda bi, i: (0, 0)) for w in w_list]
    in_specs += [pl.BlockSpec((1, tm, d), lambda bi, i: (bi, i, 0)),
                 pl.BlockSpec((1, d), lambda bi, i: (0, 0)),
                 pl.BlockSpec((1, 1, d), lambda bi, i: (bi, 0, 0))]
    return pl.pallas_call(
        functools.partial(_outproj_kernel, n_lhs=n_lhs),
        grid=(b, n // tm),
        in_specs=in_specs,
        out_specs=pl.BlockSpec((1, tm, d), lambda bi, i: (bi, i, 0)),
        out_shape=jax.ShapeDtypeStruct((b, n, d), F32),
        compiler_params=_cparams("arbitrary", "arbitrary"),
        name="outproj",
    )(*lhs_list, *w_list, x, g.reshape(1, d), gate.reshape(b, 1, d))


def _ffn_kernel(xm_ref, xp_ref, xn_ref, gn_ref, sc_ref, sh_ref, wa_ref, wg_ref, cwa_ref, cwg_ref,
                cba_ref, cbg_ref, wd_ref, g3_ref, gate_ref, o_ref, hx_ref,
                *, tm, halo, n_tiles, n_f):
    i = pl.program_id(1)
    j = pl.program_id(2)

    @pl.when(j == 0)
    def _prologue():
        def nm(x):
            return (_rms(x, gn_ref[...]) * (1.0 + sc_ref[0]) + sh_ref[0]).astype(BF16)
        hx_ref[halo:halo + tm] = nm(xm_ref[0])
        zero = jnp.zeros((halo, xm_ref.shape[2]), BF16)
        hx_ref[0:halo] = jnp.where(i > 0, nm(xp_ref[0]), zero)
        hx_ref[halo + tm:] = jnp.where(i < n_tiles - 1, nm(xn_ref[0]), zero)
        o_ref[0] = jnp.zeros(o_ref.shape[1:], F32)

    hx = hx_ref[...]
    rows = tm + 2 * halo

    def conv(u, cw, cb):
        up = pltpu.roll(u, 1, 0)
        dn = pltpu.roll(u, rows - 1, 0)
        return (cw[0:1] * up + cw[1:2] * u + cw[2:3] * dn + cb)[halo:halo + tm]

    a = conv(jnp.dot(hx, wa_ref[...], preferred_element_type=F32), cwa_ref[...], cba_ref[...])
    g = conv(jnp.dot(hx, wg_ref[...], preferred_element_type=F32), cwg_ref[...], cbg_ref[...])
    act = (a * (g * jax.nn.sigmoid(g))).astype(BF16)
    o_ref[0] += jnp.dot(act, wd_ref[...], preferred_element_type=F32)

    @pl.when(j == n_f - 1)
    def _epilogue():
        o_ref[0] = xm_ref[0] + gate_ref[0] * _rms(o_ref[0], g3_ref[...])


def conv_ffn_residual(x, gn, sc, sh, w_up, conv_w, conv_b, w_down, g3, gate, layer):
    b, n, d = x.shape
    dff = w_down.shape[1]
    tm = min(n, 512)
    tf = 512
    halo = BF16_SUBLANES
    nt = n // tm
    n_f = dff // tf
    row = lambda bi, i, j: (bi, i, 0)
    vec = lambda bi, i, j: (0, 0)
    bvec = lambda bi, i, j: (bi, 0, 0)
    up_a = lambda bi, i, j: (0, j)
    up_g = lambda bi, i, j: (0, n_f + j)
    wup_a = lambda bi, i, j: (layer, 0, j)
    wup_g = lambda bi, i, j: (layer, 0, n_f + j)
    conv_b = conv_b.reshape(1, 2 * dff)
    return pl.pallas_call(
        functools.partial(_ffn_kernel, tm=tm, halo=halo, n_tiles=nt, n_f=n_f),
        grid=(b, nt, n_f),
        in_specs=[pl.BlockSpec((1, tm, d), row),
                  pl.BlockSpec((1, halo, d),
                               lambda bi, i, j: (bi, jnp.maximum(i * (tm // halo) - 1, 0), 0)),
                  pl.BlockSpec((1, halo, d),
                               lambda bi, i, j: (bi, jnp.minimum((i + 1) * (tm // halo), n // halo - 1), 0)),
                  pl.BlockSpec((1, d), vec),
                  pl.BlockSpec((1, 1, d), bvec),
                  pl.BlockSpec((1, 1, d), bvec),
                  pl.BlockSpec((None, d, tf), wup_a),
                  pl.BlockSpec((None, d, tf), wup_g),
                  pl.BlockSpec((3, tf), up_a),
                  pl.BlockSpec((3, tf), up_g),
                  pl.BlockSpec((1, tf), up_a),
                  pl.BlockSpec((1, tf), up_g),
                  pl.BlockSpec((None, tf, d), lambda bi, i, j: (layer, j, 0)),
                  pl.BlockSpec((1, d), vec),
                  pl.BlockSpec((1, 1, d), bvec)],
        out_specs=pl.BlockSpec((1, tm, d), row),
        out_shape=jax.ShapeDtypeStruct((b, n, d), F32),
        scratch_shapes=[pltpu.VMEM((tm + 2 * halo, d), BF16)],
        compiler_params=_cparams("arbitrary", "arbitrary", "arbitrary"),
        name="conv_ffn",
    )(x, x, x, gn.reshape(1, d), sc.reshape(b, 1, d), sh.reshape(b, 1, d), w_up, w_up,
      conv_w, conv_w, conv_b, conv_b, w_down, g3.reshape(1, d), gate.reshape(b, 1, d))


def _filter_rows(r, bands_ref, w1_ref, b1_ref, w2_ref, b2_ref, fr_ref, dl_ref, n_tok):
    t = jnp.where(r < n_tok, r, 2.0 * n_tok - r)
    t_norm = t / float(max(n_tok - 1, 1))
    wang = (2.0 * math.pi / n_tok) * t
    z = wang * bands_ref[...]
    lane = lax.broadcasted_iota(jnp.int32, z.shape, 1)
    feats = jnp.where(lane == 0, t_norm,
                      jnp.where(lane <= HY_BANDS, jnp.cos(z),
                                jnp.where(lane <= 2 * HY_BANDS, -jnp.sin(z), 0.0)))
    fr = fr_ref[...]
    h = jnp.sin(fr[0:1] * (jnp.dot(feats, w1_ref[...], precision=HIGHEST,
                                   preferred_element_type=F32) + b1_ref[...]))
    h = jnp.sin(fr[1:2] * (jnp.dot(h, w2_ref[...], precision=HIGHEST,
                                   preferred_element_type=F32) + b2_ref[...]))
    decay = jnp.where(r == n_tok, 0.0, jnp.exp(-t_norm * dl_ref[...]))
    return h, decay


def _filter_kernel(bands_ref, w1_ref, b1_ref, w2_ref, b2_ref, w3_ref, fr_ref, dl_ref,
                   o_ref, s_ref, *, n_tok, tr):
    i = pl.program_id(0)

    @pl.when(i == 0)
    def _init():
        s_ref[...] = jnp.zeros(s_ref.shape, F32)

    r = (i * tr + lax.broadcasted_iota(jnp.int32, (tr, 1), 0)).astype(F32)
    h, decay = _filter_rows(r, bands_ref, w1_ref, b1_ref, w2_ref, b2_ref, fr_ref, dl_ref, n_tok)
    hb = h.astype(BF16)
    for o in range(HY_ORDER):
        out = jnp.dot(hb, w3_ref[o].astype(BF16), preferred_element_type=F32) * decay
        o_ref[o] = out
        s_ref[o] += jnp.sum(jnp.abs(out), axis=0, keepdims=True)


def _filter_stage_a_kernel(bands_ref, w1_ref, b1_ref, w2_ref, b2_ref, w3_ref, fr_ref, dl_ref,
                           f_ref, *out_refs, n_tok, sub):
    o_refs, s_ref = out_refs[:HY_ORDER], out_refs[HY_ORDER]
    j = pl.program_id(0)

    @pl.when(j == 0)
    def _init():
        s_ref[...] = jnp.zeros(s_ref.shape, F32)

    k1n = o_refs[0].shape[1]
    half = k1n // 2
    c = dl_ref.shape[1]
    n1 = lax.broadcasted_iota(jnp.int32, (k1n, 1), 0)
    f = f_ref[...]
    for q in range(sub):
        r = (n1 * DFT_RADIX + (j * sub + q)).astype(F32)
        h, decay = _filter_rows(r, bands_ref, w1_ref, b1_ref, w2_ref, b2_ref, fr_ref, dl_ref, n_tok)
        h_fwd, h_bwd = h[:half].astype(BF16), h[half:].astype(BF16)
        for o in range(HY_ORDER):
            w3 = w3_ref[o].astype(BF16)
            h3 = jnp.concatenate([jnp.dot(h_fwd, w3[:, :c], preferred_element_type=F32),
                                  jnp.dot(h_bwd, w3[:, c:], preferred_element_type=F32)],
                                 axis=0) * decay
            s_ref[o] += jnp.sum(jnp.abs(h3), axis=0, keepdims=True)
            y = _dft_dot(f, h3)
            o_refs[o][0, :, q, :] = y[:k1n]
            o_refs[o][1, :, q, :] = y[k1n:]


def _filter_operands(w1, b1, w2, b2, w3, freq):
    ch = HY_CH
    lane = jnp.arange(LANES)
    bands = jnp.linspace(1e-4, HY_BANDS - 1, HY_BANDS, dtype=F32)
    bands_l = jnp.where((lane >= 1) & (lane <= HY_BANDS), bands[jnp.clip(lane - 1, 0, HY_BANDS - 1)],
                        jnp.where((lane > HY_BANDS) & (lane <= 2 * HY_BANDS),
                                  bands[jnp.clip(lane - 1 - HY_BANDS, 0, HY_BANDS - 1)], 0.0))
    w1p = jnp.zeros((LANES, HY_FILTER_W), F32).at[:w1.shape[0]].set(w1)
    deltas = jnp.abs(jnp.linspace(math.log(HY_TARGET) / HY_FAST, math.log(HY_TARGET) / HY_SLOW,
                                  ch, dtype=F32))
    w3o = jnp.swapaxes(w3.reshape(HY_FILTER_W, HY_ORDER, 2 * ch), 0, 1)
    return (bands_l.reshape(1, LANES), w1p, b1.reshape(1, -1), w2, b2.reshape(1, -1), w3o, freq,
            deltas.reshape(1, ch))


def _filter_in_specs(w3_spec):
    const = lambda i: (0, 0)
    return [pl.BlockSpec((1, LANES), const),
            pl.BlockSpec((LANES, HY_FILTER_W), const),
            pl.BlockSpec((1, HY_FILTER_W), const),
            pl.BlockSpec((HY_FILTER_W, HY_FILTER_W), const),
            pl.BlockSpec((1, HY_FILTER_W), const),
            w3_spec,
            pl.BlockSpec((2, HY_FILTER_W), const),
            pl.BlockSpec((1, HY_CH), const)]


def hyena_filter_time(n_tok, w1, b1, w2, b2, w3, freq):
    ch = HY_CH
    ll = 2 * n_tok
    tr = min(n_tok, 512)
    per_half = n_tok // tr
    return pl.pallas_call(
        functools.partial(_filter_kernel, n_tok=n_tok, tr=tr),
        grid=(ll // tr,),
        in_specs=_filter_in_specs(
            pl.BlockSpec((HY_ORDER, HY_FILTER_W, ch), lambda i: (0, 0, i // per_half))),
        out_specs=[pl.BlockSpec((HY_ORDER, tr, ch), lambda i: (0, i, 0)),
                   pl.BlockSpec((HY_ORDER, 1, ch), lambda i: (0, 0, 0))],
        out_shape=[jax.ShapeDtypeStruct((HY_ORDER, ll, ch), F32),
                   jax.ShapeDtypeStruct((HY_ORDER, 1, ch), F32)],
        compiler_params=_cparams("arbitrary"),
        name="hyena_filter",
    )(*_filter_operands(w1, b1, w2, b2, w3, freq))


def hyena_filter_stage_a(n_tok, f_filt, w1, b1, w2, b2, w3, freq):
    ch = HY_CH
    r = DFT_RADIX
    k1n = 2 * n_tok // r
    sub = 8
    return pl.pallas_call(
        functools.partial(_filter_stage_a_kernel, n_tok=n_tok, sub=sub),
        grid=(r // sub,),
        in_specs=_filter_in_specs(
            pl.BlockSpec((HY_ORDER, HY_FILTER_W, 2 * ch), lambda j: (0, 0, 0)))
        + [pl.BlockSpec(f_filt.shape, lambda j: (0, 0))],
        out_specs=[pl.BlockSpec((2, k1n, sub, ch), lambda j: (0, 0, j, 0))] * HY_ORDER
        + [pl.BlockSpec((HY_ORDER, 1, ch), lambda j: (0, 0, 0))],
        out_shape=[jax.ShapeDtypeStruct((2, k1n, r, ch), F32)] * HY_ORDER
        + [jax.ShapeDtypeStruct((HY_ORDER, 1, ch), F32)],
        compiler_params=_cparams("arbitrary"),
        name="hyena_filter_a",
    )(*_filter_operands(w1, b1, w2, b2, w3, freq), f_filt)


def _dft_dot(f, x):
    return jnp.dot(f.astype(BF16), x.astype(BF16), preferred_element_type=F32)


def _stage_a_kernel(f_ref, x_ref, o_ref):
    parts = x_ref.shape[0]
    f = f_ref[...]
    for r in range(x_ref.shape[2]):
        z = jnp.concatenate([x_ref[p, :, r, :] for p in range(parts)], axis=0)
        y = _dft_dot(f, z)
        half = y.shape[0] // 2
        o_ref[0, :, r, :] = y[:half]
        o_ref[1, :, r, :] = y[half:]


def dft_stage_a(f, x, g):
    _, p, rows, r, c = x.shape
    k1 = p * rows
    sub = 8
    return pl.pallas_call(
        _stage_a_kernel,
        grid=(r // sub,),
        in_specs=[pl.BlockSpec(f.shape, lambda j: (0, 0)),
                  pl.BlockSpec((None, p, rows, sub, c), lambda j: (g, 0, 0, j, 0))],
        out_specs=pl.BlockSpec((2, k1, sub, c), lambda j: (0, 0, j, 0)),
        out_shape=jax.ShapeDtypeStruct((2, k1, r, c), F32),
        compiler_params=_cparams("arbitrary"),
        name="dft_stage_a",
    )(f, x)


def _stage_c_filter_kernel(a_ref, g_ref, sc_ref, o_ref, *, kc):
    for k in range(kc):
        r = jnp.concatenate([a_ref[0, k], a_ref[1, k]], axis=0)
        y = (_dft_dot(g_ref[k], r) * sc_ref[...]).astype(o_ref.dtype)
        half = y.shape[0] // 2
        o_ref[0, k] = y[:half]
        o_ref[1, k] = y[half:]


def dft_stage_c_filter(a, g, scale):
    _, k1, r, c = a.shape
    kc = 8
    tc = 256
    blk = pl.BlockSpec((2, kc, r, tc), lambda i, j: (0, i, 0, j))
    return pl.pallas_call(
        functools.partial(_stage_c_filter_kernel, kc=kc),
        grid=(k1 // kc, c // tc),
        in_specs=[blk,
                  pl.BlockSpec((kc, 2 * r, 2 * r), lambda i, j: (i, 0, 0)),
                  pl.BlockSpec((1, tc), lambda i, j: (0, j))],
        out_specs=blk,
        out_shape=jax.ShapeDtypeStruct(a.shape, BF16),
        compiler_params=_cparams("arbitrary", "arbitrary"),
        name="dft_stage_c_filter",
    )(a, g, scale)


def _stage_c_conv_kernel(a_ref, h_ref, g_ref, gi_ref, o_ref, *, kc):
    for k in range(kc):
        r = jnp.concatenate([a_ref[0, k], a_ref[1, k]], axis=0)
        y = _dft_dot(g_ref[k], r)
        half = y.shape[0] // 2
        yre, yim = y[:half], y[half:]
        hre, him = h_ref[0, k].astype(F32), h_ref[1, k].astype(F32)
        p = jnp.concatenate([yre * hre - yim * him, yre * him + yim * hre], axis=0)
        d = _dft_dot(gi_ref[k], p).astype(o_ref.dtype)
        o_ref[0, k] = d[:half]
        o_ref[1, k] = d[half:]


def dft_stage_c_conv(a, hspec, g, gi):
    _, k1, r, c = a.shape
    kc = 8
    tc = 256
    blk = pl.BlockSpec((2, kc, r, tc), lambda i, j: (0, i, 0, j))
    mat = pl.BlockSpec((kc, 2 * r, 2 * r), lambda i, j: (i, 0, 0))
    return pl.pallas_call(
        functools.partial(_stage_c_conv_kernel, kc=kc),
        grid=(k1 // kc, c // tc),
        in_specs=[blk, blk, mat, mat],
        out_specs=blk,
        out_shape=jax.ShapeDtypeStruct(a.shape, F32),
        compiler_params=_cparams("arbitrary", "arbitrary"),
        name="dft_stage_c_conv",
    )(a, hspec, g, gi)


def _stage_a_inv_kernel(f_ref, d_ref, u_ref, gt_ref, sk_ref, o_ref):
    f = f_ref[...]
    sk = sk_ref[...]
    for r in range(d_ref.shape[2]):
        d = jnp.concatenate([d_ref[0, :, r, :], d_ref[1, :, r, :]], axis=0)
        y = _dft_dot(f, d)
        half = y.shape[0] // 2
        for bi in range(2):
            yb = y[bi * half:(bi + 1) * half]
            o_ref[bi, :, r, :] = gt_ref[bi, :, r, :] * (yb + u_ref[bi, :, r, :] * sk)


def dft_stage_a_inv(f, d, u, gu, gate, gg, skip):
    _, k1, r, c = d.shape
    half = k1 // 2
    sub = 8
    io = pl.BlockSpec((2, half, sub, c), lambda j: (0, 0, j, 0))
    return pl.pallas_call(
        _stage_a_inv_kernel,
        grid=(r // sub,),
        in_specs=[pl.BlockSpec(f.shape, lambda j: (0, 0)),
                  pl.BlockSpec((2, k1, sub, c), lambda j: (0, 0, j, 0)),
                  pl.BlockSpec((None, 2, half, sub, c), lambda j: (gu, 0, 0, j, 0)),
                  pl.BlockSpec((None, 2, half, sub, c), lambda j: (gg, 0, 0, j, 0)),
                  pl.BlockSpec((1, c), lambda j: (0, 0))],
        out_specs=io,
        out_shape=jax.ShapeDtypeStruct((2, half, r, c), F32),
        compiler_params=_cparams("arbitrary"),
        name="dft_stage_a_inv",
    )(f, d, u, gate, skip)


def _dft_tables(n_tok):
    ll = 2 * n_tok
    r = DFT_RADIX
    k1n = ll // r
    half = k1n // 2
    two_pi = 2.0 * math.pi

    def cs(m, period):
        ang = (m % period).astype(F32) * (two_pi / period)
        return jnp.cos(ang), jnp.sin(ang)

    k1 = jnp.arange(k1n, dtype=jnp.int32)
    c, s = cs(k1[:, None] * k1[None, :half], k1n)
    fa_data = jnp.concatenate([jnp.concatenate([c, s], 1), jnp.concatenate([-s, c], 1)], 0)
    c, s = cs(k1[:, None] * k1[None, :], k1n)
    fa_filt = jnp.concatenate([c, -s], 0)
    c, s = cs(k1[:half, None] * k1[None, :], k1n)
    fa_inv = jnp.concatenate([jnp.concatenate([c, -s], 1), jnp.concatenate([s, c], 1)], 0)
    idx = jnp.arange(r, dtype=jnp.int32)
    ca, sa = cs(idx[:, None] * idx[None, :], r)
    cb, sb = cs(k1[:, None] * idx[None, :], ll)
    c = ca[None] * cb[:, None, :] - sa[None] * sb[:, None, :]
    s = sa[None] * cb[:, None, :] + ca[None] * sb[:, None, :]
    g_fwd = jnp.concatenate([jnp.concatenate([c, s], 2), jnp.concatenate([-s, c], 2)], 1)
    ct, st = jnp.swapaxes(c, 1, 2), jnp.swapaxes(s, 1, 2)
    g_inv = jnp.concatenate([jnp.concatenate([ct, -st], 2), jnp.concatenate([st, ct], 2)], 1)
    return tuple(t.astype(BF16) for t in (fa_data, fa_filt, fa_inv, g_fwd, g_inv))


def hyena_long(pc, filter_params, skip):
    _, b, n, c = pc.shape
    assert b == 2, "the two batch entries ride as real/imaginary parts"
    ll = 2 * n
    r = DFT_RADIX
    half = ll // r // 2
    fa_data, fa_filt, fa_inv, g_fwd, g_inv = _dft_tables(n)
    *afs, l1 = hyena_filter_stage_a(n, fa_filt, *filter_params)
    pcs = pc.reshape(HY_ORDER + 1, 2, half, r, c)
    u, gu = pcs, 0
    for o in range(HY_ORDER):
        hspec = dft_stage_c_filter(afs[o], g_fwd, 1.0 / (l1[o] * ll))
        a = dft_stage_a(fa_data, u, gu)
        d = dft_stage_c_conv(a, hspec, g_fwd, g_inv)
        z = dft_stage_a_inv(fa_inv, d, u, gu, pcs, o + 1, skip[o].reshape(1, c))
        u, gu = z[None], 0
    return z.reshape(b, n, c)


def _hyena_short_kernel(ff_ref, fr_ref, fi_ref, pc_ref, filt_ref, l1_ref, skip_ref, o_ref, *, n, ll):
    u = [pc_ref[0, 0], pc_ref[0, 1]]
    for o in range(HY_ORDER):
        hs = _dft_dot(fr_ref[...], filt_ref[o]) * (1.0 / (l1_ref[o] * ll))
        y = _dft_dot(ff_ref[...], jnp.concatenate(u, axis=0))
        yre, yim, hre, him = y[:ll], y[ll:], hs[:ll], hs[ll:]
        p = jnp.concatenate([yre * hre - yim * him, yre * him + yim * hre], axis=0)
        d = _dft_dot(fi_ref[...], p)
        sk = skip_ref[o:o + 1]
        u = [pc_ref[o + 1, bi] * (d[bi * n:(bi + 1) * n] + u[bi] * sk) for bi in range(2)]
    o_ref[0] = u[0].astype(o_ref.dtype)
    o_ref[1] = u[1].astype(o_ref.dtype)


def hyena_short(pc, filt, l1, skip, out_dtype):
    _, b, n, c = pc.shape
    assert b == 2
    ll = 2 * n
    two_pi = 2.0 * math.pi
    kk = jnp.arange(ll, dtype=jnp.int32)
    ang = ((kk[:, None] * kk[None, :]) % ll).astype(F32) * (two_pi / ll)
    cf, sf = jnp.cos(ang), jnp.sin(ang)
    f_real = jnp.concatenate([cf, -sf], 0)
    cn, sn = cf[:, :n], sf[:, :n]
    f_fwd = jnp.concatenate([jnp.concatenate([cn, sn], 1), jnp.concatenate([-sn, cn], 1)], 0)
    ci, si = cf[:n, :], sf[:n, :]
    f_inv = jnp.concatenate([jnp.concatenate([ci, -si], 1), jnp.concatenate([si, ci], 1)], 0)
    tc = 256
    const = lambda j: (0, 0)
    return pl.pallas_call(
        functools.partial(_hyena_short_kernel, n=n, ll=ll),
        grid=(c // tc,),
        in_specs=[pl.BlockSpec(f_fwd.shape, const),
                  pl.BlockSpec(f_real.shape, const),
                  pl.BlockSpec(f_inv.shape, const),
                  pl.BlockSpec((3, 2, n, tc), lambda j: (0, 0, 0, j)),
                  pl.BlockSpec((HY_ORDER, ll, tc), lambda j: (0, 0, j)),
                  pl.BlockSpec((HY_ORDER, 1, tc), lambda j: (0, 0, j)),
                  pl.BlockSpec((HY_ORDER, tc), lambda j: (0, j))],
        out_specs=pl.BlockSpec((2, n, tc), lambda j: (0, 0, j)),
        out_shape=jax.ShapeDtypeStruct((2, n, c), out_dtype),
        compiler_params=_cparams("arbitrary"),
        name="hyena_short",
    )(f_fwd, f_real, f_inv, pc, filt, l1, skip)


def hyena_mixer(pc, w1, b1, w2, b2, w3, freq, skip, out_dtype):
    n = pc.shape[2]
    if (2 * n) % (DFT_RADIX * BF16_SUBLANES) == 0 and n >= 1024:
        return hyena_long(pc, (w1, b1, w2, b2, w3, freq), skip)
    filt, l1 = hyena_filter_time(n, w1, b1, w2, b2, w3, freq)
    return hyena_short(pc, filt, l1, skip, out_dtype)


def _deinterleave_cols(w, hd):
    lead = w.shape[:-1]
    wh = w.reshape(lead + (w.shape[-1] // hd, hd // 2, 2))
    return jnp.concatenate([wh[..., 0], wh[..., 1]], axis=-1).reshape(w.shape)


def _rope_tables(n_tok, rot_dim):
    rows = n_tok // GRID_W
    row = jnp.broadcast_to(jnp.arange(rows, dtype=jnp.int32)[:, None], (rows, GRID_W)).reshape(n_tok)
    col = jnp.broadcast_to(jnp.arange(GRID_W, dtype=jnp.int32)[None, :], (rows, GRID_W)).reshape(n_tok)
    axis_dim = rot_dim // 2
    inv_freq = ROPE_THETA ** (-jnp.arange(0, axis_dim, 2, dtype=F32) / axis_dim)
    ang = jnp.concatenate([row.astype(F32)[:, None] * inv_freq,
                           col.astype(F32)[:, None] * inv_freq], axis=-1)
    c, s = jnp.cos(ang), jnp.sin(ang)
    reps = LANES // rot_dim
    return jnp.tile(jnp.concatenate([c, c], -1), (1, reps)), jnp.tile(jnp.concatenate([-s, s], -1), (1, reps))


def _mixer_ab(h, hc, with_ctx, w_in, w_out, qk_g, hy):
    b, n, _ = h.shape
    conv_w, conv_b, w1, b1, w2, b2, w3, freq, skip = hy
    scale = A_HEAD_DIM ** -0.5 * LOG2E
    wq = _deinterleave_cols(w_in[:, :A_Q_W], A_HEAD_DIM).astype(BF16)
    wk = _deinterleave_cols(w_in[:, A_Q_W:A_Q_W + A_KV_W], A_HEAD_DIM).astype(BF16)
    wv = w_in[:, A_Q_W + A_KV_W:A_QKV_W].astype(BF16)
    wh = w_in[:, A_QKV_W:].astype(BF16)
    gq = jnp.tile(_deinterleave_cols(qk_g[0], A_HEAD_DIM) * scale, A_HEADS)
    gk = jnp.tile(_deinterleave_cols(qk_g[1], A_HEAD_DIM), A_KV_HEADS)
    ones_v = jnp.ones((A_KV_W,), F32)
    cosf, sinf = _rope_tables(n, A_HEAD_DIM)

    n_ctx = hc.shape[1]
    q = inproj(h, wq, gq, cosf, sinf, norm=True)
    k_all = inproj(hc, wk, gk, norm=True, append_at=n,
                   append_to=inproj(h, wk, gk, cosf, sinf, norm=True, tail_rows=n_ctx))
    v_all = inproj(hc, wv, ones_v, append_at=n, append_to=inproj(h, wv, ones_v, tail_rows=n_ctx))
    o_att = flash_attention(q, k_all, v_all, mode="gqa", kv_len=n + n_ctx)
    pc = inproj_conv(h, wh, conv_w, conv_b, HY_ORDER + 1)
    o_hy = hyena_mixer(pc, w1, b1, w2, b2, w3, freq, skip, BF16)
    wo = w_out.astype(BF16)
    lat = ([o_att, o_hy], [wo[:A_Q_W], wo[A_Q_W:]])
    if not with_ctx:
        return lat, None
    q_c = inproj(hc, wq, gq, norm=True)
    o_att_c = flash_attention(q_c, k_all, v_all, mode="gqa", kv_start=n, kv_len=n_ctx)
    pc_c = inproj_conv(hc, wh, conv_w, conv_b, HY_ORDER + 1)
    o_hy_c = hyena_mixer(pc_c, w1, b1, w2, b2, w3, freq, skip, BF16)
    return lat, ([o_att_c, o_hy_c], [wo[:A_Q_W], wo[A_Q_W:]])


def _mixer_c(h, hc, with_ctx, lambda_init, w_in, w_out, lam_vecs, subln_g):
    b, n, _ = h.shape
    scale = C_HEAD_DIM ** -0.5 * LOG2E
    wq = _deinterleave_cols(w_in[:, :C_Q_W], C_HEAD_DIM).astype(BF16)
    wk = _deinterleave_cols(w_in[:, C_Q_W:2 * C_Q_W], C_HEAD_DIM).astype(BF16)
    wv = w_in[:, 2 * C_Q_W:].astype(BF16)
    gq = jnp.full((C_Q_W,), scale, F32)
    ones = jnp.ones((C_Q_W,), F32)
    cosf, sinf = _rope_tables(n, C_HEAD_DIM)

    n_ctx = hc.shape[1]
    q = inproj(h, wq, gq, cosf, sinf, hd=C_HEAD_DIM)
    k_all = inproj(hc, wk, ones, append_at=n,
                   append_to=inproj(h, wk, ones, cosf, sinf, hd=C_HEAD_DIM, tail_rows=n_ctx))
    v_all = inproj(hc, wv, ones, append_at=n, append_to=inproj(h, wv, ones, tail_rows=n_ctx))
    attn = functools.partial(flash_attention, mode="diff", lam_vecs=lam_vecs, subln_g=subln_g,
                             lambda_init=lambda_init)
    wo = w_out.astype(BF16)
    lat = ([attn(q, k_all, v_all, kv_len=n + n_ctx)], [wo])
    if not with_ctx:
        return lat, None
    q_c = inproj(hc, wq, gq)
    return lat, ([attn(q_c, k_all, v_all, kv_start=n, kv_len=n_ctx)], [wo])


def kernel(x, c, ctx, c_ctx, ada_w, ada_b, norm_g, ab_w_in, ab_w_out, ab_qk_g,
           hy_conv_w, hy_conv_b, hy_w1, hy_b1, hy_w2, hy_b2, hy_w3, hy_freq, hy_skip,
           dc_w_in, dc_w_out, dc_lambda, dc_subln_g,
           ffn_w_up, ffn_conv_w, ffn_conv_b, ffn_w_down):
    depth = ada_w.shape[0]
    b, n, d = x.shape
    pad_rows = (-(b + 1)) % 8
    cvec = jnp.concatenate([c, c_ctx[None, :], jnp.zeros((pad_rows, d), F32)], axis=0)
    mod_all = ada_mod(cvec, ada_w, ada_b)
    w_up = ffn_w_up.astype(BF16)
    w_down = ffn_w_down.astype(BF16)
    for i in range(depth):
        j = i // 2
        with_ctx = i < depth - 1
        mod = mod_all[i, :b].reshape(b, 6, d)
        mod_c = jnp.broadcast_to(mod_all[i, b].reshape(1, 6, d), (b, 6, d))
        sh1, sc1, g1, sh2, sc2, g2 = (mod[:, t] for t in range(6))
        sh1c, sc1c, g1c, sh2c, sc2c, g2c = (mod_c[:, t] for t in range(6))
        h = norm_mod(x, norm_g[i, 0], sc1, sh1)
        hc = norm_mod(ctx, norm_g[i, 0], sc1c, sh1c)
        if i % 2 == 0:
            hy = (hy_conv_w[j], hy_conv_b[j], hy_w1[j], hy_b1[j], hy_w2[j], hy_b2[j],
                  hy_w3[j], hy_freq[j], hy_skip[j])
            lat, cx = _mixer_ab(h, hc, with_ctx, ab_w_in[j], ab_w_out[j], ab_qk_g[j], hy)
        else:
            lambda_init = 0.8 - 0.6 * math.exp(-0.3 * i)
            lat, cx = _mixer_c(h, hc, with_ctx, lambda_init, dc_w_in[j], dc_w_out[j],
                               dc_lambda[j], dc_subln_g[j])
        x = outproj_residual(lat[0], lat[1], x, norm_g[i, 1], g1)
        x = conv_ffn_residual(x, norm_g[i, 2], sc2, sh2, w_up, ffn_conv_w[i], ffn_conv_b[i],
                              w_down, norm_g[i, 3], g2, i)
        if with_ctx:
            ctx = outproj_residual(cx[0], cx[1], ctx, norm_g[i, 1], g1c)
            ctx = conv_ffn_residual(ctx, norm_g[i, 2], sc2c, sh2c, w_up, ffn_conv_w[i],
                                    ffn_conv_b[i], w_down, norm_g[i, 3], g2c, i)
    return x
```

```python
import functools
import math

import jax
import jax.numpy as jnp
from jax import lax
from jax.experimental import pallas as pl
from jax.experimental.pallas import tpu as pltpu

F32 = jnp.float32
BF16 = jnp.bfloat16
HIGHEST = lax.Precision.HIGHEST

D_MODEL = 2048
GRID_W = 64
EPS = 1e-6
ROPE_THETA = 10000.0
A_HEADS, A_KV_HEADS, A_HEAD_DIM = 8, 2, 128
A_Q_W = A_HEADS * A_HEAD_DIM
A_KV_W = A_KV_HEADS * A_HEAD_DIM
A_QKV_W = A_Q_W + 2 * A_KV_W
HY_CH = D_MODEL // 2
HY_ORDER = 2
HY_FILTER_W = 64
HY_BANDS = 16
HY_FAST, HY_SLOW, HY_TARGET = 0.3, 1.5, 1e-2
C_HEADS, C_HEAD_DIM = 16, 64
C_Q_W = C_HEADS * 2 * C_HEAD_DIM
D_FF = 5632

LANES = 128
BF16_SUBLANES = 16
VMEM_LIMIT = 56 * 1024 * 1024
DFT_RADIX = 128
FLASH_ROWS = 2048
FLASH_SUB_ROWS = 128
FLASH_MAX_SINGLE_KV = 16384
LOG2E = math.log2(math.e)


def _cparams(*sem):
    return pltpu.CompilerParams(dimension_semantics=sem, vmem_limit_bytes=VMEM_LIMIT)


def _rms(x, g):
    return x * lax.rsqrt(jnp.mean(x * x, axis=-1, keepdims=True) + EPS) * g


def _ada_kernel(c_ref, w_ref, b_ref, o_ref):
    c = c_ref[...]
    s = c * jax.nn.sigmoid(c)
    rows = s.shape[0]
    s_hi = s.astype(BF16).astype(F32)
    s2 = jnp.concatenate([s_hi, s - s_hi], axis=0).astype(BF16)
    w = w_ref[0]
    w_hi = w.astype(BF16)
    w_lo = (w - w_hi.astype(F32)).astype(BF16)
    both = jnp.dot(s2, w_hi, preferred_element_type=F32)
    hi_lo = jnp.dot(s2, w_lo, preferred_element_type=F32)[:rows]
    o_ref[0] = both[:rows] + both[rows:] + hi_lo + b_ref[0]


def ada_mod(cvec, ada_w, ada_b):
    depth, d, n6 = ada_w.shape
    rows = cvec.shape[0]
    tn = 1024
    return pl.pallas_call(
        _ada_kernel,
        grid=(depth, n6 // tn),
        in_specs=[pl.BlockSpec((rows, d), lambda l, j: (0, 0)),
                  pl.BlockSpec((1, d, tn), lambda l, j: (l, 0, j)),
                  pl.BlockSpec((1, 1, tn), lambda l, j: (l, 0, j))],
        out_specs=pl.BlockSpec((1, rows, tn), lambda l, j: (l, 0, j)),
        out_shape=jax.ShapeDtypeStruct((depth, rows, n6), F32),
        compiler_params=_cparams("arbitrary", "arbitrary"),
        name="ada_mod",
    )(cvec, ada_w, ada_b.reshape(depth, 1, n6))


def _norm_mod_kernel(x_ref, g_ref, sc_ref, sh_ref, o_ref):
    y = _rms(x_ref[0], g_ref[...])
    o_ref[0] = (y * (1.0 + sc_ref[0]) + sh_ref[0]).astype(o_ref.dtype)


def norm_mod(x, g, sc, sh):
    b, n, d = x.shape
    tm = min(n, 512)
    return pl.pallas_call(
        _norm_mod_kernel,
        grid=(b, n // tm),
        in_specs=[pl.BlockSpec((1, tm, d), lambda bi, i: (bi, i, 0)),
                  pl.BlockSpec((1, d), lambda bi, i: (0, 0)),
                  pl.BlockSpec((1, 1, d), lambda bi, i: (bi, 0, 0)),
                  pl.BlockSpec((1, 1, d), lambda bi, i: (bi, 0, 0))],
        out_specs=pl.BlockSpec((1, tm, d), lambda bi, i: (bi, i, 0)),
        out_shape=jax.ShapeDtypeStruct((b, n, d), BF16),
        compiler_params=_cparams("arbitrary", "arbitrary"),
        name="norm_mod",
    )(x, g.reshape(1, d), sc.reshape(b, 1, d), sh.reshape(b, 1, d))


def _inproj_kernel(*refs, hd, norm, rope, appending, n_tiles, has_tail):
    o_ref = refs[-1]
    if appending:
        refs = refs[:-2]
    else:
        refs = refs[:-1]
    if rope:
        h_ref, w_ref, g_ref, cos_ref, sin_ref = refs
    else:
        h_ref, w_ref, g_ref = refs
    i = pl.program_id(1)

    def _tail():
        o_ref[0] = jnp.zeros(o_ref.shape[1:], o_ref.dtype)

    def _body():
        acc = jnp.dot(h_ref[0], w_ref[...], preferred_element_type=F32)
        tn = acc.shape[1]
        if rope:
            cosf = cos_ref[...]
            sinf = sin_ref[...]
            if hd != LANES:
                lane = lax.broadcasted_iota(jnp.int32, cosf.shape, 1)
                first_half = (lane % hd) < (hd // 2)
        for c in range(tn // LANES):
            x = acc[:, c * LANES:(c + 1) * LANES]
            if norm:
                x = x * lax.rsqrt(jnp.mean(x * x, axis=-1, keepdims=True) + EPS)
            x = x * g_ref[:, c * LANES:(c + 1) * LANES]
            if rope:
                if hd == LANES:
                    partner = pltpu.roll(x, LANES // 2, 1)
                else:
                    partner = jnp.where(first_half, pltpu.roll(x, LANES - hd // 2, 1),
                                        pltpu.roll(x, hd // 2, 1))
                x = x * cosf + partner * sinf
            o_ref[0, :, c * LANES:(c + 1) * LANES] = x.astype(o_ref.dtype)

    if has_tail:
        pl.when(i >= n_tiles)(_tail)
        pl.when(i < n_tiles)(_body)
    else:
        _body()


def inproj(h, w, gvec, cosf=None, sinf=None, *, hd=LANES, norm=False, out_dtype=BF16,
           tail_rows=0, append_to=None, append_at=0):
    b, n, d = h.shape
    ncol = w.shape[1]
    rope = cosf is not None
    tm = min(n, 512)
    tn = min(ncol, 2048)
    n_tiles = n // tm
    last = n_tiles - 1
    in_specs = [pl.BlockSpec((1, tm, d), lambda bi, i, j: (bi, jnp.minimum(i, last), 0)),
                pl.BlockSpec((d, tn), lambda bi, i, j: (0, j)),
                pl.BlockSpec((1, tn), lambda bi, i, j: (0, j))]
    args = [h, w, gvec.reshape(1, ncol)]
    if rope:
        in_specs += [pl.BlockSpec((tm, LANES), lambda bi, i, j: (jnp.minimum(i, last), 0)),
                     pl.BlockSpec((tm, LANES), lambda bi, i, j: (jnp.minimum(i, last), 0))]
        args += [cosf, sinf]
    appending = append_to is not None
    if appending:
        total = append_to.shape[1]
        first = append_at // tm
        assert first * tm == append_at and append_at + n <= total and append_to.dtype == out_dtype
        tail_tiles = 0
        in_specs.append(pl.BlockSpec(memory_space=pl.ANY))
        args.append(append_to)
        aliases = {len(args) - 1: 0}
    else:
        tail_tiles = -(-tail_rows // tm)
        total, first, aliases = (n_tiles + tail_tiles) * tm, 0, {}
    return pl.pallas_call(
        functools.partial(_inproj_kernel, hd=hd, norm=norm, rope=rope, appending=appending,
                          n_tiles=n_tiles, has_tail=tail_tiles > 0),
        grid=(b, n_tiles + tail_tiles, ncol // tn),
        in_specs=in_specs,
        out_specs=pl.BlockSpec((1, tm, tn), lambda bi, i, j: (bi, first + i, j)),
        out_shape=jax.ShapeDtypeStruct((b, total, ncol), out_dtype),
        input_output_aliases=aliases,
        compiler_params=_cparams("arbitrary", "arbitrary", "arbitrary"),
        name="inproj",
    )(*args)


def _inproj_conv_kernel(hm_ref, hp_ref, hn_ref, w_ref, cw_ref, cb_ref, o_ref, *, n_tiles):
    i = pl.program_id(2)
    hp, hm, hn = hp_ref[0], hm_ref[0], hn_ref[0]
    halo, tm = hp.shape[0], hm.shape[0]
    zero = jnp.zeros_like(hp)
    lhs = jnp.concatenate([jnp.where(i > 0, hp, zero), hm,
                           jnp.where(i < n_tiles - 1, hn, zero)], axis=0)
    u = jnp.dot(lhs, w_ref[...], preferred_element_type=F32)
    rows = tm + 2 * halo
    cw = cw_ref[...]
    conv = cw[0:1] * pltpu.roll(u, 1, 0) + cw[1:2] * u + cw[2:3] * pltpu.roll(u, rows - 1, 0)
    o_ref[0, 0] = conv[halo:halo + tm] + cb_ref[...]


def inproj_conv(h, w, conv_w, conv_b, groups):
    b, n, d = h.shape
    ncol = w.shape[1]
    c = ncol // groups
    tm = min(n, 512)
    tn = min(c, 1024)
    halo = BF16_SUBLANES
    nt = n // tm
    per = c // tn
    return pl.pallas_call(
        functools.partial(_inproj_conv_kernel, n_tiles=nt),
        grid=(ncol // tn, b, nt),
        in_specs=[pl.BlockSpec((1, tm, d), lambda j, bi, i: (bi, i, 0)),
                  pl.BlockSpec((1, halo, d),
                               lambda j, bi, i: (bi, jnp.maximum(i * (tm // halo) - 1, 0), 0)),
                  pl.BlockSpec((1, halo, d),
                               lambda j, bi, i: (bi, jnp.minimum((i + 1) * (tm // halo), n // halo - 1), 0)),
                  pl.BlockSpec((d, tn), lambda j, bi, i: (0, j)),
                  pl.BlockSpec((3, tn), lambda j, bi, i: (0, j)),
                  pl.BlockSpec((1, tn), lambda j, bi, i: (0, j))],
        out_specs=pl.BlockSpec((1, 1, tm, tn), lambda j, bi, i: (j // per, bi, i, j % per)),
        out_shape=jax.ShapeDtypeStruct((groups, b, n, c), F32),
        compiler_params=_cparams("arbitrary", "arbitrary", "arbitrary"),
        name="inproj_conv",
    )(h, h, h, w, conv_w, conv_b.reshape(1, ncol))


def _flash_kernel(*refs, mode, group, tq, tk, n_kv, lambda_init):
    if mode == "diff":
        q_ref, k_ref, v_ref, lam_ref, sg_ref, o_ref, qs_ref, m_ref, acc_ref = refs
    else:
        q_ref, k_ref, v_ref, o_ref, qs_ref, m_ref, acc_ref = refs
    kv = pl.program_id(3)
    single = n_kv == 1

    def _init():
        q = q_ref[0]
        if mode == "diff":
            lane = lax.broadcasted_iota(jnp.int32, q.shape, 1)
            zero = jnp.zeros_like(q)
            qs_ref[0:tq] = jnp.where(lane < C_HEAD_DIM, q, zero)
            qs_ref[tq:2 * tq] = jnp.where(lane >= C_HEAD_DIM, q, zero)
        else:
            for g in range(group):
                qs_ref[g * tq:(g + 1) * tq] = q[:, g * LANES:(g + 1) * LANES]
        if not single:
            m_ref[...] = jnp.full(m_ref.shape, -jnp.inf, F32)
            acc_ref[...] = jnp.zeros(acc_ref.shape, F32)

    if single:
        _init()
    else:
        pl.when(kv == 0)(_init)

    k = k_ref[0]
    v = v_ref[0]
    v_aug = jnp.concatenate([v, jnp.ones_like(v)], axis=1)
    for c in range(group * tq // FLASH_SUB_ROWS):
        sl = slice(c * FLASH_SUB_ROWS, (c + 1) * FLASH_SUB_ROWS)
        s = lax.dot_general(qs_ref[sl], k, (((1,), (1,)), ((), ())),
                            preferred_element_type=F32)
        if single:
            p = jnp.exp2(s - jnp.max(s, axis=-1, keepdims=True))
            acc_ref[sl] = jnp.dot(p.astype(BF16), v_aug, preferred_element_type=F32)
            continue
        m_prev = m_ref[sl]
        m_next = jnp.maximum(m_prev, jnp.max(s, axis=-1, keepdims=True))
        alpha = jnp.exp2(m_prev - m_next)
        p = jnp.exp2(s - jnp.tile(m_next, (1, tk // LANES)))
        acc_ref[sl] = jnp.tile(alpha, (1, 2)) * acc_ref[sl] + jnp.dot(
            p.astype(BF16), v_aug, preferred_element_type=F32)
        m_ref[sl] = m_next

    def _fin():
        acc = acc_ref[...]
        o = acc[:, :LANES] / acc[:, LANES:]
        if mode == "diff":
            lv = lam_ref[...]
            lam = (jnp.exp(jnp.sum(lv[0:1] * lv[1:2], axis=-1, keepdims=True))
                   - jnp.exp(jnp.sum(lv[2:3] * lv[3:4], axis=-1, keepdims=True)) + lambda_init)
            dlt = o[0:tq] - lam * o[tq:2 * tq]
            o_ref[0] = (_rms(dlt, sg_ref[...]) * (1.0 - lambda_init)).astype(o_ref.dtype)
        else:
            for g in range(group):
                o_ref[0, :, g * LANES:(g + 1) * LANES] = o[g * tq:(g + 1) * tq].astype(o_ref.dtype)

    if single:
        _fin()
    else:
        pl.when(kv == n_kv - 1)(_fin)


def _kv_tile(s):
    if s <= FLASH_MAX_SINGLE_KV and s % LANES == 0:
        return s
    for t in (2816, 2048, 1024, 768, 512, 256, 128):
        if s % t == 0:
            return t
    return s


def flash_attention(q, k, v, *, mode, kv_len, lam_vecs=None, subln_g=None, lambda_init=0.0,
                    kv_start=0):
    b, n, qw = q.shape
    s = kv_len
    if mode == "diff":
        group, n_groups, qblk = 2, qw // LANES, LANES
    else:
        group = A_HEADS // A_KV_HEADS
        n_groups, qblk = A_KV_HEADS, group * LANES
    tq = min(n, FLASH_ROWS // group)
    tk = _kv_tile(s)
    n_kv = s // tk
    j0 = kv_start // tk
    assert j0 * tk == kv_start
    in_specs = [pl.BlockSpec((1, tq, qblk), lambda bi, g, i, j: (bi, i, g)),
                pl.BlockSpec((1, tk, LANES), lambda bi, g, i, j: (bi, j0 + j, g)),
                pl.BlockSpec((1, tk, LANES), lambda bi, g, i, j: (bi, j0 + j, g))]
    args = [q, k, v]
    if mode == "diff":
        in_specs += [pl.BlockSpec(lam_vecs.shape, lambda bi, g, i, j: (0, 0)),
                     pl.BlockSpec((1, LANES), lambda bi, g, i, j: (0, 0))]
        args += [lam_vecs, subln_g.reshape(1, LANES)]
    return pl.pallas_call(
        functools.partial(_flash_kernel, mode=mode, group=group, tq=tq, tk=tk, n_kv=n_kv,
                          lambda_init=lambda_init),
        grid=(b, n_groups, n // tq, n_kv),
        in_specs=in_specs,
        out_specs=pl.BlockSpec((1, tq, qblk), lambda bi, g, i, j: (bi, i, g)),
        out_shape=jax.ShapeDtypeStruct((b, n, qw), BF16),
        scratch_shapes=[pltpu.VMEM((group * tq, LANES), BF16),
                        pltpu.VMEM((group * tq, LANES), F32),
                        pltpu.VMEM((group * tq, 2 * LANES), F32)],
        compiler_params=_cparams("arbitrary", "arbitrary", "arbitrary", "arbitrary"),
        name="flash_" + mode,
    )(*args)


def _outproj_kernel(*refs, n_lhs):
    lhs = refs[:n_lhs]
    ws = refs[n_lhs:2 * n_lhs]
    x_ref, g_ref, gate_ref, o_ref = refs[2 * n_lhs:]
    y = jnp.dot(lhs[0][0].astype(BF16), ws[0][...], preferred_element_type=F32)
    for a, w in zip(lhs[1:], ws[1:]):
        y = y + jnp.dot(a[0].astype(BF16), w[...], preferred_element_type=F32)
    o_ref[0] = x_ref[0] + gate_ref[0] * _rms(y, g_ref[...])


def outproj_residual(lhs_list, w_list, x, g, gate):
    b, n, d = x.shape
    tm = min(n, 512)
    n_lhs = len(lhs_list)
    in_specs = [pl.BlockSpec((1, tm, a.shape[2]), lambda bi, i: (bi, i, 0)) for a in lhs_list]
    in_specs += [pl.BlockSpec(w.shape, lambda bi, i: (0, 0)) for w in w_list]
    in_specs += [pl.BlockSpec((1, tm, d), lambda bi, i: (bi, i, 0)),
                 pl.BlockSpec((1, d), lambda bi, i: (0, 0)),
                 pl.BlockSpec((1, 1, d), lambda bi, i: (bi, 0, 0))]
    return pl.pallas_call(
        functools.partial(_outproj_kernel, n_lhs=n_lhs),
        grid=(b, n // tm),
        in_specs=in_specs,
        out_specs=pl.BlockSpec((1, tm, d), lambda bi, i: (bi, i, 0)),
        out_shape=jax.ShapeDtypeStruct((b, n, d), F32),
        compiler_params=_cparams("arbitrary", "arbitrary"),
        name="outproj",
    )(*lhs_list, *w_list, x, g.reshape(1, d), gate.reshape(b, 1, d))


def _ffn_kernel(xm_ref, xp_ref, xn_ref, gn_ref, sc_ref, sh_ref, wa_ref, wg_ref, cwa_ref, cwg_ref,
                cba_ref, cbg_ref, wd_ref, g3_ref, gate_ref, o_ref, hx_ref,
                *, tm, halo, n_tiles, n_f):
    i = pl.program_id(1)
    j = pl.program_id(2)

    @pl.when(j == 0)
    def _prologue():
        def nm(x):
            return (_rms(x, gn_ref[...]) * (1.0 + sc_ref[0]) + sh_ref[0]).astype(BF16)
        hx_ref[halo:halo + tm] = nm(xm_ref[0])
        zero = jnp.zeros((halo, xm_ref.shape[2]), BF16)
        hx_ref[0:halo] = jnp.where(i > 0, nm(xp_ref[0]), zero)
        hx_ref[halo + tm:] = jnp.where(i < n_tiles - 1, nm(xn_ref[0]), zero)
        o_ref[0] = jnp.zeros(o_ref.shape[1:], F32)

    hx = hx_ref[...]
    rows = tm + 2 * halo

    def conv(u, cw, cb):
        up = pltpu.roll(u, 1, 0)
        dn = pltpu.roll(u, rows - 1, 0)
        return (cw[0:1] * up + cw[1:2] * u + cw[2:3] * dn + cb)[halo:halo + tm]

    a = conv(jnp.dot(hx, wa_ref[...], preferred_element_type=F32), cwa_ref[...], cba_ref[...])
    g = conv(jnp.dot(hx, wg_ref[...], preferred_element_type=F32), cwg_ref[...], cbg_ref[...])
    act = (a * (g * jax.nn.sigmoid(g))).astype(BF16)
    o_ref[0] += jnp.dot(act, wd_ref[...], preferred_element_type=F32)

    @pl.when(j == n_f - 1)
    def _epilogue():
        o_ref[0] = xm_ref[0] + gate_ref[0] * _rms(o_ref[0], g3_ref[...])


def conv_ffn_residual(x, gn, sc, sh, w_up, conv_w, conv_b, w_down, g3, gate, layer):
    b, n, d = x.shape
    dff = w_down.shape[1]
    tm = min(n, 512)
    tf = 512
    halo = BF16_SUBLANES
    nt = n // tm
    n_f = dff // tf
    row = lambda bi, i, j: (bi, i, 0)
    vec = lambda bi, i, j: (0, 0)
    bvec = lambda bi, i, j: (bi, 0, 0)
    up_a = lambda bi, i, j: (0, j)
    up_g = lambda bi, i, j: (0, n_f + j)
    wup_a = lambda bi, i, j: (layer, 0, j)
    wup_g = lambda bi, i, j: (layer, 0, n_f + j)
    conv_b = conv_b.reshape(1, 2 * dff)
    return pl.pallas_call(
        functools.partial(_ffn_kernel, tm=tm, halo=halo, n_tiles=nt, n_f=n_f),
        grid=(b, nt, n_f),
        in_specs=[pl.BlockSpec((1, tm, d), row),
                  pl.BlockSpec((1, halo, d),
                               lambda bi, i, j: (bi, jnp.maximum(i * (tm // halo) - 1, 0), 0)),
                  pl.BlockSpec((1, halo, d),
                               lambda bi, i, j: (bi, jnp.minimum((i + 1) * (tm // halo), n // halo - 1), 0)),
                  pl.BlockSpec((1, d), vec),
                  pl.BlockSpec((1, 1, d), bvec),
                  pl.BlockSpec((1, 1, d), bvec),
                  pl.BlockSpec((None, d, tf), wup_a),
                  pl.BlockSpec((None, d, tf), wup_g),
                  pl.BlockSpec((3, tf), up_a),
                  pl.BlockSpec((3, tf), up_g),
                  pl.BlockSpec((1, tf), up_a),
                  pl.BlockSpec((1, tf), up_g),
                  pl.BlockSpec((None, tf, d), lambda bi, i, j: (layer, j, 0)),
                  pl.BlockSpec((1, d), vec),
                  pl.BlockSpec((1, 1, d), bvec)],
        out_specs=pl.BlockSpec((1, tm, d), row),
        out_shape=jax.ShapeDtypeStruct((b, n, d), F32),
        scratch_shapes=[pltpu.VMEM((tm + 2 * halo, d), BF16)],
        compiler_params=_cparams("arbitrary", "arbitrary", "arbitrary"),
        name="conv_ffn",
    )(x, x, x, gn.reshape(1, d), sc.reshape(b, 1, d), sh.reshape(b, 1, d), w_up, w_up,
      conv_w, conv_w, conv_b, conv_b, w_down, g3.reshape(1, d), gate.reshape(b, 1, d))


def _filter_rows(r, bands_ref, w1_ref, b1_ref, w2_ref, b2_ref, fr_ref, dl_ref, n_tok):
    t = jnp.where(r < n_tok, r, 2.0 * n_tok - r)
    t_norm = t / float(max(n_tok - 1, 1))
    wang = (2.0 * math.pi / n_tok) * t
    z = wang * bands_ref[...]
    lane = lax.broadcasted_iota(jnp.int32, z.shape, 1)
    feats = jnp.where(lane == 0, t_norm,
                      jnp.where(lane <= HY_BANDS, jnp.cos(z),
                                jnp.where(lane <= 2 * HY_BANDS, -jnp.sin(z), 0.0)))
    fr = fr_ref[...]
    h = jnp.sin(fr[0:1] * (jnp.dot(feats, w1_ref[...], precision=HIGHEST,
                                   preferred_element_type=F32) + b1_ref[...]))
    h = jnp.sin(fr[1:2] * (jnp.dot(h, w2_ref[...], precision=HIGHEST,
                                   preferred_element_type=F32) + b2_ref[...]))
    decay = jnp.where(r == n_tok, 0.0, jnp.exp(-t_norm * dl_ref[...]))
    return h, decay


def _filter_kernel(bands_ref, w1_ref, b1_ref, w2_ref, b2_ref, w3_ref, fr_ref, dl_ref,
                   o_ref, s_ref, *, n_tok, tr):
    i = pl.program_id(0)

    @pl.when(i == 0)
    def _init():
        s_ref[...] = jnp.zeros(s_ref.shape, F32)

    r = (i * tr + lax.broadcasted_iota(jnp.int32, (tr, 1), 0)).astype(F32)
    h, decay = _filter_rows(r, bands_ref, w1_ref, b1_ref, w2_ref, b2_ref, fr_ref, dl_ref, n_tok)
    hb = h.astype(BF16)
    for o in range(HY_ORDER):
        out = jnp.dot(hb, w3_ref[o].astype(BF16), preferred_element_type=F32) * decay
        o_ref[o] = out
        s_ref[o] += jnp.sum(jnp.abs(out), axis=0, keepdims=True)


def _filter_stage_a_kernel(bands_ref, w1_ref, b1_ref, w2_ref, b2_ref, w3_ref, fr_ref, dl_ref,
                           f_ref, *out_refs, n_tok, sub):
    o_refs, s_ref = out_refs[:HY_ORDER], out_refs[HY_ORDER]
    j = pl.program_id(0)

    @pl.when(j == 0)
    def _init():
        s_ref[...] = jnp.zeros(s_ref.shape, F32)

    k1n = o_refs[0].shape[1]
    half = k1n // 2
    c = dl_ref.shape[1]
    n1 = lax.broadcasted_iota(jnp.int32, (k1n, 1), 0)
    f = f_ref[...]
    for q in range(sub):
        r = (n1 * DFT_RADIX + (j * sub + q)).astype(F32)
        h, decay = _filter_rows(r, bands_ref, w1_ref, b1_ref, w2_ref, b2_ref, fr_ref, dl_ref, n_tok)
        h_fwd, h_bwd = h[:half].astype(BF16), h[half:].astype(BF16)
        for o in range(HY_ORDER):
            w3 = w3_ref[o].astype(BF16)
            h3 = jnp.concatenate([jnp.dot(h_fwd, w3[:, :c], preferred_element_type=F32),
                                  jnp.dot(h_bwd, w3[:, c:], preferred_element_type=F32)],
                                 axis=0) * decay
            s_ref[o] += jnp.sum(jnp.abs(h3), axis=0, keepdims=True)
            y = _dft_dot(f, h3)
            o_refs[o][0, :, q, :] = y[:k1n]
            o_refs[o][1, :, q, :] = y[k1n:]


def _filter_operands(w1, b1, w2, b2, w3, freq):
    ch = HY_CH
    lane = jnp.arange(LANES)
    bands = jnp.linspace(1e-4, HY_BANDS - 1, HY_BANDS, dtype=F32)
    bands_l = jnp.where((lane >= 1) & (lane <= HY_BANDS), bands[jnp.clip(lane - 1, 0, HY_BANDS - 1)],
                        jnp.where((lane > HY_BANDS) & (lane <= 2 * HY_BANDS),
                                  bands[jnp.clip(lane - 1 - HY_BANDS, 0, HY_BANDS - 1)], 0.0))
    w1p = jnp.zeros((LANES, HY_FILTER_W), F32).at[:w1.shape[0]].set(w1)
    deltas = jnp.abs(jnp.linspace(math.log(HY_TARGET) / HY_FAST, math.log(HY_TARGET) / HY_SLOW,
                                  ch, dtype=F32))
    w3o = jnp.swapaxes(w3.reshape(HY_FILTER_W, HY_ORDER, 2 * ch), 0, 1)
    return (bands_l.reshape(1, LANES), w1p, b1.reshape(1, -1), w2, b2.reshape(1, -1), w3o, freq,
            deltas.reshape(1, ch))


def _filter_in_specs(w3_spec):
    const = lambda i: (0, 0)
    return [pl.BlockSpec((1, LANES), const),
            pl.BlockSpec((LANES, HY_FILTER_W), const),
            pl.BlockSpec((1, HY_FILTER_W), const),
            pl.BlockSpec((HY_FILTER_W, HY_FILTER_W), const),
            pl.BlockSpec((1, HY_FILTER_W), const),
            w3_spec,
            pl.BlockSpec((2, HY_FILTER_W), const),
            pl.BlockSpec((1, HY_CH), const)]


def hyena_filter_time(n_tok, w1, b1, w2, b2, w3, freq):
    ch = HY_CH
    ll = 2 * n_tok
    tr = min(n_tok, 512)
    per_half = n_tok // tr
    return pl.pallas_call(
        functools.partial(_filter_kernel, n_tok=n_tok, tr=tr),
        grid=(ll // tr,),
        in_specs=_filter_in_specs(
            pl.BlockSpec((HY_ORDER, HY_FILTER_W, ch), lambda i: (0, 0, i // per_half))),
        out_specs=[pl.BlockSpec((HY_ORDER, tr, ch), lambda i: (0, i, 0)),
                   pl.BlockSpec((HY_ORDER, 1, ch), lambda i: (0, 0, 0))],
        out_shape=[jax.ShapeDtypeStruct((HY_ORDER, ll, ch), F32),
                   jax.ShapeDtypeStruct((HY_ORDER, 1, ch), F32)],
        compiler_params=_cparams("arbitrary"),
        name="hyena_filter",
    )(*_filter_operands(w1, b1, w2, b2, w3, freq))


def hyena_filter_stage_a(n_tok, f_filt, w1, b1, w2, b2, w3, freq):
    ch = HY_CH
    r = DFT_RADIX
    k1n = 2 * n_tok // r
    sub = 8
    return pl.pallas_call(
        functools.partial(_filter_stage_a_kernel, n_tok=n_tok, sub=sub),
        grid=(r // sub,),
        in_specs=_filter_in_specs(
            pl.BlockSpec((HY_ORDER, HY_FILTER_W, 2 * ch), lambda j: (0, 0, 0)))
        + [pl.BlockSpec(f_filt.shape, lambda j: (0, 0))],
        out_specs=[pl.BlockSpec((2, k1n, sub, ch), lambda j: (0, 0, j, 0))] * HY_ORDER
        + [pl.BlockSpec((HY_ORDER, 1, ch), lambda j: (0, 0, 0))],
        out_shape=[jax.ShapeDtypeStruct((2, k1n, r, ch), F32)] * HY_ORDER
        + [jax.ShapeDtypeStruct((HY_ORDER, 1, ch), F32)],
        compiler_params=_cparams("arbitrary"),
        name="hyena_filter_a",
    )(*_filter_operands(w1, b1, w2, b2, w3, freq), f_filt)


def _dft_dot(f, x):
    return jnp.dot(f.astype(BF16), x.astype(BF16), preferred_element_type=F32)


def _stage_a_kernel(f_ref, x_ref, o_ref):
    parts = x_ref.shape[0]
    f = f_ref[...]
    for r in range(x_ref.shape[2]):
        z = jnp.concatenate([x_ref[p, :, r, :] for p in range(parts)], axis=0)
        y = _dft_dot(f, z)
        half = y.shape[0] // 2
        o_ref[0, :, r, :] = y[:half]
        o_ref[1, :, r, :] = y[half:]


def dft_stage_a(f, x, g):
    _, p, rows, r, c = x.shape
    k1 = p * rows
    sub = 8
    return pl.pallas_call(
        _stage_a_kernel,
        grid=(r // sub,),
        in_specs=[pl.BlockSpec(f.shape, lambda j: (0, 0)),
                  pl.BlockSpec((None, p, rows, sub, c), lambda j: (g, 0, 0, j, 0))],
        out_specs=pl.BlockSpec((2, k1, sub, c), lambda j: (0, 0, j, 0)),
        out_shape=jax.ShapeDtypeStruct((2, k1, r, c), F32),
        compiler_params=_cparams("arbitrary"),
        name="dft_stage_a",
    )(f, x)


def _stage_c_filter_kernel(a_ref, g_ref, sc_ref, o_ref, *, kc):
    for k in range(kc):
        r = jnp.concatenate([a_ref[0, k], a_ref[1, k]], axis=0)
        y = (_dft_dot(g_ref[k], r) * sc_ref[...]).astype(o_ref.dtype)
        half = y.shape[0] // 2
        o_ref[0, k] = y[:half]
        o_ref[1, k] = y[half:]


def dft_stage_c_filter(a, g, scale):
    _, k1, r, c = a.shape
    kc = 8
    tc = 256
    blk = pl.BlockSpec((2, kc, r, tc), lambda i, j: (0, i, 0, j))
    return pl.pallas_call(
        functools.partial(_stage_c_filter_kernel, kc=kc),
        grid=(k1 // kc, c // tc),
        in_specs=[blk,
                  pl.BlockSpec((kc, 2 * r, 2 * r), lambda i, j: (i, 0, 0)),
                  pl.BlockSpec((1, tc), lambda i, j: (0, j))],
        out_specs=blk,
        out_shape=jax.ShapeDtypeStruct(a.shape, BF16),
        compiler_params=_cparams("arbitrary", "arbitrary"),
        name="dft_stage_c_filter",
    )(a, g, scale)


def _stage_c_conv_kernel(a_ref, h_ref, g_ref, gi_ref, o_ref, *, kc):
    for k in range(kc):
        r = jnp.concatenate([a_ref[0, k], a_ref[1, k]], axis=0)
        y = _dft_dot(g_ref[k], r)
        half = y.shape[0] // 2
        yre, yim = y[:half], y[half:]
        hre, him = h_ref[0, k].astype(F32), h_ref[1, k].astype(F32)
        p = jnp.concatenate([yre * hre - yim * him, yre * him + yim * hre], axis=0)
        d = _dft_dot(gi_ref[k], p).astype(o_ref.dtype)
        o_ref[0, k] = d[:half]
        o_ref[1, k] = d[half:]


def dft_stage_c_conv(a, hspec, g, gi):
    _, k1, r, c = a.shape
    kc = 8
    tc = 256
    blk = pl.BlockSpec((2, kc, r, tc), lambda i, j: (0, i, 0, j))
    mat = pl.BlockSpec((kc, 2 * r, 2 * r), lambda i, j: (i, 0, 0))
    return pl.pallas_call(
        functools.partial(_stage_c_conv_kernel, kc=kc),
        grid=(k1 // kc, c // tc),
        in_specs=[blk, blk, mat, mat],
        out_specs=blk,
        out_shape=jax.ShapeDtypeStruct(a.shape, F32),
        compiler_params=_cparams("arbitrary", "arbitrary"),
        name="dft_stage_c_conv",
    )(a, hspec, g, gi)


def _stage_a_inv_kernel(f_ref, d_ref, u_ref, gt_ref, sk_ref, o_ref):
    f = f_ref[...]
    sk = sk_ref[...]
    for r in range(d_ref.shape[2]):
        d = jnp.concatenate([d_ref[0, :, r, :], d_ref[1, :, r, :]], axis=0)
        y = _dft_dot(f, d)
        half = y.shape[0] // 2
        for bi in range(2):
            yb = y[bi * half:(bi + 1) * half]
            o_ref[bi, :, r, :] = gt_ref[bi, :, r, :] * (yb + u_ref[bi, :, r, :] * sk)


def dft_stage_a_inv(f, d, u, gu, gate, gg, skip):
    _, k1, r, c = d.shape
    half = k1 // 2
    sub = 8
    io = pl.BlockSpec((2, half, sub, c), lambda j: (0, 0, j, 0))
    return pl.pallas_call(
        _stage_a_inv_kernel,
        grid=(r // sub,),
        in_specs=[pl.BlockSpec(f.shape, lambda j: (0, 0)),
                  pl.BlockSpec((2, k1, sub, c), lambda j: (0, 0, j, 0)),
                  pl.BlockSpec((None, 2, half, sub, c), lambda j: (gu, 0, 0, j, 0)),
                  pl.BlockSpec((None, 2, half, sub, c), lambda j: (gg, 0, 0, j, 0)),
                  pl.BlockSpec((1, c), lambda j: (0, 0))],
        out_specs=io,
        out_shape=jax.ShapeDtypeStruct((2, half, r, c), F32),
        compiler_params=_cparams("arbitrary"),
        name="dft_stage_a_inv",
    )(f, d, u, gate, skip)


def _dft_tables(n_tok):
    ll = 2 * n_tok
    r = DFT_RADIX
    k1n = ll // r
    half = k1n // 2
    two_pi = 2.0 * math.pi

    def cs(m, period):
        ang = (m % period).astype(F32) * (two_pi / period)
        return jnp.cos(ang), jnp.sin(ang)

    k1 = jnp.arange(k1n, dtype=jnp.int32)
    c, s = cs(k1[:, None] * k1[None, :half], k1n)
    fa_data = jnp.concatenate([jnp.concatenate([c, s], 1), jnp.concatenate([-s, c], 1)], 0)
    c, s = cs(k1[:, None] * k1[None, :], k1n)
    fa_filt = jnp.concatenate([c, -s], 0)
    c, s = cs(k1[:half, None] * k1[None, :], k1n)
    fa_inv = jnp.concatenate([jnp.concatenate([c, -s], 1), jnp.concatenate([s, c], 1)], 0)
    idx = jnp.arange(r, dtype=jnp.int32)
    ca, sa = cs(idx[:, None] * idx[None, :], r)
    cb, sb = cs(k1[:, None] * idx[None, :], ll)
    c = ca[None] * cb[:, None, :] - sa[None] * sb[:, None, :]
    s = sa[None] * cb[:, None, :] + ca[None] * sb[:, None, :]
    g_fwd = jnp.concatenate([jnp.concatenate([c, s], 2), jnp.concatenate([-s, c], 2)], 1)
    ct, st = jnp.swapaxes(c, 1, 2), jnp.swapaxes(s, 1, 2)
    g_inv = jnp.concatenate([jnp.concatenate([ct, -st], 2), jnp.concatenate([st, ct], 2)], 1)
    return tuple(t.astype(BF16) for t in (fa_data, fa_filt, fa_inv, g_fwd, g_inv))


def hyena_long(pc, filter_params, skip):
    _, b, n, c = pc.shape
    assert b == 2, "the two batch entries ride as real/imaginary parts"
    ll = 2 * n
    r = DFT_RADIX
    half = ll // r // 2
    fa_data, fa_filt, fa_inv, g_fwd, g_inv = _dft_tables(n)
    *afs, l1 = hyena_filter_stage_a(n, fa_filt, *filter_params)
    pcs = pc.reshape(HY_ORDER + 1, 2, half, r, c)
    u, gu = pcs, 0
    for o in range(HY_ORDER):
        hspec = dft_stage_c_filter(afs[o], g_fwd, 1.0 / (l1[o] * ll))
        a = dft_stage_a(fa_data, u, gu)
        d = dft_stage_c_conv(a, hspec, g_fwd, g_inv)
        z = dft_stage_a_inv(fa_inv, d, u, gu, pcs, o + 1, skip[o].reshape(1, c))
        u, gu = z[None], 0
    return z.reshape(b, n, c)


def _hyena_short_kernel(ff_ref, fr_ref, fi_ref, pc_ref, filt_ref, l1_ref, skip_ref, o_ref, *, n, ll):
    u = [pc_ref[0, 0], pc_ref[0, 1]]
    for o in range(HY_ORDER):
        hs = _dft_dot(fr_ref[...], filt_ref[o]) * (1.0 / (l1_ref[o] * ll))
        y = _dft_dot(ff_ref[...], jnp.concatenate(u, axis=0))
        yre, yim, hre, him = y[:ll], y[ll:], hs[:ll], hs[ll:]
        p = jnp.concatenate([yre * hre - yim * him, yre * him + yim * hre], axis=0)
        d = _dft_dot(fi_ref[...], p)
        sk = skip_ref[o:o + 1]
        u = [pc_ref[o + 1, bi] * (d[bi * n:(bi + 1) * n] + u[bi] * sk) for bi in range(2)]
    o_ref[0] = u[0].astype(o_ref.dtype)
    o_ref[1] = u[1].astype(o_ref.dtype)


def hyena_short(pc, filt, l1, skip, out_dtype):
    _, b, n, c = pc.shape
    assert b == 2
    ll = 2 * n
    two_pi = 2.0 * math.pi
    kk = jnp.arange(ll, dtype=jnp.int32)
    ang = ((kk[:, None] * kk[None, :]) % ll).astype(F32) * (two_pi / ll)
    cf, sf = jnp.cos(ang), jnp.sin(ang)
    f_real = jnp.concatenate([cf, -sf], 0)
    cn, sn = cf[:, :n], sf[:, :n]
    f_fwd = jnp.concatenate([jnp.concatenate([cn, sn], 1), jnp.concatenate([-sn, cn], 1)], 0)
    ci, si = cf[:n, :], sf[:n, :]
    f_inv = jnp.concatenate([jnp.concatenate([ci, -si], 1), jnp.concatenate([si, ci], 1)], 0)
    tc = 256
    const = lambda j: (0, 0)
    return pl.pallas_call(
        functools.partial(_hyena_short_kernel, n=n, ll=ll),
        grid=(c // tc,),
        in_specs=[pl.BlockSpec(f_fwd.shape, const),
                  pl.BlockSpec(f_real.shape, const),
                  pl.BlockSpec(f_inv.shape, const),
                  pl.BlockSpec((3, 2, n, tc), lambda j: (0, 0, 0, j)),
                  pl.BlockSpec((HY_ORDER, ll, tc), lambda j: (0, 0, j)),
                  pl.BlockSpec((HY_ORDER, 1, tc), lambda j: (0, 0, j)),
                  pl.BlockSpec((HY_ORDER, tc), lambda j: (0, j))],
        out_specs=pl.BlockSpec((2, n, tc), lambda j: (0, 0, j)),
        out_shape=jax.ShapeDtypeStruct((2, n, c), out_dtype),
        compiler_params=_cparams("arbitrary"),
        name="hyena_short",
    )(f_fwd, f_real, f_inv, pc, filt, l1, skip)


def hyena_mixer(pc, w1, b1, w2, b2, w3, freq, skip, out_dtype):
    n = pc.shape[2]
    if (2 * n) % (DFT_RADIX * BF16_SUBLANES) == 0 and n >= 1024:
        return hyena_long(pc, (w1, b1, w2, b2, w3, freq), skip)
    filt, l1 = hyena_filter_time(n, w1, b1, w2, b2, w3, freq)
    return hyena_short(pc, filt, l1, skip, out_dtype)


def _deinterleave_cols(w, hd):
    lead = w.shape[:-1]
    wh = w.reshape(lead + (w.shape[-1] // hd, hd // 2, 2))
    return jnp.concatenate([wh[..., 0], wh[..., 1]], axis=-1).reshape(w.shape)


def _rope_tables(n_tok, rot_dim):
    rows = n_tok // GRID_W
    row = jnp.broadcast_to(jnp.arange(rows, dtype=jnp.int32)[:, None], (rows, GRID_W)).reshape(n_tok)
    col = jnp.broadcast_to(jnp.arange(GRID_W, dtype=jnp.int32)[None, :], (rows, GRID_W)).reshape(n_tok)
    axis_dim = rot_dim // 2
    inv_freq = ROPE_THETA ** (-jnp.arange(0, axis_dim, 2, dtype=F32) / axis_dim)
    ang = jnp.concatenate([row.astype(F32)[:, None] * inv_freq,
                           col.astype(F32)[:, None] * inv_freq], axis=-1)
    c, s = jnp.cos(ang), jnp.sin(ang)
    reps = LANES // rot_dim
    return jnp.tile(jnp.concatenate([c, c], -1), (1, reps)), jnp.tile(jnp.concatenate([-s, s], -1), (1, reps))


def _mixer_ab(h, hc, with_ctx, w_in, w_out, qk_g, hy):
    b, n, _ = h.shape
    conv_w, conv_b, w1, b1, w2, b2, w3, freq, skip = hy
    scale = A_HEAD_DIM ** -0.5 * LOG2E
    wq = _deinterleave_cols(w_in[:, :A_Q_W], A_HEAD_DIM).astype(BF16)
    wk = _deinterleave_cols(w_in[:, A_Q_W:A_Q_W + A_KV_W], A_HEAD_DIM).astype(BF16)
    wv = w_in[:, A_Q_W + A_KV_W:A_QKV_W].astype(BF16)
    wh = w_in[:, A_QKV_W:].astype(BF16)
    gq = jnp.tile(_deinterleave_cols(qk_g[0], A_HEAD_DIM) * scale, A_HEADS)
    gk = jnp.tile(_deinterleave_cols(qk_g[1], A_HEAD_DIM), A_KV_HEADS)
    ones_v = jnp.ones((A_KV_W,), F32)
    cosf, sinf = _rope_tables(n, A_HEAD_DIM)

    n_ctx = hc.shape[1]
    q = inproj(h, wq, gq, cosf, sinf, norm=True)
    k_all = inproj(hc, wk, gk, norm=True, append_at=n,
                   append_to=inproj(h, wk, gk, cosf, sinf, norm=True, tail_rows=n_ctx))
    v_all = inproj(hc, wv, ones_v, append_at=n, append_to=inproj(h, wv, ones_v, tail_rows=n_ctx))
    o_att = flash_attention(q, k_all, v_all, mode="gqa", kv_len=n + n_ctx)
    pc = inproj_conv(h, wh, conv_w, conv_b, HY_ORDER + 1)
    o_hy = hyena_mixer(pc, w1, b1, w2, b2, w3, freq, skip, BF16)
    wo = w_out.astype(BF16)
    lat = ([o_att, o_hy], [wo[:A_Q_W], wo[A_Q_W:]])
    if not with_ctx:
        return lat, None
    q_c = inproj(hc, wq, gq, norm=True)
    o_att_c = flash_attention(q_c, k_all, v_all, mode="gqa", kv_start=n, kv_len=n_ctx)
    pc_c = inproj_conv(hc, wh, conv_w, conv_b, HY_ORDER + 1)
    o_hy_c = hyena_mixer(pc_c, w1, b1, w2, b2, w3, freq, skip, BF16)
    return lat, ([o_att_c, o_hy_c], [wo[:A_Q_W], wo[A_Q_W:]])


def _mixer_c(h, hc, with_ctx, lambda_init, w_in, w_out, lam_vecs, subln_g):
    b, n, _ = h.shape
    scale = C_HEAD_DIM ** -0.5 * LOG2E
    wq = _deinterleave_cols(w_in[:, :C_Q_W], C_HEAD_DIM).astype(BF16)
    wk = _deinterleave_cols(w_in[:, C_Q_W:2 * C_Q_W], C_HEAD_DIM).astype(BF16)
    wv = w_in[:, 2 * C_Q_W:].astype(BF16)
    gq = jnp.full((C_Q_W,), scale, F32)
    ones = jnp.ones((C_Q_W,), F32)
    cosf, sinf = _rope_tables(n, C_HEAD_DIM)

    n_ctx = hc.shape[1]
    q = inproj(h, wq, gq, cosf, sinf, hd=C_HEAD_DIM)
    k_all = inproj(hc, wk, ones, append_at=n,
                   append_to=inproj(h, wk, ones, cosf, sinf, hd=C_HEAD_DIM, tail_rows=n_ctx))
    v_all = inproj(hc, wv, ones, append_at=n, append_to=inproj(h, wv, ones, tail_rows=n_ctx))
    attn = functools.partial(flash_attention, mode="diff", lam_vecs=lam_vecs, subln_g=subln_g,
                             lambda_init=lambda_init)
    wo = w_out.astype(BF16)
    lat = ([attn(q, k_all, v_all, kv_len=n + n_ctx)], [wo])
    if not with_ctx:
        return lat, None
    q_c = inproj(hc, wq, gq)
    return lat, ([attn(q_c, k_all, v_all, kv_start=n, kv_len=n_ctx)], [wo])


def kernel(x, c, ctx, c_ctx, ada_w, ada_b, norm_g, ab_w_in, ab_w_out, ab_qk_g,
           hy_conv_w, hy_conv_b, hy_w1, hy_b1, hy_w2, hy_b2, hy_w3, hy_freq, hy_skip,
           dc_w_in, dc_w_out, dc_lambda, dc_subln_g,
           ffn_w_up, ffn_conv_w, ffn_conv_b, ffn_w_down):
    depth = ada_w.shape[0]
    b, n, d = x.shape
    pad_rows = (-(b + 1)) % 8
    cvec = jnp.concatenate([c, c_ctx[None, :], jnp.zeros((pad_rows, d), F32)], axis=0)
    mod_all = ada_mod(cvec, ada_w, ada_b)
    w_up = ffn_w_up.astype(BF16)
    w_down = ffn_w_down.astype(BF16)
    for i in range(depth):
        j = i // 2
        with_ctx = i < depth - 1
        mod = mod_all[i, :b].reshape(b, 6, d)
        mod_c = jnp.broadcast_to(mod_all[i, b].reshape(1, 6, d), (b, 6, d))
        sh1, sc1, g1, sh2, sc2, g2 = (mod[:, t] for t in range(6))
        sh1c, sc1c, g1c, sh2c, sc2c, g2c = (mod_c[:, t] for t in range(6))
        h = norm_mod(x, norm_g[i, 0], sc1, sh1)
        hc = norm_mod(ctx, norm_g[i, 0], sc1c, sh1c)
        if i % 2 == 0:
            hy = (hy_conv_w[j], hy_conv_b[j], hy_w1[j], hy_b1[j], hy_w2[j], hy_b2[j],
                  hy_w3[j], hy_freq[j], hy_skip[j])
            lat, cx = _mixer_ab(h, hc, with_ctx, ab_w_in[j], ab_w_out[j], ab_qk_g[j], hy)
        else:
            lambda_init = 0.8 - 0.6 * math.exp(-0.3 * i)
            lat, cx = _mixer_c(h, hc, with_ctx, lambda_init, dc_w_in[j], dc_w_out[j],
                               dc_lambda[j], dc_subln_g[j])
        x = outproj_residual(lat[0], lat[1], x, norm_g[i, 1], g1)
        x = conv_ffn_residual(x, norm_g[i, 2], sc2, sh2, w_up, ffn_conv_w[i], ffn_conv_b[i],
                              w_down, norm_g[i, 3], g2, i)
        if with_ctx:
            ctx = outproj_residual(cx[0], cx[1], ctx, norm_g[i, 1], g1c)
            ctx = conv_ffn_residual(ctx, norm_g[i, 2], sc2c, sh2c, w_up, ffn_conv_w[i],
                                    ffn_conv_b[i], w_down, norm_g[i, 3], g2c, i)
    return x
```

```python
import functools
import math

import jax
import jax.numpy as jnp
from jax import lax
from jax.experimental import pallas as pl
from jax.experimental.pallas import tpu as pltpu

F32 = jnp.float32
BF16 = jnp.bfloat16
HIGHEST = lax.Precision.HIGHEST

D_MODEL = 2048
GRID_W = 64
EPS = 1e-6
ROPE_THETA = 10000.0
A_HEADS, A_KV_HEADS, A_HEAD_DIM = 8, 2, 128
A_Q_W = A_HEADS * A_HEAD_DIM
A_KV_W = A_KV_HEADS * A_HEAD_DIM
A_QKV_W = A_Q_W + 2 * A_KV_W
HY_CH = D_MODEL // 2
HY_ORDER = 2
HY_FILTER_W = 64
HY_BANDS = 16
HY_FAST, HY_SLOW, HY_TARGET = 0.3, 1.5, 1e-2
C_HEADS, C_HEAD_DIM = 16, 64
C_Q_W = C_HEADS * 2 * C_HEAD_DIM
D_FF = 5632

LANES = 128
BF16_SUBLANES = 16
VMEM_LIMIT = 56 * 1024 * 1024
DFT_RADIX = 128
FLASH_ROWS = 2048
FLASH_SUB_ROWS = 128
FLASH_MAX_SINGLE_KV = 16384
LOG2E = math.log2(math.e)


def _cparams(*sem):
    return pltpu.CompilerParams(dimension_semantics=sem, vmem_limit_bytes=VMEM_LIMIT)


def _rms(x, g):
    return x * lax.rsqrt(jnp.mean(x * x, axis=-1, keepdims=True) + EPS) * g


def _ada_kernel(c_ref, w_ref, b_ref, o_ref):
    c = c_ref[...]
    s = c * jax.nn.sigmoid(c)
    rows = s.shape[0]
    s_hi = s.astype(BF16).astype(F32)
    s2 = jnp.concatenate([s_hi, s - s_hi], axis=0).astype(BF16)
    w = w_ref[0]
    w_hi = w.astype(BF16)
    w_lo = (w - w_hi.astype(F32)).astype(BF16)
    both = jnp.dot(s2, w_hi, preferred_element_type=F32)
    hi_lo = jnp.dot(s2, w_lo, preferred_element_type=F32)[:rows]
    o_ref[0] = both[:rows] + both[rows:] + hi_lo + b_ref[0]


def ada_mod(cvec, ada_w, ada_b):
    depth, d, n6 = ada_w.shape
    rows = cvec.shape[0]
    tn = 1024
    return pl.pallas_call(
        _ada_kernel,
        grid=(depth, n6 // tn),
        in_specs=[pl.BlockSpec((rows, d), lambda l, j: (0, 0)),
                  pl.BlockSpec((1, d, tn), lambda l, j: (l, 0, j)),
                  pl.BlockSpec((1, 1, tn), lambda l, j: (l, 0, j))],
        out_specs=pl.BlockSpec((1, rows, tn), lambda l, j: (l, 0, j)),
        out_shape=jax.ShapeDtypeStruct((depth, rows, n6), F32),
        compiler_params=_cparams("arbitrary", "arbitrary"),
        name="ada_mod",
    )(cvec, ada_w, ada_b.reshape(depth, 1, n6))


def _norm_mod_kernel(x_ref, g_ref, sc_ref, sh_ref, o_ref):
    y = _rms(x_ref[0], g_ref[...])
    o_ref[0] = (y * (1.0 + sc_ref[0]) + sh_ref[0]).astype(o_ref.dtype)


def norm_mod(x, g, sc, sh):
    b, n, d = x.shape
    tm = min(n, 512)
    return pl.pallas_call(
        _norm_mod_kernel,
        grid=(b, n // tm),
        in_specs=[pl.BlockSpec((1, tm, d), lambda bi, i: (bi, i, 0)),
                  pl.BlockSpec((1, d), lambda bi, i: (0, 0)),
                  pl.BlockSpec((1, 1, d), lambda bi, i: (bi, 0, 0)),
                  pl.BlockSpec((1, 1, d), lambda bi, i: (bi, 0, 0))],
        out_specs=pl.BlockSpec((1, tm, d), lambda bi, i: (bi, i, 0)),
        out_shape=jax.ShapeDtypeStruct((b, n, d), BF16),
        compiler_params=_cparams("arbitrary", "arbitrary"),
        name="norm_mod",
    )(x, g.reshape(1, d), sc.reshape(b, 1, d), sh.reshape(b, 1, d))


def _inproj_kernel(*refs, hd, norm, rope, appending, n_tiles, has_tail):
    o_ref = refs[-1]
    if appending:
        refs = refs[:-2]
    else:
        refs = refs[:-1]
    if rope:
        h_ref, w_ref, g_ref, cos_ref, sin_ref = refs
    else:
        h_ref, w_ref, g_ref = refs
    i = pl.program_id(1)

    def _tail():
        o_ref[0] = jnp.zeros(o_ref.shape[1:], o_ref.dtype)

    def _body():
        acc = jnp.dot(h_ref[0], w_ref[...], preferred_element_type=F32)
        tn = acc.shape[1]
        if rope:
            cosf = cos_ref[...]
            sinf = sin_ref[...]
            if hd != LANES:
                lane = lax.broadcasted_iota(jnp.int32, cosf.shape, 1)
                first_half = (lane % hd) < (hd // 2)
        for c in range(tn // LANES):
            x = acc[:, c * LANES:(c + 1) * LANES]
            if norm:
                x = x * lax.rsqrt(jnp.mean(x * x, axis=-1, keepdims=True) + EPS)
            x = x * g_ref[:, c * LANES:(c + 1) * LANES]
            if rope:
                if hd == LANES:
                    partner = pltpu.roll(x, LANES // 2, 1)
                else:
                    partner = jnp.where(first_half, pltpu.roll(x, LANES - hd // 2, 1),
                                        pltpu.roll(x, hd // 2, 1))
                x = x * cosf + partner * sinf
            o_ref[0, :, c * LANES:(c + 1) * LANES] = x.astype(o_ref.dtype)

    if has_tail:
        pl.when(i >= n_tiles)(_tail)
        pl.when(i < n_tiles)(_body)
    else:
        _body()


def inproj(h, w, gvec, cosf=None, sinf=None, *, hd=LANES, norm=False, out_dtype=BF16,
           tail_rows=0, append_to=None, append_at=0):
    b, n, d = h.shape
    ncol = w.shape[1]
    rope = cosf is not None
    tm = min(n, 512)
    tn = min(ncol, 2048)
    n_tiles = n // tm
    last = n_tiles - 1
    in_specs = [pl.BlockSpec((1, tm, d), lambda bi, i, j: (bi, jnp.minimum(i, last), 0)),
                pl.BlockSpec((d, tn), lambda bi, i, j: (0, j)),
                pl.BlockSpec((1, tn), lambda bi, i, j: (0, j))]
    args = [h, w, gvec.reshape(1, ncol)]
    if rope:
        in_specs += [pl.BlockSpec((tm, LANES), lambda bi, i, j: (jnp.minimum(i, last), 0)),
                     pl.BlockSpec((tm, LANES), lambda bi, i, j: (jnp.minimum(i, last), 0))]
        args += [cosf, sinf]
    appending = append_to is not None
    if appending:
        total = append_to.shape[1]
        first = append_at // tm
        assert first * tm == append_at and append_at + n <= total and append_to.dtype == out_dtype
        tail_tiles = 0
        in_specs.append(pl.BlockSpec(memory_space=pl.ANY))
        args.append(append_to)
        aliases = {len(args) - 1: 0}
    else:
        tail_tiles = -(-tail_rows // tm)
        total, first, aliases = (n_tiles + tail_tiles) * tm, 0, {}
    return pl.pallas_call(
        functools.partial(_inproj_kernel, hd=hd, norm=norm, rope=rope, appending=appending,
                          n_tiles=n_tiles, has_tail=tail_tiles > 0),
        grid=(b, n_tiles + tail_tiles, ncol // tn),
        in_specs=in_specs,
        out_specs=pl.BlockSpec((1, tm, tn), lambda bi, i, j: (bi, first + i, j)),
        out_shape=jax.ShapeDtypeStruct((b, total, ncol), out_dtype),
        input_output_aliases=aliases,
        compiler_params=_cparams("arbitrary", "arbitrary", "arbitrary"),
        name="inproj",
    )(*args)


def _inproj_conv_kernel(hm_ref, hp_ref, hn_ref, w_ref, cw_ref, cb_ref, o_ref, *, n_tiles):
    i = pl.program_id(2)
    hp, hm, hn = hp_ref[0], hm_ref[0], hn_ref[0]
    halo, tm = hp.shape[0], hm.shape[0]
    zero = jnp.zeros_like(hp)
    lhs = jnp.concatenate([jnp.where(i > 0, hp, zero), hm,
                           jnp.where(i < n_tiles - 1, hn, zero)], axis=0)
    u = jnp.dot(lhs, w_ref[...], preferred_element_type=F32)
    rows = tm + 2 * halo
    cw = cw_ref[...]
    conv = cw[0:1] * pltpu.roll(u, 1, 0) + cw[1:2] * u + cw[2:3] * pltpu.roll(u, rows - 1, 0)
    o_ref[0, 0] = conv[halo:halo + tm] + cb_ref[...]


def inproj_conv(h, w, conv_w, conv_b, groups):
    b, n, d = h.shape
    ncol = w.shape[1]
    c = ncol // groups
    tm = min(n, 512)
    tn = min(c, 1024)
    halo = BF16_SUBLANES
    nt = n // tm
    per = c // tn
    return pl.pallas_call(
        functools.partial(_inproj_conv_kernel, n_tiles=nt),
        grid=(ncol // tn, b, nt),
        in_specs=[pl.BlockSpec((1, tm, d), lambda j, bi, i: (bi, i, 0)),
                  pl.BlockSpec((1, halo, d),
                               lambda j, bi, i: (bi, jnp.maximum(i * (tm // halo) - 1, 0), 0)),
                  pl.BlockSpec((1, halo, d),
                               lambda j, bi, i: (bi, jnp.minimum((i + 1) * (tm // halo), n // halo - 1), 0)),
                  pl.BlockSpec((d, tn), lambda j, bi, i: (0, j)),
                  pl.BlockSpec((3, tn), lambda j, bi, i: (0, j)),
                  pl.BlockSpec((1, tn), lambda j, bi, i: (0, j))],
        out_specs=pl.BlockSpec((1, 1, tm, tn), lambda j, bi, i: (j // per, bi, i, j % per)),
        out_shape=jax.ShapeDtypeStruct((groups, b, n, c), F32),
        compiler_params=_cparams("arbitrary", "arbitrary", "arbitrary"),
        name="inproj_conv",
    )(h, h, h, w, conv_w, conv_b.reshape(1, ncol))


def _flash_kernel(*refs, mode, group, tq, tk, n_kv, lambda_init):
    if mode == "diff":
        q_ref, k_ref, v_ref, lam_ref, sg_ref, o_ref, qs_ref, m_ref, acc_ref = refs
    else:
        q_ref, k_ref, v_ref, o_ref, qs_ref, m_ref, acc_ref = refs
    kv = pl.program_id(3)
    single = n_kv == 1

    def _init():
        q = q_ref[0]
        if mode == "diff":
            lane = lax.broadcasted_iota(jnp.int32, q.shape, 1)
            zero = jnp.zeros_like(q)
            qs_ref[0:tq] = jnp.where(lane < C_HEAD_DIM, q, zero)
            qs_ref[tq:2 * tq] = jnp.where(lane >= C_HEAD_DIM, q, zero)
        else:
            for g in range(group):
                qs_ref[g * tq:(g + 1) * tq] = q[:, g * LANES:(g + 1) * LANES]
        if not single:
            m_ref[...] = jnp.full(m_ref.shape, -jnp.inf, F32)
            acc_ref[...] = jnp.zeros(acc_ref.shape, F32)

    if single:
        _init()
    else:
        pl.when(kv == 0)(_init)

    k = k_ref[0]
    v = v_ref[0]
    v_aug = jnp.concatenate([v, jnp.ones_like(v)], axis=1)
    for c in range(group * tq // FLASH_SUB_ROWS):
        sl = slice(c * FLASH_SUB_ROWS, (c + 1) * FLASH_SUB_ROWS)
        s = lax.dot_general(qs_ref[sl], k, (((1,), (1,)), ((), ())),
                            preferred_element_type=F32)
        if single:
            p = jnp.exp2(s - jnp.max(s, axis=-1, keepdims=True))
            acc_ref[sl] = jnp.dot(p.astype(BF16), v_aug, preferred_element_type=F32)
            continue
        m_prev = m_ref[sl]
        m_next = jnp.maximum(m_prev, jnp.max(s, axis=-1, keepdims=True))
        alpha = jnp.exp2(m_prev - m_next)
        p = jnp.exp2(s - jnp.tile(m_next, (1, tk // LANES)))
        acc_ref[sl] = jnp.tile(alpha, (1, 2)) * acc_ref[sl] + jnp.dot(
            p.astype(BF16), v_aug, preferred_element_type=F32)
        m_ref[sl] = m_next

    def _fin():
        acc = acc_ref[...]
        o = acc[:, :LANES] / acc[:, LANES:]
        if mode == "diff":
            lv = lam_ref[...]
            lam = (jnp.exp(jnp.sum(lv[0:1] * lv[1:2], axis=-1, keepdims=True))
                   - jnp.exp(jnp.sum(lv[2:3] * lv[3:4], axis=-1, keepdims=True)) + lambda_init)
            dlt = o[0:tq] - lam * o[tq:2 * tq]
            o_ref[0] = (_rms(dlt, sg_ref[...]) * (1.0 - lambda_init)).astype(o_ref.dtype)
        else:
            for g in range(group):
                o_ref[0, :, g * LANES:(g + 1) * LANES] = o[g * tq:(g + 1) * tq].astype(o_ref.dtype)

    if single:
        _fin()
    else:
        pl.when(kv == n_kv - 1)(_fin)


def _kv_tile(s):
    if s <= FLASH_MAX_SINGLE_KV and s % LANES == 0:
        return s
    for t in (2816, 2048, 1024, 768, 512, 256, 128):
        if s % t == 0:
            return t
    return s


def flash_attention(q, k, v, *, mode, kv_len, lam_vecs=None, subln_g=None, lambda_init=0.0,
                    kv_start=0):
    b, n, qw = q.shape
    s = kv_len
    if mode == "diff":
        group, n_groups, qblk = 2, qw // LANES, LANES
    else:
        group = A_HEADS // A_KV_HEADS
        n_groups, qblk = A_KV_HEADS, group * LANES
    tq = min(n, FLASH_ROWS // group)
    tk = _kv_tile(s)
    n_kv = s // tk
    j0 = kv_start // tk
    assert j0 * tk == kv_start
    in_specs = [pl.BlockSpec((1, tq, qblk), lambda bi, g, i, j: (bi, i, g)),
                pl.BlockSpec((1, tk, LANES), lambda bi, g, i, j: (bi, j0 + j, g)),
                pl.BlockSpec((1, tk, LANES), lambda bi, g, i, j: (bi, j0 + j, g))]
    args = [q, k, v]
    if mode == "diff":
        in_specs += [pl.BlockSpec(lam_vecs.shape, lambda bi, g, i, j: (0, 0)),
                     pl.BlockSpec((1, LANES), lambda bi, g, i, j: (0, 0))]
        args += [lam_vecs, subln_g.reshape(1, LANES)]
    return pl.pallas_call(
        functools.partial(_flash_kernel, mode=mode, group=group, tq=tq, tk=tk, n_kv=n_kv,
                          lambda_init=lambda_init),
        grid=(b, n_groups, n // tq, n_kv),
        in_specs=in_specs,
        out_specs=pl.BlockSpec((1, tq, qblk), lambda bi, g, i, j: (bi, i, g)),
        out_shape=jax.ShapeDtypeStruct((b, n, qw), BF16),
        scratch_shapes=[pltpu.VMEM((group * tq, LANES), BF16),
                        pltpu.VMEM((group * tq, LANES), F32),
                        pltpu.VMEM((group * tq, 2 * LANES), F32)],
        compiler_params=_cparams("arbitrary", "arbitrary", "arbitrary", "arbitrary"),
        name="flash_" + mode,
    )(*args)


def _outproj_kernel(*refs, n_lhs):
    lhs = refs[:n_lhs]
    ws = refs[n_lhs:2 * n_lhs]
    x_ref, g_ref, gate_ref, o_ref = refs[2 * n_lhs:]
    y = jnp.dot(lhs[0][0].astype(BF16), ws[0][...], preferred_element_type=F32)
    for a, w in zip(lhs[1:], ws[1:]):
        y = y + jnp.dot(a[0].astype(BF16), w[...], preferred_element_type=F32)
    o_ref[0] = x_ref[0] + gate_ref[0] * _rms(y, g_ref[...])


def outproj_residual(lhs_list, w_list, x, g, gate):
    b, n, d = x.shape
    tm = min(n, 512)
    n_lhs = len(lhs_list)
    in_specs = [pl.BlockSpec((1, tm, a.shape[2]), lambda bi, i: (bi, i, 0)) for a in lhs_list]
    in_specs += [pl.BlockSpec(w.shape, lambda bi, i: (0, 0)) for w in w_list]
    in_specs += [pl.BlockSpec((1, tm, d), lambda bi, i: (bi, i, 0)),
                 pl.BlockSpec((1, d), lambda bi, i: (0, 0)),
                 pl.BlockSpec((1, 1, d), lambda bi, i: (bi, 0, 0))]
    return pl.pallas_call(
        functools.partial(_outproj_kernel, n_lhs=n_lhs),
        grid=(b, n // tm),
        in_specs=in_specs,
        out_specs=pl.BlockSpec((1, tm, d), lambda bi, i: (bi, i, 0)),
        out_shape=jax.ShapeDtypeStruct((b, n, d), F32),
        compiler_params=_cparams("arbitrary", "arbitrary"),
        name="outproj",
    )(*lhs_list, *w_list, x, g.reshape(1, d), gate.reshape(b, 1, d))


def _ffn_kernel(xm_ref, xp_ref, xn_ref, gn_ref, sc_ref, sh_ref, wa_ref, wg_ref, cwa_ref, cwg_ref,
                cba_ref, cbg_ref, wd_ref, g3_ref, gate_ref, o_ref, hx_ref,
                *, tm, halo, n_tiles, n_f):
    i = pl.program_id(1)
    j = pl.program_id(2)

    @pl.when(j == 0)
    def _prologue():
        def nm(x):
            return (_rms(x, gn_ref[...]) * (1.0 + sc_ref[0]) + sh_ref[0]).astype(BF16)
        hx_ref[halo:halo + tm] = nm(xm_ref[0])
        zero = jnp.zeros((halo, xm_ref.shape[2]), BF16)
        hx_ref[0:halo] = jnp.where(i > 0, nm(xp_ref[0]), zero)
        hx_ref[halo + tm:] = jnp.where(i < n_tiles - 1, nm(xn_ref[0]), zero)
        o_ref[0] = jnp.zeros(o_ref.shape[1:], F32)

    hx = hx_ref[...]
    rows = tm + 2 * halo

    def conv(u, cw, cb):
        up = pltpu.roll(u, 1, 0)
        dn = pltpu.roll(u, rows - 1, 0)
        return (cw[0:1] * up + cw[1:2] * u + cw[2:3] * dn + cb)[halo:halo + tm]

    a = conv(jnp.dot(hx, wa_ref[...], preferred_element_type=F32), cwa_ref[...], cba_ref[...])
    g = conv(jnp.dot(hx, wg_ref[...], preferred_element_type=F32), cwg_ref[...], cbg_ref[...])
    act = (a * (g * jax.nn.sigmoid(g))).astype(BF16)
    o_ref[0] += jnp.dot(act, wd_ref[...], preferred_element_type=F32)

    @pl.when(j == n_f - 1)
    def _epilogue():
        o_ref[0] = xm_ref[0] + gate_ref[0] * _rms(o_ref[0], g3_ref[...])


def conv_ffn_residual(x, gn, sc, sh, w_up, conv_w, conv_b, w_down, g3, gate, layer):
    b, n, d = x.shape
    dff = w_down.shape[1]
    tm = min(n, 512)
    tf = 512
    halo = BF16_SUBLANES
    nt = n // tm
    n_f = dff // tf
    row = lambda bi, i, j: (bi, i, 0)
    vec = lambda bi, i, j: (0, 0)
    bvec = lambda bi, i, j: (bi, 0, 0)
    up_a = lambda bi, i, j: (0, j)
    up_g = lambda bi, i, j: (0, n_f + j)
    wup_a = lambda bi, i, j: (layer, 0, j)
    wup_g = lambda bi, i, j: (layer, 0, n_f + j)
    conv_b = conv_b.reshape(1, 2 * dff)
    return pl.pallas_call(
        functools.partial(_ffn_kernel, tm=tm, halo=halo, n_tiles=nt, n_f=n_f),
        grid=(b, nt, n_f),
        in_specs=[pl.BlockSpec((1, tm, d), row),
                  pl.BlockSpec((1, halo, d),
                               lambda bi, i, j: (bi, jnp.maximum(i * (tm // halo) - 1, 0), 0)),
                  pl.BlockSpec((1, halo, d),
                               lambda bi, i, j: (bi, jnp.minimum((i + 1) * (tm // halo), n // halo - 1), 0)),
                  pl.BlockSpec((1, d), vec),
                  pl.BlockSpec((1, 1, d), bvec),
                  pl.BlockSpec((1, 1, d), bvec),
                  pl.BlockSpec((None, d, tf), wup_a),
                  pl.BlockSpec((None, d, tf), wup_g),
                  pl.BlockSpec((3, tf), up_a),
                  pl.BlockSpec((3, tf), up_g),
                  pl.BlockSpec((1, tf), up_a),
                  pl.BlockSpec((1, tf), up_g),
                  pl.BlockSpec((None, tf, d), lambda bi, i, j: (layer, j, 0)),
                  pl.BlockSpec((1, d), vec),
                  pl.BlockSpec((1, 1, d), bvec)],
        out_specs=pl.BlockSpec((1, tm, d), row),
        out_shape=jax.ShapeDtypeStruct((b, n, d), F32),
        scratch_shapes=[pltpu.VMEM((tm + 2 * halo, d), BF16)],
        compiler_params=_cparams("arbitrary", "arbitrary", "arbitrary"),
        name="conv_ffn",
    )(x, x, x, gn.reshape(1, d), sc.reshape(b, 1, d), sh.reshape(b, 1, d), w_up, w_up,
      conv_w, conv_w, conv_b, conv_b, w_down, g3.reshape(1, d), gate.reshape(b, 1, d))


def _filter_rows(r, bands_ref, w1_ref, b1_ref, w2_ref, b2_ref, fr_ref, dl_ref, n_tok):
    t = jnp.where(r < n_tok, r, 2.0 * n_tok - r)
    t_norm = t / float(max(n_tok - 1, 1))
    wang = (2.0 * math.pi / n_tok) * t
    z = wang * bands_ref[...]
    lane = lax.broadcasted_iota(jnp.int32, z.shape, 1)
    feats = jnp.where(lane == 0, t_norm,
                      jnp.where(lane <= HY_BANDS, jnp.cos(z),
                                jnp.where(lane <= 2 * HY_BANDS, -jnp.sin(z), 0.0)))
    fr = fr_ref[...]
    h = jnp.sin(fr[0:1] * (jnp.dot(feats, w1_ref[...], precision=HIGHEST,
                                   preferred_element_type=F32) + b1_ref[...]))
    h = jnp.sin(fr[1:2] * (jnp.dot(h, w2_ref[...], precision=HIGHEST,
                                   preferred_element_type=F32) + b2_ref[...]))
    decay = jnp.where(r == n_tok, 0.0, jnp.exp(-t_norm * dl_ref[...]))
    return h, decay


def _filter_kernel(bands_ref, w1_ref, b1_ref, w2_ref, b2_ref, w3_ref, fr_ref, dl_ref,
                   o_ref, s_ref, *, n_tok, tr):
    i = pl.program_id(0)

    @pl.when(i == 0)
    def _init():
        s_ref[...] = jnp.zeros(s_ref.shape, F32)

    r = (i * tr + lax.broadcasted_iota(jnp.int32, (tr, 1), 0)).astype(F32)
    h, decay = _filter_rows(r, bands_ref, w1_ref, b1_ref, w2_ref, b2_ref, fr_ref, dl_ref, n_tok)
    hb = h.astype(BF16)
    for o in range(HY_ORDER):
        out = jnp.dot(hb, w3_ref[o].astype(BF16), preferred_element_type=F32) * decay
        o_ref[o] = out
        s_ref[o] += jnp.sum(jnp.abs(out), axis=0, keepdims=True)


def _filter_stage_a_kernel(bands_ref, w1_ref, b1_ref, w2_ref, b2_ref, w3_ref, fr_ref, dl_ref,
                           f_ref, *out_refs, n_tok, sub):
    o_refs, s_ref = out_refs[:HY_ORDER], out_refs[HY_ORDER]
    j = pl.program_id(0)

    @pl.when(j == 0)
    def _init():
        s_ref[...] = jnp.zeros(s_ref.shape, F32)

    k1n = o_refs[0].shape[1]
    half = k1n // 2
    c = dl_ref.shape[1]
    n1 = lax.broadcasted_iota(jnp.int32, (k1n, 1), 0)
    f = f_ref[...]
    for q in range(sub):
        r = (n1 * DFT_RADIX + (j * sub + q)).astype(F32)
        h, decay = _filter_rows(r, bands_ref, w1_ref, b1_ref, w2_ref, b2_ref, fr_ref, dl_ref, n_tok)
        h_fwd, h_bwd = h[:half].astype(BF16), h[half:].astype(BF16)
        for o in range(HY_ORDER):
            w3 = w3_ref[o].astype(BF16)
            h3 = jnp.concatenate([jnp.dot(h_fwd, w3[:, :c], preferred_element_type=F32),
                                  jnp.dot(h_bwd, w3[:, c:], preferred_element_type=F32)],
                                 axis=0) * decay
            s_ref[o] += jnp.sum(jnp.abs(h3), axis=0, keepdims=True)
            y = _dft_dot(f, h3)
            o_refs[o][0, :, q, :] = y[:k1n]
            o_refs[o][1, :, q, :] = y[k1n:]


def _filter_operands(w1, b1, w2, b2, w3, freq):
    ch = HY_CH
    lane = jnp.arange(LANES)
    bands = jnp.linspace(1e-4, HY_BANDS - 1, HY_BANDS, dtype=F32)
    bands_l = jnp.where((lane >= 1) & (lane <= HY_BANDS), bands[jnp.clip(lane - 1, 0, HY_BANDS - 1)],
                        jnp.where((lane > HY_BANDS) & (lane <= 2 * HY_BANDS),
                                  bands[jnp.clip(lane - 1 - HY_BANDS, 0, HY_BANDS - 1)], 0.0))
    w1p = jnp.zeros((LANES, HY_FILTER_W), F32).at[:w1.shape[0]].set(w1)
    deltas = jnp.abs(jnp.linspace(math.log(HY_TARGET) / HY_FAST, math.log(HY_TARGET) / HY_SLOW,
                                  ch, dtype=F32))
    w3o = jnp.swapaxes(w3.reshape(HY_FILTER_W, HY_ORDER, 2 * ch), 0, 1)
    return (bands_l.reshape(1, LANES), w1p, b1.reshape(1, -1), w2, b2.reshape(1, -1), w3o, freq,
            deltas.reshape(1, ch))


def _filter_in_specs(w3_spec):
    const = lambda i: (0, 0)
    return [pl.BlockSpec((1, LANES), const),
            pl.BlockSpec((LANES, HY_FILTER_W), const),
            pl.BlockSpec((1, HY_FILTER_W), const),
            pl.BlockSpec((HY_FILTER_W, HY_FILTER_W), const),
            pl.BlockSpec((1, HY_FILTER_W), const),
            w3_spec,
            pl.BlockSpec((2, HY_FILTER_W), const),
            pl.BlockSpec((1, HY_CH), const)]


def hyena_filter_time(n_tok, w1, b1, w2, b2, w3, freq):
    ch = HY_CH
    ll = 2 * n_tok
    tr = min(n_tok, 512)
    per_half = n_tok // tr
    return pl.pallas_call(
        functools.partial(_filter_kernel, n_tok=n_tok, tr=tr),
        grid=(ll // tr,),
        in_specs=_filter_in_specs(
            pl.BlockSpec((HY_ORDER, HY_FILTER_W, ch), lambda i: (0, 0, i // per_half))),
        out_specs=[pl.BlockSpec((HY_ORDER, tr, ch), lambda i: (0, i, 0)),
                   pl.BlockSpec((HY_ORDER, 1, ch), lambda i: (0, 0, 0))],
        out_shape=[jax.ShapeDtypeStruct((HY_ORDER, ll, ch), F32),
                   jax.ShapeDtypeStruct((HY_ORDER, 1, ch), F32)],
        compiler_params=_cparams("arbitrary"),
        name="hyena_filter",
    )(*_filter_operands(w1, b1, w2, b2, w3, freq))


def hyena_filter_stage_a(n_tok, f_filt, w1, b1, w2, b2, w3, freq):
    ch = HY_CH
    r = DFT_RADIX
    k1n = 2 * n_tok // r
    sub = 8
    return pl.pallas_call(
        functools.partial(_filter_stage_a_kernel, n_tok=n_tok, sub=sub),
        grid=(r // sub,),
        in_specs=_filter_in_specs(
            pl.BlockSpec((HY_ORDER, HY_FILTER_W, 2 * ch), lambda j: (0, 0, 0)))
        + [pl.BlockSpec(f_filt.shape, lambda j: (0, 0))],
        out_specs=[pl.BlockSpec((2, k1n, sub, ch), lambda j: (0, 0, j, 0))] * HY_ORDER
        + [pl.BlockSpec((HY_ORDER, 1, ch), lambda j: (0, 0, 0))],
        out_shape=[jax.ShapeDtypeStruct((2, k1n, r, ch), F32)] * HY_ORDER
        + [jax.ShapeDtypeStruct((HY_ORDER, 1, ch), F32)],
        compiler_params=_cparams("arbitrary"),
        name="hyena_filter_a",
    )(*_filter_operands(w1, b1, w2, b2, w3, freq), f_filt)


def _dft_dot(f, x):
    return jnp.dot(f.astype(BF16), x.astype(BF16), preferred_element_type=F32)


def _stage_a_kernel(f_ref, x_ref, o_ref):
    parts = x_ref.shape[0]
    f = f_ref[...]
    for r in range(x_ref.shape[2]):
        z = jnp.concatenate([x_ref[p, :, r, :] for p in range(parts)], axis=0)
        y = _dft_dot(f, z)
        half = y.shape[0] // 2
        o_ref[0, :, r, :] = y[:half]
        o_ref[1, :, r, :] = y[half:]


def dft_stage_a(f, x, g):
    _, p, rows, r, c = x.shape
    k1 = p * rows
    sub = 8
    return pl.pallas_call(
        _stage_a_kernel,
        grid=(r // sub,),
        in_specs=[pl.BlockSpec(f.shape, lambda j: (0, 0)),
                  pl.BlockSpec((None, p, rows, sub, c), lambda j: (g, 0, 0, j, 0))],
        out_specs=pl.BlockSpec((2, k1, sub, c), lambda j: (0, 0, j, 0)),
        out_shape=jax.ShapeDtypeStruct((2, k1, r, c), F32),
        compiler_params=_cparams("arbitrary"),
        name="dft_stage_a",
    )(f, x)


def _stage_c_conv_kernel(a_ref, af_ref, sc_ref, g_ref, gi_ref, o_ref, *, kc):
    tc = a_ref.shape[3]
    for k in range(kc):
        r = jnp.concatenate([jnp.concatenate([a_ref[0, k], af_ref[0, k]], axis=1),
                             jnp.concatenate([a_ref[1, k], af_ref[1, k]], axis=1)], axis=0)
        y = _dft_dot(g_ref[k], r)
        half = y.shape[0] // 2
        yre, yim = y[:half, :tc], y[half:, :tc]
        hre, him = y[:half, tc:] * sc_ref[...], y[half:, tc:] * sc_ref[...]
        p = jnp.concatenate([yre * hre - yim * him, yre * him + yim * hre], axis=0)
        d = _dft_dot(gi_ref[k], p)
        o_ref[0, k] = d[:half]
        o_ref[1, k] = d[half:]


def dft_stage_c_conv(a, af, scale, g, gi):
    _, k1, r, c = a.shape
    kc = 8
    tc = 256
    blk = pl.BlockSpec((2, kc, r, tc), lambda i, j: (0, i, 0, j))
    mat = pl.BlockSpec((kc, 2 * r, 2 * r), lambda i, j: (i, 0, 0))
    return pl.pallas_call(
        functools.partial(_stage_c_conv_kernel, kc=kc),
        grid=(k1 // kc, c // tc),
        in_specs=[blk, blk, pl.BlockSpec((1, tc), lambda i, j: (0, j)), mat, mat],
        out_specs=blk,
        out_shape=jax.ShapeDtypeStruct(a.shape, F32),
        compiler_params=_cparams("arbitrary", "arbitrary"),
        name="dft_stage_c_conv",
    )(a, af, scale, g, gi)


def _stage_a_inv_kernel(f_ref, d_ref, u_ref, gt_ref, sk_ref, o_ref):
    f = f_ref[...]
    sk = sk_ref[...]
    for r in range(d_ref.shape[2]):
        d = jnp.concatenate([d_ref[0, :, r, :], d_ref[1, :, r, :]], axis=0)
        y = _dft_dot(f, d)
        half = y.shape[0] // 2
        for bi in range(2):
            yb = y[bi * half:(bi + 1) * half]
            o_ref[bi, :, r, :] = gt_ref[bi, :, r, :] * (yb + u_ref[bi, :, r, :] * sk)


def dft_stage_a_inv(f, d, u, gu, gate, gg, skip):
    _, k1, r, c = d.shape
    half = k1 // 2
    sub = 8
    io = pl.BlockSpec((2, half, sub, c), lambda j: (0, 0, j, 0))
    return pl.pallas_call(
        _stage_a_inv_kernel,
        grid=(r // sub,),
        in_specs=[pl.BlockSpec(f.shape, lambda j: (0, 0)),
                  pl.BlockSpec((2, k1, sub, c), lambda j: (0, 0, j, 0)),
                  pl.BlockSpec((None, 2, half, sub, c), lambda j: (gu, 0, 0, j, 0)),
                  pl.BlockSpec((None, 2, half, sub, c), lambda j: (gg, 0, 0, j, 0)),
                  pl.BlockSpec((1, c), lambda j: (0, 0))],
        out_specs=io,
        out_shape=jax.ShapeDtypeStruct((2, half, r, c), F32),
        compiler_params=_cparams("arbitrary"),
        name="dft_stage_a_inv",
    )(f, d, u, gate, skip)


def _dft_tables(n_tok):
    ll = 2 * n_tok
    r = DFT_RADIX
    k1n = ll // r
    half = k1n // 2
    two_pi = 2.0 * math.pi

    def cs(m, period):
        ang = (m % period).astype(F32) * (two_pi / period)
        return jnp.cos(ang), jnp.sin(ang)

    k1 = jnp.arange(k1n, dtype=jnp.int32)
    c, s = cs(k1[:, None] * k1[None, :half], k1n)
    fa_data = jnp.concatenate([jnp.concatenate([c, s], 1), jnp.concatenate([-s, c], 1)], 0)
    c, s = cs(k1[:, None] * k1[None, :], k1n)
    fa_filt = jnp.concatenate([c, -s], 0)
    c, s = cs(k1[:half, None] * k1[None, :], k1n)
    fa_inv = jnp.concatenate([jnp.concatenate([c, -s], 1), jnp.concatenate([s, c], 1)], 0)
    idx = jnp.arange(r, dtype=jnp.int32)
    ca, sa = cs(idx[:, None] * idx[None, :], r)
    cb, sb = cs(k1[:, None] * idx[None, :], ll)
    c = ca[None] * cb[:, None, :] - sa[None] * sb[:, None, :]
    s = sa[None] * cb[:, None, :] + ca[None] * sb[:, None, :]
    g_fwd = jnp.concatenate([jnp.concatenate([c, s], 2), jnp.concatenate([-s, c], 2)], 1)
    ct, st = jnp.swapaxes(c, 1, 2), jnp.swapaxes(s, 1, 2)
    g_inv = jnp.concatenate([jnp.concatenate([ct, -st], 2), jnp.concatenate([st, ct], 2)], 1)
    return tuple(t.astype(BF16) for t in (fa_data, fa_filt, fa_inv, g_fwd, g_inv))


def hyena_long(pc, filter_params, skip):
    _, b, n, c = pc.shape
    assert b == 2, "the two batch entries ride as real/imaginary parts"
    ll = 2 * n
    r = DFT_RADIX
    half = ll // r // 2
    fa_data, fa_filt, fa_inv, g_fwd, g_inv = _dft_tables(n)
    *afs, l1 = hyena_filter_stage_a(n, fa_filt, *filter_params)
    pcs = pc.reshape(HY_ORDER + 1, 2, half, r, c)
    u, gu = pcs, 0
    for o in range(HY_ORDER):
        a = dft_stage_a(fa_data, u, gu)
        d = dft_stage_c_conv(a, afs[o], 1.0 / (l1[o] * ll), g_fwd, g_inv)
        z = dft_stage_a_inv(fa_inv, d, u, gu, pcs, o + 1, skip[o].reshape(1, c))
        u, gu = z[None], 0
    return z.reshape(b, n, c)


def _hyena_short_kernel(ff_ref, fr_ref, fi_ref, pc_ref, filt_ref, l1_ref, skip_ref, o_ref, *, n, ll):
    u = [pc_ref[0, 0], pc_ref[0, 1]]
    for o in range(HY_ORDER):
        hs = _dft_dot(fr_ref[...], filt_ref[o]) * (1.0 / (l1_ref[o] * ll))
        y = _dft_dot(ff_ref[...], jnp.concatenate(u, axis=0))
        yre, yim, hre, him = y[:ll], y[ll:], hs[:ll], hs[ll:]
        p = jnp.concatenate([yre * hre - yim * him, yre * him + yim * hre], axis=0)
        d = _dft_dot(fi_ref[...], p)
        sk = skip_ref[o:o + 1]
        u = [pc_ref[o + 1, bi] * (d[bi * n:(bi + 1) * n] + u[bi] * sk) for bi in range(2)]
    o_ref[0] = u[0].astype(o_ref.dtype)
    o_ref[1] = u[1].astype(o_ref.dtype)


def hyena_short(pc, filt, l1, skip, out_dtype):
    _, b, n, c = pc.shape
    assert b == 2
    ll = 2 * n
    two_pi = 2.0 * math.pi
    kk = jnp.arange(ll, dtype=jnp.int32)
    ang = ((kk[:, None] * kk[None, :]) % ll).astype(F32) * (two_pi / ll)
    cf, sf = jnp.cos(ang), jnp.sin(ang)
    f_real = jnp.concatenate([cf, -sf], 0)
    cn, sn = cf[:, :n], sf[:, :n]
    f_fwd = jnp.concatenate([jnp.concatenate([cn, sn], 1), jnp.concatenate([-sn, cn], 1)], 0)
    ci, si = cf[:n, :], sf[:n, :]
    f_inv = jnp.concatenate([jnp.concatenate([ci, -si], 1), jnp.concatenate([si, ci], 1)], 0)
    tc = 256
    const = lambda j: (0, 0)
    return pl.pallas_call(
        functools.partial(_hyena_short_kernel, n=n, ll=ll),
        grid=(c // tc,),
        in_specs=[pl.BlockSpec(f_fwd.shape, const),
                  pl.BlockSpec(f_real.shape, const),
                  pl.BlockSpec(f_inv.shape, const),
                  pl.BlockSpec((3, 2, n, tc), lambda j: (0, 0, 0, j)),
                  pl.BlockSpec((HY_ORDER, ll, tc), lambda j: (0, 0, j)),
                  pl.BlockSpec((HY_ORDER, 1, tc), lambda j: (0, 0, j)),
                  pl.BlockSpec((HY_ORDER, tc), lambda j: (0, j))],
        out_specs=pl.BlockSpec((2, n, tc), lambda j: (0, 0, j)),
        out_shape=jax.ShapeDtypeStruct((2, n, c), out_dtype),
        compiler_params=_cparams("arbitrary"),
        name="hyena_short",
    )(f_fwd, f_real, f_inv, pc, filt, l1, skip)


def hyena_mixer(pc, w1, b1, w2, b2, w3, freq, skip, out_dtype):
    n = pc.shape[2]
    if (2 * n) % (DFT_RADIX * BF16_SUBLANES) == 0 and n >= 1024:
        return hyena_long(pc, (w1, b1, w2, b2, w3, freq), skip)
    filt, l1 = hyena_filter_time(n, w1, b1, w2, b2, w3, freq)
    return hyena_short(pc, filt, l1, skip, out_dtype)


def _deinterleave_cols(w, hd):
    lead = w.shape[:-1]
    wh = w.reshape(lead + (w.shape[-1] // hd, hd // 2, 2))
    return jnp.concatenate([wh[..., 0], wh[..., 1]], axis=-1).reshape(w.shape)


def _rope_tables(n_tok, rot_dim):
    rows = n_tok // GRID_W
    row = jnp.broadcast_to(jnp.arange(rows, dtype=jnp.int32)[:, None], (rows, GRID_W)).reshape(n_tok)
    col = jnp.broadcast_to(jnp.arange(GRID_W, dtype=jnp.int32)[None, :], (rows, GRID_W)).reshape(n_tok)
    axis_dim = rot_dim // 2
    inv_freq = ROPE_THETA ** (-jnp.arange(0, axis_dim, 2, dtype=F32) / axis_dim)
    ang = jnp.concatenate([row.astype(F32)[:, None] * inv_freq,
                           col.astype(F32)[:, None] * inv_freq], axis=-1)
    c, s = jnp.cos(ang), jnp.sin(ang)
    reps = LANES // rot_dim
    return jnp.tile(jnp.concatenate([c, c], -1), (1, reps)), jnp.tile(jnp.concatenate([-s, s], -1), (1, reps))


def _mixer_ab(h, hc, with_ctx, w_in, w_out, qk_g, hy):
    b, n, _ = h.shape
    conv_w, conv_b, w1, b1, w2, b2, w3, freq, skip = hy
    scale = A_HEAD_DIM ** -0.5 * LOG2E
    wq = _deinterleave_cols(w_in[:, :A_Q_W], A_HEAD_DIM).astype(BF16)
    wk = _deinterleave_cols(w_in[:, A_Q_W:A_Q_W + A_KV_W], A_HEAD_DIM).astype(BF16)
    wv = w_in[:, A_Q_W + A_KV_W:A_QKV_W].astype(BF16)
    wh = w_in[:, A_QKV_W:].astype(BF16)
    gq = jnp.tile(_deinterleave_cols(qk_g[0], A_HEAD_DIM) * scale, A_HEADS)
    gk = jnp.tile(_deinterleave_cols(qk_g[1], A_HEAD_DIM), A_KV_HEADS)
    ones_v = jnp.ones((A_KV_W,), F32)
    cosf, sinf = _rope_tables(n, A_HEAD_DIM)

    n_ctx = hc.shape[1]
    q = inproj(h, wq, gq, cosf, sinf, norm=True)
    k_all = inproj(hc, wk, gk, norm=True, append_at=n,
                   append_to=inproj(h, wk, gk, cosf, sinf, norm=True, tail_rows=n_ctx))
    v_all = inproj(hc, wv, ones_v, append_at=n, append_to=inproj(h, wv, ones_v, tail_rows=n_ctx))
    o_att = flash_attention(q, k_all, v_all, mode="gqa", kv_len=n + n_ctx)
    pc = inproj_conv(h, wh, conv_w, conv_b, HY_ORDER + 1)
    o_hy = hyena_mixer(pc, w1, b1, w2, b2, w3, freq, skip, BF16)
    wo = w_out.astype(BF16)
    lat = ([o_att, o_hy], [wo[:A_Q_W], wo[A_Q_W:]])
    if not with_ctx:
        return lat, None
    q_c = inproj(hc, wq, gq, norm=True)
    o_att_c = flash_attention(q_c, k_all, v_all, mode="gqa", kv_start=n, kv_len=n_ctx)
    pc_c = inproj_conv(hc, wh, conv_w, conv_b, HY_ORDER + 1)
    o_hy_c = hyena_mixer(pc_c, w1, b1, w2, b2, w3, freq, skip, BF16)
    return lat, ([o_att_c, o_hy_c], [wo[:A_Q_W], wo[A_Q_W:]])


def _mixer_c(h, hc, with_ctx, lambda_init, w_in, w_out, lam_vecs, subln_g):
    b, n, _ = h.shape
    scale = C_HEAD_DIM ** -0.5 * LOG2E
    wq = _deinterleave_cols(w_in[:, :C_Q_W], C_HEAD_DIM).astype(BF16)
    wk = _deinterleave_cols(w_in[:, C_Q_W:2 * C_Q_W], C_HEAD_DIM).astype(BF16)
    wv = w_in[:, 2 * C_Q_W:].astype(BF16)
    gq = jnp.full((C_Q_W,), scale, F32)
    ones = jnp.ones((C_Q_W,), F32)
    cosf, sinf = _rope_tables(n, C_HEAD_DIM)

    n_ctx = hc.shape[1]
    q = inproj(h, wq, gq, cosf, sinf, hd=C_HEAD_DIM)
    k_all = inproj(hc, wk, ones, append_at=n,
                   append_to=inproj(h, wk, ones, cosf, sinf, hd=C_HEAD_DIM, tail_rows=n_ctx))
    v_all = inproj(hc, wv, ones, append_at=n, append_to=inproj(h, wv, ones, tail_rows=n_ctx))
    attn = functools.partial(flash_attention, mode="diff", lam_vecs=lam_vecs, subln_g=subln_g,
                             lambda_init=lambda_init)
    wo = w_out.astype(BF16)
    lat = ([attn(q, k_all, v_all, kv_len=n + n_ctx)], [wo])
    if not with_ctx:
        return lat, None
    q_c = inproj(hc, wq, gq)
    return lat, ([attn(q_c, k_all, v_all, kv_start=n, kv_len=n_ctx)], [wo])


def kernel(x, c, ctx, c_ctx, ada_w, ada_b, norm_g, ab_w_in, ab_w_out, ab_qk_g,
           hy_conv_w, hy_conv_b, hy_w1, hy_b1, hy_w2, hy_b2, hy_w3, hy_freq, hy_skip,
           dc_w_in, dc_w_out, dc_lambda, dc_subln_g,
           ffn_w_up, ffn_conv_w, ffn_conv_b, ffn_w_down):
    depth = ada_w.shape[0]
    b, n, d = x.shape
    pad_rows = (-(b + 1)) % 8
    cvec = jnp.concatenate([c, c_ctx[None, :], jnp.zeros((pad_rows, d), F32)], axis=0)
    mod_all = ada_mod(cvec, ada_w, ada_b)
    w_up = ffn_w_up.astype(BF16)
    w_down = ffn_w_down.astype(BF16)
    for i in range(depth):
        j = i // 2
        with_ctx = i < depth - 1
        mod = mod_all[i, :b].reshape(b, 6, d)
        mod_c = jnp.broadcast_to(mod_all[i, b].reshape(1, 6, d), (b, 6, d))
        sh1, sc1, g1, sh2, sc2, g2 = (mod[:, t] for t in range(6))
        sh1c, sc1c, g1c, sh2c, sc2c, g2c = (mod_c[:, t] for t in range(6))
        h = norm_mod(x, norm_g[i, 0], sc1, sh1)
        hc = norm_mod(ctx, norm_g[i, 0], sc1c, sh1c)
        if i % 2 == 0:
            hy = (hy_conv_w[j], hy_conv_b[j], hy_w1[j], hy_b1[j], hy_w2[j], hy_b2[j],
                  hy_w3[j], hy_freq[j], hy_skip[j])
            lat, cx = _mixer_ab(h, hc, with_ctx, ab_w_in[j], ab_w_out[j], ab_qk_g[j], hy)
        else:
            lambda_init = 0.8 - 0.6 * math.exp(-0.3 * i)
            lat, cx = _mixer_c(h, hc, with_ctx, lambda_init, dc_w_in[j], dc_w_out[j],
                               dc_lambda[j], dc_subln_g[j])
        x = outproj_residual(lat[0], lat[1], x, norm_g[i, 1], g1)
        x = conv_ffn_residual(x, norm_g[i, 2], sc2, sh2, w_up, ffn_conv_w[i], ffn_conv_b[i],
                              w_down, norm_g[i, 3], g2, i)
        if with_ctx:
            ctx = outproj_residual(cx[0], cx[1], ctx, norm_g[i, 1], g1c)
            ctx = conv_ffn_residual(ctx, norm_g[i, 2], sc2c, sh2c, w_up, ffn_conv_w[i],
                                    ffn_conv_b[i], w_down, norm_g[i, 3], g2c, i)
    return x
```

```python
import functools
import math

import jax
import jax.numpy as jnp
from jax import lax
from jax.experimental import pallas as pl
from jax.experimental.pallas import tpu as pltpu

F32 = jnp.float32
BF16 = jnp.bfloat16
HIGHEST = lax.Precision.HIGHEST

D_MODEL = 2048
GRID_W = 64
EPS = 1e-6
ROPE_THETA = 10000.0
A_HEADS, A_KV_HEADS, A_HEAD_DIM = 8, 2, 128
A_Q_W = A_HEADS * A_HEAD_DIM
A_KV_W = A_KV_HEADS * A_HEAD_DIM
A_QKV_W = A_Q_W + 2 * A_KV_W
HY_CH = D_MODEL // 2
HY_ORDER = 2
HY_FILTER_W = 64
HY_BANDS = 16
HY_FAST, HY_SLOW, HY_TARGET = 0.3, 1.5, 1e-2
C_HEADS, C_HEAD_DIM = 16, 64
C_Q_W = C_HEADS * 2 * C_HEAD_DIM
D_FF = 5632

LANES = 128
BF16_SUBLANES = 16
VMEM_LIMIT = 56 * 1024 * 1024
DFT_RADIX = 128
FLASH_ROWS = 4096
FLASH_SUB_ROWS = 128
FLASH_MAX_SINGLE_KV = 16384
LOG2E = math.log2(math.e)


def _cparams(*sem):
    return pltpu.CompilerParams(dimension_semantics=sem, vmem_limit_bytes=VMEM_LIMIT)


def _rms(x, g):
    return x * lax.rsqrt(jnp.mean(x * x, axis=-1, keepdims=True) + EPS) * g


def _ada_kernel(c_ref, w_ref, b_ref, o_ref):
    c = c_ref[...]
    s = c * jax.nn.sigmoid(c)
    rows = s.shape[0]
    s_hi = s.astype(BF16).astype(F32)
    s2 = jnp.concatenate([s_hi, s - s_hi], axis=0).astype(BF16)
    w = w_ref[0]
    w_hi = w.astype(BF16)
    w_lo = (w - w_hi.astype(F32)).astype(BF16)
    both = jnp.dot(s2, w_hi, preferred_element_type=F32)
    hi_lo = jnp.dot(s2, w_lo, preferred_element_type=F32)[:rows]
    o_ref[0] = both[:rows] + both[rows:] + hi_lo + b_ref[0]


def ada_mod(cvec, ada_w, ada_b):
    depth, d, n6 = ada_w.shape
    rows = cvec.shape[0]
    tn = 1024
    return pl.pallas_call(
        _ada_kernel,
        grid=(depth, n6 // tn),
        in_specs=[pl.BlockSpec((rows, d), lambda l, j: (0, 0)),
                  pl.BlockSpec((1, d, tn), lambda l, j: (l, 0, j)),
                  pl.BlockSpec((1, 1, tn), lambda l, j: (l, 0, j))],
        out_specs=pl.BlockSpec((1, rows, tn), lambda l, j: (l, 0, j)),
        out_shape=jax.ShapeDtypeStruct((depth, rows, n6), F32),
        compiler_params=_cparams("arbitrary", "arbitrary"),
        name="ada_mod",
    )(cvec, ada_w, ada_b.reshape(depth, 1, n6))


def _norm_mod_kernel(x_ref, g_ref, sc_ref, sh_ref, o_ref):
    y = _rms(x_ref[0], g_ref[...])
    o_ref[0] = (y * (1.0 + sc_ref[0]) + sh_ref[0]).astype(o_ref.dtype)


def norm_mod(x, g, sc, sh):
    b, n, d = x.shape
    tm = min(n, 512)
    return pl.pallas_call(
        _norm_mod_kernel,
        grid=(b, n // tm),
        in_specs=[pl.BlockSpec((1, tm, d), lambda bi, i: (bi, i, 0)),
                  pl.BlockSpec((1, d), lambda bi, i: (0, 0)),
                  pl.BlockSpec((1, 1, d), lambda bi, i: (bi, 0, 0)),
                  pl.BlockSpec((1, 1, d), lambda bi, i: (bi, 0, 0))],
        out_specs=pl.BlockSpec((1, tm, d), lambda bi, i: (bi, i, 0)),
        out_shape=jax.ShapeDtypeStruct((b, n, d), BF16),
        compiler_params=_cparams("arbitrary", "arbitrary"),
        name="norm_mod",
    )(x, g.reshape(1, d), sc.reshape(b, 1, d), sh.reshape(b, 1, d))


def _inproj_kernel(*refs, hd, norm, rope, appending, n_tiles, has_tail):
    o_ref = refs[-1]
    if appending:
        refs = refs[:-2]
    else:
        refs = refs[:-1]
    if rope:
        h_ref, w_ref, g_ref, cos_ref, sin_ref = refs
    else:
        h_ref, w_ref, g_ref = refs
    i = pl.program_id(1)

    def _tail():
        o_ref[0] = jnp.zeros(o_ref.shape[1:], o_ref.dtype)

    def _body():
        acc = jnp.dot(h_ref[0], w_ref[...], preferred_element_type=F32)
        tn = acc.shape[1]
        if rope:
            cosf = cos_ref[...]
            sinf = sin_ref[...]
            if hd != LANES:
                lane = lax.broadcasted_iota(jnp.int32, cosf.shape, 1)
                first_half = (lane % hd) < (hd // 2)
        for c in range(tn // LANES):
            x = acc[:, c * LANES:(c + 1) * LANES]
            if norm:
                x = x * lax.rsqrt(jnp.mean(x * x, axis=-1, keepdims=True) + EPS)
            x = x * g_ref[:, c * LANES:(c + 1) * LANES]
            if rope:
                if hd == LANES:
                    partner = pltpu.roll(x, LANES // 2, 1)
                else:
                    partner = jnp.where(first_half, pltpu.roll(x, LANES - hd // 2, 1),
                                        pltpu.roll(x, hd // 2, 1))
                x = x * cosf + partner * sinf
            o_ref[0, :, c * LANES:(c + 1) * LANES] = x.astype(o_ref.dtype)

    if has_tail:
        pl.when(i >= n_tiles)(_tail)
        pl.when(i < n_tiles)(_body)
    else:
        _body()


def inproj(h, w, gvec, cosf=None, sinf=None, *, hd=LANES, norm=False, out_dtype=BF16,
           tail_rows=0, append_to=None, append_at=0):
    b, n, d = h.shape
    ncol = w.shape[1]
    rope = cosf is not None
    tm = min(n, 512)
    tn = min(ncol, 2048)
    n_tiles = n // tm
    last = n_tiles - 1
    in_specs = [pl.BlockSpec((1, tm, d), lambda bi, i, j: (bi, jnp.minimum(i, last), 0)),
                pl.BlockSpec((d, tn), lambda bi, i, j: (0, j)),
                pl.BlockSpec((1, tn), lambda bi, i, j: (0, j))]
    args = [h, w, gvec.reshape(1, ncol)]
    if rope:
        in_specs += [pl.BlockSpec((tm, LANES), lambda bi, i, j: (jnp.minimum(i, last), 0)),
                     pl.BlockSpec((tm, LANES), lambda bi, i, j: (jnp.minimum(i, last), 0))]
        args += [cosf, sinf]
    appending = append_to is not None
    if appending:
        total = append_to.shape[1]
        first = append_at // tm
        assert first * tm == append_at and append_at + n <= total and append_to.dtype == out_dtype
        tail_tiles = 0
        in_specs.append(pl.BlockSpec(memory_space=pl.ANY))
        args.append(append_to)
        aliases = {len(args) - 1: 0}
    else:
        tail_tiles = -(-tail_rows // tm)
        total, first, aliases = (n_tiles + tail_tiles) * tm, 0, {}
    return pl.pallas_call(
        functools.partial(_inproj_kernel, hd=hd, norm=norm, rope=rope, appending=appending,
                          n_tiles=n_tiles, has_tail=tail_tiles > 0),
        grid=(b, n_tiles + tail_tiles, ncol // tn),
        in_specs=in_specs,
        out_specs=pl.BlockSpec((1, tm, tn), lambda bi, i, j: (bi, first + i, j)),
        out_shape=jax.ShapeDtypeStruct((b, total, ncol), out_dtype),
        input_output_aliases=aliases,
        compiler_params=_cparams("arbitrary", "arbitrary", "arbitrary"),
        name="inproj",
    )(*args)


def _inproj_conv_kernel(hm_ref, hp_ref, hn_ref, w_ref, cw_ref, cb_ref, o_ref, *, n_tiles):
    i = pl.program_id(2)
    hp, hm, hn = hp_ref[0], hm_ref[0], hn_ref[0]
    halo, tm = hp.shape[0], hm.shape[0]
    zero = jnp.zeros_like(hp)
    lhs = jnp.concatenate([jnp.where(i > 0, hp, zero), hm,
                           jnp.where(i < n_tiles - 1, hn, zero)], axis=0)
    u = jnp.dot(lhs, w_ref[...], preferred_element_type=F32)
    rows = tm + 2 * halo
    cw = cw_ref[...]
    conv = cw[0:1] * pltpu.roll(u, 1, 0) + cw[1:2] * u + cw[2:3] * pltpu.roll(u, rows - 1, 0)
    o_ref[0, 0] = conv[halo:halo + tm] + cb_ref[...]


def inproj_conv(h, w, conv_w, conv_b, groups):
    b, n, d = h.shape
    ncol = w.shape[1]
    c = ncol // groups
    tm = min(n, 512)
    tn = min(c, 1024)
    halo = BF16_SUBLANES
    nt = n // tm
    per = c // tn
    return pl.pallas_call(
        functools.partial(_inproj_conv_kernel, n_tiles=nt),
        grid=(ncol // tn, b, nt),
        in_specs=[pl.BlockSpec((1, tm, d), lambda j, bi, i: (bi, i, 0)),
                  pl.BlockSpec((1, halo, d),
                               lambda j, bi, i: (bi, jnp.maximum(i * (tm // halo) - 1, 0), 0)),
                  pl.BlockSpec((1, halo, d),
                               lambda j, bi, i: (bi, jnp.minimum((i + 1) * (tm // halo), n // halo - 1), 0)),
                  pl.BlockSpec((d, tn), lambda j, bi, i: (0, j)),
                  pl.BlockSpec((3, tn), lambda j, bi, i: (0, j)),
                  pl.BlockSpec((1, tn), lambda j, bi, i: (0, j))],
        out_specs=pl.BlockSpec((1, 1, tm, tn), lambda j, bi, i: (j // per, bi, i, j % per)),
        out_shape=jax.ShapeDtypeStruct((groups, b, n, c), F32),
        compiler_params=_cparams("arbitrary", "arbitrary", "arbitrary"),
        name="inproj_conv",
    )(h, h, h, w, conv_w, conv_b.reshape(1, ncol))


def _flash_kernel(*refs, mode, group, tq, tk, n_kv, lambda_init):
    if mode == "diff":
        q_ref, k_ref, v_ref, lam_ref, sg_ref, o_ref, qs_ref, m_ref, acc_ref = refs
    else:
        q_ref, k_ref, v_ref, o_ref, qs_ref, m_ref, acc_ref = refs
    kv = pl.program_id(3)
    single = n_kv == 1

    def _init():
        q = q_ref[0]
        if mode == "diff":
            lane = lax.broadcasted_iota(jnp.int32, q.shape, 1)
            zero = jnp.zeros_like(q)
            qs_ref[0:tq] = jnp.where(lane < C_HEAD_DIM, q, zero)
            qs_ref[tq:2 * tq] = jnp.where(lane >= C_HEAD_DIM, q, zero)
        else:
            for g in range(group):
                qs_ref[g * tq:(g + 1) * tq] = q[:, g * LANES:(g + 1) * LANES]
        if not single:
            m_ref[...] = jnp.full(m_ref.shape, -jnp.inf, F32)
            acc_ref[...] = jnp.zeros(acc_ref.shape, F32)

    if single:
        _init()
    else:
        pl.when(kv == 0)(_init)

    k = k_ref[0]
    v = v_ref[0]
    v_aug = jnp.concatenate([v, jnp.ones_like(v)], axis=1)
    for c in range(group * tq // FLASH_SUB_ROWS):
        sl = slice(c * FLASH_SUB_ROWS, (c + 1) * FLASH_SUB_ROWS)
        s = lax.dot_general(qs_ref[sl], k, (((1,), (1,)), ((), ())),
                            preferred_element_type=F32)
        if single:
            p = jnp.exp2(s - jnp.max(s, axis=-1, keepdims=True))
            acc_ref[sl] = jnp.dot(p.astype(BF16), v_aug, preferred_element_type=F32)
            continue
        m_prev = m_ref[sl]
        m_next = jnp.maximum(m_prev, jnp.max(s, axis=-1, keepdims=True))
        alpha = jnp.exp2(m_prev - m_next)
        p = jnp.exp2(s - jnp.tile(m_next, (1, tk // LANES)))
        acc_ref[sl] = jnp.tile(alpha, (1, 2)) * acc_ref[sl] + jnp.dot(
            p.astype(BF16), v_aug, preferred_element_type=F32)
        m_ref[sl] = m_next

    def _fin():
        acc = acc_ref[...]
        o = acc[:, :LANES] / acc[:, LANES:]
        if mode == "diff":
            lv = lam_ref[...]
            lam = (jnp.exp(jnp.sum(lv[0:1] * lv[1:2], axis=-1, keepdims=True))
                   - jnp.exp(jnp.sum(lv[2:3] * lv[3:4], axis=-1, keepdims=True)) + lambda_init)
            dlt = o[0:tq] - lam * o[tq:2 * tq]
            o_ref[0] = (_rms(dlt, sg_ref[...]) * (1.0 - lambda_init)).astype(o_ref.dtype)
        else:
            for g in range(group):
                o_ref[0, :, g * LANES:(g + 1) * LANES] = o[g * tq:(g + 1) * tq].astype(o_ref.dtype)

    if single:
        _fin()
    else:
        pl.when(kv == n_kv - 1)(_fin)


def _kv_tile(s):
    if s <= FLASH_MAX_SINGLE_KV and s % LANES == 0:
        return s
    for t in (2816, 2048, 1024, 768, 512, 256, 128):
        if s % t == 0:
            return t
    return s


def flash_attention(q, k, v, *, mode, kv_len, lam_vecs=None, subln_g=None, lambda_init=0.0,
                    kv_start=0):
    b, n, qw = q.shape
    s = kv_len
    if mode == "diff":
        group, n_groups, qblk = 2, qw // LANES, LANES
    else:
        group = A_HEADS // A_KV_HEADS
        n_groups, qblk = A_KV_HEADS, group * LANES
    tq = min(n, FLASH_ROWS // group)
    tk = _kv_tile(s)
    n_kv = s // tk
    j0 = kv_start // tk
    assert j0 * tk == kv_start
    in_specs = [pl.BlockSpec((1, tq, qblk), lambda bi, g, i, j: (bi, i, g)),
                pl.BlockSpec((1, tk, LANES), lambda bi, g, i, j: (bi, j0 + j, g)),
                pl.BlockSpec((1, tk, LANES), lambda bi, g, i, j: (bi, j0 + j, g))]
    args = [q, k, v]
    if mode == "diff":
        in_specs += [pl.BlockSpec(lam_vecs.shape, lambda bi, g, i, j: (0, 0)),
                     pl.BlockSpec((1, LANES), lambda bi, g, i, j: (0, 0))]
        args += [lam_vecs, subln_g.reshape(1, LANES)]
    return pl.pallas_call(
        functools.partial(_flash_kernel, mode=mode, group=group, tq=tq, tk=tk, n_kv=n_kv,
                          lambda_init=lambda_init),
        grid=(b, n_groups, n // tq, n_kv),
        in_specs=in_specs,
        out_specs=pl.BlockSpec((1, tq, qblk), lambda bi, g, i, j: (bi, i, g)),
        out_shape=jax.ShapeDtypeStruct((b, n, qw), BF16),
        scratch_shapes=[pltpu.VMEM((group * tq, LANES), BF16),
                        pltpu.VMEM((group * tq, LANES), F32),
                        pltpu.VMEM((group * tq, 2 * LANES), F32)],
        compiler_params=_cparams("arbitrary", "arbitrary", "arbitrary", "arbitrary"),
        name="flash_" + mode,
    )(*args)


def _outproj_kernel(*refs, n_lhs):
    lhs = refs[:n_lhs]
    ws = refs[n_lhs:2 * n_lhs]
    x_ref, g_ref, gate_ref, o_ref = refs[2 * n_lhs:]
    y = jnp.dot(lhs[0][0].astype(BF16), ws[0][...], preferred_element_type=F32)
    for a, w in zip(lhs[1:], ws[1:]):
        y = y + jnp.dot(a[0].astype(BF16), w[...], preferred_element_type=F32)
    o_ref[0] = x_ref[0] + gate_ref[0] * _rms(y, g_ref[...])


def outproj_residual(lhs_list, w_list, x, g, gate):
    b, n, d = x.shape
    tm = min(n, 512)
    n_lhs = len(lhs_list)
    in_specs = [pl.BlockSpec((1, tm, a.shape[2]), lambda bi, i: (bi, i, 0)) for a in lhs_list]
    in_specs += [pl.BlockSpec(w.shape, lambda bi, i: (0, 0)) for w in w_list]
    in_specs += [pl.BlockSpec((1, tm, d), lambda bi, i: (bi, i, 0)),
                 pl.BlockSpec((1, d), lambda bi, i: (0, 0)),
                 pl.BlockSpec((1, 1, d), lambda bi, i: (bi, 0, 0))]
    return pl.pallas_call(
        functools.partial(_outproj_kernel, n_lhs=n_lhs),
        grid=(b, n // tm),
        in_specs=in_specs,
        out_specs=pl.BlockSpec((1, tm, d), lambda bi, i: (bi, i, 0)),
        out_shape=jax.ShapeDtypeStruct((b, n, d), F32),
        compiler_params=_cparams("arbitrary", "arbitrary"),
        name="outproj",
    )(*lhs_list, *w_list, x, g.reshape(1, d), gate.reshape(b, 1, d))


def _ffn_kernel(xm_ref, xp_ref, xn_ref, gn_ref, sc_ref, sh_ref, wa_ref, wg_ref, cwa_ref, cwg_ref,
                cba_ref, cbg_ref, wd_ref, g3_ref, gate_ref, o_ref, hx_ref,
                *, tm, halo, n_tiles, n_f):
    i = pl.program_id(1)
    j = pl.program_id(2)

    @pl.when(j == 0)
    def _prologue():
        def nm(x):
            return (_rms(x, gn_ref[...]) * (1.0 + sc_ref[0]) + sh_ref[0]).astype(BF16)
        hx_ref[halo:halo + tm] = nm(xm_ref[0])
        zero = jnp.zeros((halo, xm_ref.shape[2]), BF16)
        hx_ref[0:halo] = jnp.where(i > 0, nm(xp_ref[0]), zero)
        hx_ref[halo + tm:] = jnp.where(i < n_tiles - 1, nm(xn_ref[0]), zero)
        o_ref[0] = jnp.zeros(o_ref.shape[1:], F32)

    hx = hx_ref[...]
    rows = tm + 2 * halo

    def conv(u, cw, cb):
        up = pltpu.roll(u, 1, 0)
        dn = pltpu.roll(u, rows - 1, 0)
        return (cw[0:1] * up + cw[1:2] * u + cw[2:3] * dn + cb)[halo:halo + tm]

    a = conv(jnp.dot(hx, wa_ref[...], preferred_element_type=F32), cwa_ref[...], cba_ref[...])
    g = conv(jnp.dot(hx, wg_ref[...], preferred_element_type=F32), cwg_ref[...], cbg_ref[...])
    act = (a * (g * jax.nn.sigmoid(g))).astype(BF16)
    o_ref[0] += jnp.dot(act, wd_ref[...], preferred_element_type=F32)

    @pl.when(j == n_f - 1)
    def _epilogue():
        o_ref[0] = xm_ref[0] + gate_ref[0] * _rms(o_ref[0], g3_ref[...])


def conv_ffn_residual(x, gn, sc, sh, w_up, conv_w, conv_b, w_down, g3, gate, layer):
    b, n, d = x.shape
    dff = w_down.shape[1]
    tm = min(n, 512)
    tf = 512
    halo = BF16_SUBLANES
    nt = n // tm
    n_f = dff // tf
    row = lambda bi, i, j: (bi, i, 0)
    vec = lambda bi, i, j: (0, 0)
    bvec = lambda bi, i, j: (bi, 0, 0)
    up_a = lambda bi, i, j: (0, j)
    up_g = lambda bi, i, j: (0, n_f + j)
    wup_a = lambda bi, i, j: (layer, 0, j)
    wup_g = lambda bi, i, j: (layer, 0, n_f + j)
    conv_b = conv_b.reshape(1, 2 * dff)
    return pl.pallas_call(
        functools.partial(_ffn_kernel, tm=tm, halo=halo, n_tiles=nt, n_f=n_f),
        grid=(b, nt, n_f),
        in_specs=[pl.BlockSpec((1, tm, d), row),
                  pl.BlockSpec((1, halo, d),
                               lambda bi, i, j: (bi, jnp.maximum(i * (tm // halo) - 1, 0), 0)),
                  pl.BlockSpec((1, halo, d),
                               lambda bi, i, j: (bi, jnp.minimum((i + 1) * (tm // halo), n // halo - 1), 0)),
                  pl.BlockSpec((1, d), vec),
                  pl.BlockSpec((1, 1, d), bvec),
                  pl.BlockSpec((1, 1, d), bvec),
                  pl.BlockSpec((None, d, tf), wup_a),
                  pl.BlockSpec((None, d, tf), wup_g),
                  pl.BlockSpec((3, tf), up_a),
                  pl.BlockSpec((3, tf), up_g),
                  pl.BlockSpec((1, tf), up_a),
                  pl.BlockSpec((1, tf), up_g),
                  pl.BlockSpec((None, tf, d), lambda bi, i, j: (layer, j, 0)),
                  pl.BlockSpec((1, d), vec),
                  pl.BlockSpec((1, 1, d), bvec)],
        out_specs=pl.BlockSpec((1, tm, d), row),
        out_shape=jax.ShapeDtypeStruct((b, n, d), F32),
        scratch_shapes=[pltpu.VMEM((tm + 2 * halo, d), BF16)],
        compiler_params=_cparams("arbitrary", "arbitrary", "arbitrary"),
        name="conv_ffn",
    )(x, x, x, gn.reshape(1, d), sc.reshape(b, 1, d), sh.reshape(b, 1, d), w_up, w_up,
      conv_w, conv_w, conv_b, conv_b, w_down, g3.reshape(1, d), gate.reshape(b, 1, d))


def _filter_rows(r, bands_ref, w1_ref, b1_ref, w2_ref, b2_ref, fr_ref, dl_ref, n_tok):
    t = jnp.where(r < n_tok, r, 2.0 * n_tok - r)
    t_norm = t / float(max(n_tok - 1, 1))
    wang = (2.0 * math.pi / n_tok) * t
    z = wang * bands_ref[...]
    lane = lax.broadcasted_iota(jnp.int32, z.shape, 1)
    feats = jnp.where(lane == 0, t_norm,
                      jnp.where(lane <= HY_BANDS, jnp.cos(z),
                                jnp.where(lane <= 2 * HY_BANDS, -jnp.sin(z), 0.0)))
    fr = fr_ref[...]
    h = jnp.sin(fr[0:1] * (jnp.dot(feats, w1_ref[...], precision=HIGHEST,
                                   preferred_element_type=F32) + b1_ref[...]))
    h = jnp.sin(fr[1:2] * (jnp.dot(h, w2_ref[...], precision=HIGHEST,
                                   preferred_element_type=F32) + b2_ref[...]))
    decay = jnp.where(r == n_tok, 0.0, jnp.exp(-t_norm * dl_ref[...]))
    return h, decay


def _filter_kernel(bands_ref, w1_ref, b1_ref, w2_ref, b2_ref, w3_ref, fr_ref, dl_ref,
                   o_ref, s_ref, *, n_tok, tr):
    i = pl.program_id(0)

    @pl.when(i == 0)
    def _init():
        s_ref[...] = jnp.zeros(s_ref.shape, F32)

    r = (i * tr + lax.broadcasted_iota(jnp.int32, (tr, 1), 0)).astype(F32)
    h, decay = _filter_rows(r, bands_ref, w1_ref, b1_ref, w2_ref, b2_ref, fr_ref, dl_ref, n_tok)
    hb = h.astype(BF16)
    for o in range(HY_ORDER):
        out = jnp.dot(hb, w3_ref[o].astype(BF16), preferred_element_type=F32) * decay
        o_ref[o] = out
        s_ref[o] += jnp.sum(jnp.abs(out), axis=0, keepdims=True)


def _filter_stage_a_kernel(bands_ref, w1_ref, b1_ref, w2_ref, b2_ref, w3_ref, fr_ref, dl_ref,
                           f_ref, *out_refs, n_tok, sub):
    o_refs, s_ref = out_refs[:HY_ORDER], out_refs[HY_ORDER]
    j = pl.program_id(0)

    @pl.when(j == 0)
    def _init():
        s_ref[...] = jnp.zeros(s_ref.shape, F32)

    k1n = o_refs[0].shape[1]
    half = k1n // 2
    c = dl_ref.shape[1]
    n1 = lax.broadcasted_iota(jnp.int32, (k1n, 1), 0)
    f = f_ref[...]
    for q in range(sub):
        r = (n1 * DFT_RADIX + (j * sub + q)).astype(F32)
        h, decay = _filter_rows(r, bands_ref, w1_ref, b1_ref, w2_ref, b2_ref, fr_ref, dl_ref, n_tok)
        h_fwd, h_bwd = h[:half].astype(BF16), h[half:].astype(BF16)
        for o in range(HY_ORDER):
            w3 = w3_ref[o].astype(BF16)
            h3 = jnp.concatenate([jnp.dot(h_fwd, w3[:, :c], preferred_element_type=F32),
                                  jnp.dot(h_bwd, w3[:, c:], preferred_element_type=F32)],
                                 axis=0) * decay
            s_ref[o] += jnp.sum(jnp.abs(h3), axis=0, keepdims=True)
            y = _dft_dot(f, h3)
            o_refs[o][0, :, q, :] = y[:k1n]
            o_refs[o][1, :, q, :] = y[k1n:]


def _filter_operands(w1, b1, w2, b2, w3, freq):
    ch = HY_CH
    lane = jnp.arange(LANES)
    bands = jnp.linspace(1e-4, HY_BANDS - 1, HY_BANDS, dtype=F32)
    bands_l = jnp.where((lane >= 1) & (lane <= HY_BANDS), bands[jnp.clip(lane - 1, 0, HY_BANDS - 1)],
                        jnp.where((lane > HY_BANDS) & (lane <= 2 * HY_BANDS),
                                  bands[jnp.clip(lane - 1 - HY_BANDS, 0, HY_BANDS - 1)], 0.0))
    w1p = jnp.zeros((LANES, HY_FILTER_W), F32).at[:w1.shape[0]].set(w1)
    deltas = jnp.abs(jnp.linspace(math.log(HY_TARGET) / HY_FAST, math.log(HY_TARGET) / HY_SLOW,
                                  ch, dtype=F32))
    w3o = jnp.swapaxes(w3.reshape(HY_FILTER_W, HY_ORDER, 2 * ch), 0, 1)
    return (bands_l.reshape(1, LANES), w1p, b1.reshape(1, -1), w2, b2.reshape(1, -1), w3o, freq,
            deltas.reshape(1, ch))


def _filter_in_specs(w3_spec):
    const = lambda i: (0, 0)
    return [pl.BlockSpec((1, LANES), const),
            pl.BlockSpec((LANES, HY_FILTER_W), const),
            pl.BlockSpec((1, HY_FILTER_W), const),
            pl.BlockSpec((HY_FILTER_W, HY_FILTER_W), const),
            pl.BlockSpec((1, HY_FILTER_W), const),
            w3_spec,
            pl.BlockSpec((2, HY_FILTER_W), const),
            pl.BlockSpec((1, HY_CH), const)]


def hyena_filter_time(n_tok, w1, b1, w2, b2, w3, freq):
    ch = HY_CH
    ll = 2 * n_tok
    tr = min(n_tok, 512)
    per_half = n_tok // tr
    return pl.pallas_call(
        functools.partial(_filter_kernel, n_tok=n_tok, tr=tr),
        grid=(ll // tr,),
        in_specs=_filter_in_specs(
            pl.BlockSpec((HY_ORDER, HY_FILTER_W, ch), lambda i: (0, 0, i // per_half))),
        out_specs=[pl.BlockSpec((HY_ORDER, tr, ch), lambda i: (0, i, 0)),
                   pl.BlockSpec((HY_ORDER, 1, ch), lambda i: (0, 0, 0))],
        out_shape=[jax.ShapeDtypeStruct((HY_ORDER, ll, ch), F32),
                   jax.ShapeDtypeStruct((HY_ORDER, 1, ch), F32)],
        compiler_params=_cparams("arbitrary"),
        name="hyena_filter",
    )(*_filter_operands(w1, b1, w2, b2, w3, freq))


def hyena_filter_stage_a(n_tok, f_filt, w1, b1, w2, b2, w3, freq):
    ch = HY_CH
    r = DFT_RADIX
    k1n = 2 * n_tok // r
    sub = 8
    return pl.pallas_call(
        functools.partial(_filter_stage_a_kernel, n_tok=n_tok, sub=sub),
        grid=(r // sub,),
        in_specs=_filter_in_specs(
            pl.BlockSpec((HY_ORDER, HY_FILTER_W, 2 * ch), lambda j: (0, 0, 0)))
        + [pl.BlockSpec(f_filt.shape, lambda j: (0, 0))],
        out_specs=[pl.BlockSpec((2, k1n, sub, ch), lambda j: (0, 0, j, 0))] * HY_ORDER
        + [pl.BlockSpec((HY_ORDER, 1, ch), lambda j: (0, 0, 0))],
        out_shape=[jax.ShapeDtypeStruct((2, k1n, r, ch), F32)] * HY_ORDER
        + [jax.ShapeDtypeStruct((HY_ORDER, 1, ch), F32)],
        compiler_params=_cparams("arbitrary"),
        name="hyena_filter_a",
    )(*_filter_operands(w1, b1, w2, b2, w3, freq), f_filt)


def _dft_dot(f, x):
    return jnp.dot(f.astype(BF16), x.astype(BF16), preferred_element_type=F32)


def _stage_a_kernel(f_ref, x_ref, o_ref):
    parts = x_ref.shape[0]
    f = f_ref[...]
    for r in range(x_ref.shape[2]):
        z = jnp.concatenate([x_ref[p, :, r, :] for p in range(parts)], axis=0)
        y = _dft_dot(f, z)
        half = y.shape[0] // 2
        o_ref[0, :, r, :] = y[:half]
        o_ref[1, :, r, :] = y[half:]


def dft_stage_a(f, x, g):
    _, p, rows, r, c = x.shape
    k1 = p * rows
    sub = 8
    return pl.pallas_call(
        _stage_a_kernel,
        grid=(r // sub,),
        in_specs=[pl.BlockSpec(f.shape, lambda j: (0, 0)),
                  pl.BlockSpec((None, p, rows, sub, c), lambda j: (g, 0, 0, j, 0))],
        out_specs=pl.BlockSpec((2, k1, sub, c), lambda j: (0, 0, j, 0)),
        out_shape=jax.ShapeDtypeStruct((2, k1, r, c), F32),
        compiler_params=_cparams("arbitrary"),
        name="dft_stage_a",
    )(f, x)


def _stage_c_conv_kernel(a_ref, af_ref, sc_ref, g_ref, gi_ref, o_ref, *, kc):
    tc = a_ref.shape[3]
    for k in range(kc):
        r = jnp.concatenate([jnp.concatenate([a_ref[0, k], af_ref[0, k]], axis=1),
                             jnp.concatenate([a_ref[1, k], af_ref[1, k]], axis=1)], axis=0)
        y = _dft_dot(g_ref[k], r)
        half = y.shape[0] // 2
        yre, yim = y[:half, :tc], y[half:, :tc]
        hre, him = y[:half, tc:] * sc_ref[...], y[half:, tc:] * sc_ref[...]
        p = jnp.concatenate([yre * hre - yim * him, yre * him + yim * hre], axis=0)
        d = _dft_dot(gi_ref[k], p)
        o_ref[0, k] = d[:half]
        o_ref[1, k] = d[half:]


def dft_stage_c_conv(a, af, scale, g, gi):
    _, k1, r, c = a.shape
    kc = 8
    tc = 256
    blk = pl.BlockSpec((2, kc, r, tc), lambda i, j: (0, i, 0, j))
    mat = pl.BlockSpec((kc, 2 * r, 2 * r), lambda i, j: (i, 0, 0))
    return pl.pallas_call(
        functools.partial(_stage_c_conv_kernel, kc=kc),
        grid=(k1 // kc, c // tc),
        in_specs=[blk, blk, pl.BlockSpec((1, tc), lambda i, j: (0, j)), mat, mat],
        out_specs=blk,
        out_shape=jax.ShapeDtypeStruct(a.shape, F32),
        compiler_params=_cparams("arbitrary", "arbitrary"),
        name="dft_stage_c_conv",
    )(a, af, scale, g, gi)


def _stage_a_inv_kernel(f_ref, d_ref, u_ref, gt_ref, sk_ref, o_ref):
    f = f_ref[...]
    sk = sk_ref[...]
    for r in range(d_ref.shape[2]):
        d = jnp.concatenate([d_ref[0, :, r, :], d_ref[1, :, r, :]], axis=0)
        y = _dft_dot(f, d)
        half = y.shape[0] // 2
        for bi in range(2):
            yb = y[bi * half:(bi + 1) * half]
            o_ref[bi, :, r, :] = gt_ref[bi, :, r, :] * (yb + u_ref[bi, :, r, :] * sk)


def dft_stage_a_inv(f, d, u, gu, gate, gg, skip):
    _, k1, r, c = d.shape
    half = k1 // 2
    sub = 8
    io = pl.BlockSpec((2, half, sub, c), lambda j: (0, 0, j, 0))
    return pl.pallas_call(
        _stage_a_inv_kernel,
        grid=(r // sub,),
        in_specs=[pl.BlockSpec(f.shape, lambda j: (0, 0)),
                  pl.BlockSpec((2, k1, sub, c), lambda j: (0, 0, j, 0)),
                  pl.BlockSpec((None, 2, half, sub, c), lambda j: (gu, 0, 0, j, 0)),
                  pl.BlockSpec((None, 2, half, sub, c), lambda j: (gg, 0, 0, j, 0)),
                  pl.BlockSpec((1, c), lambda j: (0, 0))],
        out_specs=io,
        out_shape=jax.ShapeDtypeStruct((2, half, r, c), F32),
        compiler_params=_cparams("arbitrary"),
        name="dft_stage_a_inv",
    )(f, d, u, gate, skip)


def _dft_tables(n_tok):
    ll = 2 * n_tok
    r = DFT_RADIX
    k1n = ll // r
    half = k1n // 2
    two_pi = 2.0 * math.pi

    def cs(m, period):
        ang = (m % period).astype(F32) * (two_pi / period)
        return jnp.cos(ang), jnp.sin(ang)

    k1 = jnp.arange(k1n, dtype=jnp.int32)
    c, s = cs(k1[:, None] * k1[None, :half], k1n)
    fa_data = jnp.concatenate([jnp.concatenate([c, s], 1), jnp.concatenate([-s, c], 1)], 0)
    c, s = cs(k1[:, None] * k1[None, :], k1n)
    fa_filt = jnp.concatenate([c, -s], 0)
    c, s = cs(k1[:half, None] * k1[None, :], k1n)
    fa_inv = jnp.concatenate([jnp.concatenate([c, -s], 1), jnp.concatenate([s, c], 1)], 0)
    idx = jnp.arange(r, dtype=jnp.int32)
    ca, sa = cs(idx[:, None] * idx[None, :], r)
    cb, sb = cs(k1[:, None] * idx[None, :], ll)
    c = ca[None] * cb[:, None, :] - sa[None] * sb[:, None, :]
    s = sa[None] * cb[:, None, :] + ca[None] * sb[:, None, :]
    g_fwd = jnp.concatenate([jnp.concatenate([c, s], 2), jnp.concatenate([-s, c], 2)], 1)
    ct, st = jnp.swapaxes(c, 1, 2), jnp.swapaxes(s, 1, 2)
    g_inv = jnp.concatenate([jnp.concatenate([ct, -st], 2), jnp.concatenate([st, ct], 2)], 1)
    return tuple(t.astype(BF16) for t in (fa_data, fa_filt, fa_inv, g_fwd, g_inv))


def hyena_long(pc, filter_params, skip):
    _, b, n, c = pc.shape
    assert b == 2, "the two batch entries ride as real/imaginary parts"
    ll = 2 * n
    r = DFT_RADIX
    half = ll // r // 2
    fa_data, fa_filt, fa_inv, g_fwd, g_inv = _dft_tables(n)
    *afs, l1 = hyena_filter_stage_a(n, fa_filt, *filter_params)
    pcs = pc.reshape(HY_ORDER + 1, 2, half, r, c)
    u, gu = pcs, 0
    for o in range(HY_ORDER):
        a = dft_stage_a(fa_data, u, gu)
        d = dft_stage_c_conv(a, afs[o], 1.0 / (l1[o] * ll), g_fwd, g_inv)
        z = dft_stage_a_inv(fa_inv, d, u, gu, pcs, o + 1, skip[o].reshape(1, c))
        u, gu = z[None], 0
    return z.reshape(b, n, c)


def _hyena_short_kernel(ff_ref, fr_ref, fi_ref, pc_ref, filt_ref, l1_ref, skip_ref, o_ref, *, n, ll):
    u = [pc_ref[0, 0], pc_ref[0, 1]]
    for o in range(HY_ORDER):
        hs = _dft_dot(fr_ref[...], filt_ref[o]) * (1.0 / (l1_ref[o] * ll))
        y = _dft_dot(ff_ref[...], jnp.concatenate(u, axis=0))
        yre, yim, hre, him = y[:ll], y[ll:], hs[:ll], hs[ll:]
        p = jnp.concatenate([yre * hre - yim * him, yre * him + yim * hre], axis=0)
        d = _dft_dot(fi_ref[...], p)
        sk = skip_ref[o:o + 1]
        u = [pc_ref[o + 1, bi] * (d[bi * n:(bi + 1) * n] + u[bi] * sk) for bi in range(2)]
    o_ref[0] = u[0].astype(o_ref.dtype)
    o_ref[1] = u[1].astype(o_ref.dtype)


def hyena_short(pc, filt, l1, skip, out_dtype):
    _, b, n, c = pc.shape
    assert b == 2
    ll = 2 * n
    two_pi = 2.0 * math.pi
    kk = jnp.arange(ll, dtype=jnp.int32)
    ang = ((kk[:, None] * kk[None, :]) % ll).astype(F32) * (two_pi / ll)
    cf, sf = jnp.cos(ang), jnp.sin(ang)
    f_real = jnp.concatenate([cf, -sf], 0)
    cn, sn = cf[:, :n], sf[:, :n]
    f_fwd = jnp.concatenate([jnp.concatenate([cn, sn], 1), jnp.concatenate([-sn, cn], 1)], 0)
    ci, si = cf[:n, :], sf[:n, :]
    f_inv = jnp.concatenate([jnp.concatenate([ci, -si], 1), jnp.concatenate([si, ci], 1)], 0)
    tc = 256
    const = lambda j: (0, 0)
    return pl.pallas_call(
        functools.partial(_hyena_short_kernel, n=n, ll=ll),
        grid=(c // tc,),
        in_specs=[pl.BlockSpec(f_fwd.shape, const),
                  pl.BlockSpec(f_real.shape, const),
                  pl.BlockSpec(f_inv.shape, const),
                  pl.BlockSpec((3, 2, n, tc), lambda j: (0, 0, 0, j)),
                  pl.BlockSpec((HY_ORDER, ll, tc), lambda j: (0, 0, j)),
                  pl.BlockSpec((HY_ORDER, 1, tc), lambda j: (0, 0, j)),
                  pl.BlockSpec((HY_ORDER, tc), lambda j: (0, j))],
        out_specs=pl.BlockSpec((2, n, tc), lambda j: (0, 0, j)),
        out_shape=jax.ShapeDtypeStruct((2, n, c), out_dtype),
        compiler_params=_cparams("arbitrary"),
        name="hyena_short",
    )(f_fwd, f_real, f_inv, pc, filt, l1, skip)


def hyena_mixer(pc, w1, b1, w2, b2, w3, freq, skip, out_dtype):
    n = pc.shape[2]
    if (2 * n) % (DFT_RADIX * BF16_SUBLANES) == 0 and n >= 1024:
        return hyena_long(pc, (w1, b1, w2, b2, w3, freq), skip)
    filt, l1 = hyena_filter_time(n, w1, b1, w2, b2, w3, freq)
    return hyena_short(pc, filt, l1, skip, out_dtype)


def _deinterleave_cols(w, hd):
    lead = w.shape[:-1]
    wh = w.reshape(lead + (w.shape[-1] // hd, hd // 2, 2))
    return jnp.concatenate([wh[..., 0], wh[..., 1]], axis=-1).reshape(w.shape)


def _rope_tables(n_tok, rot_dim):
    rows = n_tok // GRID_W
    row = jnp.broadcast_to(jnp.arange(rows, dtype=jnp.int32)[:, None], (rows, GRID_W)).reshape(n_tok)
    col = jnp.broadcast_to(jnp.arange(GRID_W, dtype=jnp.int32)[None, :], (rows, GRID_W)).reshape(n_tok)
    axis_dim = rot_dim // 2
    inv_freq = ROPE_THETA ** (-jnp.arange(0, axis_dim, 2, dtype=F32) / axis_dim)
    ang = jnp.concatenate([row.astype(F32)[:, None] * inv_freq,
                           col.astype(F32)[:, None] * inv_freq], axis=-1)
    c, s = jnp.cos(ang), jnp.sin(ang)
    reps = LANES // rot_dim
    return jnp.tile(jnp.concatenate([c, c], -1), (1, reps)), jnp.tile(jnp.concatenate([-s, s], -1), (1, reps))


def _mixer_ab(h, hc, with_ctx, w_in, w_out, qk_g, hy):
    b, n, _ = h.shape
    conv_w, conv_b, w1, b1, w2, b2, w3, freq, skip = hy
    scale = A_HEAD_DIM ** -0.5 * LOG2E
    wq = _deinterleave_cols(w_in[:, :A_Q_W], A_HEAD_DIM).astype(BF16)
    wk = _deinterleave_cols(w_in[:, A_Q_W:A_Q_W + A_KV_W], A_HEAD_DIM).astype(BF16)
    wv = w_in[:, A_Q_W + A_KV_W:A_QKV_W].astype(BF16)
    wh = w_in[:, A_QKV_W:].astype(BF16)
    gq = jnp.tile(_deinterleave_cols(qk_g[0], A_HEAD_DIM) * scale, A_HEADS)
    gk = jnp.tile(_deinterleave_cols(qk_g[1], A_HEAD_DIM), A_KV_HEADS)
    ones_v = jnp.ones((A_KV_W,), F32)
    cosf, sinf = _rope_tables(n, A_HEAD_DIM)

    n_ctx = hc.shape[1]
    q = inproj(h, wq, gq, cosf, sinf, norm=True)
    k_all = inproj(hc, wk, gk, norm=True, append_at=n,
                   append_to=inproj(h, wk, gk, cosf, sinf, norm=True, tail_rows=n_ctx))
    v_all = inproj(hc, wv, ones_v, append_at=n, append_to=inproj(h, wv, ones_v, tail_rows=n_ctx))
    o_att = flash_attention(q, k_all, v_all, mode="gqa", kv_len=n + n_ctx)
    pc = inproj_conv(h, wh, conv_w, conv_b, HY_ORDER + 1)
    o_hy = hyena_mixer(pc, w1, b1, w2, b2, w3, freq, skip, BF16)
    wo = w_out.astype(BF16)
    lat = ([o_att, o_hy], [wo[:A_Q_W], wo[A_Q_W:]])
    if not with_ctx:
        return lat, None
    q_c = inproj(hc, wq, gq, norm=True)
    o_att_c = flash_attention(q_c, k_all, v_all, mode="gqa", kv_start=n, kv_len=n_ctx)
    pc_c = inproj_conv(hc, wh, conv_w, conv_b, HY_ORDER + 1)
    o_hy_c = hyena_mixer(pc_c, w1, b1, w2, b2, w3, freq, skip, BF16)
    return lat, ([o_att_c, o_hy_c], [wo[:A_Q_W], wo[A_Q_W:]])


def _mixer_c(h, hc, with_ctx, lambda_init, w_in, w_out, lam_vecs, subln_g):
    b, n, _ = h.shape
    scale = C_HEAD_DIM ** -0.5 * LOG2E
    wq = _deinterleave_cols(w_in[:, :C_Q_W], C_HEAD_DIM).astype(BF16)
    wk = _deinterleave_cols(w_in[:, C_Q_W:2 * C_Q_W], C_HEAD_DIM).astype(BF16)
    wv = w_in[:, 2 * C_Q_W:].astype(BF16)
    gq = jnp.full((C_Q_W,), scale, F32)
    ones = jnp.ones((C_Q_W,), F32)
    cosf, sinf = _rope_tables(n, C_HEAD_DIM)

    n_ctx = hc.shape[1]
    q = inproj(h, wq, gq, cosf, sinf, hd=C_HEAD_DIM)
    k_all = inproj(hc, wk, ones, append_at=n,
                   append_to=inproj(h, wk, ones, cosf, sinf, hd=C_HEAD_DIM, tail_rows=n_ctx))
    v_all = inproj(hc, wv, ones, append_at=n, append_to=inproj(h, wv, ones, tail_rows=n_ctx))
    attn = functools.partial(flash_attention, mode="diff", lam_vecs=lam_vecs, subln_g=subln_g,
                             lambda_init=lambda_init)
    wo = w_out.astype(BF16)
    lat = ([attn(q, k_all, v_all, kv_len=n + n_ctx)], [wo])
    if not with_ctx:
        return lat, None
    q_c = inproj(hc, wq, gq)
    return lat, ([attn(q_c, k_all, v_all, kv_start=n, kv_len=n_ctx)], [wo])


def kernel(x, c, ctx, c_ctx, ada_w, ada_b, norm_g, ab_w_in, ab_w_out, ab_qk_g,
           hy_conv_w, hy_conv_b, hy_w1, hy_b1, hy_w2, hy_b2, hy_w3, hy_freq, hy_skip,
           dc_w_in, dc_w_out, dc_lambda, dc_subln_g,
           ffn_w_up, ffn_conv_w, ffn_conv_b, ffn_w_down):
    depth = ada_w.shape[0]
    b, n, d = x.shape
    pad_rows = (-(b + 1)) % 8
    cvec = jnp.concatenate([c, c_ctx[None, :], jnp.zeros((pad_rows, d), F32)], axis=0)
    mod_all = ada_mod(cvec, ada_w, ada_b)
    w_up = ffn_w_up.astype(BF16)
    w_down = ffn_w_down.astype(BF16)
    for i in range(depth):
        j = i // 2
        with_ctx = i < depth - 1
        mod = mod_all[i, :b].reshape(b, 6, d)
        mod_c = jnp.broadcast_to(mod_all[i, b].reshape(1, 6, d), (b, 6, d))
        sh1, sc1, g1, sh2, sc2, g2 = (mod[:, t] for t in range(6))
        sh1c, sc1c, g1c, sh2c, sc2c, g2c = (mod_c[:, t] for t in range(6))
        h = norm_mod(x, norm_g[i, 0], sc1, sh1)
        hc = norm_mod(ctx, norm_g[i, 0], sc1c, sh1c)
        if i % 2 == 0:
            hy = (hy_conv_w[j], hy_conv_b[j], hy_w1[j], hy_b1[j], hy_w2[j], hy_b2[j],
                  hy_w3[j], hy_freq[j], hy_skip[j])
            lat, cx = _mixer_ab(h, hc, with_ctx, ab_w_in[j], ab_w_out[j], ab_qk_g[j], hy)
        else:
            lambda_init = 0.8 - 0.6 * math.exp(-0.3 * i)
            lat, cx = _mixer_c(h, hc, with_ctx, lambda_init, dc_w_in[j], dc_w_out[j],
                               dc_lambda[j], dc_subln_g[j])
        x = outproj_residual(lat[0], lat[1], x, norm_g[i, 1], g1)
        x = conv_ffn_residual(x, norm_g[i, 2], sc2, sh2, w_up, ffn_conv_w[i], ffn_conv_b[i],
                              w_down, norm_g[i, 3], g2, i)
        if with_ctx:
            ctx = outproj_residual(cx[0], cx[1], ctx, norm_g[i, 1], g1c)
            ctx = conv_ffn_residual(ctx, norm_g[i, 2], sc2c, sh2c, w_up, ffn_conv_w[i],
                                    ffn_conv_b[i], w_down, norm_g[i, 3], g2c, i)
    return x
```

```python
import functools
import math

import jax
import jax.numpy as jnp
from jax import lax
from jax.experimental import pallas as pl
from jax.experimental.pallas import tpu as pltpu

F32 = jnp.float32
BF16 = jnp.bfloat16
HIGHEST = lax.Precision.HIGHEST

D_MODEL = 2048
GRID_W = 64
EPS = 1e-6
ROPE_THETA = 10000.0
A_HEADS, A_KV_HEADS, A_HEAD_DIM = 8, 2, 128
A_Q_W = A_HEADS * A_HEAD_DIM
A_KV_W = A_KV_HEADS * A_HEAD_DIM
A_QKV_W = A_Q_W + 2 * A_KV_W
HY_CH = D_MODEL // 2
HY_ORDER = 2
HY_FILTER_W = 64
HY_BANDS = 16
HY_FAST, HY_SLOW, HY_TARGET = 0.3, 1.5, 1e-2
C_HEADS, C_HEAD_DIM = 16, 64
C_Q_W = C_HEADS * 2 * C_HEAD_DIM
D_FF = 5632

LANES = 128
BF16_SUBLANES = 16
VMEM_LIMIT = 56 * 1024 * 1024
DFT_RADIX = 128
FLASH_ROWS = 2048
FLASH_SUB_ROWS = 128
FLASH_MAX_KV = 16384
LOG2E = math.log2(math.e)


def _cparams(*sem):
    return pltpu.CompilerParams(dimension_semantics=sem, vmem_limit_bytes=VMEM_LIMIT)


def _rms(x, g):
    return x * lax.rsqrt(jnp.mean(x * x, axis=-1, keepdims=True) + EPS) * g


def _ada_kernel(c_ref, w_ref, b_ref, o_ref):
    c = c_ref[...]
    s = c * jax.nn.sigmoid(c)
    rows = s.shape[0]
    s_hi = s.astype(BF16).astype(F32)
    s2 = jnp.concatenate([s_hi, s - s_hi], axis=0).astype(BF16)
    w = w_ref[0]
    w_hi = w.astype(BF16)
    w_lo = (w - w_hi.astype(F32)).astype(BF16)
    both = jnp.dot(s2, w_hi, preferred_element_type=F32)
    hi_lo = jnp.dot(s2, w_lo, preferred_element_type=F32)[:rows]
    o_ref[0] = both[:rows] + both[rows:] + hi_lo + b_ref[0]


def ada_mod(cvec, ada_w, ada_b):
    depth, d, n6 = ada_w.shape
    rows = cvec.shape[0]
    tn = 1024
    return pl.pallas_call(
        _ada_kernel,
        grid=(depth, n6 // tn),
        in_specs=[pl.BlockSpec((rows, d), lambda l, j: (0, 0)),
                  pl.BlockSpec((1, d, tn), lambda l, j: (l, 0, j)),
                  pl.BlockSpec((1, 1, tn), lambda l, j: (l, 0, j))],
        out_specs=pl.BlockSpec((1, rows, tn), lambda l, j: (l, 0, j)),
        out_shape=jax.ShapeDtypeStruct((depth, rows, n6), F32),
        compiler_params=_cparams("arbitrary", "arbitrary"),
        name="ada_mod",
    )(cvec, ada_w, ada_b.reshape(depth, 1, n6))


def _norm_mod_kernel(x_ref, g_ref, sc_ref, sh_ref, o_ref):
    y = _rms(x_ref[0], g_ref[...])
    o_ref[0] = (y * (1.0 + sc_ref[0]) + sh_ref[0]).astype(o_ref.dtype)


def norm_mod(x, g, sc, sh):
    b, n, d = x.shape
    tm = min(n, 512)
    return pl.pallas_call(
        _norm_mod_kernel,
        grid=(b, n // tm),
        in_specs=[pl.BlockSpec((1, tm, d), lambda bi, i: (bi, i, 0)),
                  pl.BlockSpec((1, d), lambda bi, i: (0, 0)),
                  pl.BlockSpec((1, 1, d), lambda bi, i: (bi, 0, 0)),
                  pl.BlockSpec((1, 1, d), lambda bi, i: (bi, 0, 0))],
        out_specs=pl.BlockSpec((1, tm, d), lambda bi, i: (bi, i, 0)),
        out_shape=jax.ShapeDtypeStruct((b, n, d), BF16),
        compiler_params=_cparams("arbitrary", "arbitrary"),
        name="norm_mod",
    )(x, g.reshape(1, d), sc.reshape(b, 1, d), sh.reshape(b, 1, d))


def _inproj_kernel(*refs, norm, rope, appending, n_tiles, has_tail):
    o_ref = refs[-1]
    if appending:
        refs = refs[:-2]
    else:
        refs = refs[:-1]
    if rope:
        h_ref, w_ref, g_ref, cos_ref, sin_ref = refs
    else:
        h_ref, w_ref, g_ref = refs
    i = pl.program_id(1)

    def _tail():
        o_ref[0] = jnp.zeros(o_ref.shape[1:], o_ref.dtype)

    def _body():
        acc = jnp.dot(h_ref[0], w_ref[...], preferred_element_type=F32)
        tn = acc.shape[1]
        if rope:
            cosf = cos_ref[...]
            sinf = sin_ref[...]
        for c in range(tn // LANES):
            x = acc[:, c * LANES:(c + 1) * LANES]
            if norm:
                x = x * lax.rsqrt(jnp.mean(x * x, axis=-1, keepdims=True) + EPS)
            x = x * g_ref[:, c * LANES:(c + 1) * LANES]
            if rope:
                x = x * cosf + pltpu.roll(x, LANES // 2, 1) * sinf
            o_ref[0, :, c * LANES:(c + 1) * LANES] = x.astype(o_ref.dtype)

    if has_tail:
        pl.when(i >= n_tiles)(_tail)
        pl.when(i < n_tiles)(_body)
    else:
        _body()


def inproj(h, w, gvec, cosf=None, sinf=None, *, norm=False, out_dtype=BF16,
           tail_rows=0, append_to=None, append_at=0):
    b, n, d = h.shape
    ncol = w.shape[1]
    rope = cosf is not None
    tm = min(n, 512)
    tn = min(ncol, 2048)
    n_tiles = n // tm
    last = n_tiles - 1
    in_specs = [pl.BlockSpec((1, tm, d), lambda bi, i, j: (bi, jnp.minimum(i, last), 0)),
                pl.BlockSpec((d, tn), lambda bi, i, j: (0, j)),
                pl.BlockSpec((1, tn), lambda bi, i, j: (0, j))]
    args = [h, w, gvec.reshape(1, ncol)]
    if rope:
        in_specs += [pl.BlockSpec((tm, LANES), lambda bi, i, j: (jnp.minimum(i, last), 0)),
                     pl.BlockSpec((tm, LANES), lambda bi, i, j: (jnp.minimum(i, last), 0))]
        args += [cosf, sinf]
    appending = append_to is not None
    if appending:
        total = append_to.shape[1]
        first = append_at // tm
        assert first * tm == append_at and append_at + n <= total and append_to.dtype == out_dtype
        tail_tiles = 0
        in_specs.append(pl.BlockSpec(memory_space=pl.ANY))
        args.append(append_to)
        aliases = {len(args) - 1: 0}
    else:
        tail_tiles = -(-tail_rows // tm)
        total, first, aliases = (n_tiles + tail_tiles) * tm, 0, {}
    return pl.pallas_call(
        functools.partial(_inproj_kernel, norm=norm, rope=rope, appending=appending,
                          n_tiles=n_tiles, has_tail=tail_tiles > 0),
        grid=(b, n_tiles + tail_tiles, ncol // tn),
        in_specs=in_specs,
        out_specs=pl.BlockSpec((1, tm, tn), lambda bi, i, j: (bi, first + i, j)),
        out_shape=jax.ShapeDtypeStruct((b, total, ncol), out_dtype),
        input_output_aliases=aliases,
        compiler_params=_cparams("arbitrary", "arbitrary", "arbitrary"),
        name="inproj",
    )(*args)


def _inproj_conv_kernel(hm_ref, hp_ref, hn_ref, w_ref, cw_ref, cb_ref, o_ref, *, n_tiles):
    i = pl.program_id(2)
    hp, hm, hn = hp_ref[0], hm_ref[0], hn_ref[0]
    halo, tm = hp.shape[0], hm.shape[0]
    zero = jnp.zeros_like(hp)
    lhs = jnp.concatenate([jnp.where(i > 0, hp, zero), hm,
                           jnp.where(i < n_tiles - 1, hn, zero)], axis=0)
    u = jnp.dot(lhs, w_ref[...], preferred_element_type=F32)
    rows = tm + 2 * halo
    cw = cw_ref[...]
    conv = cw[0:1] * pltpu.roll(u, 1, 0) + cw[1:2] * u + cw[2:3] * pltpu.roll(u, rows - 1, 0)
    o_ref[0, 0] = conv[halo:halo + tm] + cb_ref[...]


def inproj_conv(h, w, conv_w, conv_b, groups):
    b, n, d = h.shape
    ncol = w.shape[1]
    c = ncol // groups
    tm = min(n, 512)
    tn = min(c, 1024)
    halo = BF16_SUBLANES
    nt = n // tm
    per = c // tn
    return pl.pallas_call(
        functools.partial(_inproj_conv_kernel, n_tiles=nt),
        grid=(ncol // tn, b, nt),
        in_specs=[pl.BlockSpec((1, tm, d), lambda j, bi, i: (bi, i, 0)),
                  pl.BlockSpec((1, halo, d),
                               lambda j, bi, i: (bi, jnp.maximum(i * (tm // halo) - 1, 0), 0)),
                  pl.BlockSpec((1, halo, d),
                               lambda j, bi, i: (bi, jnp.minimum((i + 1) * (tm // halo), n // halo - 1), 0)),
                  pl.BlockSpec((d, tn), lambda j, bi, i: (0, j)),
                  pl.BlockSpec((3, tn), lambda j, bi, i: (0, j)),
                  pl.BlockSpec((1, tn), lambda j, bi, i: (0, j))],
        out_specs=pl.BlockSpec((1, 1, tm, tn), lambda j, bi, i: (j // per, bi, i, j % per)),
        out_shape=jax.ShapeDtypeStruct((groups, b, n, c), F32),
        compiler_params=_cparams("arbitrary", "arbitrary", "arbitrary"),
        name="inproj_conv",
    )(h, h, h, w, conv_w, conv_b.reshape(1, ncol))


def _flash_kernel(*refs, mode, group, tq, lambda_init):
    if mode == "diff":
        q_ref, k_ref, v_ref, lam_ref, sg_ref, o_ref, qs_ref = refs
    else:
        q_ref, k_ref, v_ref, o_ref, qs_ref = refs
    q = q_ref[0]
    if mode == "diff":
        lane = lax.broadcasted_iota(jnp.int32, q.shape, 1)
        zero = jnp.zeros_like(q)
        first = (lane // (C_HEAD_DIM // 2)) % 2 == 0
        qs_ref[0:tq] = jnp.where(first, q, zero)
        qs_ref[tq:2 * tq] = jnp.where(first, zero, q)
    else:
        for g in range(group):
            qs_ref[g * tq:(g + 1) * tq] = q[:, g * LANES:(g + 1) * LANES]

    k = k_ref[0]
    v = v_ref[0]
    v_aug = jnp.concatenate([v, jnp.ones_like(v)], axis=1)
    sub = FLASH_SUB_ROWS

    def attend(c):
        s = lax.dot_general(qs_ref[c * sub:(c + 1) * sub], k, (((1,), (1,)), ((), ())),
                            preferred_element_type=F32)
        p = jnp.exp2(s - jnp.max(s, axis=-1, keepdims=True))
        pv = jnp.dot(p.astype(BF16), v_aug, preferred_element_type=F32)
        return pv[:, :LANES] / pv[:, LANES:]

    per_head = tq // sub
    if mode == "diff":
        lv = lam_ref[...]
        lam = (jnp.exp(jnp.sum(lv[0:1] * lv[1:2], axis=-1, keepdims=True))
               - jnp.exp(jnp.sum(lv[2:3] * lv[3:4], axis=-1, keepdims=True)) + lambda_init)
        for c in range(per_head):
            dlt = attend(c) - lam * attend(per_head + c)
            o_ref[0, c * sub:(c + 1) * sub, :] = (
                _rms(dlt, sg_ref[...]) * (1.0 - lambda_init)).astype(o_ref.dtype)
    else:
        for c in range(group * per_head):
            g, r = divmod(c, per_head)
            o_ref[0, r * sub:(r + 1) * sub, g * LANES:(g + 1) * LANES] = attend(c).astype(o_ref.dtype)


def flash_attention(q, k, v, *, mode, kv_len, lam_vecs=None, subln_g=None, lambda_init=0.0,
                    kv_start=0):
    b, n, qw = q.shape
    tk = kv_len
    assert tk <= FLASH_MAX_KV and tk % LANES == 0 and kv_start % tk == 0
    if mode == "diff":
        group, n_groups, qblk = 2, qw // LANES, LANES
    else:
        group = A_HEADS // A_KV_HEADS
        n_groups, qblk = A_KV_HEADS, group * LANES
    tq = min(n, FLASH_ROWS // group)
    assert tq % FLASH_SUB_ROWS == 0
    j0 = kv_start // tk
    in_specs = [pl.BlockSpec((1, tq, qblk), lambda bi, g, i: (bi, i, g)),
                pl.BlockSpec((1, tk, LANES), lambda bi, g, i: (bi, j0, g)),
                pl.BlockSpec((1, tk, LANES), lambda bi, g, i: (bi, j0, g))]
    args = [q, k, v]
    if mode == "diff":
        in_specs += [pl.BlockSpec(lam_vecs.shape, lambda bi, g, i: (0, 0)),
                     pl.BlockSpec((1, LANES), lambda bi, g, i: (0, 0))]
        args += [lam_vecs, subln_g.reshape(1, LANES)]
    return pl.pallas_call(
        functools.partial(_flash_kernel, mode=mode, group=group, tq=tq, lambda_init=lambda_init),
        grid=(b, n_groups, n // tq),
        in_specs=in_specs,
        out_specs=pl.BlockSpec((1, tq, qblk), lambda bi, g, i: (bi, i, g)),
        out_shape=jax.ShapeDtypeStruct((b, n, qw), BF16),
        scratch_shapes=[pltpu.VMEM((group * tq, LANES), BF16)],
        compiler_params=_cparams("arbitrary", "arbitrary", "arbitrary"),
        name="flash_" + mode,
    )(*args)


def _outproj_kernel(*refs, n_lhs):
    lhs = refs[:n_lhs]
    ws = refs[n_lhs:2 * n_lhs]
    x_ref, g_ref, gate_ref, o_ref = refs[2 * n_lhs:]
    y = jnp.dot(lhs[0][0].astype(BF16), ws[0][...], preferred_element_type=F32)
    for a, w in zip(lhs[1:], ws[1:]):
        y = y + jnp.dot(a[0].astype(BF16), w[...], preferred_element_type=F32)
    o_ref[0] = x_ref[0] + gate_ref[0] * _rms(y, g_ref[...])


def outproj_residual(lhs_list, w_list, x, g, gate):
    b, n, d = x.shape
    tm = min(n, 512)
    n_lhs = len(lhs_list)
    in_specs = [pl.BlockSpec((1, tm, a.shape[2]), lambda bi, i: (bi, i, 0)) for a in lhs_list]
    in_specs += [pl.BlockSpec(w.shape, lambda bi, i: (0, 0)) for w in w_list]
    in_specs += [pl.BlockSpec((1, tm, d), lambda bi, i: (bi, i, 0)),
                 pl.BlockSpec((1, d), lambda bi, i: (0, 0)),
                 pl.BlockSpec((1, 1, d), lambda bi, i: (bi, 0, 0))]
    return pl.pallas_call(
        functools.partial(_outproj_kernel, n_lhs=n_lhs),
        grid=(b, n // tm),
        in_specs=in_specs,
        out_specs=pl.BlockSpec((1, tm, d), lambda bi, i: (bi, i, 0)),
        out_shape=jax.ShapeDtypeStruct((b, n, d), F32),
        compiler_params=_cparams("arbitrary", "arbitrary"),
        name="outproj",
    )(*lhs_list, *w_list, x, g.reshape(1, d), gate.reshape(b, 1, d))


def _ffn_kernel(xm_ref, xp_ref, xn_ref, gn_ref, sc_ref, sh_ref, wa_ref, wg_ref, cwa_ref, cwg_ref,
                cba_ref, cbg_ref, wd_ref, g3_ref, gate_ref, o_ref, hx_ref,
                *, tm, halo, n_tiles, n_f):
    i = pl.program_id(1)
    j = pl.program_id(2)

    @pl.when(j == 0)
    def _prologue():
        def nm(x):
            return (_rms(x, gn_ref[...]) * (1.0 + sc_ref[0]) + sh_ref[0]).astype(BF16)
        hx_ref[halo:halo + tm] = nm(xm_ref[0])
        zero = jnp.zeros((halo, xm_ref.shape[2]), BF16)
        hx_ref[0:halo] = jnp.where(i > 0, nm(xp_ref[0]), zero)
        hx_ref[halo + tm:] = jnp.where(i < n_tiles - 1, nm(xn_ref[0]), zero)
        o_ref[0] = jnp.zeros(o_ref.shape[1:], F32)

    hx = hx_ref[...]
    rows = tm + 2 * halo

    def conv(u, cw, cb):
        up = pltpu.roll(u, 1, 0)
        dn = pltpu.roll(u, rows - 1, 0)
        return (cw[0:1] * up + cw[1:2] * u + cw[2:3] * dn + cb)[halo:halo + tm]

    a = conv(jnp.dot(hx, wa_ref[...], preferred_element_type=F32), cwa_ref[...], cba_ref[...])
    g = conv(jnp.dot(hx, wg_ref[...], preferred_element_type=F32), cwg_ref[...], cbg_ref[...])
    act = (a * (g * jax.nn.sigmoid(g))).astype(BF16)
    o_ref[0] += jnp.dot(act, wd_ref[...], preferred_element_type=F32)

    @pl.when(j == n_f - 1)
    def _epilogue():
        o_ref[0] = xm_ref[0] + gate_ref[0] * _rms(o_ref[0], g3_ref[...])


def conv_ffn_residual(x, gn, sc, sh, w_up, conv_w, conv_b, w_down, g3, gate, layer):
    b, n, d = x.shape
    dff = w_down.shape[1]
    tm = min(n, 512)
    tf = 512
    halo = BF16_SUBLANES
    nt = n // tm
    n_f = dff // tf
    row = lambda bi, i, j: (bi, i, 0)
    vec = lambda bi, i, j: (0, 0)
    bvec = lambda bi, i, j: (bi, 0, 0)
    up_a = lambda bi, i, j: (0, j)
    up_g = lambda bi, i, j: (0, n_f + j)
    wup_a = lambda bi, i, j: (layer, 0, j)
    wup_g = lambda bi, i, j: (layer, 0, n_f + j)
    conv_b = conv_b.reshape(1, 2 * dff)
    return pl.pallas_call(
        functools.partial(_ffn_kernel, tm=tm, halo=halo, n_tiles=nt, n_f=n_f),
        grid=(b, nt, n_f),
        in_specs=[pl.BlockSpec((1, tm, d), row),
                  pl.BlockSpec((1, halo, d),
                               lambda bi, i, j: (bi, jnp.maximum(i * (tm // halo) - 1, 0), 0)),
                  pl.BlockSpec((1, halo, d),
                               lambda bi, i, j: (bi, jnp.minimum((i + 1) * (tm // halo), n // halo - 1), 0)),
                  pl.BlockSpec((1, d), vec),
                  pl.BlockSpec((1, 1, d), bvec),
                  pl.BlockSpec((1, 1, d), bvec),
                  pl.BlockSpec((None, d, tf), wup_a),
                  pl.BlockSpec((None, d, tf), wup_g),
                  pl.BlockSpec((3, tf), up_a),
                  pl.BlockSpec((3, tf), up_g),
                  pl.BlockSpec((1, tf), up_a),
                  pl.BlockSpec((1, tf), up_g),
                  pl.BlockSpec((None, tf, d), lambda bi, i, j: (layer, j, 0)),
                  pl.BlockSpec((1, d), vec),
                  pl.BlockSpec((1, 1, d), bvec)],
        out_specs=pl.BlockSpec((1, tm, d), row),
        out_shape=jax.ShapeDtypeStruct((b, n, d), F32),
        scratch_shapes=[pltpu.VMEM((tm + 2 * halo, d), BF16)],
        compiler_params=_cparams("arbitrary", "arbitrary", "arbitrary"),
        name="conv_ffn",
    )(x, x, x, gn.reshape(1, d), sc.reshape(b, 1, d), sh.reshape(b, 1, d), w_up, w_up,
      conv_w, conv_w, conv_b, conv_b, w_down, g3.reshape(1, d), gate.reshape(b, 1, d))


def _filter_rows(r, bands_ref, w1_ref, b1_ref, w2_ref, b2_ref, fr_ref, dl_ref, n_tok):
    t = jnp.where(r < n_tok, r, 2.0 * n_tok - r)
    t_norm = t / float(max(n_tok - 1, 1))
    wang = (2.0 * math.pi / n_tok) * t
    z = wang * bands_ref[...]
    lane = lax.broadcasted_iota(jnp.int32, z.shape, 1)
    feats = jnp.where(lane == 0, t_norm,
                      jnp.where(lane <= HY_BANDS, jnp.cos(z),
                                jnp.where(lane <= 2 * HY_BANDS, -jnp.sin(z), 0.0)))
    fr = fr_ref[...]
    h = jnp.sin(fr[0:1] * (jnp.dot(feats, w1_ref[...], precision=HIGHEST,
                                   preferred_element_type=F32) + b1_ref[...]))
    h = jnp.sin(fr[1:2] * (jnp.dot(h, w2_ref[...], precision=HIGHEST,
                                   preferred_element_type=F32) + b2_ref[...]))
    decay = jnp.where(r == n_tok, 0.0, jnp.exp(-t_norm * dl_ref[...]))
    return h, decay


def _filter_kernel(bands_ref, w1_ref, b1_ref, w2_ref, b2_ref, w3_ref, fr_ref, dl_ref,
                   o_ref, s_ref, *, n_tok, tr):
    i = pl.program_id(0)

    @pl.when(i == 0)
    def _init():
        s_ref[...] = jnp.zeros(s_ref.shape, F32)

    r = (i * tr + lax.broadcasted_iota(jnp.int32, (tr, 1), 0)).astype(F32)
    h, decay = _filter_rows(r, bands_ref, w1_ref, b1_ref, w2_ref, b2_ref, fr_ref, dl_ref, n_tok)
    hb = h.astype(BF16)
    for o in range(HY_ORDER):
        out = jnp.dot(hb, w3_ref[o].astype(BF16), preferred_element_type=F32) * decay
        o_ref[o] = out
        s_ref[o] += jnp.sum(jnp.abs(out), axis=0, keepdims=True)


def _filter_stage_a_kernel(bands_ref, w1_ref, b1_ref, w2_ref, b2_ref, w3_ref, fr_ref, dl_ref,
                           f_ref, *out_refs, n_tok, sub):
    o_refs, s_ref = out_refs[:HY_ORDER], out_refs[HY_ORDER]
    j = pl.program_id(0)

    @pl.when(j == 0)
    def _init():
        s_ref[...] = jnp.zeros(s_ref.shape, F32)

    k1n = o_refs[0].shape[1]
    half = k1n // 2
    c = dl_ref.shape[1]
    n1 = lax.broadcasted_iota(jnp.int32, (k1n, 1), 0)
    f = f_ref[...]
    for q in range(sub):
        r = (n1 * DFT_RADIX + (j * sub + q)).astype(F32)
        h, decay = _filter_rows(r, bands_ref, w1_ref, b1_ref, w2_ref, b2_ref, fr_ref, dl_ref, n_tok)
        h_fwd, h_bwd = h[:half].astype(BF16), h[half:].astype(BF16)
        for o in range(HY_ORDER):
            w3 = w3_ref[o].astype(BF16)
            h3 = jnp.concatenate([jnp.dot(h_fwd, w3[:, :c], preferred_element_type=F32),
                                  jnp.dot(h_bwd, w3[:, c:], preferred_element_type=F32)],
                                 axis=0) * decay
            s_ref[o] += jnp.sum(jnp.abs(h3), axis=0, keepdims=True)
            y = _dft_dot(f, h3)
            o_refs[o][0, :, q, :] = y[:k1n]
            o_refs[o][1, :, q, :] = y[k1n:]


def _filter_operands(w1, b1, w2, b2, w3, freq):
    ch = HY_CH
    lane = jnp.arange(LANES)
    bands = jnp.linspace(1e-4, HY_BANDS - 1, HY_BANDS, dtype=F32)
    bands_l = jnp.where((lane >= 1) & (lane <= HY_BANDS), bands[jnp.clip(lane - 1, 0, HY_BANDS - 1)],
                        jnp.where((lane > HY_BANDS) & (lane <= 2 * HY_BANDS),
                                  bands[jnp.clip(lane - 1 - HY_BANDS, 0, HY_BANDS - 1)], 0.0))
    w1p = jnp.zeros((LANES, HY_FILTER_W), F32).at[:w1.shape[0]].set(w1)
    deltas = jnp.abs(jnp.linspace(math.log(HY_TARGET) / HY_FAST, math.log(HY_TARGET) / HY_SLOW,
                                  ch, dtype=F32))
    w3o = jnp.swapaxes(w3.reshape(HY_FILTER_W, HY_ORDER, 2 * ch), 0, 1)
    return (bands_l.reshape(1, LANES), w1p, b1.reshape(1, -1), w2, b2.reshape(1, -1), w3o, freq,
            deltas.reshape(1, ch))


def _filter_in_specs(w3_spec):
    const = lambda i: (0, 0)
    return [pl.BlockSpec((1, LANES), const),
            pl.BlockSpec((LANES, HY_FILTER_W), const),
            pl.BlockSpec((1, HY_FILTER_W), const),
            pl.BlockSpec((HY_FILTER_W, HY_FILTER_W), const),
            pl.BlockSpec((1, HY_FILTER_W), const),
            w3_spec,
            pl.BlockSpec((2, HY_FILTER_W), const),
            pl.BlockSpec((1, HY_CH), const)]


def hyena_filter_time(n_tok, w1, b1, w2, b2, w3, freq):
    ch = HY_CH
    ll = 2 * n_tok
    tr = min(n_tok, 512)
    per_half = n_tok // tr
    return pl.pallas_call(
        functools.partial(_filter_kernel, n_tok=n_tok, tr=tr),
        grid=(ll // tr,),
        in_specs=_filter_in_specs(
            pl.BlockSpec((HY_ORDER, HY_FILTER_W, ch), lambda i: (0, 0, i // per_half))),
        out_specs=[pl.BlockSpec((HY_ORDER, tr, ch), lambda i: (0, i, 0)),
                   pl.BlockSpec((HY_ORDER, 1, ch), lambda i: (0, 0, 0))],
        out_shape=[jax.ShapeDtypeStruct((HY_ORDER, ll, ch), F32),
                   jax.ShapeDtypeStruct((HY_ORDER, 1, ch), F32)],
        compiler_params=_cparams("arbitrary"),
        name="hyena_filter",
    )(*_filter_operands(w1, b1, w2, b2, w3, freq))


def hyena_filter_stage_a(n_tok, f_filt, w1, b1, w2, b2, w3, freq):
    ch = HY_CH
    r = DFT_RADIX
    k1n = 2 * n_tok // r
    sub = 8
    return pl.pallas_call(
        functools.partial(_filter_stage_a_kernel, n_tok=n_tok, sub=sub),
        grid=(r // sub,),
        in_specs=_filter_in_specs(
            pl.BlockSpec((HY_ORDER, HY_FILTER_W, 2 * ch), lambda j: (0, 0, 0)))
        + [pl.BlockSpec(f_filt.shape, lambda j: (0, 0))],
        out_specs=[pl.BlockSpec((2, k1n, sub, ch), lambda j: (0, 0, j, 0))] * HY_ORDER
        + [pl.BlockSpec((HY_ORDER, 1, ch), lambda j: (0, 0, 0))],
        out_shape=[jax.ShapeDtypeStruct((2, k1n, r, ch), F32)] * HY_ORDER
        + [jax.ShapeDtypeStruct((HY_ORDER, 1, ch), F32)],
        compiler_params=_cparams("arbitrary"),
        name="hyena_filter_a",
    )(*_filter_operands(w1, b1, w2, b2, w3, freq), f_filt)


def _dft_dot(f, x):
    return jnp.dot(f.astype(BF16), x.astype(BF16), preferred_element_type=F32)


def _stage_a_kernel(f_ref, x_ref, o_ref):
    parts = x_ref.shape[0]
    f = f_ref[...]
    for r in range(x_ref.shape[2]):
        z = jnp.concatenate([x_ref[p, :, r, :] for p in range(parts)], axis=0)
        y = _dft_dot(f, z)
        half = y.shape[0] // 2
        o_ref[0, :, r, :] = y[:half]
        o_ref[1, :, r, :] = y[half:]


def dft_stage_a(f, x, g):
    _, p, rows, r, c = x.shape
    k1 = p * rows
    sub = 8
    return pl.pallas_call(
        _stage_a_kernel,
        grid=(r // sub,),
        in_specs=[pl.BlockSpec(f.shape, lambda j: (0, 0)),
                  pl.BlockSpec((None, p, rows, sub, c), lambda j: (g, 0, 0, j, 0))],
        out_specs=pl.BlockSpec((2, k1, sub, c), lambda j: (0, 0, j, 0)),
        out_shape=jax.ShapeDtypeStruct((2, k1, r, c), F32),
        compiler_params=_cparams("arbitrary"),
        name="dft_stage_a",
    )(f, x)


def _stage_c_conv_kernel(a_ref, af_ref, sc_ref, g_ref, gi_ref, o_ref, *, kc):
    tc = a_ref.shape[3]
    for k in range(kc):
        r = jnp.concatenate([jnp.concatenate([a_ref[0, k], af_ref[0, k]], axis=1),
                             jnp.concatenate([a_ref[1, k], af_ref[1, k]], axis=1)], axis=0)
        y = _dft_dot(g_ref[k], r)
        half = y.shape[0] // 2
        yre, yim = y[:half, :tc], y[half:, :tc]
        hre, him = y[:half, tc:] * sc_ref[...], y[half:, tc:] * sc_ref[...]
        p = jnp.concatenate([yre * hre - yim * him, yre * him + yim * hre], axis=0)
        d = _dft_dot(gi_ref[k], p)
        o_ref[0, k] = d[:half]
        o_ref[1, k] = d[half:]


def dft_stage_c_conv(a, af, scale, g, gi):
    _, k1, r, c = a.shape
    kc = 8
    tc = 256
    blk = pl.BlockSpec((2, kc, r, tc), lambda i, j: (0, i, 0, j))
    mat = pl.BlockSpec((kc, 2 * r, 2 * r), lambda i, j: (i, 0, 0))
    return pl.pallas_call(
        functools.partial(_stage_c_conv_kernel, kc=kc),
        grid=(k1 // kc, c // tc),
        in_specs=[blk, blk, pl.BlockSpec((1, tc), lambda i, j: (0, j)), mat, mat],
        out_specs=blk,
        out_shape=jax.ShapeDtypeStruct(a.shape, F32),
        compiler_params=_cparams("arbitrary", "arbitrary"),
        name="dft_stage_c_conv",
    )(a, af, scale, g, gi)


def _stage_a_inv_kernel(f_ref, d_ref, u_ref, gt_ref, sk_ref, o_ref):
    f = f_ref[...]
    sk = sk_ref[...]
    for r in range(d_ref.shape[2]):
        d = jnp.concatenate([d_ref[0, :, r, :], d_ref[1, :, r, :]], axis=0)
        y = _dft_dot(f, d)
        half = y.shape[0] // 2
        for bi in range(2):
            yb = y[bi * half:(bi + 1) * half]
            o_ref[bi, :, r, :] = gt_ref[bi, :, r, :] * (yb + u_ref[bi, :, r, :] * sk)


def dft_stage_a_inv(f, d, u, gu, gate, gg, skip):
    _, k1, r, c = d.shape
    half = k1 // 2
    sub = 8
    io = pl.BlockSpec((2, half, sub, c), lambda j: (0, 0, j, 0))
    return pl.pallas_call(
        _stage_a_inv_kernel,
        grid=(r // sub,),
        in_specs=[pl.BlockSpec(f.shape, lambda j: (0, 0)),
                  pl.BlockSpec((2, k1, sub, c), lambda j: (0, 0, j, 0)),
                  pl.BlockSpec((None, 2, half, sub, c), lambda j: (gu, 0, 0, j, 0)),
                  pl.BlockSpec((None, 2, half, sub, c), lambda j: (gg, 0, 0, j, 0)),
                  pl.BlockSpec((1, c), lambda j: (0, 0))],
        out_specs=io,
        out_shape=jax.ShapeDtypeStruct((2, half, r, c), F32),
        compiler_params=_cparams("arbitrary"),
        name="dft_stage_a_inv",
    )(f, d, u, gate, skip)


def _dft_tables(n_tok):
    ll = 2 * n_tok
    r = DFT_RADIX
    k1n = ll // r
    half = k1n // 2
    two_pi = 2.0 * math.pi

    def cs(m, period):
        ang = (m % period).astype(F32) * (two_pi / period)
        return jnp.cos(ang), jnp.sin(ang)

    k1 = jnp.arange(k1n, dtype=jnp.int32)
    c, s = cs(k1[:, None] * k1[None, :half], k1n)
    fa_data = jnp.concatenate([jnp.concatenate([c, s], 1), jnp.concatenate([-s, c], 1)], 0)
    c, s = cs(k1[:, None] * k1[None, :], k1n)
    fa_filt = jnp.concatenate([c, -s], 0)
    c, s = cs(k1[:half, None] * k1[None, :], k1n)
    fa_inv = jnp.concatenate([jnp.concatenate([c, -s], 1), jnp.concatenate([s, c], 1)], 0)
    idx = jnp.arange(r, dtype=jnp.int32)
    ca, sa = cs(idx[:, None] * idx[None, :], r)
    cb, sb = cs(k1[:, None] * idx[None, :], ll)
    c = ca[None] * cb[:, None, :] - sa[None] * sb[:, None, :]
    s = sa[None] * cb[:, None, :] + ca[None] * sb[:, None, :]
    g_fwd = jnp.concatenate([jnp.concatenate([c, s], 2), jnp.concatenate([-s, c], 2)], 1)
    ct, st = jnp.swapaxes(c, 1, 2), jnp.swapaxes(s, 1, 2)
    g_inv = jnp.concatenate([jnp.concatenate([ct, -st], 2), jnp.concatenate([st, ct], 2)], 1)
    return tuple(t.astype(BF16) for t in (fa_data, fa_filt, fa_inv, g_fwd, g_inv))


def hyena_long(pc, filter_params, skip):
    _, b, n, c = pc.shape
    assert b == 2, "the two batch entries ride as real/imaginary parts"
    ll = 2 * n
    r = DFT_RADIX
    half = ll // r // 2
    fa_data, fa_filt, fa_inv, g_fwd, g_inv = _dft_tables(n)
    *afs, l1 = hyena_filter_stage_a(n, fa_filt, *filter_params)
    pcs = pc.reshape(HY_ORDER + 1, 2, half, r, c)
    u, gu = pcs, 0
    for o in range(HY_ORDER):
        a = dft_stage_a(fa_data, u, gu)
        d = dft_stage_c_conv(a, afs[o], 1.0 / (l1[o] * ll), g_fwd, g_inv)
        z = dft_stage_a_inv(fa_inv, d, u, gu, pcs, o + 1, skip[o].reshape(1, c))
        u, gu = z[None], 0
    return z.reshape(b, n, c)


def _hyena_short_kernel(ff_ref, fr_ref, fi_ref, pc_ref, filt_ref, l1_ref, skip_ref, o_ref, *, n, ll):
    u = [pc_ref[0, 0], pc_ref[0, 1]]
    for o in range(HY_ORDER):
        hs = _dft_dot(fr_ref[...], filt_ref[o]) * (1.0 / (l1_ref[o] * ll))
        y = _dft_dot(ff_ref[...], jnp.concatenate(u, axis=0))
        yre, yim, hre, him = y[:ll], y[ll:], hs[:ll], hs[ll:]
        p = jnp.concatenate([yre * hre - yim * him, yre * him + yim * hre], axis=0)
        d = _dft_dot(fi_ref[...], p)
        sk = skip_ref[o:o + 1]
        u = [pc_ref[o + 1, bi] * (d[bi * n:(bi + 1) * n] + u[bi] * sk) for bi in range(2)]
    o_ref[0] = u[0].astype(o_ref.dtype)
    o_ref[1] = u[1].astype(o_ref.dtype)


def hyena_short(pc, filt, l1, skip, out_dtype):
    _, b, n, c = pc.shape
    assert b == 2
    ll = 2 * n
    two_pi = 2.0 * math.pi
    kk = jnp.arange(ll, dtype=jnp.int32)
    ang = ((kk[:, None] * kk[None, :]) % ll).astype(F32) * (two_pi / ll)
    cf, sf = jnp.cos(ang), jnp.sin(ang)
    f_real = jnp.concatenate([cf, -sf], 0)
    cn, sn = cf[:, :n], sf[:, :n]
    f_fwd = jnp.concatenate([jnp.concatenate([cn, sn], 1), jnp.concatenate([-sn, cn], 1)], 0)
    ci, si = cf[:n, :], sf[:n, :]
    f_inv = jnp.concatenate([jnp.concatenate([ci, -si], 1), jnp.concatenate([si, ci], 1)], 0)
    tc = 256
    const = lambda j: (0, 0)
    return pl.pallas_call(
        functools.partial(_hyena_short_kernel, n=n, ll=ll),
        grid=(c // tc,),
        in_specs=[pl.BlockSpec(f_fwd.shape, const),
                  pl.BlockSpec(f_real.shape, const),
                  pl.BlockSpec(f_inv.shape, const),
                  pl.BlockSpec((3, 2, n, tc), lambda j: (0, 0, 0, j)),
                  pl.BlockSpec((HY_ORDER, ll, tc), lambda j: (0, 0, j)),
                  pl.BlockSpec((HY_ORDER, 1, tc), lambda j: (0, 0, j)),
                  pl.BlockSpec((HY_ORDER, tc), lambda j: (0, j))],
        out_specs=pl.BlockSpec((2, n, tc), lambda j: (0, 0, j)),
        out_shape=jax.ShapeDtypeStruct((2, n, c), out_dtype),
        compiler_params=_cparams("arbitrary"),
        name="hyena_short",
    )(f_fwd, f_real, f_inv, pc, filt, l1, skip)


def hyena_mixer(pc, w1, b1, w2, b2, w3, freq, skip, out_dtype):
    n = pc.shape[2]
    if (2 * n) % (DFT_RADIX * BF16_SUBLANES) == 0 and n >= 1024:
        return hyena_long(pc, (w1, b1, w2, b2, w3, freq), skip)
    filt, l1 = hyena_filter_time(n, w1, b1, w2, b2, w3, freq)
    return hyena_short(pc, filt, l1, skip, out_dtype)


def _deinterleave_cols(w, hd):
    lead = w.shape[:-1]
    wh = w.reshape(lead + (w.shape[-1] // LANES, LANES // hd, hd // 2, 2))
    half = lead + (w.shape[-1] // LANES, LANES // 2)
    return jnp.concatenate([wh[..., 0].reshape(half), wh[..., 1].reshape(half)],
                           axis=-1).reshape(w.shape)


def _rope_tables(n_tok, rot_dim):
    rows = n_tok // GRID_W
    row = jnp.broadcast_to(jnp.arange(rows, dtype=jnp.int32)[:, None], (rows, GRID_W)).reshape(n_tok)
    col = jnp.broadcast_to(jnp.arange(GRID_W, dtype=jnp.int32)[None, :], (rows, GRID_W)).reshape(n_tok)
    axis_dim = rot_dim // 2
    inv_freq = ROPE_THETA ** (-jnp.arange(0, axis_dim, 2, dtype=F32) / axis_dim)
    ang = jnp.concatenate([row.astype(F32)[:, None] * inv_freq,
                           col.astype(F32)[:, None] * inv_freq], axis=-1)
    c, s = jnp.cos(ang), jnp.sin(ang)
    reps = LANES // rot_dim
    c, s = jnp.tile(c, (1, reps)), jnp.tile(s, (1, reps))
    return jnp.concatenate([c, c], -1), jnp.concatenate([-s, s], -1)


def _mixer_ab(h, hc, with_ctx, w_in, w_out, qk_g, hy):
    b, n, _ = h.shape
    conv_w, conv_b, w1, b1, w2, b2, w3, freq, skip = hy
    scale = A_HEAD_DIM ** -0.5 * LOG2E
    wq = _deinterleave_cols(w_in[:, :A_Q_W], A_HEAD_DIM).astype(BF16)
    wk = _deinterleave_cols(w_in[:, A_Q_W:A_Q_W + A_KV_W], A_HEAD_DIM).astype(BF16)
    wv = w_in[:, A_Q_W + A_KV_W:A_QKV_W].astype(BF16)
    wh = w_in[:, A_QKV_W:].astype(BF16)
    gq = jnp.tile(_deinterleave_cols(qk_g[0], A_HEAD_DIM) * scale, A_HEADS)
    gk = jnp.tile(_deinterleave_cols(qk_g[1], A_HEAD_DIM), A_KV_HEADS)
    ones_v = jnp.ones((A_KV_W,), F32)
    cosf, sinf = _rope_tables(n, A_HEAD_DIM)

    n_ctx = hc.shape[1]
    q = inproj(h, wq, gq, cosf, sinf, norm=True)
    k_all = inproj(hc, wk, gk, norm=True, append_at=n,
                   append_to=inproj(h, wk, gk, cosf, sinf, norm=True, tail_rows=n_ctx))
    v_all = inproj(hc, wv, ones_v, append_at=n, append_to=inproj(h, wv, ones_v, tail_rows=n_ctx))
    o_att = flash_attention(q, k_all, v_all, mode="gqa", kv_len=n + n_ctx)
    pc = inproj_conv(h, wh, conv_w, conv_b, HY_ORDER + 1)
    o_hy = hyena_mixer(pc, w1, b1, w2, b2, w3, freq, skip, BF16)
    wo = w_out.astype(BF16)
    lat = ([o_att, o_hy], [wo[:A_Q_W], wo[A_Q_W:]])
    if not with_ctx:
        return lat, None
    q_c = inproj(hc, wq, gq, norm=True)
    o_att_c = flash_attention(q_c, k_all, v_all, mode="gqa", kv_start=n, kv_len=n_ctx)
    pc_c = inproj_conv(hc, wh, conv_w, conv_b, HY_ORDER + 1)
    o_hy_c = hyena_mixer(pc_c, w1, b1, w2, b2, w3, freq, skip, BF16)
    return lat, ([o_att_c, o_hy_c], [wo[:A_Q_W], wo[A_Q_W:]])


def _mixer_c(h, hc, with_ctx, lambda_init, w_in, w_out, lam_vecs, subln_g):
    b, n, _ = h.shape
    scale = C_HEAD_DIM ** -0.5 * LOG2E
    wq = _deinterleave_cols(w_in[:, :C_Q_W], C_HEAD_DIM).astype(BF16)
    wk = _deinterleave_cols(w_in[:, C_Q_W:2 * C_Q_W], C_HEAD_DIM).astype(BF16)
    wv = w_in[:, 2 * C_Q_W:].astype(BF16)
    gq = jnp.full((C_Q_W,), scale, F32)
    ones = jnp.ones((C_Q_W,), F32)
    cosf, sinf = _rope_tables(n, C_HEAD_DIM)

    n_ctx = hc.shape[1]
    q = inproj(h, wq, gq, cosf, sinf)
    k_all = inproj(hc, wk, ones, append_at=n,
                   append_to=inproj(h, wk, ones, cosf, sinf, tail_rows=n_ctx))
    v_all = inproj(hc, wv, ones, append_at=n, append_to=inproj(h, wv, ones, tail_rows=n_ctx))
    attn = functools.partial(flash_attention, mode="diff", lam_vecs=lam_vecs, subln_g=subln_g,
                             lambda_init=lambda_init)
    wo = w_out.astype(BF16)
    lat = ([attn(q, k_all, v_all, kv_len=n + n_ctx)], [wo])
    if not with_ctx:
        return lat, None
    q_c = inproj(hc, wq, gq)
    return lat, ([attn(q_c, k_all, v_all, kv_start=n, kv_len=n_ctx)], [wo])


def kernel(x, c, ctx, c_ctx, ada_w, ada_b, norm_g, ab_w_in, ab_w_out, ab_qk_g,
           hy_conv_w, hy_conv_b, hy_w1, hy_b1, hy_w2, hy_b2, hy_w3, hy_freq, hy_skip,
           dc_w_in, dc_w_out, dc_lambda, dc_subln_g,
           ffn_w_up, ffn_conv_w, ffn_conv_b, ffn_w_down):
    depth = ada_w.shape[0]
    b, n, d = x.shape
    pad_rows = (-(b + 1)) % 8
    cvec = jnp.concatenate([c, c_ctx[None, :], jnp.zeros((pad_rows, d), F32)], axis=0)
    mod_all = ada_mod(cvec, ada_w, ada_b)
    w_up = ffn_w_up.astype(BF16)
    w_down = ffn_w_down.astype(BF16)
    for i in range(depth):
        j = i // 2
        with_ctx = i < depth - 1
        mod = mod_all[i, :b].reshape(b, 6, d)
        mod_c = jnp.broadcast_to(mod_all[i, b].reshape(1, 6, d), (b, 6, d))
        sh1, sc1, g1, sh2, sc2, g2 = (mod[:, t] for t in range(6))
        sh1c, sc1c, g1c, sh2c, sc2c, g2c = (mod_c[:, t] for t in range(6))
        h = norm_mod(x, norm_g[i, 0], sc1, sh1)
        hc = norm_mod(ctx, norm_g[i, 0], sc1c, sh1c)
        if i % 2 == 0:
            hy = (hy_conv_w[j], hy_conv_b[j], hy_w1[j], hy_b1[j], hy_w2[j], hy_b2[j],
                  hy_w3[j], hy_freq[j], hy_skip[j])
            lat, cx = _mixer_ab(h, hc, with_ctx, ab_w_in[j], ab_w_out[j], ab_qk_g[j], hy)
        else:
            lambda_init = 0.8 - 0.6 * math.exp(-0.3 * i)
            lat, cx = _mixer_c(h, hc, with_ctx, lambda_init, dc_w_in[j], dc_w_out[j],
                               dc_lambda[j], dc_subln_g[j])
        x = outproj_residual(lat[0], lat[1], x, norm_g[i, 1], g1)
        x = conv_ffn_residual(x, norm_g[i, 2], sc2, sh2, w_up, ffn_conv_w[i], ffn_conv_b[i],
                              w_down, norm_g[i, 3], g2, i)
        if with_ctx:
            ctx = outproj_residual(cx[0], cx[1], ctx, norm_g[i, 1], g1c)
            ctx = conv_ffn_residual(ctx, norm_g[i, 2], sc2c, sh2c, w_up, ffn_conv_w[i],
                                    ffn_conv_b[i], w_down, norm_g[i, 3], g2c, i)
    return x
```

```python
import functools
import math

import jax
import jax.numpy as jnp
from jax import lax
from jax.experimental import pallas as pl
from jax.experimental.pallas import tpu as pltpu

F32 = jnp.float32
BF16 = jnp.bfloat16
HIGHEST = lax.Precision.HIGHEST

D_MODEL = 2048
GRID_W = 64
EPS = 1e-6
ROPE_THETA = 10000.0
A_HEADS, A_KV_HEADS, A_HEAD_DIM = 8, 2, 128
A_Q_W = A_HEADS * A_HEAD_DIM
A_KV_W = A_KV_HEADS * A_HEAD_DIM
A_QKV_W = A_Q_W + 2 * A_KV_W
HY_CH = D_MODEL // 2
HY_ORDER = 2
HY_FILTER_W = 64
HY_BANDS = 16
HY_FAST, HY_SLOW, HY_TARGET = 0.3, 1.5, 1e-2
C_HEADS, C_HEAD_DIM = 16, 64
C_Q_W = C_HEADS * 2 * C_HEAD_DIM

LANES = 128
BF16_SUBLANES = 16
VMEM_LIMIT = 56 * 1024 * 1024
DFT_RADIX = 128
FLASH_ROWS = 2048
FLASH_SUB_ROWS = 128
FLASH_MAX_KV = 16384
LOG2E = math.log2(math.e)

SUBLANES = 8
ROW_TILE = 512
ADA_COLS = 1024
PROJ_COLS = 2048
CONV_PROJ_COLS = 1024
FFN_CHUNK = 512
DFT_K1_CHUNK = 8
DFT_CH_TILE = 256


def _cparams(*sem):
    return pltpu.CompilerParams(dimension_semantics=sem, vmem_limit_bytes=VMEM_LIMIT)


def _rms(x, g):
    return x * lax.rsqrt(jnp.mean(x * x, axis=-1, keepdims=True) + EPS) * g


def _ada_kernel(c_ref, w_ref, b_ref, o_ref):
    c = c_ref[...]
    s = c * jax.nn.sigmoid(c)
    rows = s.shape[0]
    s_hi = s.astype(BF16).astype(F32)
    s2 = jnp.concatenate([s_hi, s - s_hi], axis=0).astype(BF16)
    w = w_ref[0]
    w_hi = w.astype(BF16)
    w_lo = (w - w_hi.astype(F32)).astype(BF16)
    both = jnp.dot(s2, w_hi, preferred_element_type=F32)
    hi_lo = jnp.dot(s2, w_lo, preferred_element_type=F32)[:rows]
    o_ref[0] = both[:rows] + both[rows:] + hi_lo + b_ref[0]


def ada_mod(cvec, ada_w, ada_b):
    depth, d, n6 = ada_w.shape
    rows = cvec.shape[0]
    tn = ADA_COLS
    return pl.pallas_call(
        _ada_kernel,
        grid=(depth, n6 // tn),
        in_specs=[pl.BlockSpec((rows, d), lambda l, j: (0, 0)),
                  pl.BlockSpec((1, d, tn), lambda l, j: (l, 0, j)),
                  pl.BlockSpec((1, 1, tn), lambda l, j: (l, 0, j))],
        out_specs=pl.BlockSpec((1, rows, tn), lambda l, j: (l, 0, j)),
        out_shape=jax.ShapeDtypeStruct((depth, rows, n6), F32),
        compiler_params=_cparams("arbitrary", "arbitrary"),
        name="ada_mod",
    )(cvec, ada_w, ada_b.reshape(depth, 1, n6))


def _norm_mod_kernel(x_ref, g_ref, sc_ref, sh_ref, o_ref):
    y = _rms(x_ref[0], g_ref[...])
    o_ref[0] = (y * (1.0 + sc_ref[0]) + sh_ref[0]).astype(o_ref.dtype)


def norm_mod(x, g, sc, sh):
    b, n, d = x.shape
    tm = min(n, ROW_TILE)
    return pl.pallas_call(
        _norm_mod_kernel,
        grid=(b, n // tm),
        in_specs=[pl.BlockSpec((1, tm, d), lambda bi, i: (bi, i, 0)),
                  pl.BlockSpec((1, d), lambda bi, i: (0, 0)),
                  pl.BlockSpec((1, 1, d), lambda bi, i: (bi, 0, 0)),
                  pl.BlockSpec((1, 1, d), lambda bi, i: (bi, 0, 0))],
        out_specs=pl.BlockSpec((1, tm, d), lambda bi, i: (bi, i, 0)),
        out_shape=jax.ShapeDtypeStruct((b, n, d), BF16),
        compiler_params=_cparams("arbitrary", "arbitrary"),
        name="norm_mod",
    )(x, g.reshape(1, d), sc.reshape(b, 1, d), sh.reshape(b, 1, d))


def _inproj_kernel(*refs, norm, rope, appending, n_tiles, has_tail):
    o_ref = refs[-1]
    if appending:
        refs = refs[:-2]
    else:
        refs = refs[:-1]
    if rope:
        h_ref, w_ref, g_ref, cos_ref, sin_ref = refs
    else:
        h_ref, w_ref, g_ref = refs
    i = pl.program_id(1)

    def _tail():
        o_ref[0] = jnp.zeros(o_ref.shape[1:], o_ref.dtype)

    def _body():
        acc = jnp.dot(h_ref[0], w_ref[...], preferred_element_type=F32)
        tn = acc.shape[1]
        if rope:
            cosf = cos_ref[...]
            sinf = sin_ref[...]
        for c in range(tn // LANES):
            x = acc[:, c * LANES:(c + 1) * LANES]
            if norm:
                x = x * lax.rsqrt(jnp.mean(x * x, axis=-1, keepdims=True) + EPS)
            x = x * g_ref[:, c * LANES:(c + 1) * LANES]
            if rope:
                x = x * cosf + pltpu.roll(x, LANES // 2, 1) * sinf
            o_ref[0, :, c * LANES:(c + 1) * LANES] = x.astype(o_ref.dtype)

    if has_tail:
        pl.when(i >= n_tiles)(_tail)
        pl.when(i < n_tiles)(_body)
    else:
        _body()


def inproj(h, w, gvec, cosf=None, sinf=None, *, norm=False, out_dtype=BF16,
           tail_rows=0, append_to=None, append_at=0):
    b, n, d = h.shape
    ncol = w.shape[1]
    rope = cosf is not None
    tm = min(n, ROW_TILE)
    tn = min(ncol, PROJ_COLS)
    n_tiles = n // tm
    last = n_tiles - 1
    in_specs = [pl.BlockSpec((1, tm, d), lambda bi, i, j: (bi, jnp.minimum(i, last), 0)),
                pl.BlockSpec((d, tn), lambda bi, i, j: (0, j)),
                pl.BlockSpec((1, tn), lambda bi, i, j: (0, j))]
    args = [h, w, gvec.reshape(1, ncol)]
    if rope:
        in_specs += [pl.BlockSpec((tm, LANES), lambda bi, i, j: (jnp.minimum(i, last), 0)),
                     pl.BlockSpec((tm, LANES), lambda bi, i, j: (jnp.minimum(i, last), 0))]
        args += [cosf, sinf]
    appending = append_to is not None
    if appending:
        total = append_to.shape[1]
        first = append_at // tm
        assert first * tm == append_at and append_at + n <= total and append_to.dtype == out_dtype
        tail_tiles = 0
        in_specs.append(pl.BlockSpec(memory_space=pl.ANY))
        args.append(append_to)
        aliases = {len(args) - 1: 0}
    else:
        tail_tiles = -(-tail_rows // tm)
        total, first, aliases = (n_tiles + tail_tiles) * tm, 0, {}
    return pl.pallas_call(
        functools.partial(_inproj_kernel, norm=norm, rope=rope, appending=appending,
                          n_tiles=n_tiles, has_tail=tail_tiles > 0),
        grid=(b, n_tiles + tail_tiles, ncol // tn),
        in_specs=in_specs,
        out_specs=pl.BlockSpec((1, tm, tn), lambda bi, i, j: (bi, first + i, j)),
        out_shape=jax.ShapeDtypeStruct((b, total, ncol), out_dtype),
        input_output_aliases=aliases,
        compiler_params=_cparams("arbitrary", "arbitrary", "arbitrary"),
        name="inproj",
    )(*args)


def _inproj_conv_kernel(hm_ref, hp_ref, hn_ref, w_ref, cw_ref, cb_ref, o_ref, *, n_tiles):
    i = pl.program_id(2)
    hp, hm, hn = hp_ref[0], hm_ref[0], hn_ref[0]
    halo, tm = hp.shape[0], hm.shape[0]
    zero = jnp.zeros_like(hp)
    lhs = jnp.concatenate([jnp.where(i > 0, hp, zero), hm,
                           jnp.where(i < n_tiles - 1, hn, zero)], axis=0)
    u = jnp.dot(lhs, w_ref[...], preferred_element_type=F32)
    rows = tm + 2 * halo
    cw = cw_ref[...]
    conv = cw[0:1] * pltpu.roll(u, 1, 0) + cw[1:2] * u + cw[2:3] * pltpu.roll(u, rows - 1, 0)
    o_ref[0, 0] = conv[halo:halo + tm] + cb_ref[...]


def inproj_conv(h, w, conv_w, conv_b, groups):
    b, n, d = h.shape
    ncol = w.shape[1]
    c = ncol // groups
    tm = min(n, ROW_TILE)
    tn = min(c, CONV_PROJ_COLS)
    halo = BF16_SUBLANES
    nt = n // tm
    per = c // tn
    return pl.pallas_call(
        functools.partial(_inproj_conv_kernel, n_tiles=nt),
        grid=(ncol // tn, b, nt),
        in_specs=[pl.BlockSpec((1, tm, d), lambda j, bi, i: (bi, i, 0)),
                  pl.BlockSpec((1, halo, d),
                               lambda j, bi, i: (bi, jnp.maximum(i * (tm // halo) - 1, 0), 0)),
                  pl.BlockSpec((1, halo, d),
                               lambda j, bi, i: (bi, jnp.minimum((i + 1) * (tm // halo), n // halo - 1), 0)),
                  pl.BlockSpec((d, tn), lambda j, bi, i: (0, j)),
                  pl.BlockSpec((3, tn), lambda j, bi, i: (0, j)),
                  pl.BlockSpec((1, tn), lambda j, bi, i: (0, j))],
        out_specs=pl.BlockSpec((1, 1, tm, tn), lambda j, bi, i: (j // per, bi, i, j % per)),
        out_shape=jax.ShapeDtypeStruct((groups, b, n, c), F32),
        compiler_params=_cparams("arbitrary", "arbitrary", "arbitrary"),
        name="inproj_conv",
    )(h, h, h, w, conv_w, conv_b.reshape(1, ncol))


def _flash_kernel(*refs, mode, group, tq, lambda_init):
    if mode == "diff":
        q_ref, k_ref, v_ref, lam_ref, sg_ref, o_ref, qs_ref = refs
    else:
        q_ref, k_ref, v_ref, o_ref, qs_ref = refs
    q = q_ref[0]
    if mode == "diff":
        lane = lax.broadcasted_iota(jnp.int32, q.shape, 1)
        zero = jnp.zeros_like(q)
        first = (lane // (C_HEAD_DIM // 2)) % 2 == 0
        qs_ref[0:tq] = jnp.where(first, q, zero)
        qs_ref[tq:2 * tq] = jnp.where(first, zero, q)
    else:
        for g in range(group):
            qs_ref[g * tq:(g + 1) * tq] = q[:, g * LANES:(g + 1) * LANES]

    k = k_ref[0]
    v = v_ref[0]
    v_aug = jnp.concatenate([v, jnp.ones_like(v)], axis=1)
    sub = FLASH_SUB_ROWS

    def attend(c):
        s = lax.dot_general(qs_ref[c * sub:(c + 1) * sub], k, (((1,), (1,)), ((), ())),
                            preferred_element_type=F32)
        p = jnp.exp2(s - jnp.max(s, axis=-1, keepdims=True))
        pv = jnp.dot(p.astype(BF16), v_aug, preferred_element_type=F32)
        return pv[:, :LANES] / pv[:, LANES:]

    per_head = tq // sub
    if mode == "diff":
        lv = lam_ref[...]
        lam = (jnp.exp(jnp.sum(lv[0:1] * lv[1:2], axis=-1, keepdims=True))
               - jnp.exp(jnp.sum(lv[2:3] * lv[3:4], axis=-1, keepdims=True)) + lambda_init)
        for c in range(per_head):
            dlt = attend(c) - lam * attend(per_head + c)
            o_ref[0, c * sub:(c + 1) * sub, :] = (
                _rms(dlt, sg_ref[...]) * (1.0 - lambda_init)).astype(o_ref.dtype)
    else:
        for c in range(group * per_head):
            g, r = divmod(c, per_head)
            o_ref[0, r * sub:(r + 1) * sub, g * LANES:(g + 1) * LANES] = attend(c).astype(o_ref.dtype)


def flash_attention(q, k, v, *, mode, kv_len, lam_vecs=None, subln_g=None, lambda_init=0.0,
                    kv_start=0):
    b, n, qw = q.shape
    tk = kv_len
    assert tk <= FLASH_MAX_KV and tk % LANES == 0 and kv_start % tk == 0
    if mode == "diff":
        group, n_groups, qblk = 2, qw // LANES, LANES
    else:
        group = A_HEADS // A_KV_HEADS
        n_groups, qblk = A_KV_HEADS, group * LANES
    tq = min(n, FLASH_ROWS // group)
    assert tq % FLASH_SUB_ROWS == 0
    j0 = kv_start // tk
    in_specs = [pl.BlockSpec((1, tq, qblk), lambda bi, g, i: (bi, i, g)),
                pl.BlockSpec((1, tk, LANES), lambda bi, g, i: (bi, j0, g)),
                pl.BlockSpec((1, tk, LANES), lambda bi, g, i: (bi, j0, g))]
    args = [q, k, v]
    if mode == "diff":
        in_specs += [pl.BlockSpec(lam_vecs.shape, lambda bi, g, i: (0, 0)),
                     pl.BlockSpec((1, LANES), lambda bi, g, i: (0, 0))]
        args += [lam_vecs, subln_g.reshape(1, LANES)]
    return pl.pallas_call(
        functools.partial(_flash_kernel, mode=mode, group=group, tq=tq, lambda_init=lambda_init),
        grid=(b, n_groups, n // tq),
        in_specs=in_specs,
        out_specs=pl.BlockSpec((1, tq, qblk), lambda bi, g, i: (bi, i, g)),
        out_shape=jax.ShapeDtypeStruct((b, n, qw), BF16),
        scratch_shapes=[pltpu.VMEM((group * tq, LANES), BF16)],
        compiler_params=_cparams("arbitrary", "arbitrary", "arbitrary"),
        name="flash_" + mode,
    )(*args)


def _outproj_kernel(*refs, n_lhs):
    lhs = refs[:n_lhs]
    ws = refs[n_lhs:2 * n_lhs]
    x_ref, g_ref, gate_ref, o_ref = refs[2 * n_lhs:]
    y = jnp.dot(lhs[0][0].astype(BF16), ws[0][...], preferred_element_type=F32)
    for a, w in zip(lhs[1:], ws[1:]):
        y = y + jnp.dot(a[0].astype(BF16), w[...], preferred_element_type=F32)
    o_ref[0] = x_ref[0] + gate_ref[0] * _rms(y, g_ref[...])


def outproj_residual(lhs_list, w_list, x, g, gate):
    b, n, d = x.shape
    tm = min(n, ROW_TILE)
    n_lhs = len(lhs_list)
    in_specs = [pl.BlockSpec((1, tm, a.shape[2]), lambda bi, i: (bi, i, 0)) for a in lhs_list]
    in_specs += [pl.BlockSpec(w.shape, lambda bi, i: (0, 0)) for w in w_list]
    in_specs += [pl.BlockSpec((1, tm, d), lambda bi, i: (bi, i, 0)),
                 pl.BlockSpec((1, d), lambda bi, i: (0, 0)),
                 pl.BlockSpec((1, 1, d), lambda bi, i: (bi, 0, 0))]
    return pl.pallas_call(
        functools.partial(_outproj_kernel, n_lhs=n_lhs),
        grid=(b, n // tm),
        in_specs=in_specs,
        out_specs=pl.BlockSpec((1, tm, d), lambda bi, i: (bi, i, 0)),
        out_shape=jax.ShapeDtypeStruct((b, n, d), F32),
        compiler_params=_cparams("arbitrary", "arbitrary"),
        name="outproj",
    )(*lhs_list, *w_list, x, g.reshape(1, d), gate.reshape(b, 1, d))


def _ffn_kernel(xm_ref, xp_ref, xn_ref, gn_ref, sc_ref, sh_ref, wa_ref, wg_ref, cwa_ref, cwg_ref,
                cba_ref, cbg_ref, wd_ref, g3_ref, gate_ref, o_ref, hx_ref,
                *, tm, halo, n_tiles, n_f):
    i = pl.program_id(1)
    j = pl.program_id(2)

    @pl.when(j == 0)
    def _prologue():
        def nm(x):
            return (_rms(x, gn_ref[...]) * (1.0 + sc_ref[0]) + sh_ref[0]).astype(BF16)
        hx_ref[halo:halo + tm] = nm(xm_ref[0])
        zero = jnp.zeros((halo, xm_ref.shape[2]), BF16)
        hx_ref[0:halo] = jnp.where(i > 0, nm(xp_ref[0]), zero)
        hx_ref[halo + tm:] = jnp.where(i < n_tiles - 1, nm(xn_ref[0]), zero)
        o_ref[0] = jnp.zeros(o_ref.shape[1:], F32)

    hx = hx_ref[...]
    rows = tm + 2 * halo

    def conv(u, cw, cb):
        up = pltpu.roll(u, 1, 0)
        dn = pltpu.roll(u, rows - 1, 0)
        return (cw[0:1] * up + cw[1:2] * u + cw[2:3] * dn + cb)[halo:halo + tm]

    a = conv(jnp.dot(hx, wa_ref[...], preferred_element_type=F32), cwa_ref[...], cba_ref[...])
    g = conv(jnp.dot(hx, wg_ref[...], preferred_element_type=F32), cwg_ref[...], cbg_ref[...])
    act = (a * (g * jax.nn.sigmoid(g))).astype(BF16)
    o_ref[0] += jnp.dot(act, wd_ref[...], preferred_element_type=F32)

    @pl.when(j == n_f - 1)
    def _epilogue():
        o_ref[0] = xm_ref[0] + gate_ref[0] * _rms(o_ref[0], g3_ref[...])


def conv_ffn_residual(x, gn, sc, sh, w_up, conv_w, conv_b, w_down, g3, gate, layer):
    b, n, d = x.shape
    dff = w_down.shape[1]
    tm = min(n, ROW_TILE)
    tf = FFN_CHUNK
    halo = BF16_SUBLANES
    nt = n // tm
    n_f = dff // tf
    row = lambda bi, i, j: (bi, i, 0)
    vec = lambda bi, i, j: (0, 0)
    bvec = lambda bi, i, j: (bi, 0, 0)
    up_a = lambda bi, i, j: (0, j)
    up_g = lambda bi, i, j: (0, n_f + j)
    wup_a = lambda bi, i, j: (layer, 0, j)
    wup_g = lambda bi, i, j: (layer, 0, n_f + j)
    conv_b = conv_b.reshape(1, 2 * dff)
    return pl.pallas_call(
        functools.partial(_ffn_kernel, tm=tm, halo=halo, n_tiles=nt, n_f=n_f),
        grid=(b, nt, n_f),
        in_specs=[pl.BlockSpec((1, tm, d), row),
                  pl.BlockSpec((1, halo, d),
                               lambda bi, i, j: (bi, jnp.maximum(i * (tm // halo) - 1, 0), 0)),
                  pl.BlockSpec((1, halo, d),
                               lambda bi, i, j: (bi, jnp.minimum((i + 1) * (tm // halo), n // halo - 1), 0)),
                  pl.BlockSpec((1, d), vec),
                  pl.BlockSpec((1, 1, d), bvec),
                  pl.BlockSpec((1, 1, d), bvec),
                  pl.BlockSpec((None, d, tf), wup_a),
                  pl.BlockSpec((None, d, tf), wup_g),
                  pl.BlockSpec((3, tf), up_a),
                  pl.BlockSpec((3, tf), up_g),
                  pl.BlockSpec((1, tf), up_a),
                  pl.BlockSpec((1, tf), up_g),
                  pl.BlockSpec((None, tf, d), lambda bi, i, j: (layer, j, 0)),
                  pl.BlockSpec((1, d), vec),
                  pl.BlockSpec((1, 1, d), bvec)],
        out_specs=pl.BlockSpec((1, tm, d), row),
        out_shape=jax.ShapeDtypeStruct((b, n, d), F32),
        scratch_shapes=[pltpu.VMEM((tm + 2 * halo, d), BF16)],
        compiler_params=_cparams("arbitrary", "arbitrary", "arbitrary"),
        name="conv_ffn",
    )(x, x, x, gn.reshape(1, d), sc.reshape(b, 1, d), sh.reshape(b, 1, d), w_up, w_up,
      conv_w, conv_w, conv_b, conv_b, w_down, g3.reshape(1, d), gate.reshape(b, 1, d))


def _filter_rows(r, bands_ref, w1_ref, b1_ref, w2_ref, b2_ref, fr_ref, dl_ref, n_tok):
    t = jnp.where(r < n_tok, r, 2.0 * n_tok - r)
    t_norm = t / float(max(n_tok - 1, 1))
    wang = (2.0 * math.pi / n_tok) * t
    z = wang * bands_ref[...]
    lane = lax.broadcasted_iota(jnp.int32, z.shape, 1)
    feats = jnp.where(lane == 0, t_norm,
                      jnp.where(lane <= HY_BANDS, jnp.cos(z),
                                jnp.where(lane <= 2 * HY_BANDS, -jnp.sin(z), 0.0)))
    fr = fr_ref[...]
    h = jnp.sin(fr[0:1] * (jnp.dot(feats, w1_ref[...], precision=HIGHEST,
                                   preferred_element_type=F32) + b1_ref[...]))
    h = jnp.sin(fr[1:2] * (jnp.dot(h, w2_ref[...], precision=HIGHEST,
                                   preferred_element_type=F32) + b2_ref[...]))
    decay = jnp.where(r == n_tok, 0.0, jnp.exp(-t_norm * dl_ref[...]))
    return h, decay


def _filter_kernel(bands_ref, w1_ref, b1_ref, w2_ref, b2_ref, w3_ref, fr_ref, dl_ref,
                   o_ref, s_ref, *, n_tok, tr):
    i = pl.program_id(0)

    @pl.when(i == 0)
    def _init():
        s_ref[...] = jnp.zeros(s_ref.shape, F32)

    r = (i * tr + lax.broadcasted_iota(jnp.int32, (tr, 1), 0)).astype(F32)
    h, decay = _filter_rows(r, bands_ref, w1_ref, b1_ref, w2_ref, b2_ref, fr_ref, dl_ref, n_tok)
    hb = h.astype(BF16)
    for o in range(HY_ORDER):
        out = jnp.dot(hb, w3_ref[o].astype(BF16), preferred_element_type=F32) * decay
        o_ref[o] = out
        s_ref[o] += jnp.sum(jnp.abs(out), axis=0, keepdims=True)


def _filter_stage_a_kernel(bands_ref, w1_ref, b1_ref, w2_ref, b2_ref, w3_ref, fr_ref, dl_ref,
                           f_ref, *out_refs, n_tok, sub):
    o_refs, s_ref = out_refs[:HY_ORDER], out_refs[HY_ORDER]
    j = pl.program_id(0)

    @pl.when(j == 0)
    def _init():
        s_ref[...] = jnp.zeros(s_ref.shape, F32)

    k1n = o_refs[0].shape[2]
    half = k1n // 2
    c = dl_ref.shape[1]
    n1 = lax.broadcasted_iota(jnp.int32, (k1n, 1), 0)
    f = f_ref[...]
    for q in range(sub):
        r = (n1 * DFT_RADIX + (j * sub + q)).astype(F32)
        h, decay = _filter_rows(r, bands_ref, w1_ref, b1_ref, w2_ref, b2_ref, fr_ref, dl_ref, n_tok)
        h_fwd, h_bwd = h[:half].astype(BF16), h[half:].astype(BF16)
        for o in range(HY_ORDER):
            w3 = w3_ref[o].astype(BF16)
            h3 = jnp.concatenate([jnp.dot(h_fwd, w3[:, :c], preferred_element_type=F32),
                                  jnp.dot(h_bwd, w3[:, c:], preferred_element_type=F32)],
                                 axis=0) * decay
            s_ref[o] += jnp.sum(jnp.abs(h3), axis=0, keepdims=True)
            y = _dft_dot(f, h3)
            o_refs[o][q, 0] = y[:k1n]
            o_refs[o][q, 1] = y[k1n:]


def _filter_operands(w1, b1, w2, b2, w3, freq):
    ch = HY_CH
    lane = jnp.arange(LANES)
    bands = jnp.linspace(1e-4, HY_BANDS - 1, HY_BANDS, dtype=F32)
    bands_l = jnp.where((lane >= 1) & (lane <= HY_BANDS), bands[jnp.clip(lane - 1, 0, HY_BANDS - 1)],
                        jnp.where((lane > HY_BANDS) & (lane <= 2 * HY_BANDS),
                                  bands[jnp.clip(lane - 1 - HY_BANDS, 0, HY_BANDS - 1)], 0.0))
    w1p = jnp.zeros((LANES, HY_FILTER_W), F32).at[:w1.shape[0]].set(w1)
    deltas = jnp.abs(jnp.linspace(math.log(HY_TARGET) / HY_FAST, math.log(HY_TARGET) / HY_SLOW,
                                  ch, dtype=F32))
    w3o = jnp.swapaxes(w3.reshape(HY_FILTER_W, HY_ORDER, 2 * ch), 0, 1)
    return (bands_l.reshape(1, LANES), w1p, b1.reshape(1, -1), w2, b2.reshape(1, -1), w3o, freq,
            deltas.reshape(1, ch))


def _filter_in_specs(w3_spec):
    const = lambda i: (0, 0)
    return [pl.BlockSpec((1, LANES), const),
            pl.BlockSpec((LANES, HY_FILTER_W), const),
            pl.BlockSpec((1, HY_FILTER_W), const),
            pl.BlockSpec((HY_FILTER_W, HY_FILTER_W), const),
            pl.BlockSpec((1, HY_FILTER_W), const),
            w3_spec,
            pl.BlockSpec((2, HY_FILTER_W), const),
            pl.BlockSpec((1, HY_CH), const)]


def hyena_filter_time(n_tok, w1, b1, w2, b2, w3, freq):
    ch = HY_CH
    ll = 2 * n_tok
    tr = min(n_tok, ROW_TILE)
    per_half = n_tok // tr
    return pl.pallas_call(
        functools.partial(_filter_kernel, n_tok=n_tok, tr=tr),
        grid=(ll // tr,),
        in_specs=_filter_in_specs(
            pl.BlockSpec((HY_ORDER, HY_FILTER_W, ch), lambda i: (0, 0, i // per_half))),
        out_specs=[pl.BlockSpec((HY_ORDER, tr, ch), lambda i: (0, i, 0)),
                   pl.BlockSpec((HY_ORDER, 1, ch), lambda i: (0, 0, 0))],
        out_shape=[jax.ShapeDtypeStruct((HY_ORDER, ll, ch), F32),
                   jax.ShapeDtypeStruct((HY_ORDER, 1, ch), F32)],
        compiler_params=_cparams("arbitrary"),
        name="hyena_filter",
    )(*_filter_operands(w1, b1, w2, b2, w3, freq))


def hyena_filter_stage_a(n_tok, f_filt, w1, b1, w2, b2, w3, freq):
    ch = HY_CH
    r = DFT_RADIX
    k1n = 2 * n_tok // r
    sub = SUBLANES
    return pl.pallas_call(
        functools.partial(_filter_stage_a_kernel, n_tok=n_tok, sub=sub),
        grid=(r // sub,),
        in_specs=_filter_in_specs(
            pl.BlockSpec((HY_ORDER, HY_FILTER_W, 2 * ch), lambda j: (0, 0, 0)))
        + [pl.BlockSpec(f_filt.shape, lambda j: (0, 0))],
        out_specs=[pl.BlockSpec((sub, 2, k1n, ch), lambda j: (j, 0, 0, 0))] * HY_ORDER
        + [pl.BlockSpec((HY_ORDER, 1, ch), lambda j: (0, 0, 0))],
        out_shape=[jax.ShapeDtypeStruct((r, 2, k1n, ch), F32)] * HY_ORDER
        + [jax.ShapeDtypeStruct((HY_ORDER, 1, ch), F32)],
        compiler_params=_cparams("arbitrary"),
        name="hyena_filter_a",
    )(*_filter_operands(w1, b1, w2, b2, w3, freq), f_filt)


def _dft_dot(f, x):
    return jnp.dot(f.astype(BF16), x.astype(BF16), preferred_element_type=F32)


def _stage_a_kernel(f_ref, x_ref, o_ref):
    parts = x_ref.shape[0]
    f = f_ref[...]
    for r in range(x_ref.shape[2]):
        z = jnp.concatenate([x_ref[p, :, r, :] for p in range(parts)], axis=0)
        y = _dft_dot(f, z)
        half = y.shape[0] // 2
        o_ref[r, 0] = y[:half]
        o_ref[r, 1] = y[half:]


def dft_stage_a(f, x, g):
    _, p, rows, r, c = x.shape
    k1 = p * rows
    sub = SUBLANES
    return pl.pallas_call(
        _stage_a_kernel,
        grid=(r // sub,),
        in_specs=[pl.BlockSpec(f.shape, lambda j: (0, 0)),
                  pl.BlockSpec((None, p, rows, sub, c), lambda j: (g, 0, 0, j, 0))],
        out_specs=pl.BlockSpec((sub, 2, k1, c), lambda j: (j, 0, 0, 0)),
        out_shape=jax.ShapeDtypeStruct((r, 2, k1, c), F32),
        compiler_params=_cparams("arbitrary"),
        name="dft_stage_a",
    )(f, x)


def _stage_c_conv_kernel(a_ref, af_ref, sc_ref, g_ref, gi_ref, o_ref, *, kc):
    tc = a_ref.shape[3]
    for k in range(kc):
        r = jnp.concatenate(
            [jnp.concatenate([a_ref[:, 0, k, :], af_ref[:, 0, k, :]], axis=1),
             jnp.concatenate([a_ref[:, 1, k, :], af_ref[:, 1, k, :]], axis=1)], axis=0)
        y = _dft_dot(g_ref[k], r)
        half = y.shape[0] // 2
        yre, yim = y[:half, :tc], y[half:, :tc]
        hre, him = y[:half, tc:] * sc_ref[...], y[half:, tc:] * sc_ref[...]
        p = jnp.concatenate([yre * hre - yim * him, yre * him + yim * hre], axis=0)
        d = _dft_dot(gi_ref[k], p)
        o_ref[:, 0, k, :] = d[:half]
        o_ref[:, 1, k, :] = d[half:]


def dft_stage_c_conv(a, af, scale, g, gi):
    r, _, k1, c = a.shape
    kc = DFT_K1_CHUNK
    tc = DFT_CH_TILE
    blk = pl.BlockSpec((r, 2, kc, tc), lambda i, j: (0, 0, i, j))
    mat = pl.BlockSpec((kc, 2 * r, 2 * r), lambda i, j: (i, 0, 0))
    return pl.pallas_call(
        functools.partial(_stage_c_conv_kernel, kc=kc),
        grid=(k1 // kc, c // tc),
        in_specs=[blk, blk, pl.BlockSpec((1, tc), lambda i, j: (0, j)), mat, mat],
        out_specs=blk,
        out_shape=jax.ShapeDtypeStruct(a.shape, F32),
        compiler_params=_cparams("arbitrary", "arbitrary"),
        name="dft_stage_c_conv",
    )(a, af, scale, g, gi)


def _stage_a_inv_kernel(f_ref, d_ref, u_ref, gt_ref, sk_ref, o_ref):
    f = f_ref[...]
    sk = sk_ref[...]
    for r in range(d_ref.shape[0]):
        d = jnp.concatenate([d_ref[r, 0], d_ref[r, 1]], axis=0)
        y = _dft_dot(f, d)
        half = y.shape[0] // 2
        for bi in range(2):
            yb = y[bi * half:(bi + 1) * half]
            o_ref[bi, :, r, :] = gt_ref[bi, :, r, :] * (yb + u_ref[bi, :, r, :] * sk)


def dft_stage_a_inv(f, d, u, gu, gate, gg, skip):
    r, _, k1, c = d.shape
    half = k1 // 2
    sub = SUBLANES
    io = pl.BlockSpec((2, half, sub, c), lambda j: (0, 0, j, 0))
    return pl.pallas_call(
        _stage_a_inv_kernel,
        grid=(r // sub,),
        in_specs=[pl.BlockSpec(f.shape, lambda j: (0, 0)),
                  pl.BlockSpec((sub, 2, k1, c), lambda j: (j, 0, 0, 0)),
                  pl.BlockSpec((None, 2, half, sub, c), lambda j: (gu, 0, 0, j, 0)),
                  pl.BlockSpec((None, 2, half, sub, c), lambda j: (gg, 0, 0, j, 0)),
                  pl.BlockSpec((1, c), lambda j: (0, 0))],
        out_specs=io,
        out_shape=jax.ShapeDtypeStruct((2, half, r, c), F32),
        compiler_params=_cparams("arbitrary"),
        name="dft_stage_a_inv",
    )(f, d, u, gate, skip)


def _dft_tables(n_tok):
    ll = 2 * n_tok
    r = DFT_RADIX
    k1n = ll // r
    half = k1n // 2
    two_pi = 2.0 * math.pi

    def cs(m, period):
        ang = (m % period).astype(F32) * (two_pi / period)
        return jnp.cos(ang), jnp.sin(ang)

    k1 = jnp.arange(k1n, dtype=jnp.int32)
    c, s = cs(k1[:, None] * k1[None, :half], k1n)
    fa_data = jnp.concatenate([jnp.concatenate([c, s], 1), jnp.concatenate([-s, c], 1)], 0)
    c, s = cs(k1[:, None] * k1[None, :], k1n)
    fa_filt = jnp.concatenate([c, -s], 0)
    c, s = cs(k1[:half, None] * k1[None, :], k1n)
    fa_inv = jnp.concatenate([jnp.concatenate([c, -s], 1), jnp.concatenate([s, c], 1)], 0)
    idx = jnp.arange(r, dtype=jnp.int32)
    ca, sa = cs(idx[:, None] * idx[None, :], r)
    cb, sb = cs(k1[:, None] * idx[None, :], ll)
    c = ca[None] * cb[:, None, :] - sa[None] * sb[:, None, :]
    s = sa[None] * cb[:, None, :] + ca[None] * sb[:, None, :]
    g_fwd = jnp.concatenate([jnp.concatenate([c, s], 2), jnp.concatenate([-s, c], 2)], 1)
    ct, st = jnp.swapaxes(c, 1, 2), jnp.swapaxes(s, 1, 2)
    g_inv = jnp.concatenate([jnp.concatenate([ct, -st], 2), jnp.concatenate([st, ct], 2)], 1)
    return tuple(t.astype(BF16) for t in (fa_data, fa_filt, fa_inv, g_fwd, g_inv))


def hyena_long(pc, filter_params, skip):
    _, b, n, c = pc.shape
    assert b == 2, "the two batch entries ride as real/imaginary parts"
    ll = 2 * n
    r = DFT_RADIX
    half = ll // r // 2
    fa_data, fa_filt, fa_inv, g_fwd, g_inv = _dft_tables(n)
    *afs, l1 = hyena_filter_stage_a(n, fa_filt, *filter_params)
    pcs = pc.reshape(HY_ORDER + 1, 2, half, r, c)
    u, gu = pcs, 0
    for o in range(HY_ORDER):
        a = dft_stage_a(fa_data, u, gu)
        d = dft_stage_c_conv(a, afs[o], 1.0 / (l1[o] * ll), g_fwd, g_inv)
        z = dft_stage_a_inv(fa_inv, d, u, gu, pcs, o + 1, skip[o].reshape(1, c))
        u, gu = z[None], 0
    return z.reshape(b, n, c)


def _hyena_short_kernel(ff_ref, fr_ref, fi_ref, pc_ref, filt_ref, l1_ref, skip_ref, o_ref, *, n, ll):
    u = [pc_ref[0, 0], pc_ref[0, 1]]
    for o in range(HY_ORDER):
        hs = _dft_dot(fr_ref[...], filt_ref[o]) * (1.0 / (l1_ref[o] * ll))
        y = _dft_dot(ff_ref[...], jnp.concatenate(u, axis=0))
        yre, yim, hre, him = y[:ll], y[ll:], hs[:ll], hs[ll:]
        p = jnp.concatenate([yre * hre - yim * him, yre * him + yim * hre], axis=0)
        d = _dft_dot(fi_ref[...], p)
        sk = skip_ref[o:o + 1]
        u = [pc_ref[o + 1, bi] * (d[bi * n:(bi + 1) * n] + u[bi] * sk) for bi in range(2)]
    o_ref[0] = u[0].astype(o_ref.dtype)
    o_ref[1] = u[1].astype(o_ref.dtype)


def hyena_short(pc, filt, l1, skip, out_dtype):
    _, b, n, c = pc.shape
    assert b == 2
    ll = 2 * n
    two_pi = 2.0 * math.pi
    kk = jnp.arange(ll, dtype=jnp.int32)
    ang = ((kk[:, None] * kk[None, :]) % ll).astype(F32) * (two_pi / ll)
    cf, sf = jnp.cos(ang), jnp.sin(ang)
    f_real = jnp.concatenate([cf, -sf], 0)
    cn, sn = cf[:, :n], sf[:, :n]
    f_fwd = jnp.concatenate([jnp.concatenate([cn, sn], 1), jnp.concatenate([-sn, cn], 1)], 0)
    ci, si = cf[:n, :], sf[:n, :]
    f_inv = jnp.concatenate([jnp.concatenate([ci, -si], 1), jnp.concatenate([si, ci], 1)], 0)
    tc = DFT_CH_TILE
    const = lambda j: (0, 0)
    return pl.pallas_call(
        functools.partial(_hyena_short_kernel, n=n, ll=ll),
        grid=(c // tc,),
        in_specs=[pl.BlockSpec(f_fwd.shape, const),
                  pl.BlockSpec(f_real.shape, const),
                  pl.BlockSpec(f_inv.shape, const),
                  pl.BlockSpec((3, 2, n, tc), lambda j: (0, 0, 0, j)),
                  pl.BlockSpec((HY_ORDER, ll, tc), lambda j: (0, 0, j)),
                  pl.BlockSpec((HY_ORDER, 1, tc), lambda j: (0, 0, j)),
                  pl.BlockSpec((HY_ORDER, tc), lambda j: (0, j))],
        out_specs=pl.BlockSpec((2, n, tc), lambda j: (0, 0, j)),
        out_shape=jax.ShapeDtypeStruct((2, n, c), out_dtype),
        compiler_params=_cparams("arbitrary"),
        name="hyena_short",
    )(f_fwd, f_real, f_inv, pc, filt, l1, skip)


def hyena_mixer(pc, w1, b1, w2, b2, w3, freq, skip, out_dtype):
    n = pc.shape[2]
    if (2 * n) % (DFT_RADIX * BF16_SUBLANES) == 0 and n >= 1024:
        return hyena_long(pc, (w1, b1, w2, b2, w3, freq), skip)
    filt, l1 = hyena_filter_time(n, w1, b1, w2, b2, w3, freq)
    return hyena_short(pc, filt, l1, skip, out_dtype)


def _deinterleave_cols(w, hd):
    lead = w.shape[:-1]
    wh = w.reshape(lead + (w.shape[-1] // LANES, LANES // hd, hd // 2, 2))
    half = lead + (w.shape[-1] // LANES, LANES // 2)
    return jnp.concatenate([wh[..., 0].reshape(half), wh[..., 1].reshape(half)],
                           axis=-1).reshape(w.shape)


def _rope_tables(n_tok, rot_dim):
    rows = n_tok // GRID_W
    row = jnp.broadcast_to(jnp.arange(rows, dtype=jnp.int32)[:, None], (rows, GRID_W)).reshape(n_tok)
    col = jnp.broadcast_to(jnp.arange(GRID_W, dtype=jnp.int32)[None, :], (rows, GRID_W)).reshape(n_tok)
    axis_dim = rot_dim // 2
    inv_freq = ROPE_THETA ** (-jnp.arange(0, axis_dim, 2, dtype=F32) / axis_dim)
    ang = jnp.concatenate([row.astype(F32)[:, None] * inv_freq,
                           col.astype(F32)[:, None] * inv_freq], axis=-1)
    c, s = jnp.cos(ang), jnp.sin(ang)
    reps = LANES // rot_dim
    c, s = jnp.tile(c, (1, reps)), jnp.tile(s, (1, reps))
    return jnp.concatenate([c, c], -1), jnp.concatenate([-s, s], -1)


def _mixer_ab(h, hc, with_ctx, w_in, w_out, qk_g, hy):
    b, n, _ = h.shape
    conv_w, conv_b, w1, b1, w2, b2, w3, freq, skip = hy
    scale = A_HEAD_DIM ** -0.5 * LOG2E
    wq = _deinterleave_cols(w_in[:, :A_Q_W], A_HEAD_DIM).astype(BF16)
    wk = _deinterleave_cols(w_in[:, A_Q_W:A_Q_W + A_KV_W], A_HEAD_DIM).astype(BF16)
    wv = w_in[:, A_Q_W + A_KV_W:A_QKV_W].astype(BF16)
    wh = w_in[:, A_QKV_W:].astype(BF16)
    gq = jnp.tile(_deinterleave_cols(qk_g[0], A_HEAD_DIM) * scale, A_HEADS)
    gk = jnp.tile(_deinterleave_cols(qk_g[1], A_HEAD_DIM), A_KV_HEADS)
    ones_v = jnp.ones((A_KV_W,), F32)
    cosf, sinf = _rope_tables(n, A_HEAD_DIM)

    n_ctx = hc.shape[1]
    q = inproj(h, wq, gq, cosf, sinf, norm=True)
    k_all = inproj(hc, wk, gk, norm=True, append_at=n,
                   append_to=inproj(h, wk, gk, cosf, sinf, norm=True, tail_rows=n_ctx))
    v_all = inproj(hc, wv, ones_v, append_at=n, append_to=inproj(h, wv, ones_v, tail_rows=n_ctx))
    o_att = flash_attention(q, k_all, v_all, mode="gqa", kv_len=n + n_ctx)
    pc = inproj_conv(h, wh, conv_w, conv_b, HY_ORDER + 1)
    o_hy = hyena_mixer(pc, w1, b1, w2, b2, w3, freq, skip, BF16)
    wo = w_out.astype(BF16)
    lat = ([o_att, o_hy], [wo[:A_Q_W], wo[A_Q_W:]])
    if not with_ctx:
        return lat, None
    q_c = inproj(hc, wq, gq, norm=True)
    o_att_c = flash_attention(q_c, k_all, v_all, mode="gqa", kv_start=n, kv_len=n_ctx)
    pc_c = inproj_conv(hc, wh, conv_w, conv_b, HY_ORDER + 1)
    o_hy_c = hyena_mixer(pc_c, w1, b1, w2, b2, w3, freq, skip, BF16)
    return lat, ([o_att_c, o_hy_c], [wo[:A_Q_W], wo[A_Q_W:]])


def _mixer_c(h, hc, with_ctx, lambda_init, w_in, w_out, lam_vecs, subln_g):
    b, n, _ = h.shape
    scale = C_HEAD_DIM ** -0.5 * LOG2E
    wq = _deinterleave_cols(w_in[:, :C_Q_W], C_HEAD_DIM).astype(BF16)
    wk = _deinterleave_cols(w_in[:, C_Q_W:2 * C_Q_W], C_HEAD_DIM).astype(BF16)
    wv = w_in[:, 2 * C_Q_W:].astype(BF16)
    gq = jnp.full((C_Q_W,), scale, F32)
    ones = jnp.ones((C_Q_W,), F32)
    cosf, sinf = _rope_tables(n, C_HEAD_DIM)

    n_ctx = hc.shape[1]
    q = inproj(h, wq, gq, cosf, sinf)
    k_all = inproj(hc, wk, ones, append_at=n,
                   append_to=inproj(h, wk, ones, cosf, sinf, tail_rows=n_ctx))
    v_all = inproj(hc, wv, ones, append_at=n, append_to=inproj(h, wv, ones, tail_rows=n_ctx))
    attn = functools.partial(flash_attention, mode="diff", lam_vecs=lam_vecs, subln_g=subln_g,
                             lambda_init=lambda_init)
    wo = w_out.astype(BF16)
    lat = ([attn(q, k_all, v_all, kv_len=n + n_ctx)], [wo])
    if not with_ctx:
        return lat, None
    q_c = inproj(hc, wq, gq)
    return lat, ([attn(q_c, k_all, v_all, kv_start=n, kv_len=n_ctx)], [wo])


def kernel(x, c, ctx, c_ctx, ada_w, ada_b, norm_g, ab_w_in, ab_w_out, ab_qk_g,
           hy_conv_w, hy_conv_b, hy_w1, hy_b1, hy_w2, hy_b2, hy_w3, hy_freq, hy_skip,
           dc_w_in, dc_w_out, dc_lambda, dc_subln_g,
           ffn_w_up, ffn_conv_w, ffn_conv_b, ffn_w_down):
    depth = ada_w.shape[0]
    b, n, d = x.shape
    pad_rows = (-(b + 1)) % 8
    cvec = jnp.concatenate([c, c_ctx[None, :], jnp.zeros((pad_rows, d), F32)], axis=0)
    mod_all = ada_mod(cvec, ada_w, ada_b)
    w_up = ffn_w_up.astype(BF16)
    w_down = ffn_w_down.astype(BF16)
    for i in range(depth):
        j = i // 2
        with_ctx = i < depth - 1
        mod = mod_all[i, :b].reshape(b, 6, d)
        mod_c = jnp.broadcast_to(mod_all[i, b].reshape(1, 6, d), (b, 6, d))
        sh1, sc1, g1, sh2, sc2, g2 = (mod[:, t] for t in range(6))
        sh1c, sc1c, g1c, sh2c, sc2c, g2c = (mod_c[:, t] for t in range(6))
        h = norm_mod(x, norm_g[i, 0], sc1, sh1)
        hc = norm_mod(ctx, norm_g[i, 0], sc1c, sh1c)
        if i % 2 == 0:
            hy = (hy_conv_w[j], hy_conv_b[j], hy_w1[j], hy_b1[j], hy_w2[j], hy_b2[j],
                  hy_w3[j], hy_freq[j], hy_skip[j])
            lat, cx = _mixer_ab(h, hc, with_ctx, ab_w_in[j], ab_w_out[j], ab_qk_g[j], hy)
        else:
            lambda_init = 0.8 - 0.6 * math.exp(-0.3 * i)
            lat, cx = _mixer_c(h, hc, with_ctx, lambda_init, dc_w_in[j], dc_w_out[j],
                               dc_lambda[j], dc_subln_g[j])
        x = outproj_residual(lat[0], lat[1], x, norm_g[i, 1], g1)
        x = conv_ffn_residual(x, norm_g[i, 2], sc2, sh2, w_up, ffn_conv_w[i], ffn_conv_b[i],
                              w_down, norm_g[i, 3], g2, i)
        if with_ctx:
            ctx = outproj_residual(cx[0], cx[1], ctx, norm_g[i, 1], g1c)
            ctx = conv_ffn_residual(ctx, norm_g[i, 2], sc2c, sh2c, w_up, ffn_conv_w[i],
                                    ffn_conv_b[i], w_down, norm_g[i, 3], g2c, i)
    return x
```

```python
import functools
import math

import jax
import jax.numpy as jnp
from jax import lax
from jax.experimental import pallas as pl
from jax.experimental.pallas import tpu as pltpu

F32 = jnp.float32
BF16 = jnp.bfloat16

D_MODEL = 2048
GRID_W = 64
EPS = 1e-6
ROPE_THETA = 10000.0
A_HEADS, A_KV_HEADS, A_HEAD_DIM = 8, 2, 128
A_Q_W = A_HEADS * A_HEAD_DIM
A_KV_W = A_KV_HEADS * A_HEAD_DIM
A_QKV_W = A_Q_W + 2 * A_KV_W
HY_CH = D_MODEL // 2
HY_ORDER = 2
HY_FILTER_W = 64
HY_BANDS = 16
HY_FAST, HY_SLOW, HY_TARGET = 0.3, 1.5, 1e-2
C_HEADS, C_HEAD_DIM = 16, 64
C_Q_W = C_HEADS * 2 * C_HEAD_DIM

LANES = 128
BF16_SUBLANES = 16
VMEM_LIMIT = 56 * 1024 * 1024
DFT_RADIX = 128
FLASH_ROWS = 2048
FLASH_SUB_ROWS = 128
FLASH_MAX_KV = 16384
LOG2E = math.log2(math.e)

SUBLANES = 8
ROW_TILE = 512
ADA_COLS = 1024
PROJ_COLS = 2048
CONV_PROJ_COLS = 1024
FFN_CHUNK = 512
DFT_K1_CHUNK = 8
DFT_CH_TILE = 512


def _cparams(*sem):
    return pltpu.CompilerParams(dimension_semantics=sem, vmem_limit_bytes=VMEM_LIMIT)


def _rms(x, g):
    return x * lax.rsqrt(jnp.mean(x * x, axis=-1, keepdims=True) + EPS) * g


def _split_dot(a, b):
    a_hi, b_hi = a.astype(BF16), b.astype(BF16)
    a_lo = (a - a_hi.astype(F32)).astype(BF16)
    b_lo = (b - b_hi.astype(F32)).astype(BF16)
    dot = functools.partial(jnp.dot, preferred_element_type=F32)
    return dot(a_hi, b_hi) + dot(a_lo, b_hi) + dot(a_hi, b_lo)


def _ada_kernel(c_ref, w_ref, b_ref, o_ref):
    c = c_ref[...]
    s = c * jax.nn.sigmoid(c)
    rows = s.shape[0]
    s_hi = s.astype(BF16).astype(F32)
    s2 = jnp.concatenate([s_hi, s - s_hi], axis=0).astype(BF16)
    w = w_ref[0]
    w_hi = w.astype(BF16)
    w_lo = (w - w_hi.astype(F32)).astype(BF16)
    both = jnp.dot(s2, w_hi, preferred_element_type=F32)
    hi_lo = jnp.dot(s2, w_lo, preferred_element_type=F32)[:rows]
    o_ref[0] = both[:rows] + both[rows:] + hi_lo + b_ref[0]


def ada_mod(cvec, ada_w, ada_b):
    depth, d, n6 = ada_w.shape
    rows = cvec.shape[0]
    tn = ADA_COLS
    return pl.pallas_call(
        _ada_kernel,
        grid=(depth, n6 // tn),
        in_specs=[pl.BlockSpec((rows, d), lambda l, j: (0, 0)),
                  pl.BlockSpec((1, d, tn), lambda l, j: (l, 0, j)),
                  pl.BlockSpec((1, 1, tn), lambda l, j: (l, 0, j))],
        out_specs=pl.BlockSpec((1, rows, tn), lambda l, j: (l, 0, j)),
        out_shape=jax.ShapeDtypeStruct((depth, rows, n6), F32),
        compiler_params=_cparams("arbitrary", "arbitrary"),
        name="ada_mod",
    )(cvec, ada_w, ada_b.reshape(depth, 1, n6))


def _norm_mod_kernel(x_ref, g_ref, sc_ref, sh_ref, o_ref):
    y = _rms(x_ref[0], g_ref[...])
    o_ref[0] = (y * (1.0 + sc_ref[0]) + sh_ref[0]).astype(o_ref.dtype)


def norm_mod(x, g, sc, sh):
    b, n, d = x.shape
    tm = min(n, ROW_TILE)
    return pl.pallas_call(
        _norm_mod_kernel,
        grid=(b, n // tm),
        in_specs=[pl.BlockSpec((1, tm, d), lambda bi, i: (bi, i, 0)),
                  pl.BlockSpec((1, d), lambda bi, i: (0, 0)),
                  pl.BlockSpec((1, 1, d), lambda bi, i: (bi, 0, 0)),
                  pl.BlockSpec((1, 1, d), lambda bi, i: (bi, 0, 0))],
        out_specs=pl.BlockSpec((1, tm, d), lambda bi, i: (bi, i, 0)),
        out_shape=jax.ShapeDtypeStruct((b, n, d), BF16),
        compiler_params=_cparams("arbitrary", "arbitrary"),
        name="norm_mod",
    )(x, g.reshape(1, d), sc.reshape(b, 1, d), sh.reshape(b, 1, d))


def _inproj_kernel(*refs, norm, rope, appending, n_tiles, has_tail):
    o_ref = refs[-1]
    if appending:
        refs = refs[:-2]
    else:
        refs = refs[:-1]
    if rope:
        h_ref, w_ref, g_ref, cos_ref, sin_ref = refs
    else:
        h_ref, w_ref, g_ref = refs
    i = pl.program_id(1)

    def _tail():
        o_ref[0] = jnp.zeros(o_ref.shape[1:], o_ref.dtype)

    def _body():
        acc = jnp.dot(h_ref[0], w_ref[...], preferred_element_type=F32)
        tn = acc.shape[1]
        if rope:
            cosf = cos_ref[...]
            sinf = sin_ref[...]
        for c in range(tn // LANES):
            x = acc[:, c * LANES:(c + 1) * LANES]
            if norm:
                x = x * lax.rsqrt(jnp.mean(x * x, axis=-1, keepdims=True) + EPS)
            x = x * g_ref[:, c * LANES:(c + 1) * LANES]
            if rope:
                x = x * cosf + pltpu.roll(x, LANES // 2, 1) * sinf
            o_ref[0, :, c * LANES:(c + 1) * LANES] = x.astype(o_ref.dtype)

    if has_tail:
        pl.when(i >= n_tiles)(_tail)
        pl.when(i < n_tiles)(_body)
    else:
        _body()


def inproj(h, w, gvec, cosf=None, sinf=None, *, norm=False, out_dtype=BF16,
           tail_rows=0, append_to=None, append_at=0):
    b, n, d = h.shape
    ncol = w.shape[1]
    rope = cosf is not None
    tm = min(n, ROW_TILE)
    tn = min(ncol, PROJ_COLS)
    n_tiles = n // tm
    last = n_tiles - 1
    in_specs = [pl.BlockSpec((1, tm, d), lambda bi, i, j: (bi, jnp.minimum(i, last), 0)),
                pl.BlockSpec((d, tn), lambda bi, i, j: (0, j)),
                pl.BlockSpec((1, tn), lambda bi, i, j: (0, j))]
    args = [h, w, gvec.reshape(1, ncol)]
    if rope:
        in_specs += [pl.BlockSpec((tm, LANES), lambda bi, i, j: (jnp.minimum(i, last), 0)),
                     pl.BlockSpec((tm, LANES), lambda bi, i, j: (jnp.minimum(i, last), 0))]
        args += [cosf, sinf]
    appending = append_to is not None
    if appending:
        total = append_to.shape[1]
        first = append_at // tm
        assert first * tm == append_at and append_at + n <= total and append_to.dtype == out_dtype
        tail_tiles = 0
        in_specs.append(pl.BlockSpec(memory_space=pl.ANY))
        args.append(append_to)
        aliases = {len(args) - 1: 0}
    else:
        tail_tiles = -(-tail_rows // tm)
        total, first, aliases = (n_tiles + tail_tiles) * tm, 0, {}
    return pl.pallas_call(
        functools.partial(_inproj_kernel, norm=norm, rope=rope, appending=appending,
                          n_tiles=n_tiles, has_tail=tail_tiles > 0),
        grid=(b, n_tiles + tail_tiles, ncol // tn),
        in_specs=in_specs,
        out_specs=pl.BlockSpec((1, tm, tn), lambda bi, i, j: (bi, first + i, j)),
        out_shape=jax.ShapeDtypeStruct((b, total, ncol), out_dtype),
        input_output_aliases=aliases,
        compiler_params=_cparams("arbitrary", "arbitrary", "arbitrary"),
        name="inproj",
    )(*args)


def _inproj_conv_kernel(hm_ref, hp_ref, hn_ref, w_ref, cw_ref, cb_ref, o_ref, *, n_tiles):
    i = pl.program_id(2)
    hp, hm, hn = hp_ref[0], hm_ref[0], hn_ref[0]
    halo, tm = hp.shape[0], hm.shape[0]
    zero = jnp.zeros_like(hp)
    lhs = jnp.concatenate([jnp.where(i > 0, hp, zero), hm,
                           jnp.where(i < n_tiles - 1, hn, zero)], axis=0)
    u = jnp.dot(lhs, w_ref[...], preferred_element_type=F32)
    rows = tm + 2 * halo
    cw = cw_ref[...]
    conv = cw[0:1] * pltpu.roll(u, 1, 0) + cw[1:2] * u + cw[2:3] * pltpu.roll(u, rows - 1, 0)
    o_ref[0, 0] = conv[halo:halo + tm] + cb_ref[...]


def inproj_conv(h, w, conv_w, conv_b, groups):
    b, n, d = h.shape
    ncol = w.shape[1]
    c = ncol // groups
    tm = min(n, ROW_TILE)
    tn = min(c, CONV_PROJ_COLS)
    halo = BF16_SUBLANES
    nt = n // tm
    per = c // tn
    return pl.pallas_call(
        functools.partial(_inproj_conv_kernel, n_tiles=nt),
        grid=(ncol // tn, b, nt),
        in_specs=[pl.BlockSpec((1, tm, d), lambda j, bi, i: (bi, i, 0)),
                  pl.BlockSpec((1, halo, d),
                               lambda j, bi, i: (bi, jnp.maximum(i * (tm // halo) - 1, 0), 0)),
                  pl.BlockSpec((1, halo, d),
                               lambda j, bi, i: (bi, jnp.minimum((i + 1) * (tm // halo), n // halo - 1), 0)),
                  pl.BlockSpec((d, tn), lambda j, bi, i: (0, j)),
                  pl.BlockSpec((3, tn), lambda j, bi, i: (0, j)),
                  pl.BlockSpec((1, tn), lambda j, bi, i: (0, j))],
        out_specs=pl.BlockSpec((1, 1, tm, tn), lambda j, bi, i: (j // per, bi, i, j % per)),
        out_shape=jax.ShapeDtypeStruct((groups, b, n, c), F32),
        compiler_params=_cparams("arbitrary", "arbitrary", "arbitrary"),
        name="inproj_conv",
    )(h, h, h, w, conv_w, conv_b.reshape(1, ncol))


def _flash_kernel(*refs, mode, group, tq, lambda_init):
    if mode == "diff":
        q_ref, k_ref, v_ref, lam_ref, sg_ref, o_ref, qs_ref = refs
    else:
        q_ref, k_ref, v_ref, o_ref, qs_ref = refs
    q = q_ref[0]
    if mode == "diff":
        lane = lax.broadcasted_iota(jnp.int32, q.shape, 1)
        zero = jnp.zeros_like(q)
        first = (lane // (C_HEAD_DIM // 2)) % 2 == 0
        qs_ref[0:tq] = jnp.where(first, q, zero)
        qs_ref[tq:2 * tq] = jnp.where(first, zero, q)
    else:
        for g in range(group):
            qs_ref[g * tq:(g + 1) * tq] = q[:, g * LANES:(g + 1) * LANES]

    k = k_ref[0]
    v = v_ref[0]
    v_aug = jnp.concatenate([v, jnp.ones_like(v)], axis=1)
    sub = FLASH_SUB_ROWS

    def attend(c):
        s = lax.dot_general(qs_ref[c * sub:(c + 1) * sub], k, (((1,), (1,)), ((), ())),
                            preferred_element_type=F32)
        p = jnp.exp2(s - jnp.max(s, axis=-1, keepdims=True))
        pv = jnp.dot(p.astype(BF16), v_aug, preferred_element_type=F32)
        return pv[:, :LANES] / pv[:, LANES:]

    per_head = tq // sub
    if mode == "diff":
        lv = lam_ref[...]
        lam = (jnp.exp(jnp.sum(lv[0:1] * lv[1:2], axis=-1, keepdims=True))
               - jnp.exp(jnp.sum(lv[2:3] * lv[3:4], axis=-1, keepdims=True)) + lambda_init)
        for c in range(per_head):
            dlt = attend(c) - lam * attend(per_head + c)
            o_ref[0, c * sub:(c + 1) * sub, :] = (
                _rms(dlt, sg_ref[...]) * (1.0 - lambda_init)).astype(o_ref.dtype)
    else:
        for c in range(group * per_head):
            g, r = divmod(c, per_head)
            o_ref[0, r * sub:(r + 1) * sub, g * LANES:(g + 1) * LANES] = attend(c).astype(o_ref.dtype)


def flash_attention(q, k, v, *, mode, kv_len, lam_vecs=None, subln_g=None, lambda_init=0.0,
                    kv_start=0):
    b, n, qw = q.shape
    tk = kv_len
    assert tk <= FLASH_MAX_KV and tk % LANES == 0 and kv_start % tk == 0
    if mode == "diff":
        group, n_groups, qblk = 2, qw // LANES, LANES
    else:
        group = A_HEADS // A_KV_HEADS
        n_groups, qblk = A_KV_HEADS, group * LANES
    tq = min(n, FLASH_ROWS // group)
    assert tq % FLASH_SUB_ROWS == 0
    j0 = kv_start // tk
    in_specs = [pl.BlockSpec((1, tq, qblk), lambda bi, g, i: (bi, i, g)),
                pl.BlockSpec((1, tk, LANES), lambda bi, g, i: (bi, j0, g)),
                pl.BlockSpec((1, tk, LANES), lambda bi, g, i: (bi, j0, g))]
    args = [q, k, v]
    if mode == "diff":
        in_specs += [pl.BlockSpec(lam_vecs.shape, lambda bi, g, i: (0, 0)),
                     pl.BlockSpec((1, LANES), lambda bi, g, i: (0, 0))]
        args += [lam_vecs, subln_g.reshape(1, LANES)]
    return pl.pallas_call(
        functools.partial(_flash_kernel, mode=mode, group=group, tq=tq, lambda_init=lambda_init),
        grid=(b, n_groups, n // tq),
        in_specs=in_specs,
        out_specs=pl.BlockSpec((1, tq, qblk), lambda bi, g, i: (bi, i, g)),
        out_shape=jax.ShapeDtypeStruct((b, n, qw), BF16),
        scratch_shapes=[pltpu.VMEM((group * tq, LANES), BF16)],
        compiler_params=_cparams("arbitrary", "arbitrary", "arbitrary"),
        name="flash_" + mode,
    )(*args)


def _outproj_kernel(*refs, n_lhs):
    lhs = refs[:n_lhs]
    ws = refs[n_lhs:2 * n_lhs]
    x_ref, g_ref, gate_ref, o_ref = refs[2 * n_lhs:]
    y = jnp.dot(lhs[0][0].astype(BF16), ws[0][...], preferred_element_type=F32)
    for a, w in zip(lhs[1:], ws[1:]):
        y = y + jnp.dot(a[0].astype(BF16), w[...], preferred_element_type=F32)
    o_ref[0] = x_ref[0] + gate_ref[0] * _rms(y, g_ref[...])


def outproj_residual(lhs_list, w_list, x, g, gate):
    b, n, d = x.shape
    tm = min(n, ROW_TILE)
    n_lhs = len(lhs_list)
    in_specs = [pl.BlockSpec((1, tm, a.shape[2]), lambda bi, i: (bi, i, 0)) for a in lhs_list]
    in_specs += [pl.BlockSpec(w.shape, lambda bi, i: (0, 0)) for w in w_list]
    in_specs += [pl.BlockSpec((1, tm, d), lambda bi, i: (bi, i, 0)),
                 pl.BlockSpec((1, d), lambda bi, i: (0, 0)),
                 pl.BlockSpec((1, 1, d), lambda bi, i: (bi, 0, 0))]
    return pl.pallas_call(
        functools.partial(_outproj_kernel, n_lhs=n_lhs),
        grid=(b, n // tm),
        in_specs=in_specs,
        out_specs=pl.BlockSpec((1, tm, d), lambda bi, i: (bi, i, 0)),
        out_shape=jax.ShapeDtypeStruct((b, n, d), F32),
        compiler_params=_cparams("arbitrary", "arbitrary"),
        name="outproj",
    )(*lhs_list, *w_list, x, g.reshape(1, d), gate.reshape(b, 1, d))


def _ffn_kernel(xm_ref, xp_ref, xn_ref, gn_ref, sc_ref, sh_ref, wa_ref, wg_ref, cwa_ref, cwg_ref,
                cba_ref, cbg_ref, wd_ref, g3_ref, gate_ref, o_ref, hx_ref,
                *, tm, halo, n_tiles, n_f):
    i = pl.program_id(1)
    j = pl.program_id(2)

    @pl.when(j == 0)
    def _prologue():
        def nm(x):
            return (_rms(x, gn_ref[...]) * (1.0 + sc_ref[0]) + sh_ref[0]).astype(BF16)
        hx_ref[halo:halo + tm] = nm(xm_ref[0])
        zero = jnp.zeros((halo, xm_ref.shape[2]), BF16)
        hx_ref[0:halo] = jnp.where(i > 0, nm(xp_ref[0]), zero)
        hx_ref[halo + tm:] = jnp.where(i < n_tiles - 1, nm(xn_ref[0]), zero)
        o_ref[0] = jnp.zeros(o_ref.shape[1:], F32)

    hx = hx_ref[...]
    rows = tm + 2 * halo

    def conv(u, cw, cb):
        up = pltpu.roll(u, 1, 0)
        dn = pltpu.roll(u, rows - 1, 0)
        return (cw[0:1] * up + cw[1:2] * u + cw[2:3] * dn + cb)[halo:halo + tm]

    a = conv(jnp.dot(hx, wa_ref[...], preferred_element_type=F32), cwa_ref[...], cba_ref[...])
    g = conv(jnp.dot(hx, wg_ref[...], preferred_element_type=F32), cwg_ref[...], cbg_ref[...])
    act = (a * (g * jax.nn.sigmoid(g))).astype(BF16)
    o_ref[0] += jnp.dot(act, wd_ref[...], preferred_element_type=F32)

    @pl.when(j == n_f - 1)
    def _epilogue():
        o_ref[0] = xm_ref[0] + gate_ref[0] * _rms(o_ref[0], g3_ref[...])


def conv_ffn_residual(x, gn, sc, sh, w_up, conv_w, conv_b, w_down, g3, gate, layer):
    b, n, d = x.shape
    dff = w_down.shape[1]
    tm = min(n, ROW_TILE)
    tf = FFN_CHUNK
    halo = BF16_SUBLANES
    nt = n // tm
    n_f = dff // tf
    row = lambda bi, i, j: (bi, i, 0)
    vec = lambda bi, i, j: (0, 0)
    bvec = lambda bi, i, j: (bi, 0, 0)
    up_a = lambda bi, i, j: (0, j)
    up_g = lambda bi, i, j: (0, n_f + j)
    wup_a = lambda bi, i, j: (layer, 0, j)
    wup_g = lambda bi, i, j: (layer, 0, n_f + j)
    conv_b = conv_b.reshape(1, 2 * dff)
    return pl.pallas_call(
        functools.partial(_ffn_kernel, tm=tm, halo=halo, n_tiles=nt, n_f=n_f),
        grid=(b, nt, n_f),
        in_specs=[pl.BlockSpec((1, tm, d), row),
                  pl.BlockSpec((1, halo, d),
                               lambda bi, i, j: (bi, jnp.maximum(i * (tm // halo) - 1, 0), 0)),
                  pl.BlockSpec((1, halo, d),
                               lambda bi, i, j: (bi, jnp.minimum((i + 1) * (tm // halo), n // halo - 1), 0)),
                  pl.BlockSpec((1, d), vec),
                  pl.BlockSpec((1, 1, d), bvec),
                  pl.BlockSpec((1, 1, d), bvec),
                  pl.BlockSpec((None, d, tf), wup_a),
                  pl.BlockSpec((None, d, tf), wup_g),
                  pl.BlockSpec((3, tf), up_a),
                  pl.BlockSpec((3, tf), up_g),
                  pl.BlockSpec((1, tf), up_a),
                  pl.BlockSpec((1, tf), up_g),
                  pl.BlockSpec((None, tf, d), lambda bi, i, j: (layer, j, 0)),
                  pl.BlockSpec((1, d), vec),
                  pl.BlockSpec((1, 1, d), bvec)],
        out_specs=pl.BlockSpec((1, tm, d), row),
        out_shape=jax.ShapeDtypeStruct((b, n, d), F32),
        scratch_shapes=[pltpu.VMEM((tm + 2 * halo, d), BF16)],
        compiler_params=_cparams("arbitrary", "arbitrary", "arbitrary"),
        name="conv_ffn",
    )(x, x, x, gn.reshape(1, d), sc.reshape(b, 1, d), sh.reshape(b, 1, d), w_up, w_up,
      conv_w, conv_w, conv_b, conv_b, w_down, g3.reshape(1, d), gate.reshape(b, 1, d))


def _filter_rows(r, bands_ref, w1_ref, b1_ref, w2_ref, b2_ref, fr_ref, dl_ref, n_tok):
    t = jnp.where(r < n_tok, r, 2.0 * n_tok - r)
    t_norm = t / float(max(n_tok - 1, 1))
    wang = (2.0 * math.pi / n_tok) * t
    z = wang * bands_ref[...]
    lane = lax.broadcasted_iota(jnp.int32, z.shape, 1)
    feats = jnp.where(lane == 0, t_norm,
                      jnp.where(lane <= HY_BANDS, jnp.cos(z),
                                jnp.where(lane <= 2 * HY_BANDS, -jnp.sin(z), 0.0)))
    fr = fr_ref[...]
    h = jnp.sin(fr[0:1] * (_split_dot(feats, w1_ref[...]) + b1_ref[...]))
    h = jnp.sin(fr[1:2] * (_split_dot(h, w2_ref[...]) + b2_ref[...]))
    decay = jnp.where(r == n_tok, 0.0, jnp.exp(-t_norm * dl_ref[...]))
    return h, decay


def _filter_kernel(bands_ref, w1_ref, b1_ref, w2_ref, b2_ref, w3_ref, fr_ref, dl_ref,
                   o_ref, s_ref, *, n_tok, tr):
    i = pl.program_id(0)

    @pl.when(i == 0)
    def _init():
        s_ref[...] = jnp.zeros(s_ref.shape, F32)

    r = (i * tr + lax.broadcasted_iota(jnp.int32, (tr, 1), 0)).astype(F32)
    h, decay = _filter_rows(r, bands_ref, w1_ref, b1_ref, w2_ref, b2_ref, fr_ref, dl_ref, n_tok)
    hb = h.astype(BF16)
    for o in range(HY_ORDER):
        out = jnp.dot(hb, w3_ref[o].astype(BF16), preferred_element_type=F32) * decay
        o_ref[o] = out
        s_ref[o] += jnp.sum(jnp.abs(out), axis=0, keepdims=True)


def _filter_stage_a_kernel(bands_ref, w1_ref, b1_ref, w2_ref, b2_ref, w3_ref, fr_ref, dl_ref,
                           f_ref, *out_refs, n_tok, sub):
    o_refs, s_ref = out_refs[:HY_ORDER], out_refs[HY_ORDER]
    j = pl.program_id(0)

    @pl.when(j == 0)
    def _init():
        s_ref[...] = jnp.zeros(s_ref.shape, F32)

    k1n = o_refs[0].shape[1]
    half = k1n // 2
    c = dl_ref.shape[1]
    n1 = lax.broadcasted_iota(jnp.int32, (k1n, 1), 0)
    f = f_ref[...]
    for q in range(sub):
        r = (n1 * DFT_RADIX + (j * sub + q)).astype(F32)
        h, decay = _filter_rows(r, bands_ref, w1_ref, b1_ref, w2_ref, b2_ref, fr_ref, dl_ref, n_tok)
        h_fwd, h_bwd = h[:half].astype(BF16), h[half:].astype(BF16)
        for o in range(HY_ORDER):
            w3 = w3_ref[o].astype(BF16)
            h3 = jnp.concatenate([jnp.dot(h_fwd, w3[:, :c], preferred_element_type=F32),
                                  jnp.dot(h_bwd, w3[:, c:], preferred_element_type=F32)],
                                 axis=0) * decay
            s_ref[o] += jnp.sum(jnp.abs(h3), axis=0, keepdims=True)
            y = _dft_dot(f, h3)
            o_refs[o][0, :, q, :] = y[:k1n]
            o_refs[o][1, :, q, :] = y[k1n:]


def _filter_operands(w1, b1, w2, b2, w3, freq):
    ch = HY_CH
    lane = jnp.arange(LANES)
    bands = jnp.linspace(1e-4, HY_BANDS - 1, HY_BANDS, dtype=F32)
    bands_l = jnp.where((lane >= 1) & (lane <= HY_BANDS), bands[jnp.clip(lane - 1, 0, HY_BANDS - 1)],
                        jnp.where((lane > HY_BANDS) & (lane <= 2 * HY_BANDS),
                                  bands[jnp.clip(lane - 1 - HY_BANDS, 0, HY_BANDS - 1)], 0.0))
    w1p = jnp.zeros((LANES, HY_FILTER_W), F32).at[:w1.shape[0]].set(w1)
    deltas = jnp.abs(jnp.linspace(math.log(HY_TARGET) / HY_FAST, math.log(HY_TARGET) / HY_SLOW,
                                  ch, dtype=F32))
    w3o = jnp.swapaxes(w3.reshape(HY_FILTER_W, HY_ORDER, 2 * ch), 0, 1)
    return (bands_l.reshape(1, LANES), w1p, b1.reshape(1, -1), w2, b2.reshape(1, -1), w3o, freq,
            deltas.reshape(1, ch))


def _filter_in_specs(w3_spec):
    const = lambda i: (0, 0)
    return [pl.BlockSpec((1, LANES), const),
            pl.BlockSpec((LANES, HY_FILTER_W), const),
            pl.BlockSpec((1, HY_FILTER_W), const),
            pl.BlockSpec((HY_FILTER_W, HY_FILTER_W), const),
            pl.BlockSpec((1, HY_FILTER_W), const),
            w3_spec,
            pl.BlockSpec((2, HY_FILTER_W), const),
            pl.BlockSpec((1, HY_CH), const)]


def hyena_filter_time(n_tok, w1, b1, w2, b2, w3, freq):
    ch = HY_CH
    ll = 2 * n_tok
    tr = min(n_tok, ROW_TILE)
    per_half = n_tok // tr
    return pl.pallas_call(
        functools.partial(_filter_kernel, n_tok=n_tok, tr=tr),
        grid=(ll // tr,),
        in_specs=_filter_in_specs(
            pl.BlockSpec((HY_ORDER, HY_FILTER_W, ch), lambda i: (0, 0, i // per_half))),
        out_specs=[pl.BlockSpec((HY_ORDER, tr, ch), lambda i: (0, i, 0)),
                   pl.BlockSpec((HY_ORDER, 1, ch), lambda i: (0, 0, 0))],
        out_shape=[jax.ShapeDtypeStruct((HY_ORDER, ll, ch), F32),
                   jax.ShapeDtypeStruct((HY_ORDER, 1, ch), F32)],
        compiler_params=_cparams("arbitrary"),
        name="hyena_filter",
    )(*_filter_operands(w1, b1, w2, b2, w3, freq))


def hyena_filter_stage_a(n_tok, f_filt, w1, b1, w2, b2, w3, freq):
    ch = HY_CH
    r = DFT_RADIX
    k1n = 2 * n_tok // r
    sub = SUBLANES
    return pl.pallas_call(
        functools.partial(_filter_stage_a_kernel, n_tok=n_tok, sub=sub),
        grid=(r // sub,),
        in_specs=_filter_in_specs(
            pl.BlockSpec((HY_ORDER, HY_FILTER_W, 2 * ch), lambda j: (0, 0, 0)))
        + [pl.BlockSpec(f_filt.shape, lambda j: (0, 0))],
        out_specs=[pl.BlockSpec((2, k1n, sub, ch), lambda j: (0, 0, j, 0))] * HY_ORDER
        + [pl.BlockSpec((HY_ORDER, 1, ch), lambda j: (0, 0, 0))],
        out_shape=[jax.ShapeDtypeStruct((2, k1n, r, ch), F32)] * HY_ORDER
        + [jax.ShapeDtypeStruct((HY_ORDER, 1, ch), F32)],
        compiler_params=_cparams("arbitrary"),
        name="hyena_filter_a",
    )(*_filter_operands(w1, b1, w2, b2, w3, freq), f_filt)


def _dft_dot(f, x):
    return jnp.dot(f.astype(BF16), x.astype(BF16), preferred_element_type=F32)


def _stage_a_kernel(f_ref, x_ref, o_ref):
    parts = x_ref.shape[0]
    f = f_ref[...]
    for r in range(x_ref.shape[2]):
        z = jnp.concatenate([x_ref[p, :, r, :] for p in range(parts)], axis=0)
        y = _dft_dot(f, z)
        half = y.shape[0] // 2
        o_ref[0, :, r, :] = y[:half]
        o_ref[1, :, r, :] = y[half:]


def dft_stage_a(f, x, g):
    _, p, rows, r, c = x.shape
    k1 = p * rows
    sub = SUBLANES
    return pl.pallas_call(
        _stage_a_kernel,
        grid=(r // sub,),
        in_specs=[pl.BlockSpec(f.shape, lambda j: (0, 0)),
                  pl.BlockSpec((None, p, rows, sub, c), lambda j: (g, 0, 0, j, 0))],
        out_specs=pl.BlockSpec((2, k1, sub, c), lambda j: (0, 0, j, 0)),
        out_shape=jax.ShapeDtypeStruct((2, k1, r, c), F32),
        compiler_params=_cparams("arbitrary"),
        name="dft_stage_a",
    )(f, x)


def _stage_c_conv_kernel(a_ref, af_ref, sc_ref, g_ref, gi_ref, o_ref, *, kc):
    tc = a_ref.shape[3]
    for k in range(kc):
        r = jnp.concatenate([jnp.concatenate([a_ref[0, k], af_ref[0, k]], axis=1),
                             jnp.concatenate([a_ref[1, k], af_ref[1, k]], axis=1)], axis=0)
        y = _dft_dot(g_ref[k], r)
        half = y.shape[0] // 2
        yre, yim = y[:half, :tc], y[half:, :tc]
        hre, him = y[:half, tc:] * sc_ref[...], y[half:, tc:] * sc_ref[...]
        p = jnp.concatenate([yre * hre - yim * him, yre * him + yim * hre], axis=0)
        d = _dft_dot(gi_ref[k], p)
        o_ref[0, k] = d[:half]
        o_ref[1, k] = d[half:]


def dft_stage_c_conv(a, af, scale, g, gi):
    _, k1, r, c = a.shape
    kc = DFT_K1_CHUNK
    tc = DFT_CH_TILE
    blk = pl.BlockSpec((2, kc, r, tc), lambda i, j: (0, i, 0, j))
    mat = pl.BlockSpec((kc, 2 * r, 2 * r), lambda i, j: (i, 0, 0))
    return pl.pallas_call(
        functools.partial(_stage_c_conv_kernel, kc=kc),
        grid=(k1 // kc, c // tc),
        in_specs=[blk, blk, pl.BlockSpec((1, tc), lambda i, j: (0, j)), mat, mat],
        out_specs=blk,
        out_shape=jax.ShapeDtypeStruct(a.shape, F32),
        compiler_params=_cparams("arbitrary", "arbitrary"),
        name="dft_stage_c_conv",
    )(a, af, scale, g, gi)


def _stage_a_inv_kernel(f_ref, d_ref, u_ref, gt_ref, sk_ref, o_ref):
    f = f_ref[...]
    sk = sk_ref[...]
    for r in range(d_ref.shape[2]):
        d = jnp.concatenate([d_ref[0, :, r, :], d_ref[1, :, r, :]], axis=0)
        y = _dft_dot(f, d)
        half = y.shape[0] // 2
        for bi in range(2):
            yb = y[bi * half:(bi + 1) * half]
            o_ref[bi, :, r, :] = gt_ref[bi, :, r, :] * (yb + u_ref[bi, :, r, :] * sk)


def dft_stage_a_inv(f, d, u, gu, gate, gg, skip):
    _, k1, r, c = d.shape
    half = k1 // 2
    sub = SUBLANES
    io = pl.BlockSpec((2, half, sub, c), lambda j: (0, 0, j, 0))
    return pl.pallas_call(
        _stage_a_inv_kernel,
        grid=(r // sub,),
        in_specs=[pl.BlockSpec(f.shape, lambda j: (0, 0)),
                  pl.BlockSpec((2, k1, sub, c), lambda j: (0, 0, j, 0)),
                  pl.BlockSpec((None, 2, half, sub, c), lambda j: (gu, 0, 0, j, 0)),
                  pl.BlockSpec((None, 2, half, sub, c), lambda j: (gg, 0, 0, j, 0)),
                  pl.BlockSpec((1, c), lambda j: (0, 0))],
        out_specs=io,
        out_shape=jax.ShapeDtypeStruct((2, half, r, c), F32),
        compiler_params=_cparams("arbitrary"),
        name="dft_stage_a_inv",
    )(f, d, u, gate, skip)


def _dft_tables(n_tok):
    ll = 2 * n_tok
    r = DFT_RADIX
    k1n = ll // r
    half = k1n // 2
    two_pi = 2.0 * math.pi

    def cs(m, period):
        ang = (m % period).astype(F32) * (two_pi / period)
        return jnp.cos(ang), jnp.sin(ang)

    k1 = jnp.arange(k1n, dtype=jnp.int32)
    c, s = cs(k1[:, None] * k1[None, :half], k1n)
    fa_data = jnp.concatenate([jnp.concatenate([c, s], 1), jnp.concatenate([-s, c], 1)], 0)
    c, s = cs(k1[:, None] * k1[None, :], k1n)
    fa_filt = jnp.concatenate([c, -s], 0)
    c, s = cs(k1[:half, None] * k1[None, :], k1n)
    fa_inv = jnp.concatenate([jnp.concatenate([c, -s], 1), jnp.concatenate([s, c], 1)], 0)
    idx = jnp.arange(r, dtype=jnp.int32)
    ca, sa = cs(idx[:, None] * idx[None, :], r)
    cb, sb = cs(k1[:, None] * idx[None, :], ll)
    c = ca[None] * cb[:, None, :] - sa[None] * sb[:, None, :]
    s = sa[None] * cb[:, None, :] + ca[None] * sb[:, None, :]
    g_fwd = jnp.concatenate([jnp.concatenate([c, s], 2), jnp.concatenate([-s, c], 2)], 1)
    ct, st = jnp.swapaxes(c, 1, 2), jnp.swapaxes(s, 1, 2)
    g_inv = jnp.concatenate([jnp.concatenate([ct, -st], 2), jnp.concatenate([st, ct], 2)], 1)
    return tuple(t.astype(BF16) for t in (fa_data, fa_filt, fa_inv, g_fwd, g_inv))


def hyena_long(pc, filter_params, skip):
    _, b, n, c = pc.shape
    assert b == 2, "the two batch entries ride as real/imaginary parts"
    ll = 2 * n
    r = DFT_RADIX
    half = ll // r // 2
    fa_data, fa_filt, fa_inv, g_fwd, g_inv = _dft_tables(n)
    *afs, l1 = hyena_filter_stage_a(n, fa_filt, *filter_params)
    pcs = pc.reshape(HY_ORDER + 1, 2, half, r, c)
    u, gu = pcs, 0
    for o in range(HY_ORDER):
        a = dft_stage_a(fa_data, u, gu)
        d = dft_stage_c_conv(a, afs[o], 1.0 / (l1[o] * ll), g_fwd, g_inv)
        z = dft_stage_a_inv(fa_inv, d, u, gu, pcs, o + 1, skip[o].reshape(1, c))
        u, gu = z[None], 0
    return z.reshape(b, n, c)


def _hyena_short_kernel(ff_ref, fr_ref, fi_ref, pc_ref, filt_ref, l1_ref, skip_ref, o_ref, *, n, ll):
    u = [pc_ref[0, 0], pc_ref[0, 1]]
    for o in range(HY_ORDER):
        hs = _dft_dot(fr_ref[...], filt_ref[o]) * (1.0 / (l1_ref[o] * ll))
        y = _dft_dot(ff_ref[...], jnp.concatenate(u, axis=0))
        yre, yim, hre, him = y[:ll], y[ll:], hs[:ll], hs[ll:]
        p = jnp.concatenate([yre * hre - yim * him, yre * him + yim * hre], axis=0)
        d = _dft_dot(fi_ref[...], p)
        sk = skip_ref[o:o + 1]
        u = [pc_ref[o + 1, bi] * (d[bi * n:(bi + 1) * n] + u[bi] * sk) for bi in range(2)]
    o_ref[0] = u[0].astype(o_ref.dtype)
    o_ref[1] = u[1].astype(o_ref.dtype)


def hyena_short(pc, filt, l1, skip, out_dtype):
    _, b, n, c = pc.shape
    assert b == 2
    ll = 2 * n
    two_pi = 2.0 * math.pi
    kk = jnp.arange(ll, dtype=jnp.int32)
    ang = ((kk[:, None] * kk[None, :]) % ll).astype(F32) * (two_pi / ll)
    cf, sf = jnp.cos(ang), jnp.sin(ang)
    f_real = jnp.concatenate([cf, -sf], 0)
    cn, sn = cf[:, :n], sf[:, :n]
    f_fwd = jnp.concatenate([jnp.concatenate([cn, sn], 1), jnp.concatenate([-sn, cn], 1)], 0)
    ci, si = cf[:n, :], sf[:n, :]
    f_inv = jnp.concatenate([jnp.concatenate([ci, -si], 1), jnp.concatenate([si, ci], 1)], 0)
    tc = DFT_CH_TILE
    const = lambda j: (0, 0)
    return pl.pallas_call(
        functools.partial(_hyena_short_kernel, n=n, ll=ll),
        grid=(c // tc,),
        in_specs=[pl.BlockSpec(f_fwd.shape, const),
                  pl.BlockSpec(f_real.shape, const),
                  pl.BlockSpec(f_inv.shape, const),
                  pl.BlockSpec((3, 2, n, tc), lambda j: (0, 0, 0, j)),
                  pl.BlockSpec((HY_ORDER, ll, tc), lambda j: (0, 0, j)),
                  pl.BlockSpec((HY_ORDER, 1, tc), lambda j: (0, 0, j)),
                  pl.BlockSpec((HY_ORDER, tc), lambda j: (0, j))],
        out_specs=pl.BlockSpec((2, n, tc), lambda j: (0, 0, j)),
        out_shape=jax.ShapeDtypeStruct((2, n, c), out_dtype),
        compiler_params=_cparams("arbitrary"),
        name="hyena_short",
    )(f_fwd, f_real, f_inv, pc, filt, l1, skip)


def hyena_mixer(pc, w1, b1, w2, b2, w3, freq, skip, out_dtype):
    n = pc.shape[2]
    if (2 * n) % (DFT_RADIX * BF16_SUBLANES) == 0 and n >= 1024:
        return hyena_long(pc, (w1, b1, w2, b2, w3, freq), skip)
    filt, l1 = hyena_filter_time(n, w1, b1, w2, b2, w3, freq)
    return hyena_short(pc, filt, l1, skip, out_dtype)


def _deinterleave_cols(w, hd):
    lead = w.shape[:-1]
    wh = w.reshape(lead + (w.shape[-1] // LANES, LANES // hd, hd // 2, 2))
    half = lead + (w.shape[-1] // LANES, LANES // 2)
    return jnp.concatenate([wh[..., 0].reshape(half), wh[..., 1].reshape(half)],
                           axis=-1).reshape(w.shape)


def _rope_tables(n_tok, rot_dim):
    rows = n_tok // GRID_W
    row = jnp.broadcast_to(jnp.arange(rows, dtype=jnp.int32)[:, None], (rows, GRID_W)).reshape(n_tok)
    col = jnp.broadcast_to(jnp.arange(GRID_W, dtype=jnp.int32)[None, :], (rows, GRID_W)).reshape(n_tok)
    axis_dim = rot_dim // 2
    inv_freq = ROPE_THETA ** (-jnp.arange(0, axis_dim, 2, dtype=F32) / axis_dim)
    ang = jnp.concatenate([row.astype(F32)[:, None] * inv_freq,
                           col.astype(F32)[:, None] * inv_freq], axis=-1)
    c, s = jnp.cos(ang), jnp.sin(ang)
    reps = LANES // rot_dim
    c, s = jnp.tile(c, (1, reps)), jnp.tile(s, (1, reps))
    return jnp.concatenate([c, c], -1), jnp.concatenate([-s, s], -1)


def _mixer_ab(h, hc, with_ctx, w_in, w_out, qk_g, hy):
    b, n, _ = h.shape
    conv_w, conv_b, w1, b1, w2, b2, w3, freq, skip = hy
    scale = A_HEAD_DIM ** -0.5 * LOG2E
    wq = _deinterleave_cols(w_in[:, :A_Q_W], A_HEAD_DIM).astype(BF16)
    wk = _deinterleave_cols(w_in[:, A_Q_W:A_Q_W + A_KV_W], A_HEAD_DIM).astype(BF16)
    wv = w_in[:, A_Q_W + A_KV_W:A_QKV_W].astype(BF16)
    wh = w_in[:, A_QKV_W:].astype(BF16)
    gq = jnp.tile(_deinterleave_cols(qk_g[0], A_HEAD_DIM) * scale, A_HEADS)
    gk = jnp.tile(_deinterleave_cols(qk_g[1], A_HEAD_DIM), A_KV_HEADS)
    ones_v = jnp.ones((A_KV_W,), F32)
    cosf, sinf = _rope_tables(n, A_HEAD_DIM)

    n_ctx = hc.shape[1]
    q = inproj(h, wq, gq, cosf, sinf, norm=True)
    k_all = inproj(hc, wk, gk, norm=True, append_at=n,
                   append_to=inproj(h, wk, gk, cosf, sinf, norm=True, tail_rows=n_ctx))
    v_all = inproj(hc, wv, ones_v, append_at=n, append_to=inproj(h, wv, ones_v, tail_rows=n_ctx))
    o_att = flash_attention(q, k_all, v_all, mode="gqa", kv_len=n + n_ctx)
    pc = inproj_conv(h, wh, conv_w, conv_b, HY_ORDER + 1)
    o_hy = hyena_mixer(pc, w1, b1, w2, b2, w3, freq, skip, BF16)
    wo = w_out.astype(BF16)
    lat = ([o_att, o_hy], [wo[:A_Q_W], wo[A_Q_W:]])
    if not with_ctx:
        return lat, None
    q_c = inproj(hc, wq, gq, norm=True)
    o_att_c = flash_attention(q_c, k_all, v_all, mode="gqa", kv_start=n, kv_len=n_ctx)
    pc_c = inproj_conv(hc, wh, conv_w, conv_b, HY_ORDER + 1)
    o_hy_c = hyena_mixer(pc_c, w1, b1, w2, b2, w3, freq, skip, BF16)
    return lat, ([o_att_c, o_hy_c], [wo[:A_Q_W], wo[A_Q_W:]])


def _mixer_c(h, hc, with_ctx, lambda_init, w_in, w_out, lam_vecs, subln_g):
    b, n, _ = h.shape
    scale = C_HEAD_DIM ** -0.5 * LOG2E
    wq = _deinterleave_cols(w_in[:, :C_Q_W], C_HEAD_DIM).astype(BF16)
    wk = _deinterleave_cols(w_in[:, C_Q_W:2 * C_Q_W], C_HEAD_DIM).astype(BF16)
    wv = w_in[:, 2 * C_Q_W:].astype(BF16)
    gq = jnp.full((C_Q_W,), scale, F32)
    ones = jnp.ones((C_Q_W,), F32)
    cosf, sinf = _rope_tables(n, C_HEAD_DIM)

    n_ctx = hc.shape[1]
    q = inproj(h, wq, gq, cosf, sinf)
    k_all = inproj(hc, wk, ones, append_at=n,
                   append_to=inproj(h, wk, ones, cosf, sinf, tail_rows=n_ctx))
    v_all = inproj(hc, wv, ones, append_at=n, append_to=inproj(h, wv, ones, tail_rows=n_ctx))
    attn = functools.partial(flash_attention, mode="diff", lam_vecs=lam_vecs, subln_g=subln_g,
                             lambda_init=lambda_init)
    wo = w_out.astype(BF16)
    lat = ([attn(q, k_all, v_all, kv_len=n + n_ctx)], [wo])
    if not with_ctx:
        return lat, None
    q_c = inproj(hc, wq, gq)
    return lat, ([attn(q_c, k_all, v_all, kv_start=n, kv_len=n_ctx)], [wo])


def kernel(x, c, ctx, c_ctx, ada_w, ada_b, norm_g, ab_w_in, ab_w_out, ab_qk_g,
           hy_conv_w, hy_conv_b, hy_w1, hy_b1, hy_w2, hy_b2, hy_w3, hy_freq, hy_skip,
           dc_w_in, dc_w_out, dc_lambda, dc_subln_g,
           ffn_w_up, ffn_conv_w, ffn_conv_b, ffn_w_down):
    depth = ada_w.shape[0]
    b, n, d = x.shape
    pad_rows = (-(b + 1)) % 8
    cvec = jnp.concatenate([c, c_ctx[None, :], jnp.zeros((pad_rows, d), F32)], axis=0)
    mod_all = ada_mod(cvec, ada_w, ada_b)
    w_up = ffn_w_up.astype(BF16)
    w_down = ffn_w_down.astype(BF16)
    for i in range(depth):
        j = i // 2
        with_ctx = i < depth - 1
        mod = mod_all[i, :b].reshape(b, 6, d)
        mod_c = jnp.broadcast_to(mod_all[i, b].reshape(1, 6, d), (b, 6, d))
        sh1, sc1, g1, sh2, sc2, g2 = (mod[:, t] for t in range(6))
        sh1c, sc1c, g1c, sh2c, sc2c, g2c = (mod_c[:, t] for t in range(6))
        h = norm_mod(x, norm_g[i, 0], sc1, sh1)
        hc = norm_mod(ctx, norm_g[i, 0], sc1c, sh1c)
        if i % 2 == 0:
            hy = (hy_conv_w[j], hy_conv_b[j], hy_w1[j], hy_b1[j], hy_w2[j], hy_b2[j],
                  hy_w3[j], hy_freq[j], hy_skip[j])
            lat, cx = _mixer_ab(h, hc, with_ctx, ab_w_in[j], ab_w_out[j], ab_qk_g[j], hy)
        else:
            lambda_init = 0.8 - 0.6 * math.exp(-0.3 * i)
            lat, cx = _mixer_c(h, hc, with_ctx, lambda_init, dc_w_in[j], dc_w_out[j],
                               dc_lambda[j], dc_subln_g[j])
        x = outproj_residual(lat[0], lat[1], x, norm_g[i, 1], g1)
        x = conv_ffn_residual(x, norm_g[i, 2], sc2, sh2, w_up, ffn_conv_w[i], ffn_conv_b[i],
                              w_down, norm_g[i, 3], g2, i)
        if with_ctx:
            ctx = outproj_residual(cx[0], cx[1], ctx, norm_g[i, 1], g1c)
            ctx = conv_ffn_residual(ctx, norm_g[i, 2], sc2c, sh2c, w_up, ffn_conv_w[i],
                                    ffn_conv_b[i], w_down, norm_g[i, 3], g2c, i)
    return x
```

```python
import functools
import math

import jax
import jax.numpy as jnp
from jax import lax
from jax.experimental import pallas as pl
from jax.experimental.pallas import tpu as pltpu

F32 = jnp.float32
BF16 = jnp.bfloat16

D_MODEL = 2048
GRID_W = 64
EPS = 1e-6
ROPE_THETA = 10000.0
A_HEADS, A_KV_HEADS, A_HEAD_DIM = 8, 2, 128
A_Q_W = A_HEADS * A_HEAD_DIM
A_KV_W = A_KV_HEADS * A_HEAD_DIM
A_QKV_W = A_Q_W + 2 * A_KV_W
HY_CH = D_MODEL // 2
HY_ORDER = 2
HY_FILTER_W = 64
HY_BANDS = 16
HY_FAST, HY_SLOW, HY_TARGET = 0.3, 1.5, 1e-2
C_HEADS, C_HEAD_DIM = 16, 64
C_Q_W = C_HEADS * 2 * C_HEAD_DIM

LANES = 128
BF16_SUBLANES = 16
VMEM_LIMIT = 56 * 1024 * 1024
DFT_RADIX = 128
FLASH_ROWS = 2048
FLASH_SUB_ROWS = 128
FLASH_MAX_KV = 16384
LOG2E = math.log2(math.e)

SUBLANES = 8
ROW_TILE = 512
ADA_COLS = 1024
PROJ_COLS = 2048
CONV_PROJ_COLS = 1024
FFN_CHUNK = 512
DFT_K1_CHUNK = 4
DFT_CH_TILE = 1024


def _cparams(*sem):
    return pltpu.CompilerParams(dimension_semantics=sem, vmem_limit_bytes=VMEM_LIMIT)


def _rms(x, g):
    return x * lax.rsqrt(jnp.mean(x * x, axis=-1, keepdims=True) + EPS) * g


def _split_dot(a, b):
    a_hi, b_hi = a.astype(BF16), b.astype(BF16)
    a_lo = (a - a_hi.astype(F32)).astype(BF16)
    b_lo = (b - b_hi.astype(F32)).astype(BF16)
    dot = functools.partial(jnp.dot, preferred_element_type=F32)
    return dot(a_hi, b_hi) + dot(a_lo, b_hi) + dot(a_hi, b_lo)


def _ada_kernel(c_ref, w_ref, b_ref, o_ref):
    c = c_ref[...]
    s = c * jax.nn.sigmoid(c)
    rows = s.shape[0]
    s_hi = s.astype(BF16).astype(F32)
    s2 = jnp.concatenate([s_hi, s - s_hi], axis=0).astype(BF16)
    w = w_ref[0]
    w_hi = w.astype(BF16)
    w_lo = (w - w_hi.astype(F32)).astype(BF16)
    both = jnp.dot(s2, w_hi, preferred_element_type=F32)
    hi_lo = jnp.dot(s2, w_lo, preferred_element_type=F32)[:rows]
    o_ref[0] = both[:rows] + both[rows:] + hi_lo + b_ref[0]


def ada_mod(cvec, ada_w, ada_b):
    depth, d, n6 = ada_w.shape
    rows = cvec.shape[0]
    tn = ADA_COLS
    return pl.pallas_call(
        _ada_kernel,
        grid=(depth, n6 // tn),
        in_specs=[pl.BlockSpec((rows, d), lambda l, j: (0, 0)),
                  pl.BlockSpec((1, d, tn), lambda l, j: (l, 0, j)),
                  pl.BlockSpec((1, 1, tn), lambda l, j: (l, 0, j))],
        out_specs=pl.BlockSpec((1, rows, tn), lambda l, j: (l, 0, j)),
        out_shape=jax.ShapeDtypeStruct((depth, rows, n6), F32),
        compiler_params=_cparams("arbitrary", "arbitrary"),
        name="ada_mod",
    )(cvec, ada_w, ada_b.reshape(depth, 1, n6))


def _norm_mod_kernel(x_ref, g_ref, sc_ref, sh_ref, o_ref):
    y = _rms(x_ref[0], g_ref[...])
    o_ref[0] = (y * (1.0 + sc_ref[0]) + sh_ref[0]).astype(o_ref.dtype)


def norm_mod(x, g, sc, sh):
    b, n, d = x.shape
    tm = min(n, ROW_TILE)
    return pl.pallas_call(
        _norm_mod_kernel,
        grid=(b, n // tm),
        in_specs=[pl.BlockSpec((1, tm, d), lambda bi, i: (bi, i, 0)),
                  pl.BlockSpec((1, d), lambda bi, i: (0, 0)),
                  pl.BlockSpec((1, 1, d), lambda bi, i: (bi, 0, 0)),
                  pl.BlockSpec((1, 1, d), lambda bi, i: (bi, 0, 0))],
        out_specs=pl.BlockSpec((1, tm, d), lambda bi, i: (bi, i, 0)),
        out_shape=jax.ShapeDtypeStruct((b, n, d), BF16),
        compiler_params=_cparams("arbitrary", "arbitrary"),
        name="norm_mod",
    )(x, g.reshape(1, d), sc.reshape(b, 1, d), sh.reshape(b, 1, d))


def _inproj_kernel(*refs, norm, rope, appending, n_tiles, has_tail):
    o_ref = refs[-1]
    if appending:
        refs = refs[:-2]
    else:
        refs = refs[:-1]
    if rope:
        h_ref, w_ref, g_ref, cos_ref, sin_ref = refs
    else:
        h_ref, w_ref, g_ref = refs
    i = pl.program_id(1)

    def _tail():
        o_ref[0] = jnp.zeros(o_ref.shape[1:], o_ref.dtype)

    def _body():
        acc = jnp.dot(h_ref[0], w_ref[...], preferred_element_type=F32)
        tn = acc.shape[1]
        if rope:
            cosf = cos_ref[...]
            sinf = sin_ref[...]
        for c in range(tn // LANES):
            x = acc[:, c * LANES:(c + 1) * LANES]
            if norm:
                x = x * lax.rsqrt(jnp.mean(x * x, axis=-1, keepdims=True) + EPS)
            x = x * g_ref[:, c * LANES:(c + 1) * LANES]
            if rope:
                x = x * cosf + pltpu.roll(x, LANES // 2, 1) * sinf
            o_ref[0, :, c * LANES:(c + 1) * LANES] = x.astype(o_ref.dtype)

    if has_tail:
        pl.when(i >= n_tiles)(_tail)
        pl.when(i < n_tiles)(_body)
    else:
        _body()


def inproj(h, w, gvec, cosf=None, sinf=None, *, norm=False, out_dtype=BF16,
           tail_rows=0, append_to=None, append_at=0):
    b, n, d = h.shape
    ncol = w.shape[1]
    rope = cosf is not None
    tm = min(n, ROW_TILE)
    tn = min(ncol, PROJ_COLS)
    n_tiles = n // tm
    last = n_tiles - 1
    in_specs = [pl.BlockSpec((1, tm, d), lambda bi, i, j: (bi, jnp.minimum(i, last), 0)),
                pl.BlockSpec((d, tn), lambda bi, i, j: (0, j)),
                pl.BlockSpec((1, tn), lambda bi, i, j: (0, j))]
    args = [h, w, gvec.reshape(1, ncol)]
    if rope:
        in_specs += [pl.BlockSpec((tm, LANES), lambda bi, i, j: (jnp.minimum(i, last), 0)),
                     pl.BlockSpec((tm, LANES), lambda bi, i, j: (jnp.minimum(i, last), 0))]
        args += [cosf, sinf]
    appending = append_to is not None
    if appending:
        total = append_to.shape[1]
        first = append_at // tm
        assert first * tm == append_at and append_at + n <= total and append_to.dtype == out_dtype
        tail_tiles = 0
        in_specs.append(pl.BlockSpec(memory_space=pl.ANY))
        args.append(append_to)
        aliases = {len(args) - 1: 0}
    else:
        tail_tiles = -(-tail_rows // tm)
        total, first, aliases = (n_tiles + tail_tiles) * tm, 0, {}
    return pl.pallas_call(
        functools.partial(_inproj_kernel, norm=norm, rope=rope, appending=appending,
                          n_tiles=n_tiles, has_tail=tail_tiles > 0),
        grid=(b, n_tiles + tail_tiles, ncol // tn),
        in_specs=in_specs,
        out_specs=pl.BlockSpec((1, tm, tn), lambda bi, i, j: (bi, first + i, j)),
        out_shape=jax.ShapeDtypeStruct((b, total, ncol), out_dtype),
        input_output_aliases=aliases,
        compiler_params=_cparams("arbitrary", "arbitrary", "arbitrary"),
        name="inproj",
    )(*args)


def _inproj_conv_kernel(hm_ref, hp_ref, hn_ref, w_ref, cw_ref, cb_ref, o_ref, *, n_tiles):
    i = pl.program_id(2)
    hp, hm, hn = hp_ref[0], hm_ref[0], hn_ref[0]
    halo, tm = hp.shape[0], hm.shape[0]
    zero = jnp.zeros_like(hp)
    lhs = jnp.concatenate([jnp.where(i > 0, hp, zero), hm,
                           jnp.where(i < n_tiles - 1, hn, zero)], axis=0)
    u = jnp.dot(lhs, w_ref[...], preferred_element_type=F32)
    rows = tm + 2 * halo
    cw = cw_ref[...]
    conv = cw[0:1] * pltpu.roll(u, 1, 0) + cw[1:2] * u + cw[2:3] * pltpu.roll(u, rows - 1, 0)
    o_ref[0, 0] = conv[halo:halo + tm] + cb_ref[...]


def inproj_conv(h, w, conv_w, conv_b, groups):
    b, n, d = h.shape
    ncol = w.shape[1]
    c = ncol // groups
    tm = min(n, ROW_TILE)
    tn = min(c, CONV_PROJ_COLS)
    halo = BF16_SUBLANES
    nt = n // tm
    per = c // tn
    return pl.pallas_call(
        functools.partial(_inproj_conv_kernel, n_tiles=nt),
        grid=(ncol // tn, b, nt),
        in_specs=[pl.BlockSpec((1, tm, d), lambda j, bi, i: (bi, i, 0)),
                  pl.BlockSpec((1, halo, d),
                               lambda j, bi, i: (bi, jnp.maximum(i * (tm // halo) - 1, 0), 0)),
                  pl.BlockSpec((1, halo, d),
                               lambda j, bi, i: (bi, jnp.minimum((i + 1) * (tm // halo), n // halo - 1), 0)),
                  pl.BlockSpec((d, tn), lambda j, bi, i: (0, j)),
                  pl.BlockSpec((3, tn), lambda j, bi, i: (0, j)),
                  pl.BlockSpec((1, tn), lambda j, bi, i: (0, j))],
        out_specs=pl.BlockSpec((1, 1, tm, tn), lambda j, bi, i: (j // per, bi, i, j % per)),
        out_shape=jax.ShapeDtypeStruct((groups, b, n, c), F32),
        compiler_params=_cparams("arbitrary", "arbitrary", "arbitrary"),
        name="inproj_conv",
    )(h, h, h, w, conv_w, conv_b.reshape(1, ncol))


def _flash_kernel(*refs, mode, group, tq, lambda_init):
    if mode == "diff":
        q_ref, k_ref, v_ref, lam_ref, sg_ref, o_ref, qs_ref = refs
    else:
        q_ref, k_ref, v_ref, o_ref, qs_ref = refs
    q = q_ref[0]
    if mode == "diff":
        lane = lax.broadcasted_iota(jnp.int32, q.shape, 1)
        zero = jnp.zeros_like(q)
        first = (lane // (C_HEAD_DIM // 2)) % 2 == 0
        qs_ref[0:tq] = jnp.where(first, q, zero)
        qs_ref[tq:2 * tq] = jnp.where(first, zero, q)
    else:
        for g in range(group):
            qs_ref[g * tq:(g + 1) * tq] = q[:, g * LANES:(g + 1) * LANES]

    k = k_ref[0]
    v = v_ref[0]
    v_aug = jnp.concatenate([v, jnp.ones_like(v)], axis=1)
    sub = FLASH_SUB_ROWS

    def attend(c):
        s = lax.dot_general(qs_ref[c * sub:(c + 1) * sub], k, (((1,), (1,)), ((), ())),
                            preferred_element_type=F32)
        p = jnp.exp2(s - jnp.max(s, axis=-1, keepdims=True))
        pv = jnp.dot(p.astype(BF16), v_aug, preferred_element_type=F32)
        return pv[:, :LANES] / pv[:, LANES:]

    per_head = tq // sub
    if mode == "diff":
        lv = lam_ref[...]
        lam = (jnp.exp(jnp.sum(lv[0:1] * lv[1:2], axis=-1, keepdims=True))
               - jnp.exp(jnp.sum(lv[2:3] * lv[3:4], axis=-1, keepdims=True)) + lambda_init)
        for c in range(per_head):
            dlt = attend(c) - lam * attend(per_head + c)
            o_ref[0, c * sub:(c + 1) * sub, :] = (
                _rms(dlt, sg_ref[...]) * (1.0 - lambda_init)).astype(o_ref.dtype)
    else:
        for c in range(group * per_head):
            g, r = divmod(c, per_head)
            o_ref[0, r * sub:(r + 1) * sub, g * LANES:(g + 1) * LANES] = attend(c).astype(o_ref.dtype)


def flash_attention(q, k, v, *, mode, kv_len, lam_vecs=None, subln_g=None, lambda_init=0.0,
                    kv_start=0):
    b, n, qw = q.shape
    tk = kv_len
    assert tk <= FLASH_MAX_KV and tk % LANES == 0 and kv_start % tk == 0
    if mode == "diff":
        group, n_groups, qblk = 2, qw // LANES, LANES
    else:
        group = A_HEADS // A_KV_HEADS
        n_groups, qblk = A_KV_HEADS, group * LANES
    tq = min(n, FLASH_ROWS // group)
    assert tq % FLASH_SUB_ROWS == 0
    j0 = kv_start // tk
    in_specs = [pl.BlockSpec((1, tq, qblk), lambda bi, g, i: (bi, i, g)),
                pl.BlockSpec((1, tk, LANES), lambda bi, g, i: (bi, j0, g)),
                pl.BlockSpec((1, tk, LANES), lambda bi, g, i: (bi, j0, g))]
    args = [q, k, v]
    if mode == "diff":
        in_specs += [pl.BlockSpec(lam_vecs.shape, lambda bi, g, i: (0, 0)),
                     pl.BlockSpec((1, LANES), lambda bi, g, i: (0, 0))]
        args += [lam_vecs, subln_g.reshape(1, LANES)]
    return pl.pallas_call(
        functools.partial(_flash_kernel, mode=mode, group=group, tq=tq, lambda_init=lambda_init),
        grid=(b, n_groups, n // tq),
        in_specs=in_specs,
        out_specs=pl.BlockSpec((1, tq, qblk), lambda bi, g, i: (bi, i, g)),
        out_shape=jax.ShapeDtypeStruct((b, n, qw), BF16),
        scratch_shapes=[pltpu.VMEM((group * tq, LANES), BF16)],
        compiler_params=_cparams("arbitrary", "arbitrary", "arbitrary"),
        name="flash_" + mode,
    )(*args)


def _outproj_kernel(*refs, n_lhs):
    lhs = refs[:n_lhs]
    ws = refs[n_lhs:2 * n_lhs]
    x_ref, g_ref, gate_ref, o_ref = refs[2 * n_lhs:]
    y = jnp.dot(lhs[0][0].astype(BF16), ws[0][...], preferred_element_type=F32)
    for a, w in zip(lhs[1:], ws[1:]):
        y = y + jnp.dot(a[0].astype(BF16), w[...], preferred_element_type=F32)
    o_ref[0] = x_ref[0] + gate_ref[0] * _rms(y, g_ref[...])


def outproj_residual(lhs_list, w_list, x, g, gate):
    b, n, d = x.shape
    tm = min(n, ROW_TILE)
    n_lhs = len(lhs_list)
    in_specs = [pl.BlockSpec((1, tm, a.shape[2]), lambda bi, i: (bi, i, 0)) for a in lhs_list]
    in_specs += [pl.BlockSpec(w.shape, lambda bi, i: (0, 0)) for w in w_list]
    in_specs += [pl.BlockSpec((1, tm, d), lambda bi, i: (bi, i, 0)),
                 pl.BlockSpec((1, d), lambda bi, i: (0, 0)),
                 pl.BlockSpec((1, 1, d), lambda bi, i: (bi, 0, 0))]
    return pl.pallas_call(
        functools.partial(_outproj_kernel, n_lhs=n_lhs),
        grid=(b, n // tm),
        in_specs=in_specs,
        out_specs=pl.BlockSpec((1, tm, d), lambda bi, i: (bi, i, 0)),
        out_shape=jax.ShapeDtypeStruct((b, n, d), F32),
        compiler_params=_cparams("arbitrary", "arbitrary"),
        name="outproj",
    )(*lhs_list, *w_list, x, g.reshape(1, d), gate.reshape(b, 1, d))


def _ffn_kernel(xm_ref, xp_ref, xn_ref, gn_ref, sc_ref, sh_ref, wa_ref, wg_ref, cwa_ref, cwg_ref,
                cba_ref, cbg_ref, wd_ref, g3_ref, gate_ref, o_ref, hx_ref,
                *, tm, halo, n_tiles, n_f):
    i = pl.program_id(1)
    j = pl.program_id(2)

    @pl.when(j == 0)
    def _prologue():
        def nm(x):
            return (_rms(x, gn_ref[...]) * (1.0 + sc_ref[0]) + sh_ref[0]).astype(BF16)
        hx_ref[halo:halo + tm] = nm(xm_ref[0])
        zero = jnp.zeros((halo, xm_ref.shape[2]), BF16)
        hx_ref[0:halo] = jnp.where(i > 0, nm(xp_ref[0]), zero)
        hx_ref[halo + tm:] = jnp.where(i < n_tiles - 1, nm(xn_ref[0]), zero)
        o_ref[0] = jnp.zeros(o_ref.shape[1:], F32)

    hx = hx_ref[...]
    rows = tm + 2 * halo

    def conv(u, cw, cb):
        up = pltpu.roll(u, 1, 0)
        dn = pltpu.roll(u, rows - 1, 0)
        return (cw[0:1] * up + cw[1:2] * u + cw[2:3] * dn + cb)[halo:halo + tm]

    a = conv(jnp.dot(hx, wa_ref[...], preferred_element_type=F32), cwa_ref[...], cba_ref[...])
    g = conv(jnp.dot(hx, wg_ref[...], preferred_element_type=F32), cwg_ref[...], cbg_ref[...])
    act = (a * (g * jax.nn.sigmoid(g))).astype(BF16)
    o_ref[0] += jnp.dot(act, wd_ref[...], preferred_element_type=F32)

    @pl.when(j == n_f - 1)
    def _epilogue():
        o_ref[0] = xm_ref[0] + gate_ref[0] * _rms(o_ref[0], g3_ref[...])


def conv_ffn_residual(x, gn, sc, sh, w_up, conv_w, conv_b, w_down, g3, gate, layer):
    b, n, d = x.shape
    dff = w_down.shape[1]
    tm = min(n, ROW_TILE)
    tf = FFN_CHUNK
    halo = BF16_SUBLANES
    nt = n // tm
    n_f = dff // tf
    row = lambda bi, i, j: (bi, i, 0)
    vec = lambda bi, i, j: (0, 0)
    bvec = lambda bi, i, j: (bi, 0, 0)
    up_a = lambda bi, i, j: (0, j)
    up_g = lambda bi, i, j: (0, n_f + j)
    wup_a = lambda bi, i, j: (layer, 0, j)
    wup_g = lambda bi, i, j: (layer, 0, n_f + j)
    conv_b = conv_b.reshape(1, 2 * dff)
    return pl.pallas_call(
        functools.partial(_ffn_kernel, tm=tm, halo=halo, n_tiles=nt, n_f=n_f),
        grid=(b, nt, n_f),
        in_specs=[pl.BlockSpec((1, tm, d), row),
                  pl.BlockSpec((1, halo, d),
                               lambda bi, i, j: (bi, jnp.maximum(i * (tm // halo) - 1, 0), 0)),
                  pl.BlockSpec((1, halo, d),
                               lambda bi, i, j: (bi, jnp.minimum((i + 1) * (tm // halo), n // halo - 1), 0)),
                  pl.BlockSpec((1, d), vec),
                  pl.BlockSpec((1, 1, d), bvec),
                  pl.BlockSpec((1, 1, d), bvec),
                  pl.BlockSpec((None, d, tf), wup_a),
                  pl.BlockSpec((None, d, tf), wup_g),
                  pl.BlockSpec((3, tf), up_a),
                  pl.BlockSpec((3, tf), up_g),
                  pl.BlockSpec((1, tf), up_a),
                  pl.BlockSpec((1, tf), up_g),
                  pl.BlockSpec((None, tf, d), lambda bi, i, j: (layer, j, 0)),
                  pl.BlockSpec((1, d), vec),
                  pl.BlockSpec((1, 1, d), bvec)],
        out_specs=pl.BlockSpec((1, tm, d), row),
        out_shape=jax.ShapeDtypeStruct((b, n, d), F32),
        scratch_shapes=[pltpu.VMEM((tm + 2 * halo, d), BF16)],
        compiler_params=_cparams("arbitrary", "arbitrary", "arbitrary"),
        name="conv_ffn",
    )(x, x, x, gn.reshape(1, d), sc.reshape(b, 1, d), sh.reshape(b, 1, d), w_up, w_up,
      conv_w, conv_w, conv_b, conv_b, w_down, g3.reshape(1, d), gate.reshape(b, 1, d))


def _filter_rows(r, bands_ref, w1_ref, b1_ref, w2_ref, b2_ref, fr_ref, dl_ref, n_tok):
    t = jnp.where(r < n_tok, r, 2.0 * n_tok - r)
    t_norm = t / float(max(n_tok - 1, 1))
    wang = (2.0 * math.pi / n_tok) * t
    z = wang * bands_ref[...]
    lane = lax.broadcasted_iota(jnp.int32, z.shape, 1)
    feats = jnp.where(lane == 0, t_norm,
                      jnp.where(lane <= HY_BANDS, jnp.cos(z),
                                jnp.where(lane <= 2 * HY_BANDS, -jnp.sin(z), 0.0)))
    fr = fr_ref[...]
    h = jnp.sin(fr[0:1] * (_split_dot(feats, w1_ref[...]) + b1_ref[...]))
    h = jnp.sin(fr[1:2] * (_split_dot(h, w2_ref[...]) + b2_ref[...]))
    decay = jnp.where(r == n_tok, 0.0, jnp.exp(-t_norm * dl_ref[...]))
    return h, decay


def _filter_kernel(bands_ref, w1_ref, b1_ref, w2_ref, b2_ref, w3_ref, fr_ref, dl_ref,
                   o_ref, s_ref, *, n_tok, tr):
    i = pl.program_id(0)

    @pl.when(i == 0)
    def _init():
        s_ref[...] = jnp.zeros(s_ref.shape, F32)

    r = (i * tr + lax.broadcasted_iota(jnp.int32, (tr, 1), 0)).astype(F32)
    h, decay = _filter_rows(r, bands_ref, w1_ref, b1_ref, w2_ref, b2_ref, fr_ref, dl_ref, n_tok)
    hb = h.astype(BF16)
    for o in range(HY_ORDER):
        out = jnp.dot(hb, w3_ref[o].astype(BF16), preferred_element_type=F32) * decay
        o_ref[o] = out
        s_ref[o] += jnp.sum(jnp.abs(out), axis=0, keepdims=True)


def _filter_stage_a_kernel(bands_ref, w1_ref, b1_ref, w2_ref, b2_ref, w3_ref, fr_ref, dl_ref,
                           f_ref, *out_refs, n_tok, sub):
    o_refs, s_ref = out_refs[:HY_ORDER], out_refs[HY_ORDER]
    j = pl.program_id(0)

    @pl.when(j == 0)
    def _init():
        s_ref[...] = jnp.zeros(s_ref.shape, F32)

    k1n = o_refs[0].shape[1]
    half = k1n // 2
    c = dl_ref.shape[1]
    n1 = lax.broadcasted_iota(jnp.int32, (k1n, 1), 0)
    f = f_ref[...]
    for q in range(sub):
        r = (n1 * DFT_RADIX + (j * sub + q)).astype(F32)
        h, decay = _filter_rows(r, bands_ref, w1_ref, b1_ref, w2_ref, b2_ref, fr_ref, dl_ref, n_tok)
        h_fwd, h_bwd = h[:half].astype(BF16), h[half:].astype(BF16)
        for o in range(HY_ORDER):
            w3 = w3_ref[o].astype(BF16)
            h3 = jnp.concatenate([jnp.dot(h_fwd, w3[:, :c], preferred_element_type=F32),
                                  jnp.dot(h_bwd, w3[:, c:], preferred_element_type=F32)],
                                 axis=0) * decay
            s_ref[o] += jnp.sum(jnp.abs(h3), axis=0, keepdims=True)
            y = _dft_dot(f, h3)
            o_refs[o][0, :, q, :] = y[:k1n]
            o_refs[o][1, :, q, :] = y[k1n:]


def _filter_operands(w1, b1, w2, b2, w3, freq):
    ch = HY_CH
    lane = jnp.arange(LANES)
    bands = jnp.linspace(1e-4, HY_BANDS - 1, HY_BANDS, dtype=F32)
    bands_l = jnp.where((lane >= 1) & (lane <= HY_BANDS), bands[jnp.clip(lane - 1, 0, HY_BANDS - 1)],
                        jnp.where((lane > HY_BANDS) & (lane <= 2 * HY_BANDS),
                                  bands[jnp.clip(lane - 1 - HY_BANDS, 0, HY_BANDS - 1)], 0.0))
    w1p = jnp.zeros((LANES, HY_FILTER_W), F32).at[:w1.shape[0]].set(w1)
    deltas = jnp.abs(jnp.linspace(math.log(HY_TARGET) / HY_FAST, math.log(HY_TARGET) / HY_SLOW,
                                  ch, dtype=F32))
    w3o = jnp.swapaxes(w3.reshape(HY_FILTER_W, HY_ORDER, 2 * ch), 0, 1)
    return (bands_l.reshape(1, LANES), w1p, b1.reshape(1, -1), w2, b2.reshape(1, -1), w3o, freq,
            deltas.reshape(1, ch))


def _filter_in_specs(w3_spec):
    const = lambda i: (0, 0)
    return [pl.BlockSpec((1, LANES), const),
            pl.BlockSpec((LANES, HY_FILTER_W), const),
            pl.BlockSpec((1, HY_FILTER_W), const),
            pl.BlockSpec((HY_FILTER_W, HY_FILTER_W), const),
            pl.BlockSpec((1, HY_FILTER_W), const),
            w3_spec,
            pl.BlockSpec((2, HY_FILTER_W), const),
            pl.BlockSpec((1, HY_CH), const)]


def hyena_filter_time(n_tok, w1, b1, w2, b2, w3, freq):
    ch = HY_CH
    ll = 2 * n_tok
    tr = min(n_tok, ROW_TILE)
    per_half = n_tok // tr
    return pl.pallas_call(
        functools.partial(_filter_kernel, n_tok=n_tok, tr=tr),
        grid=(ll // tr,),
        in_specs=_filter_in_specs(
            pl.BlockSpec((HY_ORDER, HY_FILTER_W, ch), lambda i: (0, 0, i // per_half))),
        out_specs=[pl.BlockSpec((HY_ORDER, tr, ch), lambda i: (0, i, 0)),
                   pl.BlockSpec((HY_ORDER, 1, ch), lambda i: (0, 0, 0))],
        out_shape=[jax.ShapeDtypeStruct((HY_ORDER, ll, ch), F32),
                   jax.ShapeDtypeStruct((HY_ORDER, 1, ch), F32)],
        compiler_params=_cparams("arbitrary"),
        name="hyena_filter",
    )(*_filter_operands(w1, b1, w2, b2, w3, freq))


def hyena_filter_stage_a(n_tok, f_filt, w1, b1, w2, b2, w3, freq):
    ch = HY_CH
    r = DFT_RADIX
    k1n = 2 * n_tok // r
    sub = SUBLANES
    return pl.pallas_call(
        functools.partial(_filter_stage_a_kernel, n_tok=n_tok, sub=sub),
        grid=(r // sub,),
        in_specs=_filter_in_specs(
            pl.BlockSpec((HY_ORDER, HY_FILTER_W, 2 * ch), lambda j: (0, 0, 0)))
        + [pl.BlockSpec(f_filt.shape, lambda j: (0, 0))],
        out_specs=[pl.BlockSpec((2, k1n, sub, ch), lambda j: (0, 0, j, 0))] * HY_ORDER
        + [pl.BlockSpec((HY_ORDER, 1, ch), lambda j: (0, 0, 0))],
        out_shape=[jax.ShapeDtypeStruct((2, k1n, r, ch), F32)] * HY_ORDER
        + [jax.ShapeDtypeStruct((HY_ORDER, 1, ch), F32)],
        compiler_params=_cparams("arbitrary"),
        name="hyena_filter_a",
    )(*_filter_operands(w1, b1, w2, b2, w3, freq), f_filt)


def _dft_dot(f, x):
    return jnp.dot(f.astype(BF16), x.astype(BF16), preferred_element_type=F32)


def _stage_a_kernel(f_ref, x_ref, o_ref):
    parts = x_ref.shape[0]
    f = f_ref[...]
    for r in range(x_ref.shape[2]):
        z = jnp.concatenate([x_ref[p, :, r, :] for p in range(parts)], axis=0)
        y = _dft_dot(f, z)
        half = y.shape[0] // 2
        o_ref[0, :, r, :] = y[:half]
        o_ref[1, :, r, :] = y[half:]


def dft_stage_a(f, x, g):
    _, p, rows, r, c = x.shape
    k1 = p * rows
    sub = SUBLANES
    return pl.pallas_call(
        _stage_a_kernel,
        grid=(r // sub,),
        in_specs=[pl.BlockSpec(f.shape, lambda j: (0, 0)),
                  pl.BlockSpec((None, p, rows, sub, c), lambda j: (g, 0, 0, j, 0))],
        out_specs=pl.BlockSpec((2, k1, sub, c), lambda j: (0, 0, j, 0)),
        out_shape=jax.ShapeDtypeStruct((2, k1, r, c), F32),
        compiler_params=_cparams("arbitrary"),
        name="dft_stage_a",
    )(f, x)


def _stage_c_conv_kernel(a_ref, af_ref, sc_ref, g_ref, gi_ref, o_ref, *, kc):
    tc = a_ref.shape[3]
    for k in range(kc):
        r = jnp.concatenate([jnp.concatenate([a_ref[0, k], af_ref[0, k]], axis=1),
                             jnp.concatenate([a_ref[1, k], af_ref[1, k]], axis=1)], axis=0)
        y = _dft_dot(g_ref[k], r)
        half = y.shape[0] // 2
        yre, yim = y[:half, :tc], y[half:, :tc]
        hre, him = y[:half, tc:] * sc_ref[...], y[half:, tc:] * sc_ref[...]
        p = jnp.concatenate([yre * hre - yim * him, yre * him + yim * hre], axis=0)
        d = _dft_dot(gi_ref[k], p)
        o_ref[0, k] = d[:half]
        o_ref[1, k] = d[half:]


def dft_stage_c_conv(a, af, scale, g, gi):
    _, k1, r, c = a.shape
    kc = DFT_K1_CHUNK
    tc = DFT_CH_TILE
    blk = pl.BlockSpec((2, kc, r, tc), lambda i, j: (0, i, 0, j))
    mat = pl.BlockSpec((kc, 2 * r, 2 * r), lambda i, j: (i, 0, 0))
    return pl.pallas_call(
        functools.partial(_stage_c_conv_kernel, kc=kc),
        grid=(k1 // kc, c // tc),
        in_specs=[blk, blk, pl.BlockSpec((1, tc), lambda i, j: (0, j)), mat, mat],
        out_specs=blk,
        out_shape=jax.ShapeDtypeStruct(a.shape, F32),
        compiler_params=_cparams("arbitrary", "arbitrary"),
        name="dft_stage_c_conv",
    )(a, af, scale, g, gi)


def _stage_a_inv_kernel(f_ref, d_ref, u_ref, gt_ref, sk_ref, o_ref):
    f = f_ref[...]
    sk = sk_ref[...]
    for r in range(d_ref.shape[2]):
        d = jnp.concatenate([d_ref[0, :, r, :], d_ref[1, :, r, :]], axis=0)
        y = _dft_dot(f, d)
        half = y.shape[0] // 2
        for bi in range(2):
            yb = y[bi * half:(bi + 1) * half]
            o_ref[bi, :, r, :] = gt_ref[bi, :, r, :] * (yb + u_ref[bi, :, r, :] * sk)


def dft_stage_a_inv(f, d, u, gu, gate, gg, skip):
    _, k1, r, c = d.shape
    half = k1 // 2
    sub = SUBLANES
    io = pl.BlockSpec((2, half, sub, c), lambda j: (0, 0, j, 0))
    return pl.pallas_call(
        _stage_a_inv_kernel,
        grid=(r // sub,),
        in_specs=[pl.BlockSpec(f.shape, lambda j: (0, 0)),
                  pl.BlockSpec((2, k1, sub, c), lambda j: (0, 0, j, 0)),
                  pl.BlockSpec((None, 2, half, sub, c), lambda j: (gu, 0, 0, j, 0)),
                  pl.BlockSpec((None, 2, half, sub, c), lambda j: (gg, 0, 0, j, 0)),
                  pl.BlockSpec((1, c), lambda j: (0, 0))],
        out_specs=io,
        out_shape=jax.ShapeDtypeStruct((2, half, r, c), F32),
        compiler_params=_cparams("arbitrary"),
        name="dft_stage_a_inv",
    )(f, d, u, gate, skip)


def _dft_tables(n_tok):
    ll = 2 * n_tok
    r = DFT_RADIX
    k1n = ll // r
    half = k1n // 2
    two_pi = 2.0 * math.pi

    def cs(m, period):
        ang = (m % period).astype(F32) * (two_pi / period)
        return jnp.cos(ang), jnp.sin(ang)

    k1 = jnp.arange(k1n, dtype=jnp.int32)
    c, s = cs(k1[:, None] * k1[None, :half], k1n)
    fa_data = jnp.concatenate([jnp.concatenate([c, s], 1), jnp.concatenate([-s, c], 1)], 0)
    c, s = cs(k1[:, None] * k1[None, :], k1n)
    fa_filt = jnp.concatenate([c, -s], 0)
    c, s = cs(k1[:half, None] * k1[None, :], k1n)
    fa_inv = jnp.concatenate([jnp.concatenate([c, -s], 1), jnp.concatenate([s, c], 1)], 0)
    idx = jnp.arange(r, dtype=jnp.int32)
    ca, sa = cs(idx[:, None] * idx[None, :], r)
    cb, sb = cs(k1[:, None] * idx[None, :], ll)
    c = ca[None] * cb[:, None, :] - sa[None] * sb[:, None, :]
    s = sa[None] * cb[:, None, :] + ca[None] * sb[:, None, :]
    g_fwd = jnp.concatenate([jnp.concatenate([c, s], 2), jnp.concatenate([-s, c], 2)], 1)
    ct, st = jnp.swapaxes(c, 1, 2), jnp.swapaxes(s, 1, 2)
    g_inv = jnp.concatenate([jnp.concatenate([ct, -st], 2), jnp.concatenate([st, ct], 2)], 1)
    return tuple(t.astype(BF16) for t in (fa_data, fa_filt, fa_inv, g_fwd, g_inv))


def hyena_long(pc, filter_params, skip):
    _, b, n, c = pc.shape
    assert b == 2, "the two batch entries ride as real/imaginary parts"
    ll = 2 * n
    r = DFT_RADIX
    half = ll // r // 2
    fa_data, fa_filt, fa_inv, g_fwd, g_inv = _dft_tables(n)
    *afs, l1 = hyena_filter_stage_a(n, fa_filt, *filter_params)
    pcs = pc.reshape(HY_ORDER + 1, 2, half, r, c)
    u, gu = pcs, 0
    for o in range(HY_ORDER):
        a = dft_stage_a(fa_data, u, gu)
        d = dft_stage_c_conv(a, afs[o], 1.0 / (l1[o] * ll), g_fwd, g_inv)
        z = dft_stage_a_inv(fa_inv, d, u, gu, pcs, o + 1, skip[o].reshape(1, c))
        u, gu = z[None], 0
    return z.reshape(b, n, c)


def _hyena_short_kernel(ff_ref, fr_ref, fi_ref, pc_ref, filt_ref, l1_ref, skip_ref, o_ref, *, n, ll):
    u = [pc_ref[0, 0], pc_ref[0, 1]]
    for o in range(HY_ORDER):
        hs = _dft_dot(fr_ref[...], filt_ref[o]) * (1.0 / (l1_ref[o] * ll))
        y = _dft_dot(ff_ref[...], jnp.concatenate(u, axis=0))
        yre, yim, hre, him = y[:ll], y[ll:], hs[:ll], hs[ll:]
        p = jnp.concatenate([yre * hre - yim * him, yre * him + yim * hre], axis=0)
        d = _dft_dot(fi_ref[...], p)
        sk = skip_ref[o:o + 1]
        u = [pc_ref[o + 1, bi] * (d[bi * n:(bi + 1) * n] + u[bi] * sk) for bi in range(2)]
    o_ref[0] = u[0].astype(o_ref.dtype)
    o_ref[1] = u[1].astype(o_ref.dtype)


def hyena_short(pc, filt, l1, skip, out_dtype):
    _, b, n, c = pc.shape
    assert b == 2
    ll = 2 * n
    two_pi = 2.0 * math.pi
    kk = jnp.arange(ll, dtype=jnp.int32)
    ang = ((kk[:, None] * kk[None, :]) % ll).astype(F32) * (two_pi / ll)
    cf, sf = jnp.cos(ang), jnp.sin(ang)
    f_real = jnp.concatenate([cf, -sf], 0)
    cn, sn = cf[:, :n], sf[:, :n]
    f_fwd = jnp.concatenate([jnp.concatenate([cn, sn], 1), jnp.concatenate([-sn, cn], 1)], 0)
    ci, si = cf[:n, :], sf[:n, :]
    f_inv = jnp.concatenate([jnp.concatenate([ci, -si], 1), jnp.concatenate([si, ci], 1)], 0)
    tc = DFT_CH_TILE
    const = lambda j: (0, 0)
    return pl.pallas_call(
        functools.partial(_hyena_short_kernel, n=n, ll=ll),
        grid=(c // tc,),
        in_specs=[pl.BlockSpec(f_fwd.shape, const),
                  pl.BlockSpec(f_real.shape, const),
                  pl.BlockSpec(f_inv.shape, const),
                  pl.BlockSpec((3, 2, n, tc), lambda j: (0, 0, 0, j)),
                  pl.BlockSpec((HY_ORDER, ll, tc), lambda j: (0, 0, j)),
                  pl.BlockSpec((HY_ORDER, 1, tc), lambda j: (0, 0, j)),
                  pl.BlockSpec((HY_ORDER, tc), lambda j: (0, j))],
        out_specs=pl.BlockSpec((2, n, tc), lambda j: (0, 0, j)),
        out_shape=jax.ShapeDtypeStruct((2, n, c), out_dtype),
        compiler_params=_cparams("arbitrary"),
        name="hyena_short",
    )(f_fwd, f_real, f_inv, pc, filt, l1, skip)


def hyena_mixer(pc, w1, b1, w2, b2, w3, freq, skip, out_dtype):
    n = pc.shape[2]
    if (2 * n) % (DFT_RADIX * BF16_SUBLANES) == 0 and n >= 1024:
        return hyena_long(pc, (w1, b1, w2, b2, w3, freq), skip)
    filt, l1 = hyena_filter_time(n, w1, b1, w2, b2, w3, freq)
    return hyena_short(pc, filt, l1, skip, out_dtype)


def _deinterleave_cols(w, hd):
    lead = w.shape[:-1]
    wh = w.reshape(lead + (w.shape[-1] // LANES, LANES // hd, hd // 2, 2))
    half = lead + (w.shape[-1] // LANES, LANES // 2)
    return jnp.concatenate([wh[..., 0].reshape(half), wh[..., 1].reshape(half)],
                           axis=-1).reshape(w.shape)


def _rope_tables(n_tok, rot_dim):
    rows = n_tok // GRID_W
    row = jnp.broadcast_to(jnp.arange(rows, dtype=jnp.int32)[:, None], (rows, GRID_W)).reshape(n_tok)
    col = jnp.broadcast_to(jnp.arange(GRID_W, dtype=jnp.int32)[None, :], (rows, GRID_W)).reshape(n_tok)
    axis_dim = rot_dim // 2
    inv_freq = ROPE_THETA ** (-jnp.arange(0, axis_dim, 2, dtype=F32) / axis_dim)
    ang = jnp.concatenate([row.astype(F32)[:, None] * inv_freq,
                           col.astype(F32)[:, None] * inv_freq], axis=-1)
    c, s = jnp.cos(ang), jnp.sin(ang)
    reps = LANES // rot_dim
    c, s = jnp.tile(c, (1, reps)), jnp.tile(s, (1, reps))
    return jnp.concatenate([c, c], -1), jnp.concatenate([-s, s], -1)


def _mixer_ab(h, hc, with_ctx, w_in, w_out, qk_g, hy):
    b, n, _ = h.shape
    conv_w, conv_b, w1, b1, w2, b2, w3, freq, skip = hy
    scale = A_HEAD_DIM ** -0.5 * LOG2E
    wq = _deinterleave_cols(w_in[:, :A_Q_W], A_HEAD_DIM).astype(BF16)
    wk = _deinterleave_cols(w_in[:, A_Q_W:A_Q_W + A_KV_W], A_HEAD_DIM).astype(BF16)
    wv = w_in[:, A_Q_W + A_KV_W:A_QKV_W].astype(BF16)
    wh = w_in[:, A_QKV_W:].astype(BF16)
    gq = jnp.tile(_deinterleave_cols(qk_g[0], A_HEAD_DIM) * scale, A_HEADS)
    gk = jnp.tile(_deinterleave_cols(qk_g[1], A_HEAD_DIM), A_KV_HEADS)
    ones_v = jnp.ones((A_KV_W,), F32)
    cosf, sinf = _rope_tables(n, A_HEAD_DIM)

    n_ctx = hc.shape[1]
    q = inproj(h, wq, gq, cosf, sinf, norm=True)
    k_all = inproj(hc, wk, gk, norm=True, append_at=n,
                   append_to=inproj(h, wk, gk, cosf, sinf, norm=True, tail_rows=n_ctx))
    v_all = inproj(hc, wv, ones_v, append_at=n, append_to=inproj(h, wv, ones_v, tail_rows=n_ctx))
    o_att = flash_attention(q, k_all, v_all, mode="gqa", kv_len=n + n_ctx)
    pc = inproj_conv(h, wh, conv_w, conv_b, HY_ORDER + 1)
    o_hy = hyena_mixer(pc, w1, b1, w2, b2, w3, freq, skip, BF16)
    wo = w_out.astype(BF16)
    lat = ([o_att, o_hy], [wo[:A_Q_W], wo[A_Q_W:]])
    if not with_ctx:
        return lat, None
    q_c = inproj(hc, wq, gq, norm=True)
    o_att_c = flash_attention(q_c, k_all, v_all, mode="gqa", kv_start=n, kv_len=n_ctx)
    pc_c = inproj_conv(hc, wh, conv_w, conv_b, HY_ORDER + 1)
    o_hy_c = hyena_mixer(pc_c, w1, b1, w2, b2, w3, freq, skip, BF16)
    return lat, ([o_att_c, o_hy_c], [wo[:A_Q_W], wo[A_Q_W:]])


def _mixer_c(h, hc, with_ctx, lambda_init, w_in, w_out, lam_vecs, subln_g):
    b, n, _ = h.shape
    scale = C_HEAD_DIM ** -0.5 * LOG2E
    wq = _deinterleave_cols(w_in[:, :C_Q_W], C_HEAD_DIM).astype(BF16)
    wk = _deinterleave_cols(w_in[:, C_Q_W:2 * C_Q_W], C_HEAD_DIM).astype(BF16)
    wv = w_in[:, 2 * C_Q_W:].astype(BF16)
    gq = jnp.full((C_Q_W,), scale, F32)
    ones = jnp.ones((C_Q_W,), F32)
    cosf, sinf = _rope_tables(n, C_HEAD_DIM)

    n_ctx = hc.shape[1]
    q = inproj(h, wq, gq, cosf, sinf)
    k_all = inproj(hc, wk, ones, append_at=n,
                   append_to=inproj(h, wk, ones, cosf, sinf, tail_rows=n_ctx))
    v_all = inproj(hc, wv, ones, append_at=n, append_to=inproj(h, wv, ones, tail_rows=n_ctx))
    attn = functools.partial(flash_attention, mode="diff", lam_vecs=lam_vecs, subln_g=subln_g,
                             lambda_init=lambda_init)
    wo = w_out.astype(BF16)
    lat = ([attn(q, k_all, v_all, kv_len=n + n_ctx)], [wo])
    if not with_ctx:
        return lat, None
    q_c = inproj(hc, wq, gq)
    return lat, ([attn(q_c, k_all, v_all, kv_start=n, kv_len=n_ctx)], [wo])


def kernel(x, c, ctx, c_ctx, ada_w, ada_b, norm_g, ab_w_in, ab_w_out, ab_qk_g,
           hy_conv_w, hy_conv_b, hy_w1, hy_b1, hy_w2, hy_b2, hy_w3, hy_freq, hy_skip,
           dc_w_in, dc_w_out, dc_lambda, dc_subln_g,
           ffn_w_up, ffn_conv_w, ffn_conv_b, ffn_w_down):
    depth = ada_w.shape[0]
    b, n, d = x.shape
    pad_rows = (-(b + 1)) % 8
    cvec = jnp.concatenate([c, c_ctx[None, :], jnp.zeros((pad_rows, d), F32)], axis=0)
    mod_all = ada_mod(cvec, ada_w, ada_b)
    w_up = ffn_w_up.astype(BF16)
    w_down = ffn_w_down.astype(BF16)
    for i in range(depth):
        j = i // 2
        with_ctx = i < depth - 1
        mod = mod_all[i, :b].reshape(b, 6, d)
        mod_c = jnp.broadcast_to(mod_all[i, b].reshape(1, 6, d), (b, 6, d))
        sh1, sc1, g1, sh2, sc2, g2 = (mod[:, t] for t in range(6))
        sh1c, sc1c, g1c, sh2c, sc2c, g2c = (mod_c[:, t] for t in range(6))
        h = norm_mod(x, norm_g[i, 0], sc1, sh1)
        hc = norm_mod(ctx, norm_g[i, 0], sc1c, sh1c)
        if i % 2 == 0:
            hy = (hy_conv_w[j], hy_conv_b[j], hy_w1[j], hy_b1[j], hy_w2[j], hy_b2[j],
                  hy_w3[j], hy_freq[j], hy_skip[j])
            lat, cx = _mixer_ab(h, hc, with_ctx, ab_w_in[j], ab_w_out[j], ab_qk_g[j], hy)
        else:
            lambda_init = 0.8 - 0.6 * math.exp(-0.3 * i)
            lat, cx = _mixer_c(h, hc, with_ctx, lambda_init, dc_w_in[j], dc_w_out[j],
                               dc_lambda[j], dc_subln_g[j])
        x = outproj_residual(lat[0], lat[1], x, norm_g[i, 1], g1)
        x = conv_ffn_residual(x, norm_g[i, 2], sc2, sh2, w_up, ffn_conv_w[i], ffn_conv_b[i],
                              w_down, norm_g[i, 3], g2, i)
        if with_ctx:
            ctx = outproj_residual(cx[0], cx[1], ctx, norm_g[i, 1], g1c)
            ctx = conv_ffn_residual(ctx, norm_g[i, 2], sc2c, sh2c, w_up, ffn_conv_w[i],
                                    ffn_conv_b[i], w_down, norm_g[i, 3], g2c, i)
    return x
```

```python
import functools
import math

import jax
import jax.numpy as jnp
from jax import lax
from jax.experimental import pallas as pl
from jax.experimental.pallas import tpu as pltpu

F32 = jnp.float32
BF16 = jnp.bfloat16

D_MODEL = 2048
GRID_W = 64
EPS = 1e-6
ROPE_THETA = 10000.0
A_HEADS, A_KV_HEADS, A_HEAD_DIM = 8, 2, 128
A_Q_W = A_HEADS * A_HEAD_DIM
A_KV_W = A_KV_HEADS * A_HEAD_DIM
A_QKV_W = A_Q_W + 2 * A_KV_W
HY_CH = D_MODEL // 2
HY_ORDER = 2
HY_FILTER_W = 64
HY_BANDS = 16
HY_FAST, HY_SLOW, HY_TARGET = 0.3, 1.5, 1e-2
C_HEADS, C_HEAD_DIM = 16, 64
C_Q_W = C_HEADS * 2 * C_HEAD_DIM

LANES = 128
BF16_SUBLANES = 16
VMEM_LIMIT = 56 * 1024 * 1024
DFT_RADIX = 128
FLASH_ROWS = 2048
FLASH_SUB_ROWS = 128
FLASH_MAX_KV = 16384
LOG2E = math.log2(math.e)

SUBLANES = 8
ROW_TILE = 512
ADA_COLS = 1024
PROJ_COLS = 2048
CONV_PROJ_COLS = 1024
FFN_CHUNK = 512
DFT_K1_CHUNK = 4
DFT_CH_TILE = 1024


def _cparams(*sem):
    return pltpu.CompilerParams(dimension_semantics=sem, vmem_limit_bytes=VMEM_LIMIT)


def _rms(x, g):
    return x * lax.rsqrt(jnp.mean(x * x, axis=-1, keepdims=True) + EPS) * g


def _split_dot(a, b):
    a_hi, b_hi = a.astype(BF16), b.astype(BF16)
    a_lo = (a - a_hi.astype(F32)).astype(BF16)
    b_lo = (b - b_hi.astype(F32)).astype(BF16)
    dot = functools.partial(jnp.dot, preferred_element_type=F32)
    return dot(a_hi, b_hi) + dot(a_lo, b_hi) + dot(a_hi, b_lo)


def _ada_kernel(c_ref, w_ref, b_ref, o_ref):
    c = c_ref[...]
    s = c * jax.nn.sigmoid(c)
    rows = s.shape[0]
    s_hi = s.astype(BF16).astype(F32)
    s2 = jnp.concatenate([s_hi, s - s_hi], axis=0).astype(BF16)
    w = w_ref[0]
    w_hi = w.astype(BF16)
    w_lo = (w - w_hi.astype(F32)).astype(BF16)
    both = jnp.dot(s2, w_hi, preferred_element_type=F32)
    hi_lo = jnp.dot(s2, w_lo, preferred_element_type=F32)[:rows]
    o_ref[0] = both[:rows] + both[rows:] + hi_lo + b_ref[0]


def ada_mod(cvec, ada_w, ada_b):
    depth, d, n6 = ada_w.shape
    rows = cvec.shape[0]
    tn = ADA_COLS
    return pl.pallas_call(
        _ada_kernel,
        grid=(depth, n6 // tn),
        in_specs=[pl.BlockSpec((rows, d), lambda l, j: (0, 0)),
                  pl.BlockSpec((1, d, tn), lambda l, j: (l, 0, j)),
                  pl.BlockSpec((1, 1, tn), lambda l, j: (l, 0, j))],
        out_specs=pl.BlockSpec((1, rows, tn), lambda l, j: (l, 0, j)),
        out_shape=jax.ShapeDtypeStruct((depth, rows, n6), F32),
        compiler_params=_cparams("arbitrary", "arbitrary"),
        name="ada_mod",
    )(cvec, ada_w, ada_b.reshape(depth, 1, n6))


def _norm_mod_kernel(x_ref, g_ref, sc_ref, sh_ref, o_ref):
    y = _rms(x_ref[0], g_ref[...])
    o_ref[0] = (y * (1.0 + sc_ref[0]) + sh_ref[0]).astype(o_ref.dtype)


def norm_mod(x, g, sc, sh):
    b, n, d = x.shape
    tm = min(n, ROW_TILE)
    return pl.pallas_call(
        _norm_mod_kernel,
        grid=(b, n // tm),
        in_specs=[pl.BlockSpec((1, tm, d), lambda bi, i: (bi, i, 0)),
                  pl.BlockSpec((1, d), lambda bi, i: (0, 0)),
                  pl.BlockSpec((1, 1, d), lambda bi, i: (bi, 0, 0)),
                  pl.BlockSpec((1, 1, d), lambda bi, i: (bi, 0, 0))],
        out_specs=pl.BlockSpec((1, tm, d), lambda bi, i: (bi, i, 0)),
        out_shape=jax.ShapeDtypeStruct((b, n, d), BF16),
        compiler_params=_cparams("arbitrary", "arbitrary"),
        name="norm_mod",
    )(x, g.reshape(1, d), sc.reshape(b, 1, d), sh.reshape(b, 1, d))


def _inproj_kernel(*refs, norm, rope, appending, n_tiles, has_tail):
    o_ref = refs[-1]
    if appending:
        refs = refs[:-2]
    else:
        refs = refs[:-1]
    if rope:
        h_ref, w_ref, g_ref, cos_ref, sin_ref = refs
    else:
        h_ref, w_ref, g_ref = refs
    i = pl.program_id(1)

    def _tail():
        o_ref[0] = jnp.zeros(o_ref.shape[1:], o_ref.dtype)

    def _body():
        acc = jnp.dot(h_ref[0], w_ref[...], preferred_element_type=F32)
        tn = acc.shape[1]
        if rope:
            cosf = cos_ref[...]
            sinf = sin_ref[...]
        for c in range(tn // LANES):
            x = acc[:, c * LANES:(c + 1) * LANES]
            if norm:
                x = x * lax.rsqrt(jnp.mean(x * x, axis=-1, keepdims=True) + EPS)
            x = x * g_ref[:, c * LANES:(c + 1) * LANES]
            if rope:
                x = x * cosf + pltpu.roll(x, LANES // 2, 1) * sinf
            o_ref[0, :, c * LANES:(c + 1) * LANES] = x.astype(o_ref.dtype)

    if has_tail:
        pl.when(i >= n_tiles)(_tail)
        pl.when(i < n_tiles)(_body)
    else:
        _body()


def inproj(h, w, gvec, cosf=None, sinf=None, *, norm=False, out_dtype=BF16,
           tail_rows=0, append_to=None, append_at=0):
    b, n, d = h.shape
    ncol = w.shape[1]
    rope = cosf is not None
    tm = min(n, ROW_TILE)
    tn = min(ncol, PROJ_COLS)
    n_tiles = n // tm
    last = n_tiles - 1
    in_specs = [pl.BlockSpec((1, tm, d), lambda bi, i, j: (bi, jnp.minimum(i, last), 0)),
                pl.BlockSpec((d, tn), lambda bi, i, j: (0, j)),
                pl.BlockSpec((1, tn), lambda bi, i, j: (0, j))]
    args = [h, w, gvec.reshape(1, ncol)]
    if rope:
        in_specs += [pl.BlockSpec((tm, LANES), lambda bi, i, j: (jnp.minimum(i, last), 0)),
                     pl.BlockSpec((tm, LANES), lambda bi, i, j: (jnp.minimum(i, last), 0))]
        args += [cosf, sinf]
    appending = append_to is not None
    if appending:
        total = append_to.shape[1]
        first = append_at // tm
        assert first * tm == append_at and append_at + n <= total and append_to.dtype == out_dtype
        tail_tiles = 0
        in_specs.append(pl.BlockSpec(memory_space=pl.ANY))
        args.append(append_to)
        aliases = {len(args) - 1: 0}
    else:
        tail_tiles = -(-tail_rows // tm)
        total, first, aliases = (n_tiles + tail_tiles) * tm, 0, {}
    return pl.pallas_call(
        functools.partial(_inproj_kernel, norm=norm, rope=rope, appending=appending,
                          n_tiles=n_tiles, has_tail=tail_tiles > 0),
        grid=(b, n_tiles + tail_tiles, ncol // tn),
        in_specs=in_specs,
        out_specs=pl.BlockSpec((1, tm, tn), lambda bi, i, j: (bi, first + i, j)),
        out_shape=jax.ShapeDtypeStruct((b, total, ncol), out_dtype),
        input_output_aliases=aliases,
        compiler_params=_cparams("arbitrary", "arbitrary", "arbitrary"),
        name="inproj",
    )(*args)


def _inproj_conv_kernel(hm_ref, hp_ref, hn_ref, w_ref, cw_ref, cb_ref, o_ref, *, n_tiles):
    i = pl.program_id(2)
    hp, hm, hn = hp_ref[0], hm_ref[0], hn_ref[0]
    halo, tm = hp.shape[0], hm.shape[0]
    zero = jnp.zeros_like(hp)
    lhs = jnp.concatenate([jnp.where(i > 0, hp, zero), hm,
                           jnp.where(i < n_tiles - 1, hn, zero)], axis=0)
    u = jnp.dot(lhs, w_ref[...], preferred_element_type=F32)
    rows = tm + 2 * halo
    cw = cw_ref[...]
    conv = cw[0:1] * pltpu.roll(u, 1, 0) + cw[1:2] * u + cw[2:3] * pltpu.roll(u, rows - 1, 0)
    o_ref[0, 0] = conv[halo:halo + tm] + cb_ref[...]


def inproj_conv(h, w, conv_w, conv_b, groups):
    b, n, d = h.shape
    ncol = w.shape[1]
    c = ncol // groups
    tm = min(n, ROW_TILE)
    tn = min(c, CONV_PROJ_COLS)
    halo = BF16_SUBLANES
    nt = n // tm
    per = c // tn
    return pl.pallas_call(
        functools.partial(_inproj_conv_kernel, n_tiles=nt),
        grid=(ncol // tn, b, nt),
        in_specs=[pl.BlockSpec((1, tm, d), lambda j, bi, i: (bi, i, 0)),
                  pl.BlockSpec((1, halo, d),
                               lambda j, bi, i: (bi, jnp.maximum(i * (tm // halo) - 1, 0), 0)),
                  pl.BlockSpec((1, halo, d),
                               lambda j, bi, i: (bi, jnp.minimum((i + 1) * (tm // halo), n // halo - 1), 0)),
                  pl.BlockSpec((d, tn), lambda j, bi, i: (0, j)),
                  pl.BlockSpec((3, tn), lambda j, bi, i: (0, j)),
                  pl.BlockSpec((1, tn), lambda j, bi, i: (0, j))],
        out_specs=pl.BlockSpec((1, 1, tm, tn), lambda j, bi, i: (j // per, bi, i, j % per)),
        out_shape=jax.ShapeDtypeStruct((groups, b, n, c), F32),
        compiler_params=_cparams("arbitrary", "arbitrary", "arbitrary"),
        name="inproj_conv",
    )(h, h, h, w, conv_w, conv_b.reshape(1, ncol))


def _flash_kernel(*refs, mode, group, tq, lambda_init):
    if mode == "diff":
        q_ref, k_ref, v_ref, lam_ref, sg_ref, o_ref, qs_ref = refs
    else:
        q_ref, k_ref, v_ref, o_ref, qs_ref = refs
    q = q_ref[0]
    if mode == "diff":
        lane = lax.broadcasted_iota(jnp.int32, q.shape, 1)
        zero = jnp.zeros_like(q)
        first = (lane // (C_HEAD_DIM // 2)) % 2 == 0
        qs_ref[0:tq] = jnp.where(first, q, zero)
        qs_ref[tq:2 * tq] = jnp.where(first, zero, q)
    else:
        for g in range(group):
            qs_ref[g * tq:(g + 1) * tq] = q[:, g * LANES:(g + 1) * LANES]

    k = k_ref[0]
    v = v_ref[0]
    v_aug = jnp.concatenate([v, jnp.ones_like(v)], axis=1)
    sub = FLASH_SUB_ROWS

    def attend(c):
        s = lax.dot_general(qs_ref[c * sub:(c + 1) * sub], k, (((1,), (1,)), ((), ())),
                            preferred_element_type=F32)
        p = jnp.exp2(s - jnp.max(s, axis=-1, keepdims=True))
        pv = jnp.dot(p.astype(BF16), v_aug, preferred_element_type=F32)
        return pv[:, :LANES] / pv[:, LANES:]

    per_head = tq // sub
    if mode == "diff":
        lv = lam_ref[...]
        lam = (jnp.exp(jnp.sum(lv[0:1] * lv[1:2], axis=-1, keepdims=True))
               - jnp.exp(jnp.sum(lv[2:3] * lv[3:4], axis=-1, keepdims=True)) + lambda_init)
        for c in range(per_head):
            dlt = attend(c) - lam * attend(per_head + c)
            o_ref[0, c * sub:(c + 1) * sub, :] = (
                _rms(dlt, sg_ref[...]) * (1.0 - lambda_init)).astype(o_ref.dtype)
    else:
        for c in range(group * per_head):
            g, r = divmod(c, per_head)
            o_ref[0, r * sub:(r + 1) * sub, g * LANES:(g + 1) * LANES] = attend(c).astype(o_ref.dtype)


def flash_attention(q, k, v, *, mode, kv_len, lam_vecs=None, subln_g=None, lambda_init=0.0,
                    kv_start=0):
    b, n, qw = q.shape
    tk = kv_len
    assert tk <= FLASH_MAX_KV and tk % LANES == 0 and kv_start % tk == 0
    if mode == "diff":
        group, n_groups, qblk = 2, qw // LANES, LANES
    else:
        group = A_HEADS // A_KV_HEADS
        n_groups, qblk = A_KV_HEADS, group * LANES
    tq = min(n, FLASH_ROWS // group)
    assert tq % FLASH_SUB_ROWS == 0
    j0 = kv_start // tk
    in_specs = [pl.BlockSpec((1, tq, qblk), lambda bi, g, i: (bi, i, g)),
                pl.BlockSpec((1, tk, LANES), lambda bi, g, i: (bi, j0, g)),
                pl.BlockSpec((1, tk, LANES), lambda bi, g, i: (bi, j0, g))]
    args = [q, k, v]
    if mode == "diff":
        in_specs += [pl.BlockSpec(lam_vecs.shape, lambda bi, g, i: (0, 0)),
                     pl.BlockSpec((1, LANES), lambda bi, g, i: (0, 0))]
        args += [lam_vecs, subln_g.reshape(1, LANES)]
    return pl.pallas_call(
        functools.partial(_flash_kernel, mode=mode, group=group, tq=tq, lambda_init=lambda_init),
        grid=(b, n_groups, n // tq),
        in_specs=in_specs,
        out_specs=pl.BlockSpec((1, tq, qblk), lambda bi, g, i: (bi, i, g)),
        out_shape=jax.ShapeDtypeStruct((b, n, qw), BF16),
        scratch_shapes=[pltpu.VMEM((group * tq, LANES), BF16)],
        compiler_params=_cparams("arbitrary", "arbitrary", "arbitrary"),
        name="flash_" + mode,
    )(*args)


def _outproj_kernel(*refs, n_lhs):
    lhs = refs[:n_lhs]
    ws = refs[n_lhs:2 * n_lhs]
    x_ref, g_ref, gate_ref, o_ref = refs[2 * n_lhs:]
    y = jnp.dot(lhs[0][0].astype(BF16), ws[0][...], preferred_element_type=F32)
    for a, w in zip(lhs[1:], ws[1:]):
        y = y + jnp.dot(a[0].astype(BF16), w[...], preferred_element_type=F32)
    o_ref[0] = x_ref[0] + gate_ref[0] * _rms(y, g_ref[...])


def outproj_residual(lhs_list, w_list, x, g, gate):
    b, n, d = x.shape
    tm = min(n, ROW_TILE)
    n_lhs = len(lhs_list)
    in_specs = [pl.BlockSpec((1, tm, a.shape[2]), lambda bi, i: (bi, i, 0)) for a in lhs_list]
    in_specs += [pl.BlockSpec(w.shape, lambda bi, i: (0, 0)) for w in w_list]
    in_specs += [pl.BlockSpec((1, tm, d), lambda bi, i: (bi, i, 0)),
                 pl.BlockSpec((1, d), lambda bi, i: (0, 0)),
                 pl.BlockSpec((1, 1, d), lambda bi, i: (bi, 0, 0))]
    return pl.pallas_call(
        functools.partial(_outproj_kernel, n_lhs=n_lhs),
        grid=(b, n // tm),
        in_specs=in_specs,
        out_specs=pl.BlockSpec((1, tm, d), lambda bi, i: (bi, i, 0)),
        out_shape=jax.ShapeDtypeStruct((b, n, d), F32),
        compiler_params=_cparams("arbitrary", "arbitrary"),
        name="outproj",
    )(*lhs_list, *w_list, x, g.reshape(1, d), gate.reshape(b, 1, d))


def _ffn_kernel(xm_ref, xp_ref, xn_ref, gn_ref, sc_ref, sh_ref, wa_ref, wg_ref, cwa_ref, cwg_ref,
                cba_ref, cbg_ref, wd_ref, g3_ref, gate_ref, o_ref, hx_ref,
                *, tm, halo, n_tiles, n_f):
    i = pl.program_id(1)
    j = pl.program_id(2)

    @pl.when(j == 0)
    def _prologue():
        def nm(x):
            return (_rms(x, gn_ref[...]) * (1.0 + sc_ref[0]) + sh_ref[0]).astype(BF16)
        hx_ref[halo:halo + tm] = nm(xm_ref[0])
        zero = jnp.zeros((halo, xm_ref.shape[2]), BF16)
        hx_ref[0:halo] = jnp.where(i > 0, nm(xp_ref[0]), zero)
        hx_ref[halo + tm:] = jnp.where(i < n_tiles - 1, nm(xn_ref[0]), zero)
        o_ref[0] = jnp.zeros(o_ref.shape[1:], F32)

    hx = hx_ref[...]
    rows = tm + 2 * halo

    def conv(u, cw, cb):
        up = pltpu.roll(u, 1, 0)
        dn = pltpu.roll(u, rows - 1, 0)
        return (cw[0:1] * up + cw[1:2] * u + cw[2:3] * dn + cb)[halo:halo + tm]

    ua = jnp.dot(hx, wa_ref[...], preferred_element_type=F32)
    ug = jnp.dot(hx, wg_ref[...], preferred_element_type=F32)
    pieces = []
    for c0 in range(0, ua.shape[1], LANES):
        cs = slice(c0, c0 + LANES)
        a = conv(ua[:, cs], cwa_ref[:, cs], cba_ref[:, cs])
        g = conv(ug[:, cs], cwg_ref[:, cs], cbg_ref[:, cs])
        pieces.append((a * (g * jax.nn.sigmoid(g))).astype(BF16))
    act = jnp.concatenate(pieces, axis=1)
    o_ref[0] += jnp.dot(act, wd_ref[...], preferred_element_type=F32)

    @pl.when(j == n_f - 1)
    def _epilogue():
        o_ref[0] = xm_ref[0] + gate_ref[0] * _rms(o_ref[0], g3_ref[...])


def conv_ffn_residual(x, gn, sc, sh, w_up, conv_w, conv_b, w_down, g3, gate, layer):
    b, n, d = x.shape
    dff = w_down.shape[1]
    tm = min(n, ROW_TILE)
    tf = FFN_CHUNK
    halo = BF16_SUBLANES
    nt = n // tm
    n_f = dff // tf
    row = lambda bi, i, j: (bi, i, 0)
    vec = lambda bi, i, j: (0, 0)
    bvec = lambda bi, i, j: (bi, 0, 0)
    up_a = lambda bi, i, j: (0, j)
    up_g = lambda bi, i, j: (0, n_f + j)
    wup_a = lambda bi, i, j: (layer, 0, j)
    wup_g = lambda bi, i, j: (layer, 0, n_f + j)
    conv_b = conv_b.reshape(1, 2 * dff)
    return pl.pallas_call(
        functools.partial(_ffn_kernel, tm=tm, halo=halo, n_tiles=nt, n_f=n_f),
        grid=(b, nt, n_f),
        in_specs=[pl.BlockSpec((1, tm, d), row),
                  pl.BlockSpec((1, halo, d),
                               lambda bi, i, j: (bi, jnp.maximum(i * (tm // halo) - 1, 0), 0)),
                  pl.BlockSpec((1, halo, d),
                               lambda bi, i, j: (bi, jnp.minimum((i + 1) * (tm // halo), n // halo - 1), 0)),
                  pl.BlockSpec((1, d), vec),
                  pl.BlockSpec((1, 1, d), bvec),
                  pl.BlockSpec((1, 1, d), bvec),
                  pl.BlockSpec((None, d, tf), wup_a),
                  pl.BlockSpec((None, d, tf), wup_g),
                  pl.BlockSpec((3, tf), up_a),
                  pl.BlockSpec((3, tf), up_g),
                  pl.BlockSpec((1, tf), up_a),
                  pl.BlockSpec((1, tf), up_g),
                  pl.BlockSpec((None, tf, d), lambda bi, i, j: (layer, j, 0)),
                  pl.BlockSpec((1, d), vec),
                  pl.BlockSpec((1, 1, d), bvec)],
        out_specs=pl.BlockSpec((1, tm, d), row),
        out_shape=jax.ShapeDtypeStruct((b, n, d), F32),
        scratch_shapes=[pltpu.VMEM((tm + 2 * halo, d), BF16)],
        compiler_params=_cparams("arbitrary", "arbitrary", "arbitrary"),
        name="conv_ffn",
    )(x, x, x, gn.reshape(1, d), sc.reshape(b, 1, d), sh.reshape(b, 1, d), w_up, w_up,
      conv_w, conv_w, conv_b, conv_b, w_down, g3.reshape(1, d), gate.reshape(b, 1, d))


def _filter_rows(r, bands_ref, w1_ref, b1_ref, w2_ref, b2_ref, fr_ref, dl_ref, n_tok):
    t = jnp.where(r < n_tok, r, 2.0 * n_tok - r)
    t_norm = t / float(max(n_tok - 1, 1))
    wang = (2.0 * math.pi / n_tok) * t
    z = wang * bands_ref[...]
    lane = lax.broadcasted_iota(jnp.int32, z.shape, 1)
    feats = jnp.where(lane == 0, t_norm,
                      jnp.where(lane <= HY_BANDS, jnp.cos(z),
                                jnp.where(lane <= 2 * HY_BANDS, -jnp.sin(z), 0.0)))
    fr = fr_ref[...]
    h = jnp.sin(fr[0:1] * (_split_dot(feats, w1_ref[...]) + b1_ref[...]))
    h = jnp.sin(fr[1:2] * (_split_dot(h, w2_ref[...]) + b2_ref[...]))
    decay = jnp.where(r == n_tok, 0.0, jnp.exp(-t_norm * dl_ref[...]))
    return h, decay


def _filter_kernel(bands_ref, w1_ref, b1_ref, w2_ref, b2_ref, w3_ref, fr_ref, dl_ref,
                   o_ref, s_ref, *, n_tok, tr):
    i = pl.program_id(0)

    @pl.when(i == 0)
    def _init():
        s_ref[...] = jnp.zeros(s_ref.shape, F32)

    r = (i * tr + lax.broadcasted_iota(jnp.int32, (tr, 1), 0)).astype(F32)
    h, decay = _filter_rows(r, bands_ref, w1_ref, b1_ref, w2_ref, b2_ref, fr_ref, dl_ref, n_tok)
    hb = h.astype(BF16)
    for o in range(HY_ORDER):
        out = jnp.dot(hb, w3_ref[o].astype(BF16), preferred_element_type=F32) * decay
        o_ref[o] = out
        s_ref[o] += jnp.sum(jnp.abs(out), axis=0, keepdims=True)


def _filter_stage_a_kernel(bands_ref, w1_ref, b1_ref, w2_ref, b2_ref, w3_ref, fr_ref, dl_ref,
                           f_ref, *out_refs, n_tok, sub):
    o_refs, s_ref = out_refs[:HY_ORDER], out_refs[HY_ORDER]
    j = pl.program_id(0)

    @pl.when(j == 0)
    def _init():
        s_ref[...] = jnp.zeros(s_ref.shape, F32)

    k1n = o_refs[0].shape[1]
    half = k1n // 2
    c = dl_ref.shape[1]
    n1 = lax.broadcasted_iota(jnp.int32, (k1n, 1), 0)
    f = f_ref[...]
    for q in range(sub):
        r = (n1 * DFT_RADIX + (j * sub + q)).astype(F32)
        h, decay = _filter_rows(r, bands_ref, w1_ref, b1_ref, w2_ref, b2_ref, fr_ref, dl_ref, n_tok)
        h_fwd, h_bwd = h[:half].astype(BF16), h[half:].astype(BF16)
        for o in range(HY_ORDER):
            w3 = w3_ref[o].astype(BF16)
            h3 = jnp.concatenate([jnp.dot(h_fwd, w3[:, :c], preferred_element_type=F32),
                                  jnp.dot(h_bwd, w3[:, c:], preferred_element_type=F32)],
                                 axis=0) * decay
            s_ref[o] += jnp.sum(jnp.abs(h3), axis=0, keepdims=True)
            y = _dft_dot(f, h3)
            o_refs[o][0, :, q, :] = y[:k1n]
            o_refs[o][1, :, q, :] = y[k1n:]


def _filter_operands(w1, b1, w2, b2, w3, freq):
    ch = HY_CH
    lane = jnp.arange(LANES)
    bands = jnp.linspace(1e-4, HY_BANDS - 1, HY_BANDS, dtype=F32)
    bands_l = jnp.where((lane >= 1) & (lane <= HY_BANDS), bands[jnp.clip(lane - 1, 0, HY_BANDS - 1)],
                        jnp.where((lane > HY_BANDS) & (lane <= 2 * HY_BANDS),
                                  bands[jnp.clip(lane - 1 - HY_BANDS, 0, HY_BANDS - 1)], 0.0))
    w1p = jnp.zeros((LANES, HY_FILTER_W), F32).at[:w1.shape[0]].set(w1)
    deltas = jnp.abs(jnp.linspace(math.log(HY_TARGET) / HY_FAST, math.log(HY_TARGET) / HY_SLOW,
                                  ch, dtype=F32))
    w3o = jnp.swapaxes(w3.reshape(HY_FILTER_W, HY_ORDER, 2 * ch), 0, 1)
    return (bands_l.reshape(1, LANES), w1p, b1.reshape(1, -1), w2, b2.reshape(1, -1), w3o, freq,
            deltas.reshape(1, ch))


def _filter_in_specs(w3_spec):
    const = lambda i: (0, 0)
    return [pl.BlockSpec((1, LANES), const),
            pl.BlockSpec((LANES, HY_FILTER_W), const),
            pl.BlockSpec((1, HY_FILTER_W), const),
            pl.BlockSpec((HY_FILTER_W, HY_FILTER_W), const),
            pl.BlockSpec((1, HY_FILTER_W), const),
            w3_spec,
            pl.BlockSpec((2, HY_FILTER_W), const),
            pl.BlockSpec((1, HY_CH), const)]


def hyena_filter_time(n_tok, w1, b1, w2, b2, w3, freq):
    ch = HY_CH
    ll = 2 * n_tok
    tr = min(n_tok, ROW_TILE)
    per_half = n_tok // tr
    return pl.pallas_call(
        functools.partial(_filter_kernel, n_tok=n_tok, tr=tr),
        grid=(ll // tr,),
        in_specs=_filter_in_specs(
            pl.BlockSpec((HY_ORDER, HY_FILTER_W, ch), lambda i: (0, 0, i // per_half))),
        out_specs=[pl.BlockSpec((HY_ORDER, tr, ch), lambda i: (0, i, 0)),
                   pl.BlockSpec((HY_ORDER, 1, ch), lambda i: (0, 0, 0))],
        out_shape=[jax.ShapeDtypeStruct((HY_ORDER, ll, ch), F32),
                   jax.ShapeDtypeStruct((HY_ORDER, 1, ch), F32)],
        compiler_params=_cparams("arbitrary"),
        name="hyena_filter",
    )(*_filter_operands(w1, b1, w2, b2, w3, freq))


def hyena_filter_stage_a(n_tok, f_filt, w1, b1, w2, b2, w3, freq):
    ch = HY_CH
    r = DFT_RADIX
    k1n = 2 * n_tok // r
    sub = SUBLANES
    return pl.pallas_call(
        functools.partial(_filter_stage_a_kernel, n_tok=n_tok, sub=sub),
        grid=(r // sub,),
        in_specs=_filter_in_specs(
            pl.BlockSpec((HY_ORDER, HY_FILTER_W, 2 * ch), lambda j: (0, 0, 0)))
        + [pl.BlockSpec(f_filt.shape, lambda j: (0, 0))],
        out_specs=[pl.BlockSpec((2, k1n, sub, ch), lambda j: (0, 0, j, 0))] * HY_ORDER
        + [pl.BlockSpec((HY_ORDER, 1, ch), lambda j: (0, 0, 0))],
        out_shape=[jax.ShapeDtypeStruct((2, k1n, r, ch), F32)] * HY_ORDER
        + [jax.ShapeDtypeStruct((HY_ORDER, 1, ch), F32)],
        compiler_params=_cparams("arbitrary"),
        name="hyena_filter_a",
    )(*_filter_operands(w1, b1, w2, b2, w3, freq), f_filt)


def _dft_dot(f, x):
    return jnp.dot(f.astype(BF16), x.astype(BF16), preferred_element_type=F32)


def _stage_a_kernel(f_ref, x_ref, o_ref):
    parts = x_ref.shape[0]
    f = f_ref[...]
    for r in range(x_ref.shape[2]):
        z = jnp.concatenate([x_ref[p, :, r, :] for p in range(parts)], axis=0)
        y = _dft_dot(f, z)
        half = y.shape[0] // 2
        o_ref[0, :, r, :] = y[:half]
        o_ref[1, :, r, :] = y[half:]


def dft_stage_a(f, x, g):
    _, p, rows, r, c = x.shape
    k1 = p * rows
    sub = SUBLANES
    return pl.pallas_call(
        _stage_a_kernel,
        grid=(r // sub,),
        in_specs=[pl.BlockSpec(f.shape, lambda j: (0, 0)),
                  pl.BlockSpec((None, p, rows, sub, c), lambda j: (g, 0, 0, j, 0))],
        out_specs=pl.BlockSpec((2, k1, sub, c), lambda j: (0, 0, j, 0)),
        out_shape=jax.ShapeDtypeStruct((2, k1, r, c), F32),
        compiler_params=_cparams("arbitrary"),
        name="dft_stage_a",
    )(f, x)


def _stage_c_conv_kernel(a_ref, af_ref, sc_ref, g_ref, gi_ref, o_ref, *, kc):
    tc = a_ref.shape[3]
    for k in range(kc):
        r = jnp.concatenate([jnp.concatenate([a_ref[0, k], af_ref[0, k]], axis=1),
                             jnp.concatenate([a_ref[1, k], af_ref[1, k]], axis=1)], axis=0)
        y = _dft_dot(g_ref[k], r)
        half = y.shape[0] // 2
        yre, yim = y[:half, :tc], y[half:, :tc]
        hre, him = y[:half, tc:] * sc_ref[...], y[half:, tc:] * sc_ref[...]
        p = jnp.concatenate([yre * hre - yim * him, yre * him + yim * hre], axis=0)
        d = _dft_dot(gi_ref[k], p)
        o_ref[0, k] = d[:half]
        o_ref[1, k] = d[half:]


def dft_stage_c_conv(a, af, scale, g, gi):
    _, k1, r, c = a.shape
    kc = DFT_K1_CHUNK
    tc = DFT_CH_TILE
    blk = pl.BlockSpec((2, kc, r, tc), lambda i, j: (0, i, 0, j))
    mat = pl.BlockSpec((kc, 2 * r, 2 * r), lambda i, j: (i, 0, 0))
    return pl.pallas_call(
        functools.partial(_stage_c_conv_kernel, kc=kc),
        grid=(k1 // kc, c // tc),
        in_specs=[blk, blk, pl.BlockSpec((1, tc), lambda i, j: (0, j)), mat, mat],
        out_specs=blk,
        out_shape=jax.ShapeDtypeStruct(a.shape, F32),
        compiler_params=_cparams("arbitrary", "arbitrary"),
        name="dft_stage_c_conv",
    )(a, af, scale, g, gi)


def _stage_a_inv_kernel(f_ref, d_ref, u_ref, gt_ref, sk_ref, o_ref):
    f = f_ref[...]
    sk = sk_ref[...]
    for r in range(d_ref.shape[2]):
        d = jnp.concatenate([d_ref[0, :, r, :], d_ref[1, :, r, :]], axis=0)
        y = _dft_dot(f, d)
        half = y.shape[0] // 2
        for bi in range(2):
            yb = y[bi * half:(bi + 1) * half]
            o_ref[bi, :, r, :] = gt_ref[bi, :, r, :] * (yb + u_ref[bi, :, r, :] * sk)


def dft_stage_a_inv(f, d, u, gu, gate, gg, skip):
    _, k1, r, c = d.shape
    half = k1 // 2
    sub = SUBLANES
    io = pl.BlockSpec((2, half, sub, c), lambda j: (0, 0, j, 0))
    return pl.pallas_call(
        _stage_a_inv_kernel,
        grid=(r // sub,),
        in_specs=[pl.BlockSpec(f.shape, lambda j: (0, 0)),
                  pl.BlockSpec((2, k1, sub, c), lambda j: (0, 0, j, 0)),
                  pl.BlockSpec((None, 2, half, sub, c), lambda j: (gu, 0, 0, j, 0)),
                  pl.BlockSpec((None, 2, half, sub, c), lambda j: (gg, 0, 0, j, 0)),
                  pl.BlockSpec((1, c), lambda j: (0, 0))],
        out_specs=io,
        out_shape=jax.ShapeDtypeStruct((2, half, r, c), F32),
        compiler_params=_cparams("arbitrary"),
        name="dft_stage_a_inv",
    )(f, d, u, gate, skip)


def _dft_tables(n_tok):
    ll = 2 * n_tok
    r = DFT_RADIX
    k1n = ll // r
    half = k1n // 2
    two_pi = 2.0 * math.pi

    def cs(m, period):
        ang = (m % period).astype(F32) * (two_pi / period)
        return jnp.cos(ang), jnp.sin(ang)

    k1 = jnp.arange(k1n, dtype=jnp.int32)
    c, s = cs(k1[:, None] * k1[None, :half], k1n)
    fa_data = jnp.concatenate([jnp.concatenate([c, s], 1), jnp.concatenate([-s, c], 1)], 0)
    c, s = cs(k1[:, None] * k1[None, :], k1n)
    fa_filt = jnp.concatenate([c, -s], 0)
    c, s = cs(k1[:half, None] * k1[None, :], k1n)
    fa_inv = jnp.concatenate([jnp.concatenate([c, -s], 1), jnp.concatenate([s, c], 1)], 0)
    idx = jnp.arange(r, dtype=jnp.int32)
    ca, sa = cs(idx[:, None] * idx[None, :], r)
    cb, sb = cs(k1[:, None] * idx[None, :], ll)
    c = ca[None] * cb[:, None, :] - sa[None] * sb[:, None, :]
    s = sa[None] * cb[:, None, :] + ca[None] * sb[:, None, :]
    g_fwd = jnp.concatenate([jnp.concatenate([c, s], 2), jnp.concatenate([-s, c], 2)], 1)
    ct, st = jnp.swapaxes(c, 1, 2), jnp.swapaxes(s, 1, 2)
    g_inv = jnp.concatenate([jnp.concatenate([ct, -st], 2), jnp.concatenate([st, ct], 2)], 1)
    return tuple(t.astype(BF16) for t in (fa_data, fa_filt, fa_inv, g_fwd, g_inv))


def hyena_long(pc, filter_params, skip):
    _, b, n, c = pc.shape
    assert b == 2, "the two batch entries ride as real/imaginary parts"
    ll = 2 * n
    r = DFT_RADIX
    half = ll // r // 2
    fa_data, fa_filt, fa_inv, g_fwd, g_inv = _dft_tables(n)
    *afs, l1 = hyena_filter_stage_a(n, fa_filt, *filter_params)
    pcs = pc.reshape(HY_ORDER + 1, 2, half, r, c)
    u, gu = pcs, 0
    for o in range(HY_ORDER):
        a = dft_stage_a(fa_data, u, gu)
        d = dft_stage_c_conv(a, afs[o], 1.0 / (l1[o] * ll), g_fwd, g_inv)
        z = dft_stage_a_inv(fa_inv, d, u, gu, pcs, o + 1, skip[o].reshape(1, c))
        u, gu = z[None], 0
    return z.reshape(b, n, c)


def _hyena_short_kernel(ff_ref, fr_ref, fi_ref, pc_ref, filt_ref, l1_ref, skip_ref, o_ref, *, n, ll):
    u = [pc_ref[0, 0], pc_ref[0, 1]]
    for o in range(HY_ORDER):
        hs = _dft_dot(fr_ref[...], filt_ref[o]) * (1.0 / (l1_ref[o] * ll))
        y = _dft_dot(ff_ref[...], jnp.concatenate(u, axis=0))
        yre, yim, hre, him = y[:ll], y[ll:], hs[:ll], hs[ll:]
        p = jnp.concatenate([yre * hre - yim * him, yre * him + yim * hre], axis=0)
        d = _dft_dot(fi_ref[...], p)
        sk = skip_ref[o:o + 1]
        u = [pc_ref[o + 1, bi] * (d[bi * n:(bi + 1) * n] + u[bi] * sk) for bi in range(2)]
    o_ref[0] = u[0].astype(o_ref.dtype)
    o_ref[1] = u[1].astype(o_ref.dtype)


def hyena_short(pc, filt, l1, skip, out_dtype):
    _, b, n, c = pc.shape
    assert b == 2
    ll = 2 * n
    two_pi = 2.0 * math.pi
    kk = jnp.arange(ll, dtype=jnp.int32)
    ang = ((kk[:, None] * kk[None, :]) % ll).astype(F32) * (two_pi / ll)
    cf, sf = jnp.cos(ang), jnp.sin(ang)
    f_real = jnp.concatenate([cf, -sf], 0)
    cn, sn = cf[:, :n], sf[:, :n]
    f_fwd = jnp.concatenate([jnp.concatenate([cn, sn], 1), jnp.concatenate([-sn, cn], 1)], 0)
    ci, si = cf[:n, :], sf[:n, :]
    f_inv = jnp.concatenate([jnp.concatenate([ci, -si], 1), jnp.concatenate([si, ci], 1)], 0)
    tc = DFT_CH_TILE
    const = lambda j: (0, 0)
    return pl.pallas_call(
        functools.partial(_hyena_short_kernel, n=n, ll=ll),
        grid=(c // tc,),
        in_specs=[pl.BlockSpec(f_fwd.shape, const),
                  pl.BlockSpec(f_real.shape, const),
                  pl.BlockSpec(f_inv.shape, const),
                  pl.BlockSpec((3, 2, n, tc), lambda j: (0, 0, 0, j)),
                  pl.BlockSpec((HY_ORDER, ll, tc), lambda j: (0, 0, j)),
                  pl.BlockSpec((HY_ORDER, 1, tc), lambda j: (0, 0, j)),
                  pl.BlockSpec((HY_ORDER, tc), lambda j: (0, j))],
        out_specs=pl.BlockSpec((2, n, tc), lambda j: (0, 0, j)),
        out_shape=jax.ShapeDtypeStruct((2, n, c), out_dtype),
        compiler_params=_cparams("arbitrary"),
        name="hyena_short",
    )(f_fwd, f_real, f_inv, pc, filt, l1, skip)


def hyena_mixer(pc, w1, b1, w2, b2, w3, freq, skip, out_dtype):
    n = pc.shape[2]
    if (2 * n) % (DFT_RADIX * BF16_SUBLANES) == 0 and n >= 1024:
        return hyena_long(pc, (w1, b1, w2, b2, w3, freq), skip)
    filt, l1 = hyena_filter_time(n, w1, b1, w2, b2, w3, freq)
    return hyena_short(pc, filt, l1, skip, out_dtype)


def _deinterleave_cols(w, hd):
    lead = w.shape[:-1]
    wh = w.reshape(lead + (w.shape[-1] // LANES, LANES // hd, hd // 2, 2))
    half = lead + (w.shape[-1] // LANES, LANES // 2)
    return jnp.concatenate([wh[..., 0].reshape(half), wh[..., 1].reshape(half)],
                           axis=-1).reshape(w.shape)


def _rope_tables(n_tok, rot_dim):
    rows = n_tok // GRID_W
    row = jnp.broadcast_to(jnp.arange(rows, dtype=jnp.int32)[:, None], (rows, GRID_W)).reshape(n_tok)
    col = jnp.broadcast_to(jnp.arange(GRID_W, dtype=jnp.int32)[None, :], (rows, GRID_W)).reshape(n_tok)
    axis_dim = rot_dim // 2
    inv_freq = ROPE_THETA ** (-jnp.arange(0, axis_dim, 2, dtype=F32) / axis_dim)
    ang = jnp.concatenate([row.astype(F32)[:, None] * inv_freq,
                           col.astype(F32)[:, None] * inv_freq], axis=-1)
    c, s = jnp.cos(ang), jnp.sin(ang)
    reps = LANES // rot_dim
    c, s = jnp.tile(c, (1, reps)), jnp.tile(s, (1, reps))
    return jnp.concatenate([c, c], -1), jnp.concatenate([-s, s], -1)


def _mixer_ab(h, hc, with_ctx, w_in, w_out, qk_g, hy):
    b, n, _ = h.shape
    conv_w, conv_b, w1, b1, w2, b2, w3, freq, skip = hy
    scale = A_HEAD_DIM ** -0.5 * LOG2E
    wq = _deinterleave_cols(w_in[:, :A_Q_W], A_HEAD_DIM).astype(BF16)
    wk = _deinterleave_cols(w_in[:, A_Q_W:A_Q_W + A_KV_W], A_HEAD_DIM).astype(BF16)
    wv = w_in[:, A_Q_W + A_KV_W:A_QKV_W].astype(BF16)
    wh = w_in[:, A_QKV_W:].astype(BF16)
    gq = jnp.tile(_deinterleave_cols(qk_g[0], A_HEAD_DIM) * scale, A_HEADS)
    gk = jnp.tile(_deinterleave_cols(qk_g[1], A_HEAD_DIM), A_KV_HEADS)
    ones_v = jnp.ones((A_KV_W,), F32)
    cosf, sinf = _rope_tables(n, A_HEAD_DIM)

    n_ctx = hc.shape[1]
    q = inproj(h, wq, gq, cosf, sinf, norm=True)
    k_all = inproj(hc, wk, gk, norm=True, append_at=n,
                   append_to=inproj(h, wk, gk, cosf, sinf, norm=True, tail_rows=n_ctx))
    v_all = inproj(hc, wv, ones_v, append_at=n, append_to=inproj(h, wv, ones_v, tail_rows=n_ctx))
    o_att = flash_attention(q, k_all, v_all, mode="gqa", kv_len=n + n_ctx)
    pc = inproj_conv(h, wh, conv_w, conv_b, HY_ORDER + 1)
    o_hy = hyena_mixer(pc, w1, b1, w2, b2, w3, freq, skip, BF16)
    wo = w_out.astype(BF16)
    lat = ([o_att, o_hy], [wo[:A_Q_W], wo[A_Q_W:]])
    if not with_ctx:
        return lat, None
    q_c = inproj(hc, wq, gq, norm=True)
    o_att_c = flash_attention(q_c, k_all, v_all, mode="gqa", kv_start=n, kv_len=n_ctx)
    pc_c = inproj_conv(hc, wh, conv_w, conv_b, HY_ORDER + 1)
    o_hy_c = hyena_mixer(pc_c, w1, b1, w2, b2, w3, freq, skip, BF16)
    return lat, ([o_att_c, o_hy_c], [wo[:A_Q_W], wo[A_Q_W:]])


def _mixer_c(h, hc, with_ctx, lambda_init, w_in, w_out, lam_vecs, subln_g):
    b, n, _ = h.shape
    scale = C_HEAD_DIM ** -0.5 * LOG2E
    wq = _deinterleave_cols(w_in[:, :C_Q_W], C_HEAD_DIM).astype(BF16)
    wk = _deinterleave_cols(w_in[:, C_Q_W:2 * C_Q_W], C_HEAD_DIM).astype(BF16)
    wv = w_in[:, 2 * C_Q_W:].astype(BF16)
    gq = jnp.full((C_Q_W,), scale, F32)
    ones = jnp.ones((C_Q_W,), F32)
    cosf, sinf = _rope_tables(n, C_HEAD_DIM)

    n_ctx = hc.shape[1]
    q = inproj(h, wq, gq, cosf, sinf)
    k_all = inproj(hc, wk, ones, append_at=n,
                   append_to=inproj(h, wk, ones, cosf, sinf, tail_rows=n_ctx))
    v_all = inproj(hc, wv, ones, append_at=n, append_to=inproj(h, wv, ones, tail_rows=n_ctx))
    attn = functools.partial(flash_attention, mode="diff", lam_vecs=lam_vecs, subln_g=subln_g,
                             lambda_init=lambda_init)
    wo = w_out.astype(BF16)
    lat = ([attn(q, k_all, v_all, kv_len=n + n_ctx)], [wo])
    if not with_ctx:
        return lat, None
    q_c = inproj(hc, wq, gq)
    return lat, ([attn(q_c, k_all, v_all, kv_start=n, kv_len=n_ctx)], [wo])


def kernel(x, c, ctx, c_ctx, ada_w, ada_b, norm_g, ab_w_in, ab_w_out, ab_qk_g,
           hy_conv_w, hy_conv_b, hy_w1, hy_b1, hy_w2, hy_b2, hy_w3, hy_freq, hy_skip,
           dc_w_in, dc_w_out, dc_lambda, dc_subln_g,
           ffn_w_up, ffn_conv_w, ffn_conv_b, ffn_w_down):
    depth = ada_w.shape[0]
    b, n, d = x.shape
    pad_rows = (-(b + 1)) % 8
    cvec = jnp.concatenate([c, c_ctx[None, :], jnp.zeros((pad_rows, d), F32)], axis=0)
    mod_all = ada_mod(cvec, ada_w, ada_b)
    w_up = ffn_w_up.astype(BF16)
    w_down = ffn_w_down.astype(BF16)
    for i in range(depth):
        j = i // 2
        with_ctx = i < depth - 1
        mod = mod_all[i, :b].reshape(b, 6, d)
        mod_c = jnp.broadcast_to(mod_all[i, b].reshape(1, 6, d), (b, 6, d))
        sh1, sc1, g1, sh2, sc2, g2 = (mod[:, t] for t in range(6))
        sh1c, sc1c, g1c, sh2c, sc2c, g2c = (mod_c[:, t] for t in range(6))
        h = norm_mod(x, norm_g[i, 0], sc1, sh1)
        hc = norm_mod(ctx, norm_g[i, 0], sc1c, sh1c)
        if i % 2 == 0:
            hy = (hy_conv_w[j], hy_conv_b[j], hy_w1[j], hy_b1[j], hy_w2[j], hy_b2[j],
                  hy_w3[j], hy_freq[j], hy_skip[j])
            lat, cx = _mixer_ab(h, hc, with_ctx, ab_w_in[j], ab_w_out[j], ab_qk_g[j], hy)
        else:
            lambda_init = 0.8 - 0.6 * math.exp(-0.3 * i)
            lat, cx = _mixer_c(h, hc, with_ctx, lambda_init, dc_w_in[j], dc_w_out[j],
                               dc_lambda[j], dc_subln_g[j])
        x = outproj_residual(lat[0], lat[1], x, norm_g[i, 1], g1)
        x = conv_ffn_residual(x, norm_g[i, 2], sc2, sh2, w_up, ffn_conv_w[i], ffn_conv_b[i],
                              w_down, norm_g[i, 3], g2, i)
        if with_ctx:
            ctx = outproj_residual(cx[0], cx[1], ctx, norm_g[i, 1], g1c)
            ctx = conv_ffn_residual(ctx, norm_g[i, 2], sc2c, sh2c, w_up, ffn_conv_w[i],
                                    ffn_conv_b[i], w_down, norm_g[i, 3], g2c, i)
    return x
```
